```python
import math
import jax, jax.numpy as jnp
from jax import lax
import numpy as np

D_MODEL = 1024
BATCH = 8
SEQ = 2048
DEPTH = 2

N_A_LAYERS = DEPTH // 2
N_B_LAYERS = DEPTH - N_A_LAYERS
S5_GROUP = 16
S5_GROUPS = D_MODEL // S5_GROUP
S5_STATE = 64
DT_MIN = 1e-3
DT_MAX = 1e-1
LAMBDA_RE_MAX = -1e-4
HEAD_DIM = 64
N_Q_HEADS = D_MODEL // HEAD_DIM
N_KV_HEADS = 4
Q_PER_KV = N_Q_HEADS // N_KV_HEADS
WINDOW = 128
BLOCK = 128
D_FF = 4 * D_MODEL
NORM_EPS = 1e-5

kernel_name = "yoco_s5_swa_sink_hybrid"


def rmsnorm(x, g):
    x32 = x.astype(jnp.float32)
    y = x32 * lax.rsqrt(jnp.mean(x32 * x32, axis=-1, keepdims=True) + NORM_EPS)
    return (y * g.astype(jnp.float32)).astype(x.dtype)


def s5_mixer(u, a_re, a_im, log_dt, b_re, b_im, c_re, c_im, d_skip, w_glu, b_glu):
    bsz, seq, _ = u.shape
    f32 = jnp.float32
    u32 = u.astype(f32)
    ug = u32.reshape(bsz, seq, S5_GROUPS, S5_GROUP)
    lam = lax.complex(jnp.minimum(a_re.astype(f32), LAMBDA_RE_MAX), a_im.astype(f32))
    dt = jnp.exp(log_dt.astype(f32))[:, None]
    lam_bar = jnp.exp(lam * dt)
    b_c = lax.complex(b_re.astype(f32), b_im.astype(f32))
    b_bar = ((lam_bar - 1.0) / lam)[:, :, None] * b_c
    bu = lax.complex(jnp.einsum('blgc,gpc->blgp', ug, jnp.real(b_bar)),
                     jnp.einsum('blgc,gpc->blgp', ug, jnp.imag(b_bar)))
    a = jnp.broadcast_to(lam_bar[None, None], (1, seq, S5_GROUPS, S5_STATE))

    def combine(left, right):
        a_l, b_l = left
        a_r, b_r = right
        return a_r * a_l, a_r * b_l + b_r

    _, states = lax.associative_scan(combine, (a, bu), axis=1)
    y = (jnp.einsum('blgp,gcp->blgc', jnp.real(states), c_re.astype(f32))
         - jnp.einsum('blgp,gcp->blgc', jnp.imag(states), c_im.astype(f32)))
    y = y.reshape(bsz, seq, D_MODEL) + d_skip.astype(f32) * u32
    y = jax.nn.gelu(y).astype(u.dtype)
    z = y @ w_glu + b_glu
    val, gate = jnp.split(z, 2, axis=-1)
    return val * jax.nn.sigmoid(gate)


def shared_kv(h, g_kv, w_kv, b_kv):
    bsz, seq, _ = h.shape
    kv = rmsnorm(h, g_kv) @ w_kv + b_kv
    k, v = jnp.split(kv, 2, axis=-1)
    return (k.reshape(bsz, seq, N_KV_HEADS, HEAD_DIM),
            v.reshape(bsz, seq, N_KV_HEADS, HEAD_DIM))


def window_blocks(t, nb):
    bsz = t.shape[0]
    tb = t.reshape(bsz, nb, BLOCK, N_KV_HEADS, HEAD_DIM)
    prev = jnp.concatenate([jnp.zeros_like(tb[:, :1]), tb[:, :-1]], axis=1)
    return jnp.moveaxis(jnp.concatenate([prev, tb], axis=2), 1, 0)


def swa_sink_attention(hn, k, v, w_q, b_q, sinks, w_o, b_o):
    bsz, seq, _ = hn.shape
    nb = seq // BLOCK
    f32 = jnp.float32
    scale = 1.0 / math.sqrt(HEAD_DIM)
    q = (hn @ w_q + b_q).reshape(bsz, nb, BLOCK, N_KV_HEADS, Q_PER_KV, HEAD_DIM)
    q = jnp.moveaxis(q, 1, 0)
    kw = window_blocks(k, nb)
    vw = window_blocks(v, nb)
    qi = jnp.arange(BLOCK)[:, None]
    kj = jnp.arange(2 * BLOCK)[None, :]
    diff = qi + BLOCK - kj
    band = (diff >= 0) & (diff < WINDOW)
    sink = sinks.astype(f32).reshape(N_KV_HEADS, Q_PER_KV)[None, :, :, None]

    def block_fn(args):
        n, qb, kb, vb = args
        s = jnp.einsum('bqkgd,bskd->bkgqs', qb.astype(f32), kb.astype(f32)) * scale
        valid = band & ((n - 1) * BLOCK + kj >= 0)
        s = jnp.where(valid, s, -jnp.inf)
        m = jnp.maximum(jnp.max(s, axis=-1), sink)
        p = jnp.exp(s - m[..., None])
        denom = jnp.sum(p, axis=-1) + jnp.exp(sink - m)
        o = jnp.einsum('bkgqs,bskd->bqkgd', p / denom[..., None], vb.astype(f32))
        return o.astype(hn.dtype)

    o = lax.map(block_fn, (jnp.arange(nb), q, kw, vw))
    o = jnp.moveaxis(o, 0, 1).reshape(bsz, seq, N_Q_HEADS * HEAD_DIM)
    return o @ w_o + b_o


def sq_relu_mlp(h, w_in, w_out):
    return jnp.square(jax.nn.relu(h @ w_in)) @ w_out


def setup_inputs(seed: int = 0) -> dict:
    key = jax.random.key(seed)
    ks = jax.random.split(key, 26)
    f32 = jnp.float32
    nrm = lambda k, shape, s: jax.random.normal(k, shape, f32) * s
    G, P, GC, D = S5_GROUPS, S5_STATE, S5_GROUP, D_MODEL
    HQ = N_Q_HEADS * HEAD_DIM
    HKV = N_KV_HEADS * HEAD_DIM
    x = jax.random.normal(ks[0], (BATCH, SEQ, D), f32)
    norm_mix = 1.0 + nrm(ks[1], (DEPTH, D), 0.02)
    norm_mlp = 1.0 + nrm(ks[2], (DEPTH, D), 0.02)
    norm_kv = 1.0 + nrm(ks[3], (D,), 0.02)
    norm_final = 1.0 + nrm(ks[4], (D,), 0.02)
    s5_a_re = -0.5 + nrm(ks[5], (N_A_LAYERS, G, P), 0.01)
    s5_a_im = (jnp.pi * jnp.arange(P, dtype=f32))[None, None, :] + nrm(ks[6], (N_A_LAYERS, G, P), 0.01)
    s5_log_dt = jax.random.uniform(ks[7], (N_A_LAYERS, G), f32, math.log(DT_MIN), math.log(DT_MAX))
    s5_b_re = nrm(ks[8], (N_A_LAYERS, G, P, GC), (2.0 * GC) ** -0.5)
    s5_b_im = nrm(ks[9], (N_A_LAYERS, G, P, GC), (2.0 * GC) ** -0.5)
    s5_c_re = nrm(ks[10], (N_A_LAYERS, G, GC, P), P ** -0.5)
    s5_c_im = nrm(ks[11], (N_A_LAYERS, G, GC, P), P ** -0.5)
    s5_d = nrm(ks[12], (N_A_LAYERS, D), 1.0)
    s5_w_glu = nrm(ks[13], (N_A_LAYERS, D, 2 * D), D ** -0.5)
    s5_b_glu = nrm(ks[14], (N_A_LAYERS, 2 * D), 0.01)
    w_kv = nrm(ks[15], (D, 2 * HKV), D ** -0.5)
    b_kv = nrm(ks[16], (2 * HKV,), 0.01)
    w_q = nrm(ks[17], (N_B_LAYERS, D, HQ), D ** -0.5)
    b_q = nrm(ks[18], (N_B_LAYERS, HQ), 0.01)
    sinks = nrm(ks[19], (N_B_LAYERS, N_Q_HEADS), 0.5)
    w_o = nrm(ks[20], (N_B_LAYERS, HQ, D), HQ ** -0.5)
    b_o = nrm(ks[21], (N_B_LAYERS, D), 0.01)
    w_mlp_in = nrm(ks[22], (DEPTH, D, D_FF), D ** -0.5)
    w_mlp_out = nrm(ks[23], (DEPTH, D_FF, D), D_FF ** -0.5)
    return {"x": x, "norm_mix": norm_mix, "norm_mlp": norm_mlp, "norm_kv": norm_kv,
            "norm_final": norm_final, "s5_a_re": s5_a_re, "s5_a_im": s5_a_im,
            "s5_log_dt": s5_log_dt, "s5_b_re": s5_b_re, "s5_b_im": s5_b_im,
            "s5_c_re": s5_c_re, "s5_c_im": s5_c_im, "s5_d": s5_d, "s5_w_glu": s5_w_glu,
            "s5_b_glu": s5_b_glu, "w_kv": w_kv, "b_kv": b_kv, "w_q": w_q, "b_q": b_q,
            "sinks": sinks, "w_o": w_o, "b_o": b_o, "w_mlp_in": w_mlp_in,
            "w_mlp_out": w_mlp_out}


def reference(x, norm_mix, norm_mlp, norm_kv, norm_final, s5_a_re, s5_a_im, s5_log_dt,
              s5_b_re, s5_b_im, s5_c_re, s5_c_im, s5_d, s5_w_glu, s5_b_glu, w_kv, b_kv,
              w_q, b_q, sinks, w_o, b_o, w_mlp_in, w_mlp_out):
    h = x
    k = v = None
    for layer in range(DEPTH):
        if layer < N_A_LAYERS:
            hn = rmsnorm(h, norm_mix[layer])
            h = h + s5_mixer(hn, s5_a_re[layer], s5_a_im[layer], s5_log_dt[layer],
                             s5_b_re[layer], s5_b_im[layer], s5_c_re[layer], s5_c_im[layer],
                             s5_d[layer], s5_w_glu[layer], s5_b_glu[layer])
        else:
            if layer == N_A_LAYERS:
                k, v = shared_kv(h, norm_kv, w_kv, b_kv)
            bi = layer - N_A_LAYERS
            hn = rmsnorm(h, norm_mix[layer])
            h = h + swa_sink_attention(hn, k, v, w_q[bi], b_q[bi], sinks[bi], w_o[bi], b_o[bi])
        h = h + sq_relu_mlp(rmsnorm(h, norm_mlp[layer]), w_mlp_in[layer], w_mlp_out[layer])
    return rmsnorm(h, norm_final)
```

```python
import functools
import math

import jax
import jax.numpy as jnp
from jax import lax
from jax.experimental import pallas as pl
from jax.experimental.pallas import tpu as pltpu

F32 = jnp.float32
BF16 = jnp.bfloat16

D_MODEL = 1024
BATCH = 8
SEQ = 2048
S5_GROUP = 16
S5_GROUPS = D_MODEL // S5_GROUP
S5_STATE = 64
LAMBDA_RE_MAX = -1e-4
HEAD_DIM = 64
N_Q_HEADS = D_MODEL // HEAD_DIM
N_KV_HEADS = 4
Q_PER_KV = N_Q_HEADS // N_KV_HEADS
WINDOW = 128
D_FF = 4 * D_MODEL
NORM_EPS = 1e-5

LANES = 128
SUBLANES = 8
N_LANE_TILES = D_MODEL // LANES
GROUPS_PER_TILE = LANES // S5_GROUP
STATES_PER_TILE = GROUPS_PER_TILE * S5_STATE
MIB = 1024 * 1024

S5_TL = 128
S5_EPI_ROWS = 256
MLP_TM = 512
MLP_FF_CHUNK = 1024
ATTN_TQ = 512


def _rmsnorm(x, g):
    return x * lax.rsqrt(jnp.mean(x * x, axis=-1, keepdims=True) + NORM_EPS) * g


def _gelu_tanh(x):
    c = math.sqrt(2.0 / math.pi)
    return 0.5 * x * (1.0 + jnp.tanh(c * (x + 0.044715 * (x * x * x))))


def _const_spec(shape):
    nd = len(shape)
    return pl.BlockSpec(shape, lambda *_: (0,) * nd, pipeline_mode=pl.Buffered(1))


def _s5_prep_kernel(are_ref, aim_ref, ldt_ref, bre_ref, bim_ref,
                    lbr_ref, lbi_ref, bbr_ref, bbi_ref):
    ar = jnp.minimum(are_ref[...], LAMBDA_RE_MAX)
    ai = aim_ref[...]
    dt = jnp.exp(ldt_ref[...])
    mag = jnp.exp(ar * dt)
    ang = ai * dt
    lr = mag * jnp.cos(ang)
    li = mag * jnp.sin(ang)
    den = ar * ar + ai * ai
    nr = (lr - 1.0) * ar + li * ai
    ni = li * ar - (lr - 1.0) * ai
    cr = nr / den
    ci = ni / den
    br = bre_ref[...]
    bi = bim_ref[...]
    lbr_ref[...] = lr
    lbi_ref[...] = li
    bbr_ref[...] = cr * br - ci * bi
    bbi_ref[...] = cr * bi + ci * br


def _s5_prep(a_re, a_im, log_dt, b_re, b_im):
    g, p, c = b_re.shape
    rep = lambda a: jnp.repeat(a, c, axis=-1)
    ldt = jnp.broadcast_to(log_dt[:, None], (g, p * c))
    shp = jax.ShapeDtypeStruct((g, p * c), F32)
    lbr, lbi, bbr, bbi = pl.pallas_call(
        _s5_prep_kernel, out_shape=(shp, shp, shp, shp), name="s5_prep",
    )(rep(a_re), rep(a_im), ldt, b_re.reshape(g, p * c), b_im.reshape(g, p * c))
    pick = lambda a: a.reshape(g, p, c)[:, :, 0]
    return pick(lbr), pick(lbi), bbr.reshape(g, p, c), bbi.reshape(g, p, c)


def _blockdiag_in(b_bar_re, b_bar_im):
    eye = jnp.eye(GROUPS_PER_TILE, dtype=F32)

    def one(b):
        b = b.reshape(N_LANE_TILES, GROUPS_PER_TILE, S5_STATE, S5_GROUP)
        return jnp.einsum('jgpc,gh->jgchp', b, eye).reshape(N_LANE_TILES, LANES, STATES_PER_TILE)

    return jnp.concatenate([one(b_bar_re), one(b_bar_im)], axis=-1)


def _blockdiag_out(c_re, c_im):
    eye = jnp.eye(GROUPS_PER_TILE, dtype=F32)

    def one(c):
        c = c.reshape(N_LANE_TILES, GROUPS_PER_TILE, S5_GROUP, S5_STATE)
        return jnp.einsum('jgcp,gh->jgphc', c, eye).reshape(N_LANE_TILES, STATES_PER_TILE, LANES)

    return jnp.concatenate([one(c_re), -one(c_im)], axis=1)


def _s5_kernel(x_ref, g_ref, wbu_ref, ar_ref, ai_ref, wc_ref, d_ref, wglu_ref, bglu_ref,
               o_ref, hn_s, hnb_s, bu_s, y_s, st_s, res_s, *, tl):
    rows = tl * SUBLANES
    sp = STATES_PER_TILE

    @pl.when(pl.program_id(0) == 0)
    def _():
        st_s[...] = jnp.zeros_like(st_s)

    hn = _rmsnorm(x_ref[...].reshape(rows, D_MODEL), g_ref[...])
    hn_s[...] = hn
    hnb_s[...] = hn.astype(BF16)

    for j in range(N_LANE_TILES):
        lanes = slice(j * LANES, (j + 1) * LANES)
        bu_s[...] = jnp.dot(hnb_s[:, lanes], wbu_ref[j], preferred_element_type=F32)
        ar = ar_ref[j]
        ai = ai_ref[j]

        def step(t, carry):
            xr, xi = carry
            r = pl.multiple_of(t * SUBLANES, SUBLANES)
            nxr = ar * xr - ai * xi + bu_s[pl.ds(r, SUBLANES), 0:sp]
            nxi = ar * xi + ai * xr + bu_s[pl.ds(r, SUBLANES), sp:2 * sp]
            bu_s[pl.ds(r, SUBLANES), 0:sp] = nxr
            bu_s[pl.ds(r, SUBLANES), sp:2 * sp] = nxi
            return nxr, nxi

        xr, xi = lax.fori_loop(0, tl, step, (st_s[j, :, 0:sp], st_s[j, :, sp:2 * sp]), unroll=8)
        st_s[j, :, 0:sp] = xr
        st_s[j, :, sp:2 * sp] = xi
        y_s[:, lanes] = jnp.dot(bu_s[...].astype(BF16), wc_ref[j], preferred_element_type=F32)

    def epilogue(c, _):
        r = pl.multiple_of(c * S5_EPI_ROWS, S5_EPI_ROWS)
        rs = pl.ds(r, S5_EPI_ROWS)
        y = y_s[rs, :] + d_ref[...] * hn_s[rs, :]
        z = jnp.dot(_gelu_tanh(y).astype(BF16), wglu_ref[...],
                    preferred_element_type=F32) + bglu_ref[...]
        val = z[:, :D_MODEL]
        gate = z[:, D_MODEL:]
        steps = S5_EPI_ROWS // SUBLANES
        xin = x_ref[pl.ds(pl.multiple_of(c * steps, steps), steps)].reshape(S5_EPI_ROWS, D_MODEL)
        out = xin + val * (1.0 / (1.0 + jnp.exp(-gate)))
        for j in range(N_LANE_TILES):
            res_s[j, rs, :] = out[:, j * LANES:(j + 1) * LANES]
        return 0

    lax.fori_loop(0, rows // S5_EPI_ROWS, epilogue, 0)

    for b in range(SUBLANES):
        for j in range(N_LANE_TILES):
            o_ref[b, :, j * LANES:(j + 1) * LANES] = res_s[j, pl.ds(b, tl, stride=SUBLANES), :]


def _s5_layer(xt, g_mix, wbu, a_re_t, a_im_t, wc, d_skip, w_glu, b_glu):
    seq, bsz, d = xt.shape
    tl = S5_TL
    rows = tl * bsz
    kernel = functools.partial(_s5_kernel, tl=tl)
    return pl.pallas_call(
        kernel,
        out_shape=jax.ShapeDtypeStruct((bsz, seq, d), F32),
        grid=(seq // tl,),
        in_specs=[
            pl.BlockSpec((tl, bsz, d), lambda i: (i, 0, 0)),
            _const_spec((1, d)),
            _const_spec(wbu.shape),
            _const_spec(a_re_t.shape),
            _const_spec(a_im_t.shape),
            _const_spec(wc.shape),
            _const_spec((1, d)),
            _const_spec(w_glu.shape),
            _const_spec((1, 2 * d)),
        ],
        out_specs=pl.BlockSpec((bsz, tl, d), lambda i: (0, i, 0)),
        scratch_shapes=[
            pltpu.VMEM((rows, d), F32),
            pltpu.VMEM((rows, d), BF16),
            pltpu.VMEM((rows, 2 * STATES_PER_TILE), F32),
            pltpu.VMEM((rows, d), F32),
            pltpu.VMEM((N_LANE_TILES, SUBLANES, 2 * STATES_PER_TILE), F32),
            pltpu.VMEM((N_LANE_TILES, rows, LANES), F32),
        ],
        compiler_params=pltpu.CompilerParams(
            dimension_semantics=("arbitrary",), vmem_limit_bytes=56 * MIB),
        name="s5_layer",
    )(xt, g_mix.reshape(1, d), wbu, a_re_t, a_im_t, wc, d_skip.reshape(1, d), w_glu,
      b_glu.reshape(1, 2 * d))


def _mlp_body(h, g, win_ref, wout_ref):
    hn = _rmsnorm(h, g).astype(BF16)
    acc = jnp.zeros(h.shape, F32)
    for c in range(D_FF // MLP_FF_CHUNK):
        cols = slice(c * MLP_FF_CHUNK, (c + 1) * MLP_FF_CHUNK)
        a = jnp.dot(hn, win_ref[:, cols], preferred_element_type=F32)
        a = jnp.square(jnp.maximum(a, 0.0)).astype(BF16)
        acc = acc + jnp.dot(a, wout_ref[cols, :], preferred_element_type=F32)
    return h + acc


def _mlp_kv_kernel(h_ref, g_ref, win_ref, wout_ref, gkv_ref, wkv_ref, bkv_ref, o_ref, kv_ref):
    out = _mlp_body(h_ref[...], g_ref[...], win_ref, wout_ref)
    o_ref[...] = out
    kvn = _rmsnorm(out, gkv_ref[...]).astype(BF16)
    kv = jnp.dot(kvn, wkv_ref[...], preferred_element_type=F32) + bkv_ref[...]
    kv_ref[...] = kv.astype(BF16)


def _mlp_final_kernel(h_ref, g_ref, win_ref, wout_ref, gfin_ref, o_ref):
    out = _mlp_body(h_ref[...], g_ref[...], win_ref, wout_ref)
    o_ref[...] = _rmsnorm(out, gfin_ref[...])


def _mlp_kv(h, g, w_in, w_out, g_kv, w_kv, b_kv):
    t, d = h.shape
    nkv = w_kv.shape[1]
    row = lambda n: pl.BlockSpec((MLP_TM, n), lambda i: (i, 0))
    return pl.pallas_call(
        _mlp_kv_kernel,
        out_shape=(jax.ShapeDtypeStruct((t, d), F32), jax.ShapeDtypeStruct((t, nkv), BF16)),
        grid=(t // MLP_TM,),
        in_specs=[row(d), _const_spec((1, d)), _const_spec(w_in.shape), _const_spec(w_out.shape),
                  _const_spec((1, d)), _const_spec(w_kv.shape), _const_spec((1, nkv))],
        out_specs=(row(d), row(nkv)),
        compiler_params=pltpu.CompilerParams(
            dimension_semantics=("parallel",), vmem_limit_bytes=56 * MIB),
        name="mlp_kv",
    )(h, g.reshape(1, d), w_in, w_out, g_kv.reshape(1, d), w_kv, b_kv.reshape(1, nkv))


def _mlp_final(h, g, w_in, w_out, g_fin):
    t, d = h.shape
    row = lambda n: pl.BlockSpec((MLP_TM, n), lambda i: (i, 0))
    return pl.pallas_call(
        _mlp_final_kernel,
        out_shape=jax.ShapeDtypeStruct((t, d), F32),
        grid=(t // MLP_TM,),
        in_specs=[row(d), _const_spec((1, d)), _const_spec(w_in.shape), _const_spec(w_out.shape),
                  _const_spec((1, d))],
        out_specs=row(d),
        compiler_params=pltpu.CompilerParams(
            dimension_semantics=("parallel",), vmem_limit_bytes=56 * MIB),
        name="mlp_final",
    )(h, g.reshape(1, d), w_in, w_out, g_fin.reshape(1, d))


def _attn_kernel(sink_ref, h_ref, kvp_ref, kvc_ref, g_ref, wq_ref, bq_ref, wo_ref, bo_ref,
                 o_ref, q_s, oh_s, *, tq):
    nsb = tq // WINDOW
    n = pl.program_id(1)
    h = h_ref[0]
    hn = _rmsnorm(h, g_ref[...]).astype(BF16)
    q = jnp.dot(hn, wq_ref[...], preferred_element_type=F32) + bq_ref[...]
    q_s[...] = (q * (1.0 / math.sqrt(HEAD_DIM))).astype(BF16)

    lo = lax.broadcasted_iota(jnp.int32, (WINDOW, LANES), 1) < HEAD_DIM
    qi = lax.broadcasted_iota(jnp.int32, (WINDOW, 2 * WINDOW), 0)
    kj = lax.broadcasted_iota(jnp.int32, (WINDOW, 2 * WINDOW), 1)
    diff = qi + WINDOW - kj
    band = (diff >= 0) & (diff < WINDOW)
    kv_off = N_KV_HEADS * LANES
    zero = jnp.zeros((WINDOW, LANES), BF16)

    for sb in range(nsb):
        rows = slice(sb * WINDOW, (sb + 1) * WINDOW)
        first = (n * nsb + sb) == 0
        valid = band & jnp.logical_or(jnp.logical_not(first), kj >= WINDOW)
        bias = jnp.where(valid, 0.0, -jnp.inf).astype(F32)
        for kh in range(N_KV_HEADS):
            kl = slice(kh * LANES, (kh + 1) * LANES)
            vl = slice(kv_off + kh * LANES, kv_off + (kh + 1) * LANES)
            if sb == 0:
                kprev, vprev = kvp_ref[0, :, kl], kvp_ref[0, :, vl]
            else:
                prow = slice((sb - 1) * WINDOW, sb * WINDOW)
                kprev, vprev = kvc_ref[0, prow, kl], kvc_ref[0, prow, vl]
            kd = jnp.concatenate([kprev, kvc_ref[0, rows, kl]], axis=0)
            vd = jnp.concatenate([vprev, kvc_ref[0, rows, vl]], axis=0)
            q01 = q_s[rows, 2 * kh * LANES:(2 * kh + 1) * LANES]
            q23 = q_s[rows, (2 * kh + 1) * LANES:(2 * kh + 2) * LANES]
            lhs = jnp.concatenate([jnp.where(lo, q01, zero), jnp.where(lo, zero, q01),
                                   jnp.where(lo, q23, zero), jnp.where(lo, zero, q23)], axis=0)
            s = lax.dot_general(lhs, kd, (((1,), (1,)), ((), ())),
                                preferred_element_type=F32)
            ps, rs = [], []
            for gq in range(Q_PER_KV):
                sg = s[gq * WINDOW:(gq + 1) * WINDOW] + bias
                sink = sink_ref[kh * Q_PER_KV + gq]
                m = jnp.maximum(jnp.max(sg, axis=-1, keepdims=True), sink)
                p = jnp.exp(sg - m)
                den = jnp.sum(p, axis=-1, keepdims=True) + jnp.exp(sink - m)
                ps.append(p.astype(BF16))
                rs.append(1.0 / den)
            od = jnp.dot(jnp.concatenate(ps, axis=0), vd, preferred_element_type=F32)
            og = [od[gq * WINDOW:(gq + 1) * WINDOW] * rs[gq] for gq in range(Q_PER_KV)]
            oh_s[rows, 2 * kh * LANES:(2 * kh + 1) * LANES] = jnp.where(lo, og[0], og[1]).astype(BF16)
            oh_s[rows, (2 * kh + 1) * LANES:(2 * kh + 2) * LANES] = jnp.where(lo, og[2], og[3]).astype(BF16)

    o_ref[0] = h + jnp.dot(oh_s[...], wo_ref[...], preferred_element_type=F32) + bo_ref[...]


def _attn_layer(h, kv, sinks, g, w_q, b_q, w_o, b_o):
    bsz, seq, d = h.shape
    nkv = kv.shape[-1]
    tq = ATTN_TQ
    nsb = tq // WINDOW
    kernel = functools.partial(_attn_kernel, tq=tq)
    return pl.pallas_call(
        kernel,
        out_shape=jax.ShapeDtypeStruct((bsz, seq, d), F32),
        grid=(bsz, seq // tq),
        in_specs=[
            pl.BlockSpec(memory_space=pltpu.SMEM),
            pl.BlockSpec((1, tq, d), lambda b, n: (b, n, 0)),
            pl.BlockSpec((1, WINDOW, nkv), lambda b, n: (b, jnp.maximum(n * nsb - 1, 0), 0)),
            pl.BlockSpec((1, tq, nkv), lambda b, n: (b, n, 0)),
            _const_spec((1, d)),
            _const_spec(w_q.shape),
            _const_spec((1, d)),
            _const_spec(w_o.shape),
            _const_spec((1, d)),
        ],
        out_specs=pl.BlockSpec((1, tq, d), lambda b, n: (b, n, 0)),
        scratch_shapes=[pltpu.VMEM((tq, d), BF16), pltpu.VMEM((tq, d), BF16)],
        compiler_params=pltpu.CompilerParams(
            dimension_semantics=("parallel", "parallel"), vmem_limit_bytes=40 * MIB),
        name="attn",
    )(sinks, h, kv, kv, g.reshape(1, d), w_q, b_q.reshape(1, d), w_o, b_o.reshape(1, d))


def _dup_heads(w):
    lead = w.shape[:-1]
    w = w.reshape(lead + (N_KV_HEADS, 1, HEAD_DIM))
    return jnp.broadcast_to(w, lead + (N_KV_HEADS, 2, HEAD_DIM)).reshape(lead + (N_KV_HEADS * LANES,))


def kernel(x, norm_mix, norm_mlp, norm_kv, norm_final, s5_a_re, s5_a_im, s5_log_dt, s5_b_re, s5_b_im, s5_c_re, s5_c_im, s5_d, s5_w_glu, s5_b_glu, w_kv, b_kv, w_q, b_q, sinks, w_o, b_o, w_mlp_in, w_mlp_out):
    bsz, seq, d = x.shape
    hkv = N_KV_HEADS * HEAD_DIM

    lam_re, lam_im, bb_re, bb_im = _s5_prep(s5_a_re[0], s5_a_im[0], s5_log_dt[0],
                                            s5_b_re[0], s5_b_im[0])
    wbu = _blockdiag_in(bb_re, bb_im).astype(BF16)
    wc = _blockdiag_out(s5_c_re[0], s5_c_im[0]).astype(BF16)
    tile_bcast = lambda a: jnp.broadcast_to(
        a.reshape(N_LANE_TILES, 1, STATES_PER_TILE), (N_LANE_TILES, SUBLANES, STATES_PER_TILE))
    xt = jnp.transpose(x, (1, 0, 2))
    h = _s5_layer(xt, norm_mix[0], wbu, tile_bcast(lam_re), tile_bcast(lam_im), wc,
                  s5_d[0], s5_w_glu[0].astype(BF16), s5_b_glu[0])

    w_kv_d = jnp.concatenate([_dup_heads(w_kv[:, :hkv]), _dup_heads(w_kv[:, hkv:])], axis=-1)
    b_kv_d = jnp.concatenate([_dup_heads(b_kv[:hkv]), _dup_heads(b_kv[hkv:])], axis=-1)
    h, kv = _mlp_kv(h.reshape(bsz * seq, d), norm_mlp[0], w_mlp_in[0].astype(BF16),
                    w_mlp_out[0].astype(BF16), norm_kv, w_kv_d.astype(BF16), b_kv_d)

    h = _attn_layer(h.reshape(bsz, seq, d), kv.reshape(bsz, seq, -1), sinks[0], norm_mix[1],
                    w_q[0].astype(BF16), b_q[0], w_o[0].astype(BF16), b_o[0])

    out = _mlp_final(h.reshape(bsz * seq, d), norm_mlp[1], w_mlp_in[1].astype(BF16),
                     w_mlp_out[1].astype(BF16), norm_final)
    return out.reshape(bsz, seq, d)
```

```python
import functools
import math

import jax
import jax.numpy as jnp
from jax import lax
from jax.experimental import pallas as pl
from jax.experimental.pallas import tpu as pltpu

F32 = jnp.float32
BF16 = jnp.bfloat16

D_MODEL = 1024
BATCH = 8
SEQ = 2048
S5_GROUP = 16
S5_GROUPS = D_MODEL // S5_GROUP
S5_STATE = 64
LAMBDA_RE_MAX = -1e-4
HEAD_DIM = 64
N_Q_HEADS = D_MODEL // HEAD_DIM
N_KV_HEADS = 4
Q_PER_KV = N_Q_HEADS // N_KV_HEADS
WINDOW = 128
D_FF = 4 * D_MODEL
NORM_EPS = 1e-5

LANES = 128
SUBLANES = 8
N_LANE_TILES = D_MODEL // LANES
GROUPS_PER_TILE = LANES // S5_GROUP
STATES_PER_TILE = GROUPS_PER_TILE * S5_STATE
MIB = 1024 * 1024

S5_TL = 64
S5_EPI_ROWS = 256
MLP_TM = 512
MLP_FF_CHUNK = 1024
ATTN_TQ = 512


def _rmsnorm(x, g):
    return x * lax.rsqrt(jnp.mean(x * x, axis=-1, keepdims=True) + NORM_EPS) * g


def _gelu_tanh(x):
    c = math.sqrt(2.0 / math.pi)
    return 0.5 * x * (1.0 + jnp.tanh(c * (x + 0.044715 * (x * x * x))))


def _const_spec(shape):
    nd = len(shape)
    return pl.BlockSpec(shape, lambda *_: (0,) * nd, pipeline_mode=pl.Buffered(1))


def _s5_prep_kernel(are_ref, aim_ref, ldt_ref, bre_ref, bim_ref,
                    lbr_ref, lbi_ref, bbr_ref, bbi_ref):
    ar = jnp.minimum(are_ref[...], LAMBDA_RE_MAX)
    ai = aim_ref[...]
    dt = jnp.exp(ldt_ref[...])
    mag = jnp.exp(ar * dt)
    ang = ai * dt
    lr = mag * jnp.cos(ang)
    li = mag * jnp.sin(ang)
    den = ar * ar + ai * ai
    nr = (lr - 1.0) * ar + li * ai
    ni = li * ar - (lr - 1.0) * ai
    cr = nr / den
    ci = ni / den
    br = bre_ref[...]
    bi = bim_ref[...]
    lbr_ref[...] = lr
    lbi_ref[...] = li
    bbr_ref[...] = cr * br - ci * bi
    bbi_ref[...] = cr * bi + ci * br


def _s5_prep(a_re, a_im, log_dt, b_re, b_im):
    g, p, c = b_re.shape
    rep = lambda a: jnp.repeat(a, c, axis=-1)
    ldt = jnp.broadcast_to(log_dt[:, None], (g, p * c))
    shp = jax.ShapeDtypeStruct((g, p * c), F32)
    lbr, lbi, bbr, bbi = pl.pallas_call(
        _s5_prep_kernel, out_shape=(shp, shp, shp, shp), name="s5_prep",
    )(rep(a_re), rep(a_im), ldt, b_re.reshape(g, p * c), b_im.reshape(g, p * c))
    pick = lambda a: a.reshape(g, p, c)[:, :, 0]
    return pick(lbr), pick(lbi), bbr.reshape(g, p, c), bbi.reshape(g, p, c)


def _blockdiag_in(b_bar_re, b_bar_im):
    eye = jnp.eye(GROUPS_PER_TILE, dtype=F32)

    def one(b):
        b = b.reshape(N_LANE_TILES, GROUPS_PER_TILE, S5_STATE, S5_GROUP)
        return jnp.einsum('jgpc,gh->jgchp', b, eye).reshape(N_LANE_TILES, LANES, STATES_PER_TILE)

    return jnp.concatenate([one(b_bar_re), one(b_bar_im)], axis=-1)


def _blockdiag_out(c_re, c_im):
    eye = jnp.eye(GROUPS_PER_TILE, dtype=F32)

    def one(c):
        c = c.reshape(N_LANE_TILES, GROUPS_PER_TILE, S5_GROUP, S5_STATE)
        return jnp.einsum('jgcp,gh->jgphc', c, eye).reshape(N_LANE_TILES, STATES_PER_TILE, LANES)

    return jnp.concatenate([one(c_re), -one(c_im)], axis=1)


def _s5_kernel(x_ref, g_ref, wbu_ref, ar_ref, ai_ref, wc_ref, d_ref, wglu_ref, bglu_ref,
               o_ref, hbm_s, hn_s, hnb_s, bu_s, xs_s, y_s, st_s, res_s, *, tl):
    rows = tl * SUBLANES
    sp = STATES_PER_TILE
    pitch = tl + SUBLANES

    @pl.when(pl.program_id(0) == 0)
    def _():
        st_s[...] = jnp.zeros_like(st_s)

    for b in range(SUBLANES):
        hn = _rmsnorm(x_ref[b], g_ref[...])
        for j in range(N_LANE_TILES):
            hbm_s[j, b * pitch:b * pitch + tl, :] = hn[:, j * LANES:(j + 1) * LANES]

    def to_time_major(t, _):
        r = pl.multiple_of(t * SUBLANES, SUBLANES)
        for j in range(N_LANE_TILES):
            hn_s[pl.ds(r, SUBLANES), j * LANES:(j + 1) * LANES] = (
                hbm_s[j, pl.ds(t, SUBLANES, stride=pitch), :])
        return 0

    lax.fori_loop(0, tl, to_time_major, 0, unroll=8)
    hnb_s[...] = hn_s[...].astype(BF16)

    def project_in(j):
        bu_s[j % 2] = jnp.dot(hnb_s[:, j * LANES:(j + 1) * LANES], wbu_ref[j],
                              preferred_element_type=F32)

    project_in(0)
    for j in range(N_LANE_TILES):
        if j + 1 < N_LANE_TILES:
            project_in(j + 1)
        p = j % 2
        ar = ar_ref[j]
        ai = ai_ref[j]
        xr = st_s[j, :, 0:sp]
        xi = st_s[j, :, sp:2 * sp]
        for t in range(tl):
            r = slice(t * SUBLANES, (t + 1) * SUBLANES)
            nxr = ar * xr - ai * xi + bu_s[p, r, 0:sp]
            nxi = ar * xi + ai * xr + bu_s[p, r, sp:2 * sp]
            xs_s[p, r, 0:sp] = nxr
            xs_s[p, r, sp:2 * sp] = nxi
            xr, xi = nxr, nxi
        st_s[j, :, 0:sp] = xr
        st_s[j, :, sp:2 * sp] = xi
        y_s[:, j * LANES:(j + 1) * LANES] = jnp.dot(xs_s[p].astype(BF16), wc_ref[j],
                                                     preferred_element_type=F32)

    def epilogue(c, _):
        rs = pl.ds(pl.multiple_of(c * S5_EPI_ROWS, S5_EPI_ROWS), S5_EPI_ROWS)
        y = y_s[rs, :] + d_ref[...] * hn_s[rs, :]
        z = jnp.dot(_gelu_tanh(y).astype(BF16), wglu_ref[...],
                    preferred_element_type=F32) + bglu_ref[...]
        mix = z[:, :D_MODEL] * (1.0 / (1.0 + jnp.exp(-z[:, D_MODEL:])))
        for j in range(N_LANE_TILES):
            res_s[j, rs, :] = mix[:, j * LANES:(j + 1) * LANES]
        return 0

    lax.fori_loop(0, rows // S5_EPI_ROWS, epilogue, 0)

    for b in range(SUBLANES):
        for j in range(N_LANE_TILES):
            lanes = slice(j * LANES, (j + 1) * LANES)
            o_ref[b, :, lanes] = x_ref[b, :, lanes] + res_s[j, pl.ds(b, tl, stride=SUBLANES), :]


def _s5_layer(x, g_mix, wbu, a_re_t, a_im_t, wc, d_skip, w_glu, b_glu):
    bsz, seq, d = x.shape
    tl = S5_TL
    rows = tl * bsz
    kernel = functools.partial(_s5_kernel, tl=tl)
    return pl.pallas_call(
        kernel,
        out_shape=jax.ShapeDtypeStruct((bsz, seq, d), F32),
        grid=(seq // tl,),
        in_specs=[
            pl.BlockSpec((bsz, tl, d), lambda i: (0, i, 0)),
            _const_spec((1, d)),
            _const_spec(wbu.shape),
            _const_spec(a_re_t.shape),
            _const_spec(a_im_t.shape),
            _const_spec(wc.shape),
            _const_spec((1, d)),
            _const_spec(w_glu.shape),
            _const_spec((1, 2 * d)),
        ],
        out_specs=pl.BlockSpec((bsz, tl, d), lambda i: (0, i, 0)),
        scratch_shapes=[
            pltpu.VMEM((N_LANE_TILES, bsz * (tl + SUBLANES), LANES), F32),
            pltpu.VMEM((rows, d), F32),
            pltpu.VMEM((rows, d), BF16),
            pltpu.VMEM((2, rows, 2 * STATES_PER_TILE), F32),
            pltpu.VMEM((2, rows, 2 * STATES_PER_TILE), F32),
            pltpu.VMEM((rows, d), F32),
            pltpu.VMEM((N_LANE_TILES, SUBLANES, 2 * STATES_PER_TILE), F32),
            pltpu.VMEM((N_LANE_TILES, rows, LANES), F32),
        ],
        compiler_params=pltpu.CompilerParams(
            dimension_semantics=("arbitrary",), vmem_limit_bytes=56 * MIB),
        name="s5_layer",
    )(x, g_mix.reshape(1, d), wbu, a_re_t, a_im_t, wc, d_skip.reshape(1, d), w_glu,
      b_glu.reshape(1, 2 * d))


def _mlp_body(h, g, win_ref, wout_ref):
    hn = _rmsnorm(h, g).astype(BF16)
    acc = jnp.zeros(h.shape, F32)
    for c in range(D_FF // MLP_FF_CHUNK):
        cols = slice(c * MLP_FF_CHUNK, (c + 1) * MLP_FF_CHUNK)
        a = jnp.dot(hn, win_ref[:, cols], preferred_element_type=F32)
        a = jnp.square(jnp.maximum(a, 0.0)).astype(BF16)
        acc = acc + jnp.dot(a, wout_ref[cols, :], preferred_element_type=F32)
    return h + acc


def _mlp_kv_kernel(h_ref, g_ref, win_ref, wout_ref, gkv_ref, wkv_ref, bkv_ref, o_ref, kv_ref):
    out = _mlp_body(h_ref[...], g_ref[...], win_ref, wout_ref)
    o_ref[...] = out
    kvn = _rmsnorm(out, gkv_ref[...]).astype(BF16)
    kv = jnp.dot(kvn, wkv_ref[...], preferred_element_type=F32) + bkv_ref[...]
    kv_ref[...] = kv.astype(BF16)


def _mlp_final_kernel(h_ref, g_ref, win_ref, wout_ref, gfin_ref, o_ref):
    out = _mlp_body(h_ref[...], g_ref[...], win_ref, wout_ref)
    o_ref[...] = _rmsnorm(out, gfin_ref[...])


def _mlp_kv(h, g, w_in, w_out, g_kv, w_kv, b_kv):
    t, d = h.shape
    nkv = w_kv.shape[1]
    row = lambda n: pl.BlockSpec((MLP_TM, n), lambda i: (i, 0))
    return pl.pallas_call(
        _mlp_kv_kernel,
        out_shape=(jax.ShapeDtypeStruct((t, d), F32), jax.ShapeDtypeStruct((t, nkv), BF16)),
        grid=(t // MLP_TM,),
        in_specs=[row(d), _const_spec((1, d)), _const_spec(w_in.shape), _const_spec(w_out.shape),
                  _const_spec((1, d)), _const_spec(w_kv.shape), _const_spec((1, nkv))],
        out_specs=(row(d), row(nkv)),
        compiler_params=pltpu.CompilerParams(
            dimension_semantics=("parallel",), vmem_limit_bytes=56 * MIB),
        name="mlp_kv",
    )(h, g.reshape(1, d), w_in, w_out, g_kv.reshape(1, d), w_kv, b_kv.reshape(1, nkv))


def _mlp_final(h, g, w_in, w_out, g_fin):
    t, d = h.shape
    row = lambda n: pl.BlockSpec((MLP_TM, n), lambda i: (i, 0))
    return pl.pallas_call(
        _mlp_final_kernel,
        out_shape=jax.ShapeDtypeStruct((t, d), F32),
        grid=(t // MLP_TM,),
        in_specs=[row(d), _const_spec((1, d)), _const_spec(w_in.shape), _const_spec(w_out.shape),
                  _const_spec((1, d))],
        out_specs=row(d),
        compiler_params=pltpu.CompilerParams(
            dimension_semantics=("parallel",), vmem_limit_bytes=56 * MIB),
        name="mlp_final",
    )(h, g.reshape(1, d), w_in, w_out, g_fin.reshape(1, d))


def _attn_kernel(sink_ref, h_ref, kvp_ref, kvc_ref, g_ref, wq_ref, bq_ref, wo_ref, bo_ref,
                 o_ref, q_s, oh_s, *, tq):
    nsb = tq // WINDOW
    n = pl.program_id(1)
    h = h_ref[0]
    hn = _rmsnorm(h, g_ref[...]).astype(BF16)
    q = jnp.dot(hn, wq_ref[...], preferred_element_type=F32) + bq_ref[...]
    q_s[...] = (q * (1.0 / math.sqrt(HEAD_DIM))).astype(BF16)

    lo = lax.broadcasted_iota(jnp.int32, (WINDOW, LANES), 1) < HEAD_DIM
    qi = lax.broadcasted_iota(jnp.int32, (WINDOW, 2 * WINDOW), 0)
    kj = lax.broadcasted_iota(jnp.int32, (WINDOW, 2 * WINDOW), 1)
    diff = qi + WINDOW - kj
    band = (diff >= 0) & (diff < WINDOW)
    kv_off = N_KV_HEADS * LANES
    zero = jnp.zeros((WINDOW, LANES), BF16)

    for sb in range(nsb):
        rows = slice(sb * WINDOW, (sb + 1) * WINDOW)
        first = (n * nsb + sb) == 0
        valid = band & jnp.logical_or(jnp.logical_not(first), kj >= WINDOW)
        bias = jnp.where(valid, 0.0, -jnp.inf).astype(F32)
        for kh in range(N_KV_HEADS):
            kl = slice(kh * LANES, (kh + 1) * LANES)
            vl = slice(kv_off + kh * LANES, kv_off + (kh + 1) * LANES)
            if sb == 0:
                kprev, vprev = kvp_ref[0, :, kl], kvp_ref[0, :, vl]
            else:
                prow = slice((sb - 1) * WINDOW, sb * WINDOW)
                kprev, vprev = kvc_ref[0, prow, kl], kvc_ref[0, prow, vl]
            kd = jnp.concatenate([kprev, kvc_ref[0, rows, kl]], axis=0)
            vd = jnp.concatenate([vprev, kvc_ref[0, rows, vl]], axis=0)
            q01 = q_s[rows, 2 * kh * LANES:(2 * kh + 1) * LANES]
            q23 = q_s[rows, (2 * kh + 1) * LANES:(2 * kh + 2) * LANES]
            lhs = jnp.concatenate([jnp.where(lo, q01, zero), jnp.where(lo, zero, q01),
                                   jnp.where(lo, q23, zero), jnp.where(lo, zero, q23)], axis=0)
            s = lax.dot_general(lhs, kd, (((1,), (1,)), ((), ())),
                                preferred_element_type=F32)
            ps, rs = [], []
            for gq in range(Q_PER_KV):
                sg = s[gq * WINDOW:(gq + 1) * WINDOW] + bias
                sink = sink_ref[kh * Q_PER_KV + gq]
                m = jnp.maximum(jnp.max(sg, axis=-1, keepdims=True), sink)
                p = jnp.exp(sg - m)
                den = jnp.sum(p, axis=-1, keepdims=True) + jnp.exp(sink - m)
                ps.append(p.astype(BF16))
                rs.append(1.0 / den)
            od = jnp.dot(jnp.concatenate(ps, axis=0), vd, preferred_element_type=F32)
            og = [od[gq * WINDOW:(gq + 1) * WINDOW] * rs[gq] for gq in range(Q_PER_KV)]
            oh_s[rows, 2 * kh * LANES:(2 * kh + 1) * LANES] = jnp.where(lo, og[0], og[1]).astype(BF16)
            oh_s[rows, (2 * kh + 1) * LANES:(2 * kh + 2) * LANES] = jnp.where(lo, og[2], og[3]).astype(BF16)

    o_ref[0] = h + jnp.dot(oh_s[...], wo_ref[...], preferred_element_type=F32) + bo_ref[...]


def _attn_layer(h, kv, sinks, g, w_q, b_q, w_o, b_o):
    bsz, seq, d = h.shape
    nkv = kv.shape[-1]
    tq = ATTN_TQ
    nsb = tq // WINDOW
    kernel = functools.partial(_attn_kernel, tq=tq)
    return pl.pallas_call(
        kernel,
        out_shape=jax.ShapeDtypeStruct((bsz, seq, d), F32),
        grid=(bsz, seq // tq),
        in_specs=[
            pl.BlockSpec(memory_space=pltpu.SMEM),
            pl.BlockSpec((1, tq, d), lambda b, n: (b, n, 0)),
            pl.BlockSpec((1, WINDOW, nkv), lambda b, n: (b, jnp.maximum(n * nsb - 1, 0), 0)),
            pl.BlockSpec((1, tq, nkv), lambda b, n: (b, n, 0)),
            _const_spec((1, d)),
            _const_spec(w_q.shape),
            _const_spec((1, d)),
            _const_spec(w_o.shape),
            _const_spec((1, d)),
        ],
        out_specs=pl.BlockSpec((1, tq, d), lambda b, n: (b, n, 0)),
        scratch_shapes=[pltpu.VMEM((tq, d), BF16), pltpu.VMEM((tq, d), BF16)],
        compiler_params=pltpu.CompilerParams(
            dimension_semantics=("parallel", "parallel"), vmem_limit_bytes=40 * MIB),
        name="attn",
    )(sinks, h, kv, kv, g.reshape(1, d), w_q, b_q.reshape(1, d), w_o, b_o.reshape(1, d))


def _dup_heads(w):
    lead = w.shape[:-1]
    w = w.reshape(lead + (N_KV_HEADS, 1, HEAD_DIM))
    return jnp.broadcast_to(w, lead + (N_KV_HEADS, 2, HEAD_DIM)).reshape(lead + (N_KV_HEADS * LANES,))


def kernel(x, norm_mix, norm_mlp, norm_kv, norm_final, s5_a_re, s5_a_im, s5_log_dt, s5_b_re, s5_b_im, s5_c_re, s5_c_im, s5_d, s5_w_glu, s5_b_glu, w_kv, b_kv, w_q, b_q, sinks, w_o, b_o, w_mlp_in, w_mlp_out):
    bsz, seq, d = x.shape
    hkv = N_KV_HEADS * HEAD_DIM

    lam_re, lam_im, bb_re, bb_im = _s5_prep(s5_a_re[0], s5_a_im[0], s5_log_dt[0],
                                            s5_b_re[0], s5_b_im[0])
    wbu = _blockdiag_in(bb_re, bb_im).astype(BF16)
    wc = _blockdiag_out(s5_c_re[0], s5_c_im[0]).astype(BF16)
    tile_bcast = lambda a: jnp.broadcast_to(
        a.reshape(N_LANE_TILES, 1, STATES_PER_TILE), (N_LANE_TILES, SUBLANES, STATES_PER_TILE))
    h = _s5_layer(x, norm_mix[0], wbu, tile_bcast(lam_re), tile_bcast(lam_im), wc,
                  s5_d[0], s5_w_glu[0].astype(BF16), s5_b_glu[0])

    w_kv_d = jnp.concatenate([_dup_heads(w_kv[:, :hkv]), _dup_heads(w_kv[:, hkv:])], axis=-1)
    b_kv_d = jnp.concatenate([_dup_heads(b_kv[:hkv]), _dup_heads(b_kv[hkv:])], axis=-1)
    h, kv = _mlp_kv(h.reshape(bsz * seq, d), norm_mlp[0], w_mlp_in[0].astype(BF16),
                    w_mlp_out[0].astype(BF16), norm_kv, w_kv_d.astype(BF16), b_kv_d)

    h = _attn_layer(h.reshape(bsz, seq, d), kv.reshape(bsz, seq, -1), sinks[0], norm_mix[1],
                    w_q[0].astype(BF16), b_q[0], w_o[0].astype(BF16), b_o[0])

    out = _mlp_final(h.reshape(bsz * seq, d), norm_mlp[1], w_mlp_in[1].astype(BF16),
                     w_mlp_out[1].astype(BF16), norm_final)
    return out.reshape(bsz, seq, d)
```

```python
import functools
import math

import jax
import jax.numpy as jnp
from jax import lax
from jax.experimental import pallas as pl
from jax.experimental.pallas import tpu as pltpu

F32 = jnp.float32
BF16 = jnp.bfloat16

D_MODEL = 1024
BATCH = 8
SEQ = 2048
S5_GROUP = 16
S5_GROUPS = D_MODEL // S5_GROUP
S5_STATE = 64
LAMBDA_RE_MAX = -1e-4
HEAD_DIM = 64
N_Q_HEADS = D_MODEL // HEAD_DIM
N_KV_HEADS = 4
Q_PER_KV = N_Q_HEADS // N_KV_HEADS
WINDOW = 128
D_FF = 4 * D_MODEL
NORM_EPS = 1e-5

LANES = 128
SUBLANES = 8
N_LANE_TILES = D_MODEL // LANES
GROUPS_PER_TILE = LANES // S5_GROUP
STATES_PER_TILE = GROUPS_PER_TILE * S5_STATE
MIB = 1024 * 1024

S5_TL = 64
S5_EPI_ROWS = 256
MLP_TM = 512
MLP_FF_CHUNK = 1024
ATTN_TQ = 512


def _rmsnorm(x, g):
    return x * lax.rsqrt(jnp.mean(x * x, axis=-1, keepdims=True) + NORM_EPS) * g


def _gelu_tanh(x):
    c = math.sqrt(2.0 / math.pi)
    return 0.5 * x * (1.0 + jnp.tanh(c * (x + 0.044715 * (x * x * x))))


def _const_spec(shape):
    nd = len(shape)
    return pl.BlockSpec(shape, lambda *_: (0,) * nd, pipeline_mode=pl.Buffered(1))


def _s5_prep_kernel(are_ref, aim_ref, ldt_ref, bre_ref, bim_ref,
                    lbr_ref, lbi_ref, bbr_ref, bbi_ref):
    ar = jnp.minimum(are_ref[...], LAMBDA_RE_MAX)
    ai = aim_ref[...]
    dt = jnp.exp(ldt_ref[...])
    mag = jnp.exp(ar * dt)
    ang = ai * dt
    lr = mag * jnp.cos(ang)
    li = mag * jnp.sin(ang)
    den = ar * ar + ai * ai
    nr = (lr - 1.0) * ar + li * ai
    ni = li * ar - (lr - 1.0) * ai
    cr = nr / den
    ci = ni / den
    br = bre_ref[...]
    bi = bim_ref[...]
    lbr_ref[...] = lr
    lbi_ref[...] = li
    bbr_ref[...] = cr * br - ci * bi
    bbi_ref[...] = cr * bi + ci * br


def _s5_prep(a_re, a_im, log_dt, b_re, b_im):
    g, p, c = b_re.shape
    rep = lambda a: jnp.repeat(a, c, axis=-1)
    ldt = jnp.broadcast_to(log_dt[:, None], (g, p * c))
    shp = jax.ShapeDtypeStruct((g, p * c), F32)
    lbr, lbi, bbr, bbi = pl.pallas_call(
        _s5_prep_kernel, out_shape=(shp, shp, shp, shp), name="s5_prep",
    )(rep(a_re), rep(a_im), ldt, b_re.reshape(g, p * c), b_im.reshape(g, p * c))
    pick = lambda a: a.reshape(g, p, c)[:, :, 0]
    return pick(lbr), pick(lbi), bbr.reshape(g, p, c), bbi.reshape(g, p, c)


def _blockdiag_in(b_bar_re, b_bar_im):
    eye = jnp.eye(GROUPS_PER_TILE, dtype=F32)

    def one(b):
        b = b.reshape(N_LANE_TILES, GROUPS_PER_TILE, S5_STATE, S5_GROUP)
        return jnp.einsum('jgpc,gh->jgchp', b, eye).reshape(N_LANE_TILES, LANES, STATES_PER_TILE)

    return jnp.concatenate([one(b_bar_re), one(b_bar_im)], axis=-1)


def _blockdiag_out(c_re, c_im):
    eye = jnp.eye(GROUPS_PER_TILE, dtype=F32)

    def one(c):
        c = c.reshape(N_LANE_TILES, GROUPS_PER_TILE, S5_GROUP, S5_STATE)
        return jnp.einsum('jgcp,gh->jgphc', c, eye).reshape(N_LANE_TILES, STATES_PER_TILE, LANES)

    return jnp.concatenate([one(c_re), -one(c_im)], axis=1)


def _s5_kernel(x_ref, g_ref, wbu_ref, ar_ref, ai_ref, wc_ref, d_ref, wglu_ref, bglu_ref,
               o_ref, hbm_s, hn_s, hnb_s, bu_s, xs_s, y_s, st_s, res_s, *, tl):
    rows = tl * SUBLANES
    sp = STATES_PER_TILE
    pitch = tl + SUBLANES

    @pl.when(pl.program_id(0) == 0)
    def _():
        st_s[...] = jnp.zeros_like(st_s)

    for b in range(SUBLANES):
        hn = _rmsnorm(x_ref[b], g_ref[...])
        for j in range(N_LANE_TILES):
            hbm_s[j, b * pitch:b * pitch + tl, :] = hn[:, j * LANES:(j + 1) * LANES]

    for j in range(N_LANE_TILES):
        lanes = slice(j * LANES, (j + 1) * LANES)
        for t in range(0, tl, 2):
            pair = jnp.concatenate([hbm_s[j, pl.ds(t, SUBLANES, stride=pitch), :],
                                    hbm_s[j, pl.ds(t + 1, SUBLANES, stride=pitch), :]], axis=0)
            hn_s[t * SUBLANES:(t + 2) * SUBLANES, lanes] = pair
            hnb_s[t * SUBLANES:(t + 2) * SUBLANES, lanes] = pair.astype(BF16)

    def project_in(j):
        bu_s[j % 2] = jnp.dot(hnb_s[:, j * LANES:(j + 1) * LANES], wbu_ref[j],
                              preferred_element_type=F32)

    project_in(0)
    for j in range(N_LANE_TILES):
        if j + 1 < N_LANE_TILES:
            project_in(j + 1)
        p = j % 2
        ar = ar_ref[j]
        ai = ai_ref[j]
        xr = st_s[j, :, 0:sp]
        xi = st_s[j, :, sp:2 * sp]
        for t in range(tl):
            r = slice(t * SUBLANES, (t + 1) * SUBLANES)
            nxr = ar * xr - ai * xi + bu_s[p, r, 0:sp]
            nxi = ar * xi + ai * xr + bu_s[p, r, sp:2 * sp]
            xs_s[p, r, 0:sp] = nxr
            xs_s[p, r, sp:2 * sp] = nxi
            xr, xi = nxr, nxi
        st_s[j, :, 0:sp] = xr
        st_s[j, :, sp:2 * sp] = xi
        y_s[:, j * LANES:(j + 1) * LANES] = jnp.dot(xs_s[p].astype(BF16), wc_ref[j],
                                                     preferred_element_type=F32)

    for c in range(rows // S5_EPI_ROWS):
        rs = slice(c * S5_EPI_ROWS, (c + 1) * S5_EPI_ROWS)
        y = y_s[rs, :] + d_ref[...] * hn_s[rs, :]
        z = jnp.dot(_gelu_tanh(y).astype(BF16), wglu_ref[...],
                    preferred_element_type=F32) + bglu_ref[...]
        mix = z[:, :D_MODEL] * (1.0 / (1.0 + jnp.exp(-z[:, D_MODEL:])))
        for j in range(N_LANE_TILES):
            res_s[j, rs, :] = mix[:, j * LANES:(j + 1) * LANES]

    for b in range(SUBLANES):
        for j in range(N_LANE_TILES):
            lanes = slice(j * LANES, (j + 1) * LANES)
            o_ref[b, :, lanes] = x_ref[b, :, lanes] + res_s[j, pl.ds(b, tl, stride=SUBLANES), :]


def _s5_layer(x, g_mix, wbu, a_re_t, a_im_t, wc, d_skip, w_glu, b_glu):
    bsz, seq, d = x.shape
    tl = S5_TL
    rows = tl * bsz
    kernel = functools.partial(_s5_kernel, tl=tl)
    return pl.pallas_call(
        kernel,
        out_shape=jax.ShapeDtypeStruct((bsz, seq, d), F32),
        grid=(seq // tl,),
        in_specs=[
            pl.BlockSpec((bsz, tl, d), lambda i: (0, i, 0)),
            _const_spec((1, d)),
            _const_spec(wbu.shape),
            _const_spec(a_re_t.shape),
            _const_spec(a_im_t.shape),
            _const_spec(wc.shape),
            _const_spec((1, d)),
            _const_spec(w_glu.shape),
            _const_spec((1, 2 * d)),
        ],
        out_specs=pl.BlockSpec((bsz, tl, d), lambda i: (0, i, 0)),
        scratch_shapes=[
            pltpu.VMEM((N_LANE_TILES, bsz * (tl + SUBLANES), LANES), F32),
            pltpu.VMEM((rows, d), F32),
            pltpu.VMEM((rows, d), BF16),
            pltpu.VMEM((2, rows, 2 * STATES_PER_TILE), F32),
            pltpu.VMEM((2, rows, 2 * STATES_PER_TILE), F32),
            pltpu.VMEM((rows, d), F32),
            pltpu.VMEM((N_LANE_TILES, SUBLANES, 2 * STATES_PER_TILE), F32),
            pltpu.VMEM((N_LANE_TILES, rows, LANES), F32),
        ],
        compiler_params=pltpu.CompilerParams(
            dimension_semantics=("arbitrary",), vmem_limit_bytes=56 * MIB),
        name="s5_layer",
    )(x, g_mix.reshape(1, d), wbu, a_re_t, a_im_t, wc, d_skip.reshape(1, d), w_glu,
      b_glu.reshape(1, 2 * d))


def _mlp_body(h, g, win_ref, wout_ref):
    hn = _rmsnorm(h, g).astype(BF16)
    acc = jnp.zeros(h.shape, F32)
    for c in range(D_FF // MLP_FF_CHUNK):
        cols = slice(c * MLP_FF_CHUNK, (c + 1) * MLP_FF_CHUNK)
        a = jnp.dot(hn, win_ref[:, cols], preferred_element_type=F32)
        a = jnp.square(jnp.maximum(a, 0.0)).astype(BF16)
        acc = acc + jnp.dot(a, wout_ref[cols, :], preferred_element_type=F32)
    return h + acc


def _mlp_kv_kernel(h_ref, g_ref, win_ref, wout_ref, gkv_ref, wkv_ref, bkv_ref, o_ref, kv_ref):
    out = _mlp_body(h_ref[...], g_ref[...], win_ref, wout_ref)
    o_ref[...] = out
    kvn = _rmsnorm(out, gkv_ref[...]).astype(BF16)
    kv = jnp.dot(kvn, wkv_ref[...], preferred_element_type=F32) + bkv_ref[...]
    kv_ref[...] = kv.astype(BF16)


def _mlp_final_kernel(h_ref, g_ref, win_ref, wout_ref, gfin_ref, o_ref):
    out = _mlp_body(h_ref[...], g_ref[...], win_ref, wout_ref)
    o_ref[...] = _rmsnorm(out, gfin_ref[...])


def _mlp_kv(h, g, w_in, w_out, g_kv, w_kv, b_kv):
    t, d = h.shape
    nkv = w_kv.shape[1]
    row = lambda n: pl.BlockSpec((MLP_TM, n), lambda i: (i, 0))
    return pl.pallas_call(
        _mlp_kv_kernel,
        out_shape=(jax.ShapeDtypeStruct((t, d), F32), jax.ShapeDtypeStruct((t, nkv), BF16)),
        grid=(t // MLP_TM,),
        in_specs=[row(d), _const_spec((1, d)), _const_spec(w_in.shape), _const_spec(w_out.shape),
                  _const_spec((1, d)), _const_spec(w_kv.shape), _const_spec((1, nkv))],
        out_specs=(row(d), row(nkv)),
        compiler_params=pltpu.CompilerParams(
            dimension_semantics=("parallel",), vmem_limit_bytes=56 * MIB),
        name="mlp_kv",
    )(h, g.reshape(1, d), w_in, w_out, g_kv.reshape(1, d), w_kv, b_kv.reshape(1, nkv))


def _mlp_final(h, g, w_in, w_out, g_fin):
    t, d = h.shape
    row = lambda n: pl.BlockSpec((MLP_TM, n), lambda i: (i, 0))
    return pl.pallas_call(
        _mlp_final_kernel,
        out_shape=jax.ShapeDtypeStruct((t, d), F32),
        grid=(t // MLP_TM,),
        in_specs=[row(d), _const_spec((1, d)), _const_spec(w_in.shape), _const_spec(w_out.shape),
                  _const_spec((1, d))],
        out_specs=row(d),
        compiler_params=pltpu.CompilerParams(
            dimension_semantics=("parallel",), vmem_limit_bytes=56 * MIB),
        name="mlp_final",
    )(h, g.reshape(1, d), w_in, w_out, g_fin.reshape(1, d))


def _attn_kernel(sink_ref, h_ref, kvp_ref, kvc_ref, g_ref, wq_ref, bq_ref, wo_ref, bo_ref,
                 o_ref, q_s, oh_s, *, tq):
    nsb = tq // WINDOW
    n = pl.program_id(1)
    h = h_ref[0]
    hn = _rmsnorm(h, g_ref[...]).astype(BF16)
    q = jnp.dot(hn, wq_ref[...], preferred_element_type=F32) + bq_ref[...]
    q_s[...] = (q * (1.0 / math.sqrt(HEAD_DIM))).astype(BF16)

    lo = lax.broadcasted_iota(jnp.int32, (WINDOW, LANES), 1) < HEAD_DIM
    qi = lax.broadcasted_iota(jnp.int32, (WINDOW, 2 * WINDOW), 0)
    kj = lax.broadcasted_iota(jnp.int32, (WINDOW, 2 * WINDOW), 1)
    diff = qi + WINDOW - kj
    band = (diff >= 0) & (diff < WINDOW)
    kv_off = N_KV_HEADS * LANES
    zero = jnp.zeros((WINDOW, LANES), BF16)

    for sb in range(nsb):
        rows = slice(sb * WINDOW, (sb + 1) * WINDOW)
        first = (n * nsb + sb) == 0
        valid = band & jnp.logical_or(jnp.logical_not(first), kj >= WINDOW)
        bias = jnp.where(valid, 0.0, -jnp.inf).astype(F32)
        for kh in range(N_KV_HEADS):
            kl = slice(kh * LANES, (kh + 1) * LANES)
            vl = slice(kv_off + kh * LANES, kv_off + (kh + 1) * LANES)
            if sb == 0:
                kprev, vprev = kvp_ref[0, :, kl], kvp_ref[0, :, vl]
            else:
                prow = slice((sb - 1) * WINDOW, sb * WINDOW)
                kprev, vprev = kvc_ref[0, prow, kl], kvc_ref[0, prow, vl]
            kd = jnp.concatenate([kprev, kvc_ref[0, rows, kl]], axis=0)
            vd = jnp.concatenate([vprev, kvc_ref[0, rows, vl]], axis=0)
            q01 = q_s[rows, 2 * kh * LANES:(2 * kh + 1) * LANES]
            q23 = q_s[rows, (2 * kh + 1) * LANES:(2 * kh + 2) * LANES]
            lhs = jnp.concatenate([jnp.where(lo, q01, zero), jnp.where(lo, zero, q01),
                                   jnp.where(lo, q23, zero), jnp.where(lo, zero, q23)], axis=0)
            s = lax.dot_general(lhs, kd, (((1,), (1,)), ((), ())),
                                preferred_element_type=F32)
            ps, rs = [], []
            for gq in range(Q_PER_KV):
                sg = s[gq * WINDOW:(gq + 1) * WINDOW] + bias
                sink = sink_ref[kh * Q_PER_KV + gq]
                m = jnp.maximum(jnp.max(sg, axis=-1, keepdims=True), sink)
                p = jnp.exp(sg - m)
                den = jnp.sum(p, axis=-1, keepdims=True) + jnp.exp(sink - m)
                ps.append(p.astype(BF16))
                rs.append(1.0 / den)
            od = jnp.dot(jnp.concatenate(ps, axis=0), vd, preferred_element_type=F32)
            og = [od[gq * WINDOW:(gq + 1) * WINDOW] * rs[gq] for gq in range(Q_PER_KV)]
            oh_s[rows, 2 * kh * LANES:(2 * kh + 1) * LANES] = jnp.where(lo, og[0], og[1]).astype(BF16)
            oh_s[rows, (2 * kh + 1) * LANES:(2 * kh + 2) * LANES] = jnp.where(lo, og[2], og[3]).astype(BF16)

    o_ref[0] = h + jnp.dot(oh_s[...], wo_ref[...], preferred_element_type=F32) + bo_ref[...]


def _attn_layer(h, kv, sinks, g, w_q, b_q, w_o, b_o):
    bsz, seq, d = h.shape
    nkv = kv.shape[-1]
    tq = ATTN_TQ
    nsb = tq // WINDOW
    kernel = functools.partial(_attn_kernel, tq=tq)
    return pl.pallas_call(
        kernel,
        out_shape=jax.ShapeDtypeStruct((bsz, seq, d), F32),
        grid=(bsz, seq // tq),
        in_specs=[
            pl.BlockSpec(memory_space=pltpu.SMEM),
            pl.BlockSpec((1, tq, d), lambda b, n: (b, n, 0)),
            pl.BlockSpec((1, WINDOW, nkv), lambda b, n: (b, jnp.maximum(n * nsb - 1, 0), 0)),
            pl.BlockSpec((1, tq, nkv), lambda b, n: (b, n, 0)),
            _const_spec((1, d)),
            _const_spec(w_q.shape),
            _const_spec((1, d)),
            _const_spec(w_o.shape),
            _const_spec((1, d)),
        ],
        out_specs=pl.BlockSpec((1, tq, d), lambda b, n: (b, n, 0)),
        scratch_shapes=[pltpu.VMEM((tq, d), BF16), pltpu.VMEM((tq, d), BF16)],
        compiler_params=pltpu.CompilerParams(
            dimension_semantics=("parallel", "parallel"), vmem_limit_bytes=40 * MIB),
        name="attn",
    )(sinks, h, kv, kv, g.reshape(1, d), w_q, b_q.reshape(1, d), w_o, b_o.reshape(1, d))


def _dup_heads(w):
    lead = w.shape[:-1]
    w = w.reshape(lead + (N_KV_HEADS, 1, HEAD_DIM))
    return jnp.broadcast_to(w, lead + (N_KV_HEADS, 2, HEAD_DIM)).reshape(lead + (N_KV_HEADS * LANES,))


def kernel(x, norm_mix, norm_mlp, norm_kv, norm_final, s5_a_re, s5_a_im, s5_log_dt, s5_b_re, s5_b_im, s5_c_re, s5_c_im, s5_d, s5_w_glu, s5_b_glu, w_kv, b_kv, w_q, b_q, sinks, w_o, b_o, w_mlp_in, w_mlp_out):
    bsz, seq, d = x.shape
    hkv = N_KV_HEADS * HEAD_DIM

    lam_re, lam_im, bb_re, bb_im = _s5_prep(s5_a_re[0], s5_a_im[0], s5_log_dt[0],
                                            s5_b_re[0], s5_b_im[0])
    wbu = _blockdiag_in(bb_re, bb_im).astype(BF16)
    wc = _blockdiag_out(s5_c_re[0], s5_c_im[0]).astype(BF16)
    tile_bcast = lambda a: jnp.broadcast_to(
        a.reshape(N_LANE_TILES, 1, STATES_PER_TILE), (N_LANE_TILES, SUBLANES, STATES_PER_TILE))
    h = _s5_layer(x, norm_mix[0], wbu, tile_bcast(lam_re), tile_bcast(lam_im), wc,
                  s5_d[0], s5_w_glu[0].astype(BF16), s5_b_glu[0])

    w_kv_d = jnp.concatenate([_dup_heads(w_kv[:, :hkv]), _dup_heads(w_kv[:, hkv:])], axis=-1)
    b_kv_d = jnp.concatenate([_dup_heads(b_kv[:hkv]), _dup_heads(b_kv[hkv:])], axis=-1)
    h, kv = _mlp_kv(h.reshape(bsz * seq, d), norm_mlp[0], w_mlp_in[0].astype(BF16),
                    w_mlp_out[0].astype(BF16), norm_kv, w_kv_d.astype(BF16), b_kv_d)

    h = _attn_layer(h.reshape(bsz, seq, d), kv.reshape(bsz, seq, -1), sinks[0], norm_mix[1],
                    w_q[0].astype(BF16), b_q[0], w_o[0].astype(BF16), b_o[0])

    out = _mlp_final(h.reshape(bsz * seq, d), norm_mlp[1], w_mlp_in[1].astype(BF16),
                     w_mlp_out[1].astype(BF16), norm_final)
    return out.reshape(bsz, seq, d)
```

```python
import functools
import math

import jax
import jax.numpy as jnp
from jax import lax
from jax.experimental import pallas as pl
from jax.experimental.pallas import tpu as pltpu

F32 = jnp.float32
BF16 = jnp.bfloat16

D_MODEL = 1024
BATCH = 8
SEQ = 2048
S5_GROUP = 16
S5_GROUPS = D_MODEL // S5_GROUP
S5_STATE = 64
LAMBDA_RE_MAX = -1e-4
HEAD_DIM = 64
N_Q_HEADS = D_MODEL // HEAD_DIM
N_KV_HEADS = 4
Q_PER_KV = N_Q_HEADS // N_KV_HEADS
WINDOW = 128
D_FF = 4 * D_MODEL
NORM_EPS = 1e-5

LANES = 128
SUBLANES = 8
N_LANE_TILES = D_MODEL // LANES
GROUPS_PER_TILE = LANES // S5_GROUP
STATES_PER_TILE = GROUPS_PER_TILE * S5_STATE
MIB = 1024 * 1024

S5_TL = 64
S5_EPI_ROWS = 256
MLP_TM = 512
MLP_FF_CHUNK = 1024
ATTN_TQ = 512


def _rmsnorm(x, g):
    return x * lax.rsqrt(jnp.mean(x * x, axis=-1, keepdims=True) + NORM_EPS) * g


def _gelu_tanh(x):
    c = math.sqrt(2.0 / math.pi)
    return 0.5 * x * (1.0 + jnp.tanh(c * (x + 0.044715 * (x * x * x))))


def _const_spec(shape):
    nd = len(shape)
    return pl.BlockSpec(shape, lambda *_: (0,) * nd, pipeline_mode=pl.Buffered(1))


def _s5_prep_kernel(are_ref, aim_ref, ldt_ref, bre_ref, bim_ref,
                    lbr_ref, lbi_ref, bbr_ref, bbi_ref):
    ar = jnp.minimum(are_ref[...], LAMBDA_RE_MAX)
    ai = aim_ref[...]
    dt = jnp.exp(ldt_ref[...])
    mag = jnp.exp(ar * dt)
    ang = ai * dt
    lr = mag * jnp.cos(ang)
    li = mag * jnp.sin(ang)
    den = ar * ar + ai * ai
    nr = (lr - 1.0) * ar + li * ai
    ni = li * ar - (lr - 1.0) * ai
    cr = nr / den
    ci = ni / den
    br = bre_ref[...]
    bi = bim_ref[...]
    lbr_ref[...] = lr
    lbi_ref[...] = li
    bbr_ref[...] = cr * br - ci * bi
    bbi_ref[...] = cr * bi + ci * br


def _s5_prep(a_re, a_im, log_dt, b_re, b_im):
    g, p, c = b_re.shape
    rep = lambda a: jnp.repeat(a, c, axis=-1)
    ldt = jnp.broadcast_to(log_dt[:, None], (g, p * c))
    shp = jax.ShapeDtypeStruct((g, p * c), F32)
    lbr, lbi, bbr, bbi = pl.pallas_call(
        _s5_prep_kernel, out_shape=(shp, shp, shp, shp), name="s5_prep",
    )(rep(a_re), rep(a_im), ldt, b_re.reshape(g, p * c), b_im.reshape(g, p * c))
    pick = lambda a: a.reshape(g, p, c)[:, :, 0]
    return pick(lbr), pick(lbi), bbr.reshape(g, p, c), bbi.reshape(g, p, c)


def _blockdiag_in(b):
    eye = jnp.eye(GROUPS_PER_TILE, dtype=F32)
    b = b.reshape(N_LANE_TILES, GROUPS_PER_TILE, S5_STATE, S5_GROUP)
    return jnp.einsum('jgpc,gh->jgchp', b, eye).reshape(N_LANE_TILES, LANES, STATES_PER_TILE)


def _blockdiag_out(c):
    eye = jnp.eye(GROUPS_PER_TILE, dtype=F32)
    c = c.reshape(N_LANE_TILES, GROUPS_PER_TILE, S5_GROUP, S5_STATE)
    return jnp.einsum('jgcp,gh->jgphc', c, eye).reshape(N_LANE_TILES, STATES_PER_TILE, LANES)


def _s5_fold_kernel(wbr_ref, wbi_ref, wcr_ref, wci_ref, lrr_ref, lir_ref, lrc_ref, lic_ref,
                    w1_ref, cx_ref, l2r_ref, l2i_ref):
    wbr, wbi = wbr_ref[0], wbi_ref[0]
    wcr, wci = wcr_ref[0], wci_ref[0]
    lr, li = lrr_ref[0], lir_ref[0]
    lrc, lic = lrc_ref[0], lic_ref[0]
    hdot = functools.partial(jnp.dot, precision=lax.Precision.HIGHEST, preferred_element_type=F32)
    w1r = wbr * lr - wbi * li
    w1i = wbr * li + wbi * lr
    k0 = hdot(wbr, wcr) - hdot(wbi, wci)
    k1 = hdot(w1r, wcr) - hdot(w1i, wci)
    top = jnp.concatenate([k0, k1, w1r, w1i], axis=1)
    bot = jnp.concatenate([jnp.zeros_like(k0), k0, wbr, wbi], axis=1)
    w1_ref[0] = jnp.concatenate([top, bot], axis=0).astype(BF16)
    l2rc = lrc * lrc - lic * lic
    l2ic = 2.0 * (lrc * lic)
    cx0 = jnp.concatenate([lrc * wcr - lic * wci, -(lic * wcr + lrc * wci)], axis=0)
    cx1 = jnp.concatenate([l2rc * wcr - l2ic * wci, -(l2ic * wcr + l2rc * wci)], axis=0)
    cx_ref[0] = jnp.concatenate([cx0, cx1], axis=1).astype(BF16)
    l2r_ref[0] = lr * lr - li * li
    l2i_ref[0] = 2.0 * (lr * li)


def _s5_fold(lam_re, lam_im, bb_re, bb_im, c_re, c_im):
    nt, sp = N_LANE_TILES, STATES_PER_TILE
    tile = lambda shape: pl.BlockSpec((1,) + shape, lambda j: (j, 0, 0))
    row = lambda a: a.reshape(nt, 1, sp)
    col = lambda a: a.reshape(nt, sp, 1)
    return pl.pallas_call(
        _s5_fold_kernel,
        out_shape=(jax.ShapeDtypeStruct((nt, 2 * LANES, 2 * LANES + 2 * sp), BF16),
                   jax.ShapeDtypeStruct((nt, 2 * sp, 2 * LANES), BF16),
                   jax.ShapeDtypeStruct((nt, 1, sp), F32),
                   jax.ShapeDtypeStruct((nt, 1, sp), F32)),
        grid=(nt,),
        in_specs=[tile((LANES, sp)), tile((LANES, sp)), tile((sp, LANES)), tile((sp, LANES)),
                  tile((1, sp)), tile((1, sp)), tile((sp, 1)), tile((sp, 1))],
        out_specs=(tile((2 * LANES, 2 * LANES + 2 * sp)), tile((2 * sp, 2 * LANES)),
                   tile((1, sp)), tile((1, sp))),
        compiler_params=pltpu.CompilerParams(dimension_semantics=("parallel",)),
        name="s5_fold",
    )(_blockdiag_in(bb_re), _blockdiag_in(bb_im), _blockdiag_out(c_re), _blockdiag_out(c_im),
      row(lam_re), row(lam_im), col(lam_re), col(lam_im))


def _s5_kernel(x_ref, g_ref, w1_ref, l2r_ref, l2i_ref, cx_ref, d_ref, wglu_ref, bglu_ref,
               o_ref, hbm_s, hn_s, hnu_s, zy_s, bu_s, xs_s, y_s, st_s, res_s, *, tl):
    half = tl // 2
    prow = half * SUBLANES
    sp = STATES_PER_TILE
    pitch = tl + SUBLANES

    @pl.when(pl.program_id(0) == 0)
    def _():
        st_s[...] = jnp.zeros_like(st_s)

    for b in range(SUBLANES):
        hn = _rmsnorm(x_ref[b], g_ref[...])
        for j in range(N_LANE_TILES):
            hbm_s[j, b * pitch:b * pitch + tl, :] = hn[:, j * LANES:(j + 1) * LANES]

    for j in range(N_LANE_TILES):
        lanes = slice(j * LANES, (j + 1) * LANES)
        for m in range(tl // 4):
            v = [hbm_s[j, pl.ds(4 * m + i, SUBLANES, stride=pitch), :] for i in range(4)]
            for i in range(4):
                g = (i % 2) * half + 2 * m + i // 2
                hn_s[g * SUBLANES:(g + 1) * SUBLANES, lanes] = v[i]
            r16 = slice(2 * m * SUBLANES, (2 * m + 2) * SUBLANES)
            hnu_s[r16, 2 * j * LANES:(2 * j + 1) * LANES] = (
                jnp.concatenate([v[0], v[2]], axis=0).astype(BF16))
            hnu_s[r16, (2 * j + 1) * LANES:(2 * j + 2) * LANES] = (
                jnp.concatenate([v[1], v[3]], axis=0).astype(BF16))

    def project_in(j):
        z = jnp.dot(hnu_s[:, 2 * j * LANES:(2 * j + 2) * LANES], w1_ref[j],
                    preferred_element_type=F32)
        zy_s[j % 2] = z[:, :2 * LANES]
        bu_s[j % 2] = z[:, 2 * LANES:]

    project_in(0)
    for j in range(N_LANE_TILES):
        if j + 1 < N_LANE_TILES:
            project_in(j + 1)
        p = j % 2
        lanes = slice(j * LANES, (j + 1) * LANES)
        ar = l2r_ref[j]
        ai = l2i_ref[j]
        xr = st_s[j, :, 0:sp]
        xi = st_s[j, :, sp:2 * sp]
        for k in range(half):
            r = slice(k * SUBLANES, (k + 1) * SUBLANES)
            xs_s[p, r, 0:sp] = xr
            xs_s[p, r, sp:2 * sp] = xi
            nxr = ar * xr - ai * xi + bu_s[p, r, 0:sp]
            nxi = ar * xi + ai * xr + bu_s[p, r, sp:2 * sp]
            xr, xi = nxr, nxi
        st_s[j, :, 0:sp] = xr
        st_s[j, :, sp:2 * sp] = xi
        y = zy_s[p] + jnp.dot(xs_s[p].astype(BF16), cx_ref[j], preferred_element_type=F32)
        y_s[0:prow, lanes] = y[:, :LANES]
        y_s[prow:2 * prow, lanes] = y[:, LANES:]

    for par in range(2):
        rs = slice(par * prow, (par + 1) * prow)
        y = y_s[rs, :] + d_ref[...] * hn_s[rs, :]
        z = jnp.dot(_gelu_tanh(y).astype(BF16), wglu_ref[...],
                    preferred_element_type=F32) + bglu_ref[...]
        mix = z[:, :D_MODEL] * (1.0 / (1.0 + jnp.exp(-z[:, D_MODEL:])))
        for k in range(half):
            t = 2 * k + par
            for j in range(N_LANE_TILES):
                res_s[j, t * SUBLANES:(t + 1) * SUBLANES, :] = (
                    mix[k * SUBLANES:(k + 1) * SUBLANES, j * LANES:(j + 1) * LANES])

    for b in range(SUBLANES):
        for j in range(N_LANE_TILES):
            lanes = slice(j * LANES, (j + 1) * LANES)
            o_ref[b, :, lanes] = x_ref[b, :, lanes] + res_s[j, pl.ds(b, tl, stride=SUBLANES), :]


def _s5_layer(x, g_mix, w1, lam2_re_t, lam2_im_t, cx, d_skip, w_glu, b_glu):
    bsz, seq, d = x.shape
    tl = S5_TL
    rows = tl * bsz
    prow = rows // 2
    kernel = functools.partial(_s5_kernel, tl=tl)
    return pl.pallas_call(
        kernel,
        out_shape=jax.ShapeDtypeStruct((bsz, seq, d), F32),
        grid=(seq // tl,),
        in_specs=[
            pl.BlockSpec((bsz, tl, d), lambda i: (0, i, 0)),
            _const_spec((1, d)),
            _const_spec(w1.shape),
            _const_spec(lam2_re_t.shape),
            _const_spec(lam2_im_t.shape),
            _const_spec(cx.shape),
            _const_spec((1, d)),
            _const_spec(w_glu.shape),
            _const_spec((1, 2 * d)),
        ],
        out_specs=pl.BlockSpec((bsz, tl, d), lambda i: (0, i, 0)),
        scratch_shapes=[
            pltpu.VMEM((N_LANE_TILES, bsz * (tl + SUBLANES), LANES), F32),
            pltpu.VMEM((rows, d), F32),
            pltpu.VMEM((prow, 2 * d), BF16),
            pltpu.VMEM((2, prow, 2 * LANES), F32),
            pltpu.VMEM((2, prow, 2 * STATES_PER_TILE), F32),
            pltpu.VMEM((2, prow, 2 * STATES_PER_TILE), F32),
            pltpu.VMEM((rows, d), F32),
            pltpu.VMEM((N_LANE_TILES, SUBLANES, 2 * STATES_PER_TILE), F32),
            pltpu.VMEM((N_LANE_TILES, rows, LANES), F32),
        ],
        compiler_params=pltpu.CompilerParams(
            dimension_semantics=("arbitrary",), vmem_limit_bytes=56 * MIB),
        name="s5_layer",
    )(x, g_mix.reshape(1, d), w1, lam2_re_t, lam2_im_t, cx, d_skip.reshape(1, d), w_glu,
      b_glu.reshape(1, 2 * d))


def _mlp_body(h, g, win_ref, wout_ref):
    hn = _rmsnorm(h, g).astype(BF16)
    acc = jnp.zeros(h.shape, F32)
    for c in range(D_FF // MLP_FF_CHUNK):
        cols = slice(c * MLP_FF_CHUNK, (c + 1) * MLP_FF_CHUNK)
        a = jnp.dot(hn, win_ref[:, cols], preferred_element_type=F32)
        a = jnp.square(jnp.maximum(a, 0.0)).astype(BF16)
        acc = acc + jnp.dot(a, wout_ref[cols, :], preferred_element_type=F32)
    return h + acc


def _mlp_kv_kernel(h_ref, g_ref, win_ref, wout_ref, gkv_ref, wkv_ref, bkv_ref, o_ref, kv_ref):
    out = _mlp_body(h_ref[...], g_ref[...], win_ref, wout_ref)
    o_ref[...] = out
    kvn = _rmsnorm(out, gkv_ref[...]).astype(BF16)
    kv = jnp.dot(kvn, wkv_ref[...], preferred_element_type=F32) + bkv_ref[...]
    kv_ref[...] = kv.astype(BF16)


def _mlp_final_kernel(h_ref, g_ref, win_ref, wout_ref, gfin_ref, o_ref):
    out = _mlp_body(h_ref[...], g_ref[...], win_ref, wout_ref)
    o_ref[...] = _rmsnorm(out, gfin_ref[...])


def _mlp_kv(h, g, w_in, w_out, g_kv, w_kv, b_kv):
    t, d = h.shape
    nkv = w_kv.shape[1]
    row = lambda n: pl.BlockSpec((MLP_TM, n), lambda i: (i, 0))
    return pl.pallas_call(
        _mlp_kv_kernel,
        out_shape=(jax.ShapeDtypeStruct((t, d), F32), jax.ShapeDtypeStruct((t, nkv), BF16)),
        grid=(t // MLP_TM,),
        in_specs=[row(d), _const_spec((1, d)), _const_spec(w_in.shape), _const_spec(w_out.shape),
                  _const_spec((1, d)), _const_spec(w_kv.shape), _const_spec((1, nkv))],
        out_specs=(row(d), row(nkv)),
        compiler_params=pltpu.CompilerParams(
            dimension_semantics=("parallel",), vmem_limit_bytes=56 * MIB),
        name="mlp_kv",
    )(h, g.reshape(1, d), w_in, w_out, g_kv.reshape(1, d), w_kv, b_kv.reshape(1, nkv))


def _mlp_final(h, g, w_in, w_out, g_fin):
    t, d = h.shape
    row = lambda n: pl.BlockSpec((MLP_TM, n), lambda i: (i, 0))
    return pl.pallas_call(
        _mlp_final_kernel,
        out_shape=jax.ShapeDtypeStruct((t, d), F32),
        grid=(t // MLP_TM,),
        in_specs=[row(d), _const_spec((1, d)), _const_spec(w_in.shape), _const_spec(w_out.shape),
                  _const_spec((1, d))],
        out_specs=row(d),
        compiler_params=pltpu.CompilerParams(
            dimension_semantics=("parallel",), vmem_limit_bytes=56 * MIB),
        name="mlp_final",
    )(h, g.reshape(1, d), w_in, w_out, g_fin.reshape(1, d))


def _attn_kernel(sink_ref, h_ref, kvp_ref, kvc_ref, g_ref, wq_ref, bq_ref, wo_ref, bo_ref,
                 o_ref, q_s, oh_s, *, tq):
    nsb = tq // WINDOW
    n = pl.program_id(1)
    h = h_ref[0]
    hn = _rmsnorm(h, g_ref[...]).astype(BF16)
    q = jnp.dot(hn, wq_ref[...], preferred_element_type=F32) + bq_ref[...]
    q_s[...] = (q * (1.0 / math.sqrt(HEAD_DIM))).astype(BF16)

    lo = lax.broadcasted_iota(jnp.int32, (WINDOW, LANES), 1) < HEAD_DIM
    qi = lax.broadcasted_iota(jnp.int32, (WINDOW, 2 * WINDOW), 0)
    kj = lax.broadcasted_iota(jnp.int32, (WINDOW, 2 * WINDOW), 1)
    diff = qi + WINDOW - kj
    band = (diff >= 0) & (diff < WINDOW)
    kv_off = N_KV_HEADS * LANES
    zero = jnp.zeros((WINDOW, LANES), BF16)

    for sb in range(nsb):
        rows = slice(sb * WINDOW, (sb + 1) * WINDOW)
        first = (n * nsb + sb) == 0
        valid = band & jnp.logical_or(jnp.logical_not(first), kj >= WINDOW)
        bias = jnp.where(valid, 0.0, -jnp.inf).astype(F32)
        for kh in range(N_KV_HEADS):
            kl = slice(kh * LANES, (kh + 1) * LANES)
            vl = slice(kv_off + kh * LANES, kv_off + (kh + 1) * LANES)
            if sb == 0:
                kprev, vprev = kvp_ref[0, :, kl], kvp_ref[0, :, vl]
            else:
                prow = slice((sb - 1) * WINDOW, sb * WINDOW)
                kprev, vprev = kvc_ref[0, prow, kl], kvc_ref[0, prow, vl]
            kd = jnp.concatenate([kprev, kvc_ref[0, rows, kl]], axis=0)
            vd = jnp.concatenate([vprev, kvc_ref[0, rows, vl]], axis=0)
            q01 = q_s[rows, 2 * kh * LANES:(2 * kh + 1) * LANES]
            q23 = q_s[rows, (2 * kh + 1) * LANES:(2 * kh + 2) * LANES]
            lhs = jnp.concatenate([jnp.where(lo, q01, zero), jnp.where(lo, zero, q01),
                                   jnp.where(lo, q23, zero), jnp.where(lo, zero, q23)], axis=0)
            s = lax.dot_general(lhs, kd, (((1,), (1,)), ((), ())),
                                preferred_element_type=F32)
            ps, rs = [], []
            for gq in range(Q_PER_KV):
                sg = s[gq * WINDOW:(gq + 1) * WINDOW] + bias
                sink = sink_ref[kh * Q_PER_KV + gq]
                m = jnp.maximum(jnp.max(sg, axis=-1, keepdims=True), sink)
                p = jnp.exp(sg - m)
                den = jnp.sum(p, axis=-1, keepdims=True) + jnp.exp(sink - m)
                ps.append(p.astype(BF16))
                rs.append(1.0 / den)
            od = jnp.dot(jnp.concatenate(ps, axis=0), vd, preferred_element_type=F32)
            og = [od[gq * WINDOW:(gq + 1) * WINDOW] * rs[gq] for gq in range(Q_PER_KV)]
            oh_s[rows, 2 * kh * LANES:(2 * kh + 1) * LANES] = jnp.where(lo, og[0], og[1]).astype(BF16)
            oh_s[rows, (2 * kh + 1) * LANES:(2 * kh + 2) * LANES] = jnp.where(lo, og[2], og[3]).astype(BF16)

    o_ref[0] = h + jnp.dot(oh_s[...], wo_ref[...], preferred_element_type=F32) + bo_ref[...]


def _attn_layer(h, kv, sinks, g, w_q, b_q, w_o, b_o):
    bsz, seq, d = h.shape
    nkv = kv.shape[-1]
    tq = ATTN_TQ
    nsb = tq // WINDOW
    kernel = functools.partial(_attn_kernel, tq=tq)
    return pl.pallas_call(
        kernel,
        out_shape=jax.ShapeDtypeStruct((bsz, seq, d), F32),
        grid=(bsz, seq // tq),
        in_specs=[
            pl.BlockSpec(memory_space=pltpu.SMEM),
            pl.BlockSpec((1, tq, d), lambda b, n: (b, n, 0)),
            pl.BlockSpec((1, WINDOW, nkv), lambda b, n: (b, jnp.maximum(n * nsb - 1, 0), 0)),
            pl.BlockSpec((1, tq, nkv), lambda b, n: (b, n, 0)),
            _const_spec((1, d)),
            _const_spec(w_q.shape),
            _const_spec((1, d)),
            _const_spec(w_o.shape),
            _const_spec((1, d)),
        ],
        out_specs=pl.BlockSpec((1, tq, d), lambda b, n: (b, n, 0)),
        scratch_shapes=[pltpu.VMEM((tq, d), BF16), pltpu.VMEM((tq, d), BF16)],
        compiler_params=pltpu.CompilerParams(
            dimension_semantics=("parallel", "parallel"), vmem_limit_bytes=40 * MIB),
        name="attn",
    )(sinks, h, kv, kv, g.reshape(1, d), w_q, b_q.reshape(1, d), w_o, b_o.reshape(1, d))


def _dup_heads(w):
    lead = w.shape[:-1]
    w = w.reshape(lead + (N_KV_HEADS, 1, HEAD_DIM))
    return jnp.broadcast_to(w, lead + (N_KV_HEADS, 2, HEAD_DIM)).reshape(lead + (N_KV_HEADS * LANES,))


def kernel(x, norm_mix, norm_mlp, norm_kv, norm_final, s5_a_re, s5_a_im, s5_log_dt, s5_b_re, s5_b_im, s5_c_re, s5_c_im, s5_d, s5_w_glu, s5_b_glu, w_kv, b_kv, w_q, b_q, sinks, w_o, b_o, w_mlp_in, w_mlp_out):
    bsz, seq, d = x.shape
    hkv = N_KV_HEADS * HEAD_DIM

    lam_re, lam_im, bb_re, bb_im = _s5_prep(s5_a_re[0], s5_a_im[0], s5_log_dt[0],
                                            s5_b_re[0], s5_b_im[0])
    w1, cx, lam2_re, lam2_im = _s5_fold(lam_re, lam_im, bb_re, bb_im, s5_c_re[0], s5_c_im[0])
    tile_bcast = lambda a: jnp.broadcast_to(a, (N_LANE_TILES, SUBLANES, STATES_PER_TILE))
    h = _s5_layer(x, norm_mix[0], w1, tile_bcast(lam2_re), tile_bcast(lam2_im), cx,
                  s5_d[0], s5_w_glu[0].astype(BF16), s5_b_glu[0])

    w_kv_d = jnp.concatenate([_dup_heads(w_kv[:, :hkv]), _dup_heads(w_kv[:, hkv:])], axis=-1)
    b_kv_d = jnp.concatenate([_dup_heads(b_kv[:hkv]), _dup_heads(b_kv[hkv:])], axis=-1)
    h, kv = _mlp_kv(h.reshape(bsz * seq, d), norm_mlp[0], w_mlp_in[0].astype(BF16),
                    w_mlp_out[0].astype(BF16), norm_kv, w_kv_d.astype(BF16), b_kv_d)

    h = _attn_layer(h.reshape(bsz, seq, d), kv.reshape(bsz, seq, -1), sinks[0], norm_mix[1],
                    w_q[0].astype(BF16), b_q[0], w_o[0].astype(BF16), b_o[0])

    out = _mlp_final(h.reshape(bsz * seq, d), norm_mlp[1], w_mlp_in[1].astype(BF16),
                     w_mlp_out[1].astype(BF16), norm_final)
    return out.reshape(bsz, seq, d)
```

```python
import functools
import math

import jax
import jax.numpy as jnp
from jax import lax
from jax.experimental import pallas as pl
from jax.experimental.pallas import tpu as pltpu

F32 = jnp.float32
BF16 = jnp.bfloat16

D_MODEL = 1024
BATCH = 8
SEQ = 2048
S5_GROUP = 16
S5_GROUPS = D_MODEL // S5_GROUP
S5_STATE = 64
LAMBDA_RE_MAX = -1e-4
HEAD_DIM = 64
N_Q_HEADS = D_MODEL // HEAD_DIM
N_KV_HEADS = 4
Q_PER_KV = N_Q_HEADS // N_KV_HEADS
WINDOW = 128
D_FF = 4 * D_MODEL
NORM_EPS = 1e-5

LANES = 128
SUBLANES = 8
N_LANE_TILES = D_MODEL // LANES
GROUPS_PER_TILE = LANES // S5_GROUP
STATES_PER_TILE = GROUPS_PER_TILE * S5_STATE
MIB = 1024 * 1024

S5_TL = 64
MLP_TM = 512
MLP_FF_CHUNK = 1024
ATTN_TQ = 512


def _rmsnorm(x, g):
    return x * lax.rsqrt(jnp.mean(x * x, axis=-1, keepdims=True) + NORM_EPS) * g


def _gelu_tanh(x):
    c = math.sqrt(2.0 / math.pi)
    return 0.5 * x * (1.0 + jnp.tanh(c * (x + 0.044715 * (x * x * x))))


def _const_spec(shape):
    nd = len(shape)
    return pl.BlockSpec(shape, lambda *_: (0,) * nd, pipeline_mode=pl.Buffered(1))


def _s5_prep_kernel(are_ref, aim_ref, ldt_ref, bre_ref, bim_ref,
                    lbr_ref, lbi_ref, bbr_ref, bbi_ref):
    ar = jnp.minimum(are_ref[...], LAMBDA_RE_MAX)
    ai = aim_ref[...]
    dt = jnp.exp(ldt_ref[...])
    mag = jnp.exp(ar * dt)
    ang = ai * dt
    lr = mag * jnp.cos(ang)
    li = mag * jnp.sin(ang)
    den = ar * ar + ai * ai
    nr = (lr - 1.0) * ar + li * ai
    ni = li * ar - (lr - 1.0) * ai
    cr = nr / den
    ci = ni / den
    br = bre_ref[...]
    bi = bim_ref[...]
    lbr_ref[...] = lr
    lbi_ref[...] = li
    bbr_ref[...] = cr * br - ci * bi
    bbi_ref[...] = cr * bi + ci * br


def _s5_prep(a_re, a_im, log_dt, b_re, b_im):
    g, p, c = b_re.shape
    rep = lambda a: jnp.repeat(a, c, axis=-1)
    ldt = jnp.broadcast_to(log_dt[:, None], (g, p * c))
    shp = jax.ShapeDtypeStruct((g, p * c), F32)
    lbr, lbi, bbr, bbi = pl.pallas_call(
        _s5_prep_kernel, out_shape=(shp, shp, shp, shp), name="s5_prep",
    )(rep(a_re), rep(a_im), ldt, b_re.reshape(g, p * c), b_im.reshape(g, p * c))
    pick = lambda a: a.reshape(g, p, c)[:, :, 0]
    return pick(lbr), pick(lbi), bbr.reshape(g, p, c), bbi.reshape(g, p, c)


def _blockdiag_in(b):
    eye = jnp.eye(GROUPS_PER_TILE, dtype=F32)
    b = b.reshape(N_LANE_TILES, GROUPS_PER_TILE, S5_STATE, S5_GROUP)
    return jnp.einsum('jgpc,gh->jgchp', b, eye).reshape(N_LANE_TILES, LANES, STATES_PER_TILE)


def _blockdiag_out(c):
    eye = jnp.eye(GROUPS_PER_TILE, dtype=F32)
    c = c.reshape(N_LANE_TILES, GROUPS_PER_TILE, S5_GROUP, S5_STATE)
    return jnp.einsum('jgcp,gh->jgphc', c, eye).reshape(N_LANE_TILES, STATES_PER_TILE, LANES)


def _s5_fold_kernel(wbr_ref, wbi_ref, wcr_ref, wci_ref, lrr_ref, lir_ref, lrc_ref, lic_ref,
                    w1_ref, cx_ref, l2r_ref, l2i_ref):
    wbr, wbi = wbr_ref[0], wbi_ref[0]
    wcr, wci = wcr_ref[0], wci_ref[0]
    lr, li = lrr_ref[0], lir_ref[0]
    lrc, lic = lrc_ref[0], lic_ref[0]
    hdot = functools.partial(jnp.dot, precision=lax.Precision.HIGHEST, preferred_element_type=F32)
    w1r = wbr * lr - wbi * li
    w1i = wbr * li + wbi * lr
    k0 = hdot(wbr, wcr) - hdot(wbi, wci)
    k1 = hdot(w1r, wcr) - hdot(w1i, wci)
    top = jnp.concatenate([k0, k1, w1r, w1i], axis=1)
    bot = jnp.concatenate([jnp.zeros_like(k0), k0, wbr, wbi], axis=1)
    w1_ref[0] = jnp.concatenate([top, bot], axis=0).astype(BF16)
    l2rc = lrc * lrc - lic * lic
    l2ic = 2.0 * (lrc * lic)
    cx0 = jnp.concatenate([lrc * wcr - lic * wci, -(lic * wcr + lrc * wci)], axis=0)
    cx1 = jnp.concatenate([l2rc * wcr - l2ic * wci, -(l2ic * wcr + l2rc * wci)], axis=0)
    cx_ref[0] = jnp.concatenate([cx0, cx1], axis=1).astype(BF16)
    l2r_ref[0] = lr * lr - li * li
    l2i_ref[0] = 2.0 * (lr * li)


def _s5_fold(lam_re, lam_im, bb_re, bb_im, c_re, c_im):
    nt, sp = N_LANE_TILES, STATES_PER_TILE
    tile = lambda shape: pl.BlockSpec((1,) + shape, lambda j: (j, 0, 0))
    row = lambda a: a.reshape(nt, 1, sp)
    col = lambda a: a.reshape(nt, sp, 1)
    return pl.pallas_call(
        _s5_fold_kernel,
        out_shape=(jax.ShapeDtypeStruct((nt, 2 * LANES, 2 * LANES + 2 * sp), BF16),
                   jax.ShapeDtypeStruct((nt, 2 * sp, 2 * LANES), BF16),
                   jax.ShapeDtypeStruct((nt, 1, sp), F32),
                   jax.ShapeDtypeStruct((nt, 1, sp), F32)),
        grid=(nt,),
        in_specs=[tile((LANES, sp)), tile((LANES, sp)), tile((sp, LANES)), tile((sp, LANES)),
                  tile((1, sp)), tile((1, sp)), tile((sp, 1)), tile((sp, 1))],
        out_specs=(tile((2 * LANES, 2 * LANES + 2 * sp)), tile((2 * sp, 2 * LANES)),
                   tile((1, sp)), tile((1, sp))),
        compiler_params=pltpu.CompilerParams(dimension_semantics=("parallel",)),
        name="s5_fold",
    )(_blockdiag_in(bb_re), _blockdiag_in(bb_im), _blockdiag_out(c_re), _blockdiag_out(c_im),
      row(lam_re), row(lam_im), col(lam_re), col(lam_im))


def _s5_kernel(x_ref, g_ref, w1_ref, l2r_ref, l2i_ref, cx_ref, d_ref, wglu_ref, bglu_ref,
               o_ref, hbm_s, hn_s, hnu_s, zy_s, bu_s, xs_s, y_s, st_s, res_s, *, tl):
    half = tl // 2
    prow = half * SUBLANES
    sp = STATES_PER_TILE
    pitch = tl + SUBLANES

    @pl.when(pl.program_id(0) == 0)
    def _():
        st_s[...] = jnp.zeros_like(st_s)

    for b in range(SUBLANES):
        hn = _rmsnorm(x_ref[b], g_ref[...])
        for j in range(N_LANE_TILES):
            hbm_s[j, b * pitch:b * pitch + tl, :] = hn[:, j * LANES:(j + 1) * LANES]

    for j in range(N_LANE_TILES):
        lanes = slice(j * LANES, (j + 1) * LANES)
        for m in range(tl // 4):
            v = [hbm_s[j, pl.ds(4 * m + i, SUBLANES, stride=pitch), :] for i in range(4)]
            for i in range(4):
                g = (i % 2) * half + 2 * m + i // 2
                hn_s[g * SUBLANES:(g + 1) * SUBLANES, lanes] = v[i]
            r16 = slice(2 * m * SUBLANES, (2 * m + 2) * SUBLANES)
            hnu_s[r16, 2 * j * LANES:(2 * j + 1) * LANES] = (
                jnp.concatenate([v[0], v[2]], axis=0).astype(BF16))
            hnu_s[r16, (2 * j + 1) * LANES:(2 * j + 2) * LANES] = (
                jnp.concatenate([v[1], v[3]], axis=0).astype(BF16))

    def project_in(j):
        z = jnp.dot(hnu_s[:, 2 * j * LANES:(2 * j + 2) * LANES], w1_ref[j],
                    preferred_element_type=F32)
        zy_s[j % 2] = z[:, :2 * LANES]
        bu_s[j % 2] = z[:, 2 * LANES:]

    project_in(0)
    for j in range(N_LANE_TILES):
        if j + 1 < N_LANE_TILES:
            project_in(j + 1)
        p = j % 2
        lanes = slice(j * LANES, (j + 1) * LANES)
        ar = l2r_ref[j]
        ai = l2i_ref[j]
        xr = st_s[j, :, 0:sp]
        xi = st_s[j, :, sp:2 * sp]
        for k in range(half):
            r = slice(k * SUBLANES, (k + 1) * SUBLANES)
            xs_s[p, r, 0:sp] = xr
            xs_s[p, r, sp:2 * sp] = xi
            nxr = ar * xr - ai * xi + bu_s[p, r, 0:sp]
            nxi = ar * xi + ai * xr + bu_s[p, r, sp:2 * sp]
            xr, xi = nxr, nxi
        st_s[j, :, 0:sp] = xr
        st_s[j, :, sp:2 * sp] = xi
        y = zy_s[p] + jnp.dot(xs_s[p].astype(BF16), cx_ref[j], preferred_element_type=F32)
        y_s[0:prow, lanes] = y[:, :LANES]
        y_s[prow:2 * prow, lanes] = y[:, LANES:]

    for par in range(2):
        rs = slice(par * prow, (par + 1) * prow)
        y = y_s[rs, :] + d_ref[...] * hn_s[rs, :]
        z = jnp.dot(_gelu_tanh(y).astype(BF16), wglu_ref[...].astype(BF16),
                    preferred_element_type=F32) + bglu_ref[...]
        mix = z[:, :D_MODEL] * (1.0 / (1.0 + jnp.exp(-z[:, D_MODEL:])))
        for k in range(half):
            t = 2 * k + par
            for j in range(N_LANE_TILES):
                res_s[j, t * SUBLANES:(t + 1) * SUBLANES, :] = (
                    mix[k * SUBLANES:(k + 1) * SUBLANES, j * LANES:(j + 1) * LANES])

    for b in range(SUBLANES):
        for j in range(N_LANE_TILES):
            lanes = slice(j * LANES, (j + 1) * LANES)
            o_ref[b, :, lanes] = x_ref[b, :, lanes] + res_s[j, pl.ds(b, tl, stride=SUBLANES), :]


def _s5_layer(x, g_mix, w1, lam2_re_t, lam2_im_t, cx, d_skip, w_glu, b_glu):
    bsz, seq, d = x.shape
    tl = S5_TL
    rows = tl * bsz
    prow = rows // 2
    kernel = functools.partial(_s5_kernel, tl=tl)
    return pl.pallas_call(
        kernel,
        out_shape=jax.ShapeDtypeStruct((bsz, seq, d), F32),
        grid=(seq // tl,),
        in_specs=[
            pl.BlockSpec((bsz, tl, d), lambda i: (0, i, 0)),
            _const_spec((1, d)),
            _const_spec(w1.shape),
            _const_spec(lam2_re_t.shape),
            _const_spec(lam2_im_t.shape),
            _const_spec(cx.shape),
            _const_spec((1, d)),
            _const_spec(w_glu.shape),
            _const_spec((1, 2 * d)),
        ],
        out_specs=pl.BlockSpec((bsz, tl, d), lambda i: (0, i, 0)),
        scratch_shapes=[
            pltpu.VMEM((N_LANE_TILES, bsz * (tl + SUBLANES), LANES), F32),
            pltpu.VMEM((rows, d), F32),
            pltpu.VMEM((prow, 2 * d), BF16),
            pltpu.VMEM((2, prow, 2 * LANES), F32),
            pltpu.VMEM((2, prow, 2 * STATES_PER_TILE), F32),
            pltpu.VMEM((2, prow, 2 * STATES_PER_TILE), F32),
            pltpu.VMEM((rows, d), F32),
            pltpu.VMEM((N_LANE_TILES, SUBLANES, 2 * STATES_PER_TILE), F32),
            pltpu.VMEM((N_LANE_TILES, rows, LANES), F32),
        ],
        compiler_params=pltpu.CompilerParams(
            dimension_semantics=("arbitrary",), vmem_limit_bytes=56 * MIB),
        name="s5_layer",
    )(x, g_mix.reshape(1, d), w1, lam2_re_t, lam2_im_t, cx, d_skip.reshape(1, d), w_glu,
      b_glu.reshape(1, 2 * d))


def _mlp_body(h, g, win_ref, wout_ref):
    hn = _rmsnorm(h, g).astype(BF16)
    acc = jnp.zeros(h.shape, F32)
    for c in range(D_FF // MLP_FF_CHUNK):
        cols = slice(c * MLP_FF_CHUNK, (c + 1) * MLP_FF_CHUNK)
        a = jnp.dot(hn, win_ref[:, cols].astype(BF16), preferred_element_type=F32)
        a = jnp.square(jnp.maximum(a, 0.0)).astype(BF16)
        acc = acc + jnp.dot(a, wout_ref[cols, :].astype(BF16), preferred_element_type=F32)
    return h + acc


def _mlp_kv_kernel(h_ref, g_ref, win_ref, wout_ref, gkv_ref, wkv_ref, bkv_ref, o_ref, kv_ref):
    out = _mlp_body(h_ref[...], g_ref[...], win_ref, wout_ref)
    o_ref[...] = out
    kvn = _rmsnorm(out, gkv_ref[...]).astype(BF16)
    kv = jnp.dot(kvn, wkv_ref[...].astype(BF16), preferred_element_type=F32) + bkv_ref[...]
    kv_ref[...] = kv.astype(BF16)


def _mlp_final_kernel(h_ref, g_ref, win_ref, wout_ref, gfin_ref, o_ref):
    out = _mlp_body(h_ref[...], g_ref[...], win_ref, wout_ref)
    o_ref[...] = _rmsnorm(out, gfin_ref[...])


def _mlp_kv(h, g, w_in, w_out, g_kv, w_kv, b_kv):
    t, d = h.shape
    nkv = w_kv.shape[1]
    row = lambda n: pl.BlockSpec((MLP_TM, n), lambda i: (i, 0))
    return pl.pallas_call(
        _mlp_kv_kernel,
        out_shape=(jax.ShapeDtypeStruct((t, d), F32), jax.ShapeDtypeStruct((t, nkv), BF16)),
        grid=(t // MLP_TM,),
        in_specs=[row(d), _const_spec((1, d)), _const_spec(w_in.shape), _const_spec(w_out.shape),
                  _const_spec((1, d)), _const_spec(w_kv.shape), _const_spec((1, nkv))],
        out_specs=(row(d), row(nkv)),
        compiler_params=pltpu.CompilerParams(
            dimension_semantics=("parallel",), vmem_limit_bytes=56 * MIB),
        name="mlp_kv",
    )(h, g.reshape(1, d), w_in, w_out, g_kv.reshape(1, d), w_kv, b_kv.reshape(1, nkv))


def _mlp_final(h, g, w_in, w_out, g_fin):
    t, d = h.shape
    row = lambda n: pl.BlockSpec((MLP_TM, n), lambda i: (i, 0))
    return pl.pallas_call(
        _mlp_final_kernel,
        out_shape=jax.ShapeDtypeStruct((t, d), F32),
        grid=(t // MLP_TM,),
        in_specs=[row(d), _const_spec((1, d)), _const_spec(w_in.shape), _const_spec(w_out.shape),
                  _const_spec((1, d))],
        out_specs=row(d),
        compiler_params=pltpu.CompilerParams(
            dimension_semantics=("parallel",), vmem_limit_bytes=56 * MIB),
        name="mlp_final",
    )(h, g.reshape(1, d), w_in, w_out, g_fin.reshape(1, d))


def _attn_kernel(sink_ref, h_ref, kvp_ref, kvc_ref, g_ref, wq_ref, bq_ref, wo_ref, bo_ref,
                 o_ref, q_s, oh_s, *, tq):
    nsb = tq // WINDOW
    n = pl.program_id(1)
    h = h_ref[0]
    hn = _rmsnorm(h, g_ref[...]).astype(BF16)
    q = jnp.dot(hn, wq_ref[...].astype(BF16), preferred_element_type=F32) + bq_ref[...]
    q_s[...] = (q * (1.0 / math.sqrt(HEAD_DIM))).astype(BF16)

    lo = lax.broadcasted_iota(jnp.int32, (WINDOW, LANES), 1) < HEAD_DIM
    qi = lax.broadcasted_iota(jnp.int32, (WINDOW, 2 * WINDOW), 0)
    kj = lax.broadcasted_iota(jnp.int32, (WINDOW, 2 * WINDOW), 1)
    diff = qi + WINDOW - kj
    band = (diff >= 0) & (diff < WINDOW)
    n_kv_tiles = N_KV_HEADS // 2
    zero = jnp.zeros((WINDOW, LANES), BF16)

    for sb in range(nsb):
        rows = slice(sb * WINDOW, (sb + 1) * WINDOW)
        first = (n * nsb + sb) == 0
        valid = band & jnp.logical_or(jnp.logical_not(first), kj >= WINDOW)
        bias = jnp.where(valid, 0.0, -jnp.inf).astype(F32)
        for a in range(n_kv_tiles):
            kl = slice(a * LANES, (a + 1) * LANES)
            vl = slice((n_kv_tiles + a) * LANES, (n_kv_tiles + a + 1) * LANES)
            if sb == 0:
                kprev, vprev = kvp_ref[0, :, kl], kvp_ref[0, :, vl]
            else:
                prow = slice((sb - 1) * WINDOW, sb * WINDOW)
                kprev, vprev = kvc_ref[0, prow, kl], kvc_ref[0, prow, vl]
            kd = jnp.concatenate([kprev, kvc_ref[0, rows, kl]], axis=0)
            vd = jnp.concatenate([vprev, kvc_ref[0, rows, vl]], axis=0)
            qt = [q_s[rows, (a * Q_PER_KV + g) * LANES:(a * Q_PER_KV + g + 1) * LANES]
                  for g in range(Q_PER_KV)]
            lhs = jnp.concatenate([jnp.where(lo, t, zero) for t in qt]
                                  + [jnp.where(lo, zero, t) for t in qt], axis=0)
            s = lax.dot_general(lhs, kd, (((1,), (1,)), ((), ())),
                                preferred_element_type=F32)
            ps, rs = [], []
            for half in range(2):
                for g in range(Q_PER_KV):
                    blk = half * Q_PER_KV + g
                    sg = s[blk * WINDOW:(blk + 1) * WINDOW] + bias
                    sink = sink_ref[(2 * a + half) * Q_PER_KV + g]
                    m = jnp.maximum(jnp.max(sg, axis=-1, keepdims=True), sink)
                    p = jnp.exp(sg - m)
                    den = jnp.sum(p, axis=-1, keepdims=True) + jnp.exp(sink - m)
                    ps.append(p.astype(BF16))
                    rs.append(1.0 / den)
            od = jnp.dot(jnp.concatenate(ps, axis=0), vd, preferred_element_type=F32)
            og = [od[blk * WINDOW:(blk + 1) * WINDOW] * rs[blk] for blk in range(2 * Q_PER_KV)]
            for g in range(Q_PER_KV):
                oh_s[rows, (a * Q_PER_KV + g) * LANES:(a * Q_PER_KV + g + 1) * LANES] = (
                    jnp.where(lo, og[g], og[Q_PER_KV + g]).astype(BF16))

    o_ref[0] = (h + jnp.dot(oh_s[...], wo_ref[...].astype(BF16), preferred_element_type=F32)
                + bo_ref[...])


def _attn_layer(h, kv, sinks, g, w_q, b_q, w_o, b_o):
    bsz, seq, d = h.shape
    nkv = kv.shape[-1]
    tq = ATTN_TQ
    nsb = tq // WINDOW
    kernel = functools.partial(_attn_kernel, tq=tq)
    return pl.pallas_call(
        kernel,
        out_shape=jax.ShapeDtypeStruct((bsz, seq, d), F32),
        grid=(bsz, seq // tq),
        in_specs=[
            pl.BlockSpec(memory_space=pltpu.SMEM),
            pl.BlockSpec((1, tq, d), lambda b, n: (b, n, 0)),
            pl.BlockSpec((1, WINDOW, nkv), lambda b, n: (b, jnp.maximum(n * nsb - 1, 0), 0)),
            pl.BlockSpec((1, tq, nkv), lambda b, n: (b, n, 0)),
            _const_spec((1, d)),
            _const_spec(w_q.shape),
            _const_spec((1, d)),
            _const_spec(w_o.shape),
            _const_spec((1, d)),
        ],
        out_specs=pl.BlockSpec((1, tq, d), lambda b, n: (b, n, 0)),
        scratch_shapes=[pltpu.VMEM((tq, d), BF16), pltpu.VMEM((tq, d), BF16)],
        compiler_params=pltpu.CompilerParams(
            dimension_semantics=("parallel", "parallel"), vmem_limit_bytes=40 * MIB),
        name="attn",
    )(sinks, h, kv, kv, g.reshape(1, d), w_q, b_q.reshape(1, d), w_o, b_o.reshape(1, d))


def _pair_heads(w, axis):
    shape = w.shape
    split = shape[:axis] + (N_KV_HEADS // 2, 2, Q_PER_KV, HEAD_DIM) + shape[axis + 1:]
    return jnp.swapaxes(w.reshape(split), axis + 1, axis + 2).reshape(shape)


def kernel(x, norm_mix, norm_mlp, norm_kv, norm_final, s5_a_re, s5_a_im, s5_log_dt, s5_b_re, s5_b_im, s5_c_re, s5_c_im, s5_d, s5_w_glu, s5_b_glu, w_kv, b_kv, w_q, b_q, sinks, w_o, b_o, w_mlp_in, w_mlp_out):
    bsz, seq, d = x.shape

    lam_re, lam_im, bb_re, bb_im = _s5_prep(s5_a_re[0], s5_a_im[0], s5_log_dt[0],
                                            s5_b_re[0], s5_b_im[0])
    w1, cx, lam2_re, lam2_im = _s5_fold(lam_re, lam_im, bb_re, bb_im, s5_c_re[0], s5_c_im[0])
    tile_bcast = lambda a: jnp.broadcast_to(a, (N_LANE_TILES, SUBLANES, STATES_PER_TILE))
    h = _s5_layer(x, norm_mix[0], w1, tile_bcast(lam2_re), tile_bcast(lam2_im), cx,
                  s5_d[0], s5_w_glu[0], s5_b_glu[0])

    h, kv = _mlp_kv(h.reshape(bsz * seq, d), norm_mlp[0], w_mlp_in[0], w_mlp_out[0],
                    norm_kv, w_kv, b_kv)

    h = _attn_layer(h.reshape(bsz, seq, d), kv.reshape(bsz, seq, -1), sinks[0], norm_mix[1],
                    _pair_heads(w_q[0], 1), _pair_heads(b_q[0], 0), _pair_heads(w_o[0], 0), b_o[0])

    out = _mlp_final(h.reshape(bsz * seq, d), norm_mlp[1], w_mlp_in[1],
                     w_mlp_out[1], norm_final)
    return out.reshape(bsz, seq, d)
```

```python
import functools
import math

import jax
import jax.numpy as jnp
from jax import lax
from jax.experimental import pallas as pl
from jax.experimental.pallas import tpu as pltpu

F32 = jnp.float32
BF16 = jnp.bfloat16

D_MODEL = 1024
BATCH = 8
SEQ = 2048
S5_GROUP = 16
S5_GROUPS = D_MODEL // S5_GROUP
S5_STATE = 64
LAMBDA_RE_MAX = -1e-4
HEAD_DIM = 64
N_Q_HEADS = D_MODEL // HEAD_DIM
N_KV_HEADS = 4
Q_PER_KV = N_Q_HEADS // N_KV_HEADS
WINDOW = 128
D_FF = 4 * D_MODEL
NORM_EPS = 1e-5

LANES = 128
SUBLANES = 8
N_LANE_TILES = D_MODEL // LANES
GROUPS_PER_TILE = LANES // S5_GROUP
STATES_PER_TILE = GROUPS_PER_TILE * S5_STATE
MIB = 1024 * 1024

S5_TL = 64
MLP_TM = 512
MLP_FF_CHUNK = 1024
ATTN_TQ = 512


def _rmsnorm(x, g):
    return x * lax.rsqrt(jnp.mean(x * x, axis=-1, keepdims=True) + NORM_EPS) * g


def _gelu_tanh(x):
    c = math.sqrt(2.0 / math.pi)
    return 0.5 * x * (1.0 + jnp.tanh(c * (x + 0.044715 * (x * x * x))))


def _const_spec(shape):
    nd = len(shape)
    return pl.BlockSpec(shape, lambda *_: (0,) * nd, pipeline_mode=pl.Buffered(1))


def _layer_spec(shape, layer):
    nd = len(shape) - 1
    return pl.BlockSpec((None,) + tuple(shape[1:]), lambda *_: (layer,) + (0,) * nd,
                        pipeline_mode=pl.Buffered(1))


def _s5_prep_kernel(are_ref, aim_ref, ldt_ref, bre_ref, bim_ref,
                    lbr_ref, lbi_ref, bbr_ref, bbi_ref):
    ar = jnp.minimum(are_ref[...], LAMBDA_RE_MAX)
    ai = aim_ref[...]
    dt = jnp.exp(ldt_ref[...])
    mag = jnp.exp(ar * dt)
    ang = ai * dt
    lr = mag * jnp.cos(ang)
    li = mag * jnp.sin(ang)
    den = ar * ar + ai * ai
    nr = (lr - 1.0) * ar + li * ai
    ni = li * ar - (lr - 1.0) * ai
    cr = nr / den
    ci = ni / den
    br = bre_ref[...]
    bi = bim_ref[...]
    lbr_ref[...] = lr
    lbi_ref[...] = li
    bbr_ref[...] = cr * br - ci * bi
    bbi_ref[...] = cr * bi + ci * br


def _s5_prep(a_re, a_im, log_dt, b_re, b_im):
    g, p, c = b_re.shape
    rep = lambda a: jnp.repeat(a, c, axis=-1)
    ldt = jnp.broadcast_to(log_dt[:, None], (g, p * c))
    shp = jax.ShapeDtypeStruct((g, p * c), F32)
    lbr, lbi, bbr, bbi = pl.pallas_call(
        _s5_prep_kernel, out_shape=(shp, shp, shp, shp), name="s5_prep",
    )(rep(a_re), rep(a_im), ldt, b_re.reshape(g, p * c), b_im.reshape(g, p * c))
    pick = lambda a: a.reshape(g, p, c)[:, :, 0]
    return pick(lbr), pick(lbi), bbr.reshape(g, p, c), bbi.reshape(g, p, c)


def _blockdiag_in(b):
    eye = jnp.eye(GROUPS_PER_TILE, dtype=F32)
    b = b.reshape(N_LANE_TILES, GROUPS_PER_TILE, S5_STATE, S5_GROUP)
    return jnp.einsum('jgpc,gh->jgchp', b, eye).reshape(N_LANE_TILES, LANES, STATES_PER_TILE)


def _blockdiag_out(c):
    eye = jnp.eye(GROUPS_PER_TILE, dtype=F32)
    c = c.reshape(N_LANE_TILES, GROUPS_PER_TILE, S5_GROUP, S5_STATE)
    return jnp.einsum('jgcp,gh->jgphc', c, eye).reshape(N_LANE_TILES, STATES_PER_TILE, LANES)


def _s5_fold_kernel(wbr_ref, wbi_ref, wcr_ref, wci_ref, lrr_ref, lir_ref, lrc_ref, lic_ref,
                    w1_ref, cx_ref, l2r_ref, l2i_ref):
    wbr, wbi = wbr_ref[0], wbi_ref[0]
    wcr, wci = wcr_ref[0], wci_ref[0]
    lr, li = lrr_ref[0], lir_ref[0]
    lrc, lic = lrc_ref[0], lic_ref[0]
    hdot = functools.partial(jnp.dot, precision=lax.Precision.HIGHEST, preferred_element_type=F32)
    w1r = wbr * lr - wbi * li
    w1i = wbr * li + wbi * lr
    k0 = hdot(wbr, wcr) - hdot(wbi, wci)
    k1 = hdot(w1r, wcr) - hdot(w1i, wci)
    top = jnp.concatenate([k0, k1, w1r, w1i], axis=1)
    bot = jnp.concatenate([jnp.zeros_like(k0), k0, wbr, wbi], axis=1)
    w1_ref[0] = jnp.concatenate([top, bot], axis=0).astype(BF16)
    l2rc = lrc * lrc - lic * lic
    l2ic = 2.0 * (lrc * lic)
    cx0 = jnp.concatenate([lrc * wcr - lic * wci, -(lic * wcr + lrc * wci)], axis=0)
    cx1 = jnp.concatenate([l2rc * wcr - l2ic * wci, -(l2ic * wcr + l2rc * wci)], axis=0)
    cx_ref[0] = jnp.concatenate([cx0, cx1], axis=1).astype(BF16)
    l2r_ref[0] = lr * lr - li * li
    l2i_ref[0] = 2.0 * (lr * li)


def _s5_fold(lam_re, lam_im, bb_re, bb_im, c_re, c_im):
    nt, sp = N_LANE_TILES, STATES_PER_TILE
    tile = lambda shape: pl.BlockSpec((1,) + shape, lambda j: (j, 0, 0))
    row = lambda a: a.reshape(nt, 1, sp)
    col = lambda a: a.reshape(nt, sp, 1)
    return pl.pallas_call(
        _s5_fold_kernel,
        out_shape=(jax.ShapeDtypeStruct((nt, 2 * LANES, 2 * LANES + 2 * sp), BF16),
                   jax.ShapeDtypeStruct((nt, 2 * sp, 2 * LANES), BF16),
                   jax.ShapeDtypeStruct((nt, 1, sp), F32),
                   jax.ShapeDtypeStruct((nt, 1, sp), F32)),
        grid=(nt,),
        in_specs=[tile((LANES, sp)), tile((LANES, sp)), tile((sp, LANES)), tile((sp, LANES)),
                  tile((1, sp)), tile((1, sp)), tile((sp, 1)), tile((sp, 1))],
        out_specs=(tile((2 * LANES, 2 * LANES + 2 * sp)), tile((2 * sp, 2 * LANES)),
                   tile((1, sp)), tile((1, sp))),
        compiler_params=pltpu.CompilerParams(dimension_semantics=("parallel",)),
        name="s5_fold",
    )(_blockdiag_in(bb_re), _blockdiag_in(bb_im), _blockdiag_out(c_re), _blockdiag_out(c_im),
      row(lam_re), row(lam_im), col(lam_re), col(lam_im))


def _s5_kernel(x_ref, g_ref, w1_ref, l2r_ref, l2i_ref, cx_ref, d_ref, wglu_ref, bglu_ref,
               o_ref, hbm_s, hn_s, hnu_s, zy_s, bu_s, xs_s, y_s, st_s, res_s, *, tl):
    half = tl // 2
    prow = half * SUBLANES
    sp = STATES_PER_TILE
    pitch = tl + SUBLANES

    @pl.when(pl.program_id(0) == 0)
    def _():
        st_s[...] = jnp.zeros_like(st_s)

    for b in range(SUBLANES):
        hn = _rmsnorm(x_ref[b], g_ref[...])
        for j in range(N_LANE_TILES):
            hbm_s[j, b * pitch:b * pitch + tl, :] = hn[:, j * LANES:(j + 1) * LANES]

    for j in range(N_LANE_TILES):
        lanes = slice(j * LANES, (j + 1) * LANES)
        for m in range(tl // 4):
            v = [hbm_s[j, pl.ds(4 * m + i, SUBLANES, stride=pitch), :] for i in range(4)]
            for i in range(4):
                g = (i % 2) * half + 2 * m + i // 2
                hn_s[g * SUBLANES:(g + 1) * SUBLANES, lanes] = v[i]
            r16 = slice(2 * m * SUBLANES, (2 * m + 2) * SUBLANES)
            hnu_s[r16, 2 * j * LANES:(2 * j + 1) * LANES] = (
                jnp.concatenate([v[0], v[2]], axis=0).astype(BF16))
            hnu_s[r16, (2 * j + 1) * LANES:(2 * j + 2) * LANES] = (
                jnp.concatenate([v[1], v[3]], axis=0).astype(BF16))

    def project_in(j):
        z = jnp.dot(hnu_s[:, 2 * j * LANES:(2 * j + 2) * LANES], w1_ref[j],
                    preferred_element_type=F32)
        zy_s[j % 2] = z[:, :2 * LANES]
        bu_s[j % 2] = z[:, 2 * LANES:]

    project_in(0)
    for j in range(N_LANE_TILES):
        if j + 1 < N_LANE_TILES:
            project_in(j + 1)
        p = j % 2
        lanes = slice(j * LANES, (j + 1) * LANES)
        ar = l2r_ref[j]
        ai = l2i_ref[j]
        xr = st_s[j, :, 0:sp]
        xi = st_s[j, :, sp:2 * sp]
        for k in range(half):
            r = slice(k * SUBLANES, (k + 1) * SUBLANES)
            xs_s[p, r, 0:sp] = xr
            xs_s[p, r, sp:2 * sp] = xi
            nxr = ar * xr - ai * xi + bu_s[p, r, 0:sp]
            nxi = ar * xi + ai * xr + bu_s[p, r, sp:2 * sp]
            xr, xi = nxr, nxi
        st_s[j, :, 0:sp] = xr
        st_s[j, :, sp:2 * sp] = xi
        y = zy_s[p] + jnp.dot(xs_s[p].astype(BF16), cx_ref[j], preferred_element_type=F32)
        y_s[0:prow, lanes] = y[:, :LANES]
        y_s[prow:2 * prow, lanes] = y[:, LANES:]

    for par in range(2):
        rs = slice(par * prow, (par + 1) * prow)
        y = y_s[rs, :] + d_ref[...] * hn_s[rs, :]
        z = jnp.dot(_gelu_tanh(y).astype(BF16), wglu_ref[...].astype(BF16),
                    preferred_element_type=F32) + bglu_ref[...]
        mix = z[:, :D_MODEL] * (1.0 / (1.0 + jnp.exp(-z[:, D_MODEL:])))
        for k in range(half):
            t = 2 * k + par
            for j in range(N_LANE_TILES):
                res_s[j, t * SUBLANES:(t + 1) * SUBLANES, :] = (
                    mix[k * SUBLANES:(k + 1) * SUBLANES, j * LANES:(j + 1) * LANES])

    for b in range(SUBLANES):
        for j in range(N_LANE_TILES):
            lanes = slice(j * LANES, (j + 1) * LANES)
            o_ref[b, :, lanes] = x_ref[b, :, lanes] + res_s[j, pl.ds(b, tl, stride=SUBLANES), :]


def _s5_layer(x, g_mix, w1, lam2_re_t, lam2_im_t, cx, d_skip, w_glu, b_glu):
    bsz, seq, d = x.shape
    tl = S5_TL
    rows = tl * bsz
    prow = rows // 2
    kernel = functools.partial(_s5_kernel, tl=tl)
    return pl.pallas_call(
        kernel,
        out_shape=jax.ShapeDtypeStruct((bsz, seq, d), F32),
        grid=(seq // tl,),
        in_specs=[
            pl.BlockSpec((bsz, tl, d), lambda i: (0, i, 0)),
            _const_spec((1, d)),
            _const_spec(w1.shape),
            _const_spec(lam2_re_t.shape),
            _const_spec(lam2_im_t.shape),
            _const_spec(cx.shape),
            _const_spec((1, d)),
            _const_spec(w_glu.shape),
            _const_spec((1, 2 * d)),
        ],
        out_specs=pl.BlockSpec((bsz, tl, d), lambda i: (0, i, 0)),
        scratch_shapes=[
            pltpu.VMEM((N_LANE_TILES, bsz * (tl + SUBLANES), LANES), F32),
            pltpu.VMEM((rows, d), F32),
            pltpu.VMEM((prow, 2 * d), BF16),
            pltpu.VMEM((2, prow, 2 * LANES), F32),
            pltpu.VMEM((2, prow, 2 * STATES_PER_TILE), F32),
            pltpu.VMEM((2, prow, 2 * STATES_PER_TILE), F32),
            pltpu.VMEM((rows, d), F32),
            pltpu.VMEM((N_LANE_TILES, SUBLANES, 2 * STATES_PER_TILE), F32),
            pltpu.VMEM((N_LANE_TILES, rows, LANES), F32),
        ],
        compiler_params=pltpu.CompilerParams(
            dimension_semantics=("arbitrary",), vmem_limit_bytes=56 * MIB),
        name="s5_layer",
    )(x, g_mix.reshape(1, d), w1, lam2_re_t, lam2_im_t, cx, d_skip.reshape(1, d), w_glu,
      b_glu.reshape(1, 2 * d))


def _mlp_body(h, g, win_ref, wout_ref):
    hn = _rmsnorm(h, g).astype(BF16)
    acc = jnp.zeros(h.shape, F32)
    for c in range(D_FF // MLP_FF_CHUNK):
        cols = slice(c * MLP_FF_CHUNK, (c + 1) * MLP_FF_CHUNK)
        a = jnp.dot(hn, win_ref[:, cols].astype(BF16), preferred_element_type=F32)
        a = jnp.square(jnp.maximum(a, 0.0)).astype(BF16)
        acc = acc + jnp.dot(a, wout_ref[cols, :].astype(BF16), preferred_element_type=F32)
    return h + acc


def _mlp_kv_kernel(h_ref, g_ref, win_ref, wout_ref, gkv_ref, wkv_ref, bkv_ref, o_ref, kv_ref):
    out = _mlp_body(h_ref[...], g_ref[...], win_ref, wout_ref)
    o_ref[...] = out
    kvn = _rmsnorm(out, gkv_ref[...]).astype(BF16)
    kv = jnp.dot(kvn, wkv_ref[...].astype(BF16), preferred_element_type=F32) + bkv_ref[...]
    kv_ref[...] = kv.astype(BF16)


def _mlp_final_kernel(h_ref, g_ref, win_ref, wout_ref, gfin_ref, o_ref):
    out = _mlp_body(h_ref[...], g_ref[...], win_ref, wout_ref)
    o_ref[...] = _rmsnorm(out, gfin_ref[...])


def _mlp_kv(h, g, w_in, w_out, layer, g_kv, w_kv, b_kv):
    t, d = h.shape
    nkv = w_kv.shape[1]
    row = lambda n: pl.BlockSpec((MLP_TM, n), lambda i: (i, 0))
    return pl.pallas_call(
        _mlp_kv_kernel,
        out_shape=(jax.ShapeDtypeStruct((t, d), F32), jax.ShapeDtypeStruct((t, nkv), BF16)),
        grid=(t // MLP_TM,),
        in_specs=[row(d), _const_spec((1, d)), _layer_spec(w_in.shape, layer),
                  _layer_spec(w_out.shape, layer),
                  _const_spec((1, d)), _const_spec(w_kv.shape), _const_spec((1, nkv))],
        out_specs=(row(d), row(nkv)),
        compiler_params=pltpu.CompilerParams(
            dimension_semantics=("parallel",), vmem_limit_bytes=56 * MIB),
        name="mlp_kv",
    )(h, g.reshape(1, d), w_in, w_out, g_kv.reshape(1, d), w_kv, b_kv.reshape(1, nkv))


def _mlp_final(h, g, w_in, w_out, layer, g_fin):
    t, d = h.shape
    row = lambda n: pl.BlockSpec((MLP_TM, n), lambda i: (i, 0))
    return pl.pallas_call(
        _mlp_final_kernel,
        out_shape=jax.ShapeDtypeStruct((t, d), F32),
        grid=(t // MLP_TM,),
        in_specs=[row(d), _const_spec((1, d)), _layer_spec(w_in.shape, layer),
                  _layer_spec(w_out.shape, layer),
                  _const_spec((1, d))],
        out_specs=row(d),
        compiler_params=pltpu.CompilerParams(
            dimension_semantics=("parallel",), vmem_limit_bytes=56 * MIB),
        name="mlp_final",
    )(h, g.reshape(1, d), w_in, w_out, g_fin.reshape(1, d))


def _attn_kernel(sink_ref, h_ref, kvp_ref, kvc_ref, g_ref, wq_ref, bq_ref, wo_ref, bo_ref,
                 o_ref, q_s, oh_s, *, tq):
    nsb = tq // WINDOW
    n = pl.program_id(1)
    h = h_ref[0]
    hn = _rmsnorm(h, g_ref[...]).astype(BF16)
    q = jnp.dot(hn, wq_ref[...].astype(BF16), preferred_element_type=F32) + bq_ref[...]
    q_s[...] = (q * (1.0 / math.sqrt(HEAD_DIM))).astype(BF16)

    lo = lax.broadcasted_iota(jnp.int32, (WINDOW, LANES), 1) < HEAD_DIM
    qi = lax.broadcasted_iota(jnp.int32, (WINDOW, 2 * WINDOW), 0)
    kj = lax.broadcasted_iota(jnp.int32, (WINDOW, 2 * WINDOW), 1)
    diff = qi + WINDOW - kj
    band = (diff >= 0) & (diff < WINDOW)
    n_kv_tiles = N_KV_HEADS // 2
    zero = jnp.zeros((WINDOW, LANES), BF16)

    for sb in range(nsb):
        rows = slice(sb * WINDOW, (sb + 1) * WINDOW)
        first = (n * nsb + sb) == 0
        valid = band & jnp.logical_or(jnp.logical_not(first), kj >= WINDOW)
        bias = jnp.where(valid, 0.0, -jnp.inf).astype(F32)
        for a in range(n_kv_tiles):
            kl = slice(a * LANES, (a + 1) * LANES)
            vl = slice((n_kv_tiles + a) * LANES, (n_kv_tiles + a + 1) * LANES)
            if sb == 0:
                kprev, vprev = kvp_ref[0, :, kl], kvp_ref[0, :, vl]
            else:
                prow = slice((sb - 1) * WINDOW, sb * WINDOW)
                kprev, vprev = kvc_ref[0, prow, kl], kvc_ref[0, prow, vl]
            kd = jnp.concatenate([kprev, kvc_ref[0, rows, kl]], axis=0)
            vd = jnp.concatenate([vprev, kvc_ref[0, rows, vl]], axis=0)
            qt = [q_s[rows, (a * Q_PER_KV + g) * LANES:(a * Q_PER_KV + g + 1) * LANES]
                  for g in range(Q_PER_KV)]
            lhs = jnp.concatenate([jnp.where(lo, t, zero) for t in qt]
                                  + [jnp.where(lo, zero, t) for t in qt], axis=0)
            s = lax.dot_general(lhs, kd, (((1,), (1,)), ((), ())),
                                preferred_element_type=F32)
            ps, rs = [], []
            for half in range(2):
                for g in range(Q_PER_KV):
                    blk = half * Q_PER_KV + g
                    sg = s[blk * WINDOW:(blk + 1) * WINDOW] + bias
                    sink = sink_ref[(2 * a + half) * Q_PER_KV + g]
                    m = jnp.maximum(jnp.max(sg, axis=-1, keepdims=True), sink)
                    p = jnp.exp(sg - m)
                    den = jnp.sum(p, axis=-1, keepdims=True) + jnp.exp(sink - m)
                    ps.append(p.astype(BF16))
                    rs.append(1.0 / den)
            od = jnp.dot(jnp.concatenate(ps, axis=0), vd, preferred_element_type=F32)
            og = [od[blk * WINDOW:(blk + 1) * WINDOW] * rs[blk] for blk in range(2 * Q_PER_KV)]
            for g in range(Q_PER_KV):
                oh_s[rows, (a * Q_PER_KV + g) * LANES:(a * Q_PER_KV + g + 1) * LANES] = (
                    jnp.where(lo, og[g], og[Q_PER_KV + g]).astype(BF16))

    o_ref[0] = (h + jnp.dot(oh_s[...], wo_ref[...].astype(BF16), preferred_element_type=F32)
                + bo_ref[...])


def _attn_layer(h, kv, sinks, g, w_q, b_q, w_o, b_o):
    bsz, seq, d = h.shape
    nkv = kv.shape[-1]
    tq = ATTN_TQ
    nsb = tq // WINDOW
    kernel = functools.partial(_attn_kernel, tq=tq)
    return pl.pallas_call(
        kernel,
        out_shape=jax.ShapeDtypeStruct((bsz, seq, d), F32),
        grid=(bsz, seq // tq),
        in_specs=[
            pl.BlockSpec(memory_space=pltpu.SMEM),
            pl.BlockSpec((1, tq, d), lambda b, n: (b, n, 0)),
            pl.BlockSpec((1, WINDOW, nkv), lambda b, n: (b, jnp.maximum(n * nsb - 1, 0), 0)),
            pl.BlockSpec((1, tq, nkv), lambda b, n: (b, n, 0)),
            _const_spec((1, d)),
            _const_spec(w_q.shape),
            _const_spec((1, d)),
            _const_spec(w_o.shape),
            _const_spec((1, d)),
        ],
        out_specs=pl.BlockSpec((1, tq, d), lambda b, n: (b, n, 0)),
        scratch_shapes=[pltpu.VMEM((tq, d), BF16), pltpu.VMEM((tq, d), BF16)],
        compiler_params=pltpu.CompilerParams(
            dimension_semantics=("parallel", "parallel"), vmem_limit_bytes=40 * MIB),
        name="attn",
    )(sinks, h, kv, kv, g.reshape(1, d), w_q, b_q.reshape(1, d), w_o, b_o.reshape(1, d))


def _pair_heads(w, axis):
    shape = w.shape
    split = shape[:axis] + (N_KV_HEADS // 2, 2, Q_PER_KV, HEAD_DIM) + shape[axis + 1:]
    return jnp.swapaxes(w.reshape(split), axis + 1, axis + 2).reshape(shape)


def kernel(x, norm_mix, norm_mlp, norm_kv, norm_final, s5_a_re, s5_a_im, s5_log_dt, s5_b_re, s5_b_im, s5_c_re, s5_c_im, s5_d, s5_w_glu, s5_b_glu, w_kv, b_kv, w_q, b_q, sinks, w_o, b_o, w_mlp_in, w_mlp_out):
    bsz, seq, d = x.shape

    lam_re, lam_im, bb_re, bb_im = _s5_prep(s5_a_re[0], s5_a_im[0], s5_log_dt[0],
                                            s5_b_re[0], s5_b_im[0])
    w1, cx, lam2_re, lam2_im = _s5_fold(lam_re, lam_im, bb_re, bb_im, s5_c_re[0], s5_c_im[0])
    tile_bcast = lambda a: jnp.broadcast_to(a, (N_LANE_TILES, SUBLANES, STATES_PER_TILE))
    h = _s5_layer(x, norm_mix[0], w1, tile_bcast(lam2_re), tile_bcast(lam2_im), cx,
                  s5_d[0], s5_w_glu[0], s5_b_glu[0])

    h, kv = _mlp_kv(h.reshape(bsz * seq, d), norm_mlp[0], w_mlp_in, w_mlp_out, 0,
                    norm_kv, w_kv, b_kv)

    h = _attn_layer(h.reshape(bsz, seq, d), kv.reshape(bsz, seq, -1), sinks[0], norm_mix[1],
                    _pair_heads(w_q[0], 1), _pair_heads(b_q[0], 0), _pair_heads(w_o[0], 0), b_o[0])

    out = _mlp_final(h.reshape(bsz * seq, d), norm_mlp[1], w_mlp_in, w_mlp_out, 1,
                     norm_final)
    return out.reshape(bsz, seq, d)
```

```python
import functools
import math

import jax
import jax.numpy as jnp
from jax import lax
from jax.experimental import pallas as pl
from jax.experimental.pallas import tpu as pltpu

F32 = jnp.float32
BF16 = jnp.bfloat16

D_MODEL = 1024
BATCH = 8
SEQ = 2048
S5_GROUP = 16
S5_GROUPS = D_MODEL // S5_GROUP
S5_STATE = 64
LAMBDA_RE_MAX = -1e-4
HEAD_DIM = 64
N_Q_HEADS = D_MODEL // HEAD_DIM
N_KV_HEADS = 4
Q_PER_KV = N_Q_HEADS // N_KV_HEADS
WINDOW = 128
D_FF = 4 * D_MODEL
NORM_EPS = 1e-5

LANES = 128
SUBLANES = 8
N_LANE_TILES = D_MODEL // LANES
GROUPS_PER_TILE = LANES // S5_GROUP
STATES_PER_TILE = GROUPS_PER_TILE * S5_STATE
MIB = 1024 * 1024

S5_TL = 64
MLP_TM = 512
MLP_FF_CHUNK = 1024
ATTN_TQ = 512


def _rmsnorm(x, g):
    return x * lax.rsqrt(jnp.mean(x * x, axis=-1, keepdims=True) + NORM_EPS) * g


def _gelu_tanh(x):
    c = math.sqrt(2.0 / math.pi)
    return 0.5 * x * (1.0 + jnp.tanh(c * (x + 0.044715 * (x * x * x))))


def _const_spec(shape):
    nd = len(shape)
    return pl.BlockSpec(shape, lambda *_: (0,) * nd, pipeline_mode=pl.Buffered(1))


def _layer_spec(shape, layer):
    nd = len(shape) - 1
    return pl.BlockSpec((None,) + tuple(shape[1:]), lambda *_: (layer,) + (0,) * nd,
                        pipeline_mode=pl.Buffered(1))


def _s5_params_kernel(abl_ref, bt_ref, cre_ref, cim_ref, w1_ref, cx_ref, l2r_ref, l2i_ref):
    gpt, grp, nst = GROUPS_PER_TILE, S5_GROUP, S5_STATE
    ar = jnp.minimum(abl_ref[0], LAMBDA_RE_MAX)
    ai = abl_ref[1]
    dt = jnp.exp(abl_ref[2])
    mag = jnp.exp(ar * dt)
    ang = ai * dt
    lr = mag * jnp.cos(ang)
    li = mag * jnp.sin(ang)
    den = ar * ar + ai * ai
    cr = ((lr - 1.0) * ar + li * ai) / den
    ci = (li * ar - (lr - 1.0) * ai) / den
    l2r = lr * lr - li * li
    l2i = 2.0 * (lr * li)

    def per_row(v):
        return jnp.concatenate([jnp.broadcast_to(v[g:g + 1], (grp, nst)) for g in range(gpt)], axis=0)

    row_g = lax.broadcasted_iota(jnp.int32, (LANES, STATES_PER_TILE), 0) // grp
    col_g = lax.broadcasted_iota(jnp.int32, (LANES, STATES_PER_TILE), 1) // nst
    own = row_g == col_g

    def blockdiag(v):
        return jnp.where(own, jnp.concatenate([v] * gpt, axis=1), 0.0)

    lr_c, li_c, cr_c, ci_c = per_row(lr), per_row(li), per_row(cr), per_row(ci)
    l2r_c, l2i_c = per_row(l2r), per_row(l2i)
    btr, bti = bt_ref[0], bt_ref[1]
    bbr = cr_c * btr - ci_c * bti
    bbi = cr_c * bti + ci_c * btr
    wbr, wbi = blockdiag(bbr), blockdiag(bbi)
    w1r = blockdiag(lr_c * bbr - li_c * bbi)
    w1i = blockdiag(lr_c * bbi + li_c * bbr)
    ccr, cci = cre_ref[...], cim_ref[...]
    wcr, wci = blockdiag(ccr), blockdiag(cci)

    def hdot_nt(a, b):
        return lax.dot_general(a, b, (((1,), (1,)), ((), ())),
                               precision=lax.Precision.HIGHEST, preferred_element_type=F32)

    k0 = hdot_nt(wbr, wcr) - hdot_nt(wbi, wci)
    k1 = hdot_nt(w1r, wcr) - hdot_nt(w1i, wci)
    top = jnp.concatenate([k0, k1, w1r, w1i], axis=1)
    bot = jnp.concatenate([jnp.zeros_like(k0), k0, wbr, wbi], axis=1)
    w1_ref[...] = jnp.concatenate([top, bot], axis=0).astype(BF16)

    def readout_t(pr, pi):
        return jnp.concatenate([blockdiag(pr * ccr - pi * cci),
                                -blockdiag(pi * ccr + pr * cci)], axis=1)

    cx_t = jnp.concatenate([readout_t(lr_c, li_c), readout_t(l2r_c, l2i_c)], axis=0)
    cx_ref[...] = cx_t.T.astype(BF16)

    def flat_row(v):
        row = jnp.concatenate([v[g:g + 1] for g in range(gpt)], axis=1)
        return jnp.broadcast_to(row, (SUBLANES, STATES_PER_TILE))

    l2r_ref[...] = flat_row(l2r)
    l2i_ref[...] = flat_row(l2i)


def _s5_params(a_re, a_im, log_dt, b_re, b_im, c_re, c_im):
    nt, gpt, sp = N_LANE_TILES, GROUPS_PER_TILE, STATES_PER_TILE
    abl = jnp.stack([a_re, a_im, jnp.broadcast_to(log_dt[:, None], a_re.shape)])
    abl = abl.reshape(3, nt, gpt, S5_STATE)
    bt = jnp.swapaxes(jnp.stack([b_re, b_im]), -1, -2).reshape(2, nt, LANES, S5_STATE)
    c_tile = lambda c: c.reshape(nt, LANES, S5_STATE)
    return pl.pallas_call(
        _s5_params_kernel,
        out_shape=(jax.ShapeDtypeStruct((nt, 2 * LANES, 2 * LANES + 2 * sp), BF16),
                   jax.ShapeDtypeStruct((nt, 2 * sp, 2 * LANES), BF16),
                   jax.ShapeDtypeStruct((nt, SUBLANES, sp), F32),
                   jax.ShapeDtypeStruct((nt, SUBLANES, sp), F32)),
        grid=(nt,),
        in_specs=[pl.BlockSpec((3, None, gpt, S5_STATE), lambda j: (0, j, 0, 0)),
                  pl.BlockSpec((2, None, LANES, S5_STATE), lambda j: (0, j, 0, 0)),
                  pl.BlockSpec((None, LANES, S5_STATE), lambda j: (j, 0, 0)),
                  pl.BlockSpec((None, LANES, S5_STATE), lambda j: (j, 0, 0))],
        out_specs=(pl.BlockSpec((None, 2 * LANES, 2 * LANES + 2 * sp), lambda j: (j, 0, 0)),
                   pl.BlockSpec((None, 2 * sp, 2 * LANES), lambda j: (j, 0, 0)),
                   pl.BlockSpec((None, SUBLANES, sp), lambda j: (j, 0, 0)),
                   pl.BlockSpec((None, SUBLANES, sp), lambda j: (j, 0, 0))),
        compiler_params=pltpu.CompilerParams(dimension_semantics=("parallel",)),
        name="s5_params",
    )(abl, bt, c_tile(c_re), c_tile(c_im))


def _s5_kernel(x_ref, g_ref, w1_ref, l2r_ref, l2i_ref, cx_ref, d_ref, wglu_ref, bglu_ref,
               o_ref, hbm_s, hn_s, hnu_s, zy_s, bu_s, xs_s, y_s, st_s, res_s, *, tl):
    half = tl // 2
    prow = half * SUBLANES
    sp = STATES_PER_TILE
    pitch = tl + SUBLANES

    @pl.when(pl.program_id(0) == 0)
    def _():
        st_s[...] = jnp.zeros_like(st_s)

    for b in range(SUBLANES):
        hn = _rmsnorm(x_ref[b], g_ref[...])
        for j in range(N_LANE_TILES):
            hbm_s[j, b * pitch:b * pitch + tl, :] = hn[:, j * LANES:(j + 1) * LANES]

    for j in range(N_LANE_TILES):
        lanes = slice(j * LANES, (j + 1) * LANES)
        for m in range(tl // 4):
            v = [hbm_s[j, pl.ds(4 * m + i, SUBLANES, stride=pitch), :] for i in range(4)]
            for i in range(4):
                g = (i % 2) * half + 2 * m + i // 2
                hn_s[g * SUBLANES:(g + 1) * SUBLANES, lanes] = v[i]
            r16 = slice(2 * m * SUBLANES, (2 * m + 2) * SUBLANES)
            hnu_s[r16, 2 * j * LANES:(2 * j + 1) * LANES] = (
                jnp.concatenate([v[0], v[2]], axis=0).astype(BF16))
            hnu_s[r16, (2 * j + 1) * LANES:(2 * j + 2) * LANES] = (
                jnp.concatenate([v[1], v[3]], axis=0).astype(BF16))

    def project_in(j):
        z = jnp.dot(hnu_s[:, 2 * j * LANES:(2 * j + 2) * LANES], w1_ref[j],
                    preferred_element_type=F32)
        zy_s[j % 2] = z[:, :2 * LANES]
        bu_s[j % 2] = z[:, 2 * LANES:]

    project_in(0)
    for j in range(N_LANE_TILES):
        if j + 1 < N_LANE_TILES:
            project_in(j + 1)
        p = j % 2
        lanes = slice(j * LANES, (j + 1) * LANES)
        ar = l2r_ref[j]
        ai = l2i_ref[j]
        xr = st_s[j, :, 0:sp]
        xi = st_s[j, :, sp:2 * sp]
        for k in range(half):
            r = slice(k * SUBLANES, (k + 1) * SUBLANES)
            xs_s[p, r, 0:sp] = xr
            xs_s[p, r, sp:2 * sp] = xi
            nxr = ar * xr - ai * xi + bu_s[p, r, 0:sp]
            nxi = ar * xi + ai * xr + bu_s[p, r, sp:2 * sp]
            xr, xi = nxr, nxi
        st_s[j, :, 0:sp] = xr
        st_s[j, :, sp:2 * sp] = xi
        y = zy_s[p] + jnp.dot(xs_s[p].astype(BF16), cx_ref[j], preferred_element_type=F32)
        y_s[0:prow, lanes] = y[:, :LANES]
        y_s[prow:2 * prow, lanes] = y[:, LANES:]

    for par in range(2):
        rs = slice(par * prow, (par + 1) * prow)
        y = y_s[rs, :] + d_ref[...] * hn_s[rs, :]
        z = jnp.dot(_gelu_tanh(y).astype(BF16), wglu_ref[...].astype(BF16),
                    preferred_element_type=F32) + bglu_ref[...]
        mix = z[:, :D_MODEL] * (1.0 / (1.0 + jnp.exp(-z[:, D_MODEL:])))
        for k in range(half):
            t = 2 * k + par
            for j in range(N_LANE_TILES):
                res_s[j, t * SUBLANES:(t + 1) * SUBLANES, :] = (
                    mix[k * SUBLANES:(k + 1) * SUBLANES, j * LANES:(j + 1) * LANES])

    for b in range(SUBLANES):
        for j in range(N_LANE_TILES):
            lanes = slice(j * LANES, (j + 1) * LANES)
            o_ref[b, :, lanes] = x_ref[b, :, lanes] + res_s[j, pl.ds(b, tl, stride=SUBLANES), :]


def _s5_layer(x, g_mix, w1, lam2_re_t, lam2_im_t, cx, d_skip, w_glu, b_glu):
    bsz, seq, d = x.shape
    tl = S5_TL
    rows = tl * bsz
    prow = rows // 2
    kernel = functools.partial(_s5_kernel, tl=tl)
    return pl.pallas_call(
        kernel,
        out_shape=jax.ShapeDtypeStruct((bsz, seq, d), F32),
        grid=(seq // tl,),
        in_specs=[
            pl.BlockSpec((bsz, tl, d), lambda i: (0, i, 0)),
            _const_spec((1, d)),
            _const_spec(w1.shape),
            _const_spec(lam2_re_t.shape),
            _const_spec(lam2_im_t.shape),
            _const_spec(cx.shape),
            _const_spec((1, d)),
            _const_spec(w_glu.shape),
            _const_spec((1, 2 * d)),
        ],
        out_specs=pl.BlockSpec((bsz, tl, d), lambda i: (0, i, 0)),
        scratch_shapes=[
            pltpu.VMEM((N_LANE_TILES, bsz * (tl + SUBLANES), LANES), F32),
            pltpu.VMEM((rows, d), F32),
            pltpu.VMEM((prow, 2 * d), BF16),
            pltpu.VMEM((2, prow, 2 * LANES), F32),
            pltpu.VMEM((2, prow, 2 * STATES_PER_TILE), F32),
            pltpu.VMEM((2, prow, 2 * STATES_PER_TILE), F32),
            pltpu.VMEM((rows, d), F32),
            pltpu.VMEM((N_LANE_TILES, SUBLANES, 2 * STATES_PER_TILE), F32),
            pltpu.VMEM((N_LANE_TILES, rows, LANES), F32),
        ],
        compiler_params=pltpu.CompilerParams(
            dimension_semantics=("arbitrary",), vmem_limit_bytes=56 * MIB),
        name="s5_layer",
    )(x, g_mix.reshape(1, d), w1, lam2_re_t, lam2_im_t, cx, d_skip.reshape(1, d), w_glu,
      b_glu.reshape(1, 2 * d))


def _mlp_body(h, g, win_ref, wout_ref):
    hn = _rmsnorm(h, g).astype(BF16)
    acc = jnp.zeros(h.shape, F32)
    for c in range(D_FF // MLP_FF_CHUNK):
        cols = slice(c * MLP_FF_CHUNK, (c + 1) * MLP_FF_CHUNK)
        a = jnp.dot(hn, win_ref[:, cols].astype(BF16), preferred_element_type=F32)
        a = jnp.square(jnp.maximum(a, 0.0)).astype(BF16)
        acc = acc + jnp.dot(a, wout_ref[cols, :].astype(BF16), preferred_element_type=F32)
    return h + acc


def _mlp_kv_kernel(h_ref, g_ref, win_ref, wout_ref, gkv_ref, wkv_ref, bkv_ref, o_ref, kv_ref):
    out = _mlp_body(h_ref[...], g_ref[...], win_ref, wout_ref)
    o_ref[...] = out
    kvn = _rmsnorm(out, gkv_ref[...]).astype(BF16)
    kv = jnp.dot(kvn, wkv_ref[...].astype(BF16), preferred_element_type=F32) + bkv_ref[...]
    kv_ref[...] = kv.astype(BF16)


def _mlp_final_kernel(h_ref, g_ref, win_ref, wout_ref, gfin_ref, o_ref):
    out = _mlp_body(h_ref[...], g_ref[...], win_ref, wout_ref)
    o_ref[...] = _rmsnorm(out, gfin_ref[...])


def _mlp_kv(h, g, w_in, w_out, layer, g_kv, w_kv, b_kv):
    t, d = h.shape
    nkv = w_kv.shape[1]
    row = lambda n: pl.BlockSpec((MLP_TM, n), lambda i: (i, 0))
    return pl.pallas_call(
        _mlp_kv_kernel,
        out_shape=(jax.ShapeDtypeStruct((t, d), F32), jax.ShapeDtypeStruct((t, nkv), BF16)),
        grid=(t // MLP_TM,),
        in_specs=[row(d), _const_spec((1, d)), _layer_spec(w_in.shape, layer),
                  _layer_spec(w_out.shape, layer),
                  _const_spec((1, d)), _const_spec(w_kv.shape), _const_spec((1, nkv))],
        out_specs=(row(d), row(nkv)),
        compiler_params=pltpu.CompilerParams(
            dimension_semantics=("parallel",), vmem_limit_bytes=56 * MIB),
        name="mlp_kv",
    )(h, g.reshape(1, d), w_in, w_out, g_kv.reshape(1, d), w_kv, b_kv.reshape(1, nkv))


def _mlp_final(h, g, w_in, w_out, layer, g_fin):
    t, d = h.shape
    row = lambda n: pl.BlockSpec((MLP_TM, n), lambda i: (i, 0))
    return pl.pallas_call(
        _mlp_final_kernel,
        out_shape=jax.ShapeDtypeStruct((t, d), F32),
        grid=(t // MLP_TM,),
        in_specs=[row(d), _const_spec((1, d)), _layer_spec(w_in.shape, layer),
                  _layer_spec(w_out.shape, layer),
                  _const_spec((1, d))],
        out_specs=row(d),
        compiler_params=pltpu.CompilerParams(
            dimension_semantics=("parallel",), vmem_limit_bytes=56 * MIB),
        name="mlp_final",
    )(h, g.reshape(1, d), w_in, w_out, g_fin.reshape(1, d))


def _pair_tile_sources(a, g):
    head_lo = (2 * a) * Q_PER_KV + g
    head_hi = (2 * a + 1) * Q_PER_KV + g
    return (head_lo // 2, head_lo % 2), (head_hi // 2, head_hi % 2)


def _attn_kernel(sink_ref, h_ref, kvp_ref, kvc_ref, g_ref, wq_ref, bq_ref, wo_ref, bo_ref,
                 o_ref, wq_s, bq_s, wo_s, q_s, oh_s, *, tq):
    nsb = tq // WINDOW
    n = pl.program_id(1)
    n_kv_tiles = N_KV_HEADS // 2
    lo_row = lax.broadcasted_iota(jnp.int32, (1, LANES), 1) < HEAD_DIM

    @pl.when((pl.program_id(0) == 0) & (n == 0))
    def _():
        for a in range(n_kv_tiles):
            for g in range(Q_PER_KV):
                (t_lo, h_lo), (t_hi, h_hi) = _pair_tile_sources(a, g)
                dst = slice((a * Q_PER_KV + g) * LANES, (a * Q_PER_KV + g + 1) * LANES)

                def pair(ref):
                    src_lo = ref[:, t_lo * LANES:(t_lo + 1) * LANES]
                    src_hi = ref[:, t_hi * LANES:(t_hi + 1) * LANES]
                    if h_lo == 1:
                        src_lo = pltpu.roll(src_lo, HEAD_DIM, axis=1)
                    if h_hi == 0:
                        src_hi = pltpu.roll(src_hi, HEAD_DIM, axis=1)
                    return jnp.where(lo_row, src_lo, src_hi)

                wq_s[:, dst] = pair(wq_ref).astype(BF16)
                bq_s[:, dst] = pair(bq_ref)
                for half in range(2):
                    head = (2 * a + half) * Q_PER_KV + g
                    r0 = (a * Q_PER_KV + g) * LANES + half * HEAD_DIM
                    wo_s[r0:r0 + HEAD_DIM, :] = (
                        wo_ref[head * HEAD_DIM:(head + 1) * HEAD_DIM, :].astype(BF16))

    h = h_ref[0]
    hn = _rmsnorm(h, g_ref[...]).astype(BF16)
    q = jnp.dot(hn, wq_s[...], preferred_element_type=F32) + bq_s[...]
    q_s[...] = (q * (1.0 / math.sqrt(HEAD_DIM))).astype(BF16)

    lo = lax.broadcasted_iota(jnp.int32, (WINDOW, LANES), 1) < HEAD_DIM
    qi = lax.broadcasted_iota(jnp.int32, (WINDOW, 2 * WINDOW), 0)
    kj = lax.broadcasted_iota(jnp.int32, (WINDOW, 2 * WINDOW), 1)
    diff = qi + WINDOW - kj
    band = (diff >= 0) & (diff < WINDOW)
    zero = jnp.zeros((WINDOW, LANES), BF16)

    for sb in range(nsb):
        rows = slice(sb * WINDOW, (sb + 1) * WINDOW)
        first = (n * nsb + sb) == 0
        valid = band & jnp.logical_or(jnp.logical_not(first), kj >= WINDOW)
        bias = jnp.where(valid, 0.0, -jnp.inf).astype(F32)
        for a in range(n_kv_tiles):
            kl = slice(a * LANES, (a + 1) * LANES)
            vl = slice((n_kv_tiles + a) * LANES, (n_kv_tiles + a + 1) * LANES)
            if sb == 0:
                kprev, vprev = kvp_ref[0, :, kl], kvp_ref[0, :, vl]
            else:
                prow = slice((sb - 1) * WINDOW, sb * WINDOW)
                kprev, vprev = kvc_ref[0, prow, kl], kvc_ref[0, prow, vl]
            kd = jnp.concatenate([kprev, kvc_ref[0, rows, kl]], axis=0)
            vd = jnp.concatenate([vprev, kvc_ref[0, rows, vl]], axis=0)
            qt = [q_s[rows, (a * Q_PER_KV + g) * LANES:(a * Q_PER_KV + g + 1) * LANES]
                  for g in range(Q_PER_KV)]
            lhs = jnp.concatenate([jnp.where(lo, t, zero) for t in qt]
                                  + [jnp.where(lo, zero, t) for t in qt], axis=0)
            s = lax.dot_general(lhs, kd, (((1,), (1,)), ((), ())),
                                preferred_element_type=F32)
            ps, rs = [], []
            for half in range(2):
                for g in range(Q_PER_KV):
                    blk = half * Q_PER_KV + g
                    sg = s[blk * WINDOW:(blk + 1) * WINDOW] + bias
                    sink = sink_ref[(2 * a + half) * Q_PER_KV + g]
                    m = jnp.maximum(jnp.max(sg, axis=-1, keepdims=True), sink)
                    p = jnp.exp(sg - m)
                    den = jnp.sum(p, axis=-1, keepdims=True) + jnp.exp(sink - m)
                    ps.append(p.astype(BF16))
                    rs.append(1.0 / den)
            od = jnp.dot(jnp.concatenate(ps, axis=0), vd, preferred_element_type=F32)
            og = [od[blk * WINDOW:(blk + 1) * WINDOW] * rs[blk] for blk in range(2 * Q_PER_KV)]
            for g in range(Q_PER_KV):
                oh_s[rows, (a * Q_PER_KV + g) * LANES:(a * Q_PER_KV + g + 1) * LANES] = (
                    jnp.where(lo, og[g], og[Q_PER_KV + g]).astype(BF16))

    o_ref[0] = h + jnp.dot(oh_s[...], wo_s[...], preferred_element_type=F32) + bo_ref[...]


def _attn_layer(h, kv, sinks, g, w_q, b_q, w_o, b_o):
    bsz, seq, d = h.shape
    nkv = kv.shape[-1]
    tq = ATTN_TQ
    nsb = tq // WINDOW
    kernel = functools.partial(_attn_kernel, tq=tq)
    return pl.pallas_call(
        kernel,
        out_shape=jax.ShapeDtypeStruct((bsz, seq, d), F32),
        grid=(bsz, seq // tq),
        in_specs=[
            pl.BlockSpec(memory_space=pltpu.SMEM),
            pl.BlockSpec((1, tq, d), lambda b, n: (b, n, 0)),
            pl.BlockSpec((1, WINDOW, nkv), lambda b, n: (b, jnp.maximum(n * nsb - 1, 0), 0)),
            pl.BlockSpec((1, tq, nkv), lambda b, n: (b, n, 0)),
            _const_spec((1, d)),
            _const_spec(w_q.shape),
            _const_spec((1, d)),
            _const_spec(w_o.shape),
            _const_spec((1, d)),
        ],
        out_specs=pl.BlockSpec((1, tq, d), lambda b, n: (b, n, 0)),
        scratch_shapes=[pltpu.VMEM((d, d), BF16),
                        pltpu.VMEM((1, d), F32),
                        pltpu.VMEM((d, d), BF16),
                        pltpu.VMEM((tq, d), BF16),
                        pltpu.VMEM((tq, d), BF16)],
        compiler_params=pltpu.CompilerParams(
            dimension_semantics=("arbitrary", "arbitrary"), vmem_limit_bytes=40 * MIB),
        name="attn",
    )(sinks, h, kv, kv, g.reshape(1, d), w_q, b_q.reshape(1, d), w_o, b_o.reshape(1, d))


def kernel(x, norm_mix, norm_mlp, norm_kv, norm_final, s5_a_re, s5_a_im, s5_log_dt, s5_b_re, s5_b_im, s5_c_re, s5_c_im, s5_d, s5_w_glu, s5_b_glu, w_kv, b_kv, w_q, b_q, sinks, w_o, b_o, w_mlp_in, w_mlp_out):
    bsz, seq, d = x.shape

    w1, cx, lam2_re_t, lam2_im_t = _s5_params(s5_a_re[0], s5_a_im[0], s5_log_dt[0], s5_b_re[0],
                                              s5_b_im[0], s5_c_re[0], s5_c_im[0])
    h = _s5_layer(x, norm_mix[0], w1, lam2_re_t, lam2_im_t, cx, s5_d[0], s5_w_glu[0], s5_b_glu[0])

    h, kv = _mlp_kv(h.reshape(bsz * seq, d), norm_mlp[0], w_mlp_in, w_mlp_out, 0,
                    norm_kv, w_kv, b_kv)

    h = _attn_layer(h.reshape(bsz, seq, d), kv.reshape(bsz, seq, -1), sinks[0], norm_mix[1],
                    w_q[0], b_q[0], w_o[0], b_o[0])

    out = _mlp_final(h.reshape(bsz * seq, d), norm_mlp[1], w_mlp_in, w_mlp_out, 1,
                     norm_final)
    return out.reshape(bsz, seq, d)
```

```python
import functools
import math

import jax
import jax.numpy as jnp
from jax import lax
from jax.experimental import pallas as pl
from jax.experimental.pallas import tpu as pltpu

F32 = jnp.float32
BF16 = jnp.bfloat16

D_MODEL = 1024
BATCH = 8
SEQ = 2048
S5_GROUP = 16
S5_GROUPS = D_MODEL // S5_GROUP
S5_STATE = 64
LAMBDA_RE_MAX = -1e-4
HEAD_DIM = 64
N_Q_HEADS = D_MODEL // HEAD_DIM
N_KV_HEADS = 4
Q_PER_KV = N_Q_HEADS // N_KV_HEADS
WINDOW = 128
D_FF = 4 * D_MODEL
NORM_EPS = 1e-5
LOG2_E = math.log2(math.e)

LANES = 128
SUBLANES = 8
N_LANE_TILES = D_MODEL // LANES
GROUPS_PER_TILE = LANES // S5_GROUP
STATES_PER_TILE = GROUPS_PER_TILE * S5_STATE
MIB = 1024 * 1024

S5_TL = 64
S5_TILES_IN_FLIGHT = 2
MLP_TM = 512
MLP_FF_CHUNK = 1024
ATTN_TQ = 512


def _rmsnorm(x, g):
    return x * lax.rsqrt(jnp.mean(x * x, axis=-1, keepdims=True) + NORM_EPS) * g


def _gelu_tanh(x):
    c = math.sqrt(2.0 / math.pi)
    return 0.5 * x * (1.0 + jnp.tanh(c * (x + 0.044715 * (x * x * x))))


def _const_spec(shape):
    nd = len(shape)
    return pl.BlockSpec(shape, lambda *_: (0,) * nd, pipeline_mode=pl.Buffered(1))


def _layer_spec(shape, layer):
    nd = len(shape) - 1
    return pl.BlockSpec((None,) + tuple(shape[1:]), lambda *_: (layer,) + (0,) * nd,
                        pipeline_mode=pl.Buffered(1))


def _s5_params_kernel(abl_ref, bt_ref, cre_ref, cim_ref, w1_ref, cx_ref, l2r_ref, l2i_ref):
    gpt, grp, nst = GROUPS_PER_TILE, S5_GROUP, S5_STATE
    ar = jnp.minimum(abl_ref[0], LAMBDA_RE_MAX)
    ai = abl_ref[1]
    dt = jnp.exp(abl_ref[2])
    mag = jnp.exp(ar * dt)
    ang = ai * dt
    lr = mag * jnp.cos(ang)
    li = mag * jnp.sin(ang)
    den = ar * ar + ai * ai
    cr = ((lr - 1.0) * ar + li * ai) / den
    ci = (li * ar - (lr - 1.0) * ai) / den
    l2r = lr * lr - li * li
    l2i = 2.0 * (lr * li)

    def per_row(v):
        return jnp.concatenate([jnp.broadcast_to(v[g:g + 1], (grp, nst)) for g in range(gpt)], axis=0)

    row_g = lax.broadcasted_iota(jnp.int32, (LANES, STATES_PER_TILE), 0) // grp
    col_g = lax.broadcasted_iota(jnp.int32, (LANES, STATES_PER_TILE), 1) // nst
    own = row_g == col_g

    def blockdiag(v):
        return jnp.where(own, jnp.concatenate([v] * gpt, axis=1), 0.0)

    lr_c, li_c, cr_c, ci_c = per_row(lr), per_row(li), per_row(cr), per_row(ci)
    l2r_c, l2i_c = per_row(l2r), per_row(l2i)
    btr, bti = bt_ref[0], bt_ref[1]
    bbr = cr_c * btr - ci_c * bti
    bbi = cr_c * bti + ci_c * btr
    wbr, wbi = blockdiag(bbr), blockdiag(bbi)
    w1r = blockdiag(lr_c * bbr - li_c * bbi)
    w1i = blockdiag(lr_c * bbi + li_c * bbr)
    ccr, cci = cre_ref[...], cim_ref[...]
    wcr, wci = blockdiag(ccr), blockdiag(cci)

    def hdot_nt(a, b):
        return lax.dot_general(a, b, (((1,), (1,)), ((), ())),
                               precision=lax.Precision.HIGHEST, preferred_element_type=F32)

    k0 = hdot_nt(wbr, wcr) - hdot_nt(wbi, wci)
    k1 = hdot_nt(w1r, wcr) - hdot_nt(w1i, wci)
    top = jnp.concatenate([k0, k1, w1r, w1i], axis=1)
    bot = jnp.concatenate([jnp.zeros_like(k0), k0, wbr, wbi], axis=1)
    w1_ref[...] = jnp.concatenate([top, bot], axis=0).astype(BF16)

    def readout_t(pr, pi):
        return jnp.concatenate([blockdiag(pr * ccr - pi * cci),
                                -blockdiag(pi * ccr + pr * cci)], axis=1)

    cx_t = jnp.concatenate([readout_t(lr_c, li_c), readout_t(l2r_c, l2i_c)], axis=0)
    cx_ref[...] = cx_t.T.astype(BF16)

    def flat_row(v):
        row = jnp.concatenate([v[g:g + 1] for g in range(gpt)], axis=1)
        return jnp.broadcast_to(row, (SUBLANES, STATES_PER_TILE))

    l2r_ref[...] = flat_row(l2r)
    l2i_ref[...] = flat_row(l2i)


def _s5_params(a_re, a_im, log_dt, b_re, b_im, c_re, c_im):
    nt, gpt, sp = N_LANE_TILES, GROUPS_PER_TILE, STATES_PER_TILE
    abl = jnp.stack([a_re, a_im, jnp.broadcast_to(log_dt[:, None], a_re.shape)])
    abl = abl.reshape(3, nt, gpt, S5_STATE)
    bt = jnp.swapaxes(jnp.stack([b_re, b_im]), -1, -2).reshape(2, nt, LANES, S5_STATE)
    c_tile = lambda c: c.reshape(nt, LANES, S5_STATE)
    return pl.pallas_call(
        _s5_params_kernel,
        out_shape=(jax.ShapeDtypeStruct((nt, 2 * LANES, 2 * LANES + 2 * sp), BF16),
                   jax.ShapeDtypeStruct((nt, 2 * sp, 2 * LANES), BF16),
                   jax.ShapeDtypeStruct((nt, SUBLANES, sp), F32),
                   jax.ShapeDtypeStruct((nt, SUBLANES, sp), F32)),
        grid=(nt,),
        in_specs=[pl.BlockSpec((3, None, gpt, S5_STATE), lambda j: (0, j, 0, 0)),
                  pl.BlockSpec((2, None, LANES, S5_STATE), lambda j: (0, j, 0, 0)),
                  pl.BlockSpec((None, LANES, S5_STATE), lambda j: (j, 0, 0)),
                  pl.BlockSpec((None, LANES, S5_STATE), lambda j: (j, 0, 0))],
        out_specs=(pl.BlockSpec((None, 2 * LANES, 2 * LANES + 2 * sp), lambda j: (j, 0, 0)),
                   pl.BlockSpec((None, 2 * sp, 2 * LANES), lambda j: (j, 0, 0)),
                   pl.BlockSpec((None, SUBLANES, sp), lambda j: (j, 0, 0)),
                   pl.BlockSpec((None, SUBLANES, sp), lambda j: (j, 0, 0))),
        compiler_params=pltpu.CompilerParams(dimension_semantics=("parallel",)),
        name="s5_params",
    )(abl, bt, c_tile(c_re), c_tile(c_im))


def _s5_kernel(x_ref, g_ref, w1_ref, l2r_ref, l2i_ref, cx_ref, d_ref, wglu_ref, bglu_ref,
               o_ref, hbm_s, hn_s, hnu_s, zy_s, bu_s, xs_s, y_s, st_s, res_s, *, tl):
    half = tl // 2
    prow = half * SUBLANES
    sp = STATES_PER_TILE
    pitch = tl + SUBLANES

    @pl.when(pl.program_id(0) == 0)
    def _():
        st_s[...] = jnp.zeros_like(st_s)

    for b in range(SUBLANES):
        hn = _rmsnorm(x_ref[b], g_ref[...])
        for j in range(N_LANE_TILES):
            hbm_s[j, b * pitch:b * pitch + tl, :] = hn[:, j * LANES:(j + 1) * LANES]

    for j in range(N_LANE_TILES):
        lanes = slice(j * LANES, (j + 1) * LANES)
        for m in range(tl // 4):
            v = [hbm_s[j, pl.ds(4 * m + i, SUBLANES, stride=pitch), :] for i in range(4)]
            for i in range(4):
                g = (i % 2) * half + 2 * m + i // 2
                hn_s[g * SUBLANES:(g + 1) * SUBLANES, lanes] = v[i]
            r16 = slice(2 * m * SUBLANES, (2 * m + 2) * SUBLANES)
            hnu_s[r16, 2 * j * LANES:(2 * j + 1) * LANES] = (
                jnp.concatenate([v[0], v[2]], axis=0).astype(BF16))
            hnu_s[r16, (2 * j + 1) * LANES:(2 * j + 2) * LANES] = (
                jnp.concatenate([v[1], v[3]], axis=0).astype(BF16))

    def project_in(j):
        z = jnp.dot(hnu_s[:, 2 * j * LANES:(2 * j + 2) * LANES], w1_ref[j],
                    preferred_element_type=F32)
        zy_s[j % S5_TILES_IN_FLIGHT] = z[:, :2 * LANES]
        bu_s[j % S5_TILES_IN_FLIGHT] = z[:, 2 * LANES:]

    ahead = S5_TILES_IN_FLIGHT - 1
    for j in range(ahead):
        project_in(j)
    for j in range(N_LANE_TILES):
        if j + ahead < N_LANE_TILES:
            project_in(j + ahead)
        p = j % S5_TILES_IN_FLIGHT
        lanes = slice(j * LANES, (j + 1) * LANES)
        ar = l2r_ref[j]
        ai = l2i_ref[j]
        xr = st_s[j, :, 0:sp]
        xi = st_s[j, :, sp:2 * sp]
        for k in range(half):
            r = slice(k * SUBLANES, (k + 1) * SUBLANES)
            xs_s[p, r, 0:sp] = xr
            xs_s[p, r, sp:2 * sp] = xi
            nxr = ar * xr - ai * xi + bu_s[p, r, 0:sp]
            nxi = ar * xi + ai * xr + bu_s[p, r, sp:2 * sp]
            xr, xi = nxr, nxi
        st_s[j, :, 0:sp] = xr
        st_s[j, :, sp:2 * sp] = xi
        y = zy_s[p] + jnp.dot(xs_s[p].astype(BF16), cx_ref[j], preferred_element_type=F32)
        y_s[0:prow, lanes] = y[:, :LANES]
        y_s[prow:2 * prow, lanes] = y[:, LANES:]

    for par in range(2):
        rs = slice(par * prow, (par + 1) * prow)
        y = y_s[rs, :] + d_ref[...] * hn_s[rs, :]
        z = jnp.dot(_gelu_tanh(y).astype(BF16), wglu_ref[...].astype(BF16),
                    preferred_element_type=F32) + bglu_ref[...]
        mix = z[:, :D_MODEL] * (1.0 / (1.0 + jnp.exp(-z[:, D_MODEL:])))
        for k in range(half):
            t = 2 * k + par
            for j in range(N_LANE_TILES):
                res_s[j, t * SUBLANES:(t + 1) * SUBLANES, :] = (
                    mix[k * SUBLANES:(k + 1) * SUBLANES, j * LANES:(j + 1) * LANES])

    for b in range(SUBLANES):
        for j in range(N_LANE_TILES):
            lanes = slice(j * LANES, (j + 1) * LANES)
            o_ref[b, :, lanes] = x_ref[b, :, lanes] + res_s[j, pl.ds(b, tl, stride=SUBLANES), :]


def _s5_layer(x, g_mix, w1, lam2_re_t, lam2_im_t, cx, d_skip, w_glu, b_glu):
    bsz, seq, d = x.shape
    tl = S5_TL
    rows = tl * bsz
    prow = rows // 2
    kernel = functools.partial(_s5_kernel, tl=tl)
    return pl.pallas_call(
        kernel,
        out_shape=jax.ShapeDtypeStruct((bsz, seq, d), F32),
        grid=(seq // tl,),
        in_specs=[
            pl.BlockSpec((bsz, tl, d), lambda i: (0, i, 0)),
            _const_spec((1, d)),
            _const_spec(w1.shape),
            _const_spec(lam2_re_t.shape),
            _const_spec(lam2_im_t.shape),
            _const_spec(cx.shape),
            _const_spec((1, d)),
            _const_spec(w_glu.shape),
            _const_spec((1, 2 * d)),
        ],
        out_specs=pl.BlockSpec((bsz, tl, d), lambda i: (0, i, 0)),
        scratch_shapes=[
            pltpu.VMEM((N_LANE_TILES, bsz * (tl + SUBLANES), LANES), F32),
            pltpu.VMEM((rows, d), F32),
            pltpu.VMEM((prow, 2 * d), BF16),
            pltpu.VMEM((S5_TILES_IN_FLIGHT, prow, 2 * LANES), F32),
            pltpu.VMEM((S5_TILES_IN_FLIGHT, prow, 2 * STATES_PER_TILE), F32),
            pltpu.VMEM((S5_TILES_IN_FLIGHT, prow, 2 * STATES_PER_TILE), F32),
            pltpu.VMEM((rows, d), F32),
            pltpu.VMEM((N_LANE_TILES, SUBLANES, 2 * STATES_PER_TILE), F32),
            pltpu.VMEM((N_LANE_TILES, rows, LANES), F32),
        ],
        compiler_params=pltpu.CompilerParams(
            dimension_semantics=("arbitrary",), vmem_limit_bytes=56 * MIB),
        name="s5_layer",
    )(x, g_mix.reshape(1, d), w1, lam2_re_t, lam2_im_t, cx, d_skip.reshape(1, d), w_glu,
      b_glu.reshape(1, 2 * d))


def _mlp_body(h, g, win_ref, wout_ref):
    hn = _rmsnorm(h, g).astype(BF16)
    acc = jnp.zeros(h.shape, F32)
    for c in range(D_FF // MLP_FF_CHUNK):
        cols = slice(c * MLP_FF_CHUNK, (c + 1) * MLP_FF_CHUNK)
        a = jnp.dot(hn, win_ref[:, cols].astype(BF16), preferred_element_type=F32)
        a = jnp.square(jnp.maximum(a, 0.0)).astype(BF16)
        acc = acc + jnp.dot(a, wout_ref[cols, :].astype(BF16), preferred_element_type=F32)
    return h + acc


def _mlp_kv_kernel(h_ref, g_ref, win_ref, wout_ref, gkv_ref, wkv_ref, bkv_ref, o_ref, kv_ref):
    out = _mlp_body(h_ref[...], g_ref[...], win_ref, wout_ref)
    o_ref[...] = out
    kvn = _rmsnorm(out, gkv_ref[...]).astype(BF16)
    kv = jnp.dot(kvn, wkv_ref[...].astype(BF16), preferred_element_type=F32) + bkv_ref[...]
    kv_ref[...] = kv.astype(BF16)


def _mlp_final_kernel(h_ref, g_ref, win_ref, wout_ref, gfin_ref, o_ref):
    out = _mlp_body(h_ref[...], g_ref[...], win_ref, wout_ref)
    o_ref[...] = _rmsnorm(out, gfin_ref[...])


def _mlp_kv(h, g, w_in, w_out, layer, g_kv, w_kv, b_kv):
    t, d = h.shape
    nkv = w_kv.shape[1]
    row = lambda n: pl.BlockSpec((MLP_TM, n), lambda i: (i, 0))
    return pl.pallas_call(
        _mlp_kv_kernel,
        out_shape=(jax.ShapeDtypeStruct((t, d), F32), jax.ShapeDtypeStruct((t, nkv), BF16)),
        grid=(t // MLP_TM,),
        in_specs=[row(d), _const_spec((1, d)), _layer_spec(w_in.shape, layer),
                  _layer_spec(w_out.shape, layer),
                  _const_spec((1, d)), _const_spec(w_kv.shape), _const_spec((1, nkv))],
        out_specs=(row(d), row(nkv)),
        compiler_params=pltpu.CompilerParams(
            dimension_semantics=("parallel",), vmem_limit_bytes=56 * MIB),
        name="mlp_kv",
    )(h, g.reshape(1, d), w_in, w_out, g_kv.reshape(1, d), w_kv, b_kv.reshape(1, nkv))


def _mlp_final(h, g, w_in, w_out, layer, g_fin):
    t, d = h.shape
    row = lambda n: pl.BlockSpec((MLP_TM, n), lambda i: (i, 0))
    return pl.pallas_call(
        _mlp_final_kernel,
        out_shape=jax.ShapeDtypeStruct((t, d), F32),
        grid=(t // MLP_TM,),
        in_specs=[row(d), _const_spec((1, d)), _layer_spec(w_in.shape, layer),
                  _layer_spec(w_out.shape, layer),
                  _const_spec((1, d))],
        out_specs=row(d),
        compiler_params=pltpu.CompilerParams(
            dimension_semantics=("parallel",), vmem_limit_bytes=56 * MIB),
        name="mlp_final",
    )(h, g.reshape(1, d), w_in, w_out, g_fin.reshape(1, d))


def _pair_tile_sources(a, g):
    head_lo = (2 * a) * Q_PER_KV + g
    head_hi = (2 * a + 1) * Q_PER_KV + g
    return (head_lo // 2, head_lo % 2), (head_hi // 2, head_hi % 2)


def _attn_kernel(sink_ref, h_ref, kvp_ref, kvc_ref, g_ref, wq_ref, bq_ref, wo_ref, bo_ref,
                 o_ref, wq_s, bq_s, wo_s, q_s, oh_s, *, tq):
    nsb = tq // WINDOW
    n = pl.program_id(1)
    n_kv_tiles = N_KV_HEADS // 2
    lo_row = lax.broadcasted_iota(jnp.int32, (1, LANES), 1) < HEAD_DIM

    @pl.when((pl.program_id(0) == 0) & (n == 0))
    def _():
        for a in range(n_kv_tiles):
            for g in range(Q_PER_KV):
                (t_lo, h_lo), (t_hi, h_hi) = _pair_tile_sources(a, g)
                dst = slice((a * Q_PER_KV + g) * LANES, (a * Q_PER_KV + g + 1) * LANES)

                def pair(ref):
                    src_lo = ref[:, t_lo * LANES:(t_lo + 1) * LANES]
                    src_hi = ref[:, t_hi * LANES:(t_hi + 1) * LANES]
                    if h_lo == 1:
                        src_lo = pltpu.roll(src_lo, HEAD_DIM, axis=1)
                    if h_hi == 0:
                        src_hi = pltpu.roll(src_hi, HEAD_DIM, axis=1)
                    return jnp.where(lo_row, src_lo, src_hi)

                wq_s[:, dst] = pair(wq_ref).astype(BF16)
                bq_s[:, dst] = pair(bq_ref)
                for half in range(2):
                    head = (2 * a + half) * Q_PER_KV + g
                    r0 = (a * Q_PER_KV + g) * LANES + half * HEAD_DIM
                    wo_s[r0:r0 + HEAD_DIM, :] = (
                        wo_ref[head * HEAD_DIM:(head + 1) * HEAD_DIM, :].astype(BF16))

    h = h_ref[0]
    hn = _rmsnorm(h, g_ref[...]).astype(BF16)
    q = jnp.dot(hn, wq_s[...], preferred_element_type=F32) + bq_s[...]
    q_s[...] = (q * (LOG2_E / math.sqrt(HEAD_DIM))).astype(BF16)

    lo = lax.broadcasted_iota(jnp.int32, (WINDOW, LANES), 1) < HEAD_DIM
    own = (lax.broadcasted_iota(jnp.int32, (WINDOW, WINDOW), 1)
           <= lax.broadcasted_iota(jnp.int32, (WINDOW, WINDOW), 0))
    zero = jnp.zeros((WINDOW, LANES), BF16)
    fzero = jnp.zeros((WINDOW, WINDOW), F32)

    for sb in range(nsb):
        rows = slice(sb * WINDOW, (sb + 1) * WINDOW)
        prev_bias = jnp.where(n == 0, -jnp.inf, 0.0).astype(F32) if sb == 0 else None
        for a in range(n_kv_tiles):
            kl = slice(a * LANES, (a + 1) * LANES)
            vl = slice((n_kv_tiles + a) * LANES, (n_kv_tiles + a + 1) * LANES)
            if sb == 0:
                kprev, vprev = kvp_ref[0, :, kl], kvp_ref[0, :, vl]
            else:
                prow = slice((sb - 1) * WINDOW, sb * WINDOW)
                kprev, vprev = kvc_ref[0, prow, kl], kvc_ref[0, prow, vl]
            kd = jnp.concatenate([kprev, kvc_ref[0, rows, kl]], axis=0)
            vd = jnp.concatenate([vprev, kvc_ref[0, rows, vl]], axis=0)
            vd1 = jnp.concatenate([vd, jnp.ones((2 * WINDOW, LANES), BF16)], axis=1)
            qt = [q_s[rows, (a * Q_PER_KV + g) * LANES:(a * Q_PER_KV + g + 1) * LANES]
                  for g in range(Q_PER_KV)]
            lhs = jnp.concatenate([jnp.where(lo, t, zero) for t in qt]
                                  + [jnp.where(lo, zero, t) for t in qt], axis=0)
            s = lax.dot_general(lhs, kd, (((1,), (1,)), ((), ())),
                                preferred_element_type=F32)
            ps, sink_terms = [], []
            for half in range(2):
                for g in range(Q_PER_KV):
                    blk = half * Q_PER_KV + g
                    s_prev = s[blk * WINDOW:(blk + 1) * WINDOW, :WINDOW]
                    s_own = s[blk * WINDOW:(blk + 1) * WINDOW, WINDOW:]
                    if prev_bias is not None:
                        s_prev = s_prev + prev_bias
                    sg = jnp.where(own, s_own, s_prev)
                    sink = sink_ref[(2 * a + half) * Q_PER_KV + g] * LOG2_E
                    m = jnp.maximum(jnp.max(sg, axis=-1, keepdims=True), sink)
                    p = jnp.exp2(sg - m)
                    ps.append(jnp.concatenate([jnp.where(own, fzero, p).astype(BF16),
                                               jnp.where(own, p, fzero).astype(BF16)], axis=1))
                    sink_terms.append(jnp.exp2(sink - m))
            od = jnp.dot(jnp.concatenate(ps, axis=0), vd1, preferred_element_type=F32)
            for g in range(Q_PER_KV):
                r_lo = slice(g * WINDOW, (g + 1) * WINDOW)
                r_hi = slice((Q_PER_KV + g) * WINDOW, (Q_PER_KV + g + 1) * WINDOW)
                num = jnp.where(lo, od[r_lo, :LANES], od[r_hi, :LANES])
                den = (jnp.where(lo, od[r_lo, LANES:], od[r_hi, LANES:])
                       + jnp.where(lo, sink_terms[g], sink_terms[Q_PER_KV + g]))
                oh_s[rows, (a * Q_PER_KV + g) * LANES:(a * Q_PER_KV + g + 1) * LANES] = (
                    (num * (1.0 / den)).astype(BF16))

    o_ref[0] = h + jnp.dot(oh_s[...], wo_s[...], preferred_element_type=F32) + bo_ref[...]


def _attn_layer(h, kv, sinks, g, w_q, b_q, w_o, b_o):
    bsz, seq, d = h.shape
    nkv = kv.shape[-1]
    tq = ATTN_TQ
    nsb = tq // WINDOW
    kernel = functools.partial(_attn_kernel, tq=tq)
    return pl.pallas_call(
        kernel,
        out_shape=jax.ShapeDtypeStruct((bsz, seq, d), F32),
        grid=(bsz, seq // tq),
        in_specs=[
            pl.BlockSpec(memory_space=pltpu.SMEM),
            pl.BlockSpec((1, tq, d), lambda b, n: (b, n, 0)),
            pl.BlockSpec((1, WINDOW, nkv), lambda b, n: (b, jnp.maximum(n * nsb - 1, 0), 0)),
            pl.BlockSpec((1, tq, nkv), lambda b, n: (b, n, 0)),
            _const_spec((1, d)),
            _const_spec(w_q.shape),
            _const_spec((1, d)),
            _const_spec(w_o.shape),
            _const_spec((1, d)),
        ],
        out_specs=pl.BlockSpec((1, tq, d), lambda b, n: (b, n, 0)),
        scratch_shapes=[pltpu.VMEM((d, d), BF16),
                        pltpu.VMEM((1, d), F32),
                        pltpu.VMEM((d, d), BF16),
                        pltpu.VMEM((tq, d), BF16),
                        pltpu.VMEM((tq, d), BF16)],
        compiler_params=pltpu.CompilerParams(
            dimension_semantics=("arbitrary", "arbitrary"), vmem_limit_bytes=40 * MIB),
        name="attn",
    )(sinks, h, kv, kv, g.reshape(1, d), w_q, b_q.reshape(1, d), w_o, b_o.reshape(1, d))


def kernel(x, norm_mix, norm_mlp, norm_kv, norm_final, s5_a_re, s5_a_im, s5_log_dt, s5_b_re, s5_b_im, s5_c_re, s5_c_im, s5_d, s5_w_glu, s5_b_glu, w_kv, b_kv, w_q, b_q, sinks, w_o, b_o, w_mlp_in, w_mlp_out):
    bsz, seq, d = x.shape

    w1, cx, lam2_re_t, lam2_im_t = _s5_params(s5_a_re[0], s5_a_im[0], s5_log_dt[0], s5_b_re[0],
                                              s5_b_im[0], s5_c_re[0], s5_c_im[0])
    h = _s5_layer(x, norm_mix[0], w1, lam2_re_t, lam2_im_t, cx, s5_d[0], s5_w_glu[0], s5_b_glu[0])

    h, kv = _mlp_kv(h.reshape(bsz * seq, d), norm_mlp[0], w_mlp_in, w_mlp_out, 0,
                    norm_kv, w_kv, b_kv)

    h = _attn_layer(h.reshape(bsz, seq, d), kv.reshape(bsz, seq, -1), sinks[0], norm_mix[1],
                    w_q[0], b_q[0], w_o[0], b_o[0])

    out = _mlp_final(h.reshape(bsz * seq, d), norm_mlp[1], w_mlp_in, w_mlp_out, 1,
                     norm_final)
    return out.reshape(bsz, seq, d)
```

```python
import functools
import math

import jax
import jax.numpy as jnp
from jax import lax
from jax.experimental import pallas as pl
from jax.experimental.pallas import tpu as pltpu

F32 = jnp.float32
BF16 = jnp.bfloat16

D_MODEL = 1024
BATCH = 8
SEQ = 2048
S5_GROUP = 16
S5_GROUPS = D_MODEL // S5_GROUP
S5_STATE = 64
LAMBDA_RE_MAX = -1e-4
HEAD_DIM = 64
N_Q_HEADS = D_MODEL // HEAD_DIM
N_KV_HEADS = 4
Q_PER_KV = N_Q_HEADS // N_KV_HEADS
WINDOW = 128
D_FF = 4 * D_MODEL
NORM_EPS = 1e-5
LOG2_E = math.log2(math.e)

LANES = 128
SUBLANES = 8
N_LANE_TILES = D_MODEL // LANES
GROUPS_PER_TILE = LANES // S5_GROUP
STATES_PER_TILE = GROUPS_PER_TILE * S5_STATE
MIB = 1024 * 1024

S5_TL = 64
S5_TILES_IN_FLIGHT = 2
MLP_TM = 512
MLP_FF_CHUNK = 1024
ATTN_TQ = 512


def _rmsnorm(x, g):
    return x * lax.rsqrt(jnp.mean(x * x, axis=-1, keepdims=True) + NORM_EPS) * g


def _gelu_tanh(x):
    c = math.sqrt(2.0 / math.pi)
    return 0.5 * x * (1.0 + jnp.tanh(c * (x + 0.044715 * (x * x * x))))


def _const_spec(shape):
    nd = len(shape)
    return pl.BlockSpec(shape, lambda *_: (0,) * nd, pipeline_mode=pl.Buffered(1))


def _s5_params_kernel(abl_ref, bt_ref, cre_ref, cim_ref, w1_ref, cx_ref, l2r_ref, l2i_ref):
    gpt, grp, nst = GROUPS_PER_TILE, S5_GROUP, S5_STATE
    ar = jnp.minimum(abl_ref[0], LAMBDA_RE_MAX)
    ai = abl_ref[1]
    dt = jnp.exp(abl_ref[2])
    mag = jnp.exp(ar * dt)
    ang = ai * dt
    lr = mag * jnp.cos(ang)
    li = mag * jnp.sin(ang)
    den = ar * ar + ai * ai
    cr = ((lr - 1.0) * ar + li * ai) / den
    ci = (li * ar - (lr - 1.0) * ai) / den
    l2r = lr * lr - li * li
    l2i = 2.0 * (lr * li)

    def per_row(v):
        return jnp.concatenate([jnp.broadcast_to(v[g:g + 1], (grp, nst)) for g in range(gpt)], axis=0)

    row_g = lax.broadcasted_iota(jnp.int32, (LANES, STATES_PER_TILE), 0) // grp
    col_g = lax.broadcasted_iota(jnp.int32, (LANES, STATES_PER_TILE), 1) // nst
    own = row_g == col_g

    def blockdiag(v):
        return jnp.where(own, jnp.concatenate([v] * gpt, axis=1), 0.0)

    lr_c, li_c, cr_c, ci_c = per_row(lr), per_row(li), per_row(cr), per_row(ci)
    l2r_c, l2i_c = per_row(l2r), per_row(l2i)
    btr, bti = bt_ref[0], bt_ref[1]
    bbr = cr_c * btr - ci_c * bti
    bbi = cr_c * bti + ci_c * btr
    wbr, wbi = blockdiag(bbr), blockdiag(bbi)
    w1r = blockdiag(lr_c * bbr - li_c * bbi)
    w1i = blockdiag(lr_c * bbi + li_c * bbr)
    ccr, cci = cre_ref[...], cim_ref[...]
    wcr, wci = blockdiag(ccr), blockdiag(cci)

    def hdot_nt(a, b):
        return lax.dot_general(a, b, (((1,), (1,)), ((), ())),
                               precision=lax.Precision.HIGHEST, preferred_element_type=F32)

    k0 = hdot_nt(wbr, wcr) - hdot_nt(wbi, wci)
    k1 = hdot_nt(w1r, wcr) - hdot_nt(w1i, wci)
    top = jnp.concatenate([k0, k1, w1r, w1i], axis=1)
    bot = jnp.concatenate([jnp.zeros_like(k0), k0, wbr, wbi], axis=1)
    w1_ref[...] = jnp.concatenate([top, bot], axis=0).astype(BF16)

    def readout_t(pr, pi):
        return jnp.concatenate([blockdiag(pr * ccr - pi * cci),
                                -blockdiag(pi * ccr + pr * cci)], axis=1)

    cx_t = jnp.concatenate([readout_t(lr_c, li_c), readout_t(l2r_c, l2i_c)], axis=0)
    cx_ref[...] = cx_t.T.astype(BF16)

    def flat_row(v):
        row = jnp.concatenate([v[g:g + 1] for g in range(gpt)], axis=1)
        return jnp.broadcast_to(row, (SUBLANES, STATES_PER_TILE))

    l2r_ref[...] = flat_row(l2r)
    l2i_ref[...] = flat_row(l2i)


def _s5_params(a_re, a_im, log_dt, b_re, b_im, c_re, c_im):
    nt, gpt, sp = N_LANE_TILES, GROUPS_PER_TILE, STATES_PER_TILE
    abl = jnp.stack([a_re, a_im, jnp.broadcast_to(log_dt[:, None], a_re.shape)])
    abl = abl.reshape(3, nt, gpt, S5_STATE)
    bt = jnp.swapaxes(jnp.stack([b_re, b_im]), -1, -2).reshape(2, nt, LANES, S5_STATE)
    c_tile = lambda c: c.reshape(nt, LANES, S5_STATE)
    return pl.pallas_call(
        _s5_params_kernel,
        out_shape=(jax.ShapeDtypeStruct((nt, 2 * LANES, 2 * LANES + 2 * sp), BF16),
                   jax.ShapeDtypeStruct((nt, 2 * sp, 2 * LANES), BF16),
                   jax.ShapeDtypeStruct((nt, SUBLANES, sp), F32),
                   jax.ShapeDtypeStruct((nt, SUBLANES, sp), F32)),
        grid=(nt,),
        in_specs=[pl.BlockSpec((3, None, gpt, S5_STATE), lambda j: (0, j, 0, 0)),
                  pl.BlockSpec((2, None, LANES, S5_STATE), lambda j: (0, j, 0, 0)),
                  pl.BlockSpec((None, LANES, S5_STATE), lambda j: (j, 0, 0)),
                  pl.BlockSpec((None, LANES, S5_STATE), lambda j: (j, 0, 0))],
        out_specs=(pl.BlockSpec((None, 2 * LANES, 2 * LANES + 2 * sp), lambda j: (j, 0, 0)),
                   pl.BlockSpec((None, 2 * sp, 2 * LANES), lambda j: (j, 0, 0)),
                   pl.BlockSpec((None, SUBLANES, sp), lambda j: (j, 0, 0)),
                   pl.BlockSpec((None, SUBLANES, sp), lambda j: (j, 0, 0))),
        compiler_params=pltpu.CompilerParams(dimension_semantics=("parallel",)),
        name="s5_params",
    )(abl, bt, c_tile(c_re), c_tile(c_im))


def _s5_kernel(x_ref, g_ref, w1_ref, l2r_ref, l2i_ref, cx_ref, d_ref, wglu_ref, bglu_ref,
               o_ref, hbm_s, hn_s, hnu_s, zy_s, bu_s, xs_s, y_s, st_s, res_s, *, tl):
    half = tl // 2
    prow = half * SUBLANES
    sp = STATES_PER_TILE
    pitch = tl + SUBLANES

    @pl.when(pl.program_id(0) == 0)
    def _():
        st_s[...] = jnp.zeros_like(st_s)

    for b in range(SUBLANES):
        hn = _rmsnorm(x_ref[b], g_ref[...])
        for j in range(N_LANE_TILES):
            hbm_s[j, b * pitch:b * pitch + tl, :] = hn[:, j * LANES:(j + 1) * LANES]

    for j in range(N_LANE_TILES):
        lanes = slice(j * LANES, (j + 1) * LANES)
        for m in range(tl // 4):
            v = [hbm_s[j, pl.ds(4 * m + i, SUBLANES, stride=pitch), :] for i in range(4)]
            for i in range(4):
                g = (i % 2) * half + 2 * m + i // 2
                hn_s[g * SUBLANES:(g + 1) * SUBLANES, lanes] = v[i]
            r16 = slice(2 * m * SUBLANES, (2 * m + 2) * SUBLANES)
            hnu_s[r16, 2 * j * LANES:(2 * j + 1) * LANES] = (
                jnp.concatenate([v[0], v[2]], axis=0).astype(BF16))
            hnu_s[r16, (2 * j + 1) * LANES:(2 * j + 2) * LANES] = (
                jnp.concatenate([v[1], v[3]], axis=0).astype(BF16))

    def project_in(j):
        z = jnp.dot(hnu_s[:, 2 * j * LANES:(2 * j + 2) * LANES], w1_ref[j],
                    preferred_element_type=F32)
        zy_s[j % S5_TILES_IN_FLIGHT] = z[:, :2 * LANES]
        bu_s[j % S5_TILES_IN_FLIGHT] = z[:, 2 * LANES:]

    ahead = S5_TILES_IN_FLIGHT - 1
    for j in range(ahead):
        project_in(j)
    for j in range(N_LANE_TILES):
        if j + ahead < N_LANE_TILES:
            project_in(j + ahead)
        p = j % S5_TILES_IN_FLIGHT
        lanes = slice(j * LANES, (j + 1) * LANES)
        ar = l2r_ref[j]
        ai = l2i_ref[j]
        xr = st_s[j, :, 0:sp]
        xi = st_s[j, :, sp:2 * sp]
        for k in range(half):
            r = slice(k * SUBLANES, (k + 1) * SUBLANES)
            xs_s[p, r, 0:sp] = xr
            xs_s[p, r, sp:2 * sp] = xi
            nxr = ar * xr - ai * xi + bu_s[p, r, 0:sp]
            nxi = ar * xi + ai * xr + bu_s[p, r, sp:2 * sp]
            xr, xi = nxr, nxi
        st_s[j, :, 0:sp] = xr
        st_s[j, :, sp:2 * sp] = xi
        y = zy_s[p] + jnp.dot(xs_s[p].astype(BF16), cx_ref[j], preferred_element_type=F32)
        y_s[0:prow, lanes] = y[:, :LANES]
        y_s[prow:2 * prow, lanes] = y[:, LANES:]

    for par in range(2):
        rs = slice(par * prow, (par + 1) * prow)
        y = y_s[rs, :] + d_ref[...] * hn_s[rs, :]
        z = jnp.dot(_gelu_tanh(y).astype(BF16), wglu_ref[...].astype(BF16),
                    preferred_element_type=F32) + bglu_ref[...]
        mix = z[:, :D_MODEL] * (1.0 / (1.0 + jnp.exp(-z[:, D_MODEL:])))
        for k in range(half):
            t = 2 * k + par
            for j in range(N_LANE_TILES):
                res_s[j, t * SUBLANES:(t + 1) * SUBLANES, :] = (
                    mix[k * SUBLANES:(k + 1) * SUBLANES, j * LANES:(j + 1) * LANES])

    for b in range(SUBLANES):
        for j in range(N_LANE_TILES):
            lanes = slice(j * LANES, (j + 1) * LANES)
            o_ref[b, :, lanes] = x_ref[b, :, lanes] + res_s[j, pl.ds(b, tl, stride=SUBLANES), :]


def _s5_layer(x, g_mix, w1, lam2_re_t, lam2_im_t, cx, d_skip, w_glu, b_glu):
    bsz, seq, d = x.shape
    tl = S5_TL
    rows = tl * bsz
    prow = rows // 2
    kernel = functools.partial(_s5_kernel, tl=tl)
    return pl.pallas_call(
        kernel,
        out_shape=jax.ShapeDtypeStruct((bsz, seq, d), F32),
        grid=(seq // tl,),
        in_specs=[
            pl.BlockSpec((bsz, tl, d), lambda i: (0, i, 0)),
            _const_spec((1, d)),
            _const_spec(w1.shape),
            _const_spec(lam2_re_t.shape),
            _const_spec(lam2_im_t.shape),
            _const_spec(cx.shape),
            _const_spec((1, d)),
            _const_spec(w_glu.shape),
            _const_spec((1, 2 * d)),
        ],
        out_specs=pl.BlockSpec((bsz, tl, d), lambda i: (0, i, 0)),
        scratch_shapes=[
            pltpu.VMEM((N_LANE_TILES, bsz * (tl + SUBLANES), LANES), F32),
            pltpu.VMEM((rows, d), F32),
            pltpu.VMEM((prow, 2 * d), BF16),
            pltpu.VMEM((S5_TILES_IN_FLIGHT, prow, 2 * LANES), F32),
            pltpu.VMEM((S5_TILES_IN_FLIGHT, prow, 2 * STATES_PER_TILE), F32),
            pltpu.VMEM((S5_TILES_IN_FLIGHT, prow, 2 * STATES_PER_TILE), F32),
            pltpu.VMEM((rows, d), F32),
            pltpu.VMEM((N_LANE_TILES, SUBLANES, 2 * STATES_PER_TILE), F32),
            pltpu.VMEM((N_LANE_TILES, rows, LANES), F32),
        ],
        compiler_params=pltpu.CompilerParams(
            dimension_semantics=("arbitrary",), vmem_limit_bytes=56 * MIB),
        name="s5_layer",
    )(x, g_mix.reshape(1, d), w1, lam2_re_t, lam2_im_t, cx, d_skip.reshape(1, d), w_glu,
      b_glu.reshape(1, 2 * d))


def _inv_rms(x):
    return lax.rsqrt(jnp.mean(x * x, axis=-1, keepdims=True) + NORM_EPS)


def _mlp_kernel(h_ref, g_ref, win_hbm, wout_hbm, *rest, layer, with_kv):
    if with_kv:
        gkv_ref, wkv_ref, bkv_ref, o_ref, kv_ref, win_s, wout_s, sem = rest
    else:
        gfin_ref, o_ref, win_s, wout_s, sem = rest
    n_chunks = D_FF // MLP_FF_CHUNK

    def weight_copies(c):
        cols = pl.ds(c * MLP_FF_CHUNK, MLP_FF_CHUNK)
        return (pltpu.make_async_copy(win_hbm.at[layer, :, cols], win_s.at[:, cols], sem.at[0, c]),
                pltpu.make_async_copy(wout_hbm.at[layer, cols, :], wout_s.at[cols, :], sem.at[1, c]))

    def body(first_step):
        h = h_ref[...]
        hg = (h * g_ref[...]).astype(BF16)
        r = _inv_rms(h)
        acc = jnp.zeros(h.shape, F32)
        for c in range(n_chunks):
            cols = slice(c * MLP_FF_CHUNK, (c + 1) * MLP_FF_CHUNK)
            if first_step:
                for copy in weight_copies(c):
                    copy.wait()
            a = jnp.dot(hg, win_s[:, cols].astype(BF16), preferred_element_type=F32)
            a = jnp.square(jnp.maximum(a, 0.0)).astype(BF16)
            acc = acc + jnp.dot(a, wout_s[cols, :].astype(BF16), preferred_element_type=F32)
        out = h + (r * r) * acc
        if with_kv:
            o_ref[...] = out
            kv = jnp.dot((out * gkv_ref[...]).astype(BF16), wkv_ref[...].astype(BF16),
                         preferred_element_type=F32)
            kv_ref[...] = (_inv_rms(out) * kv + bkv_ref[...]).astype(BF16)
        else:
            o_ref[...] = out * _inv_rms(out) * gfin_ref[...]

    @pl.when(pl.program_id(0) == 0)
    def _():
        for c in range(n_chunks):
            for copy in weight_copies(c):
                copy.start()
        body(True)

    @pl.when(pl.program_id(0) > 0)
    def _():
        body(False)


def _mlp(h, g, w_in, w_out, layer, *, g_kv=None, w_kv=None, b_kv=None, g_fin=None):
    t, d = h.shape
    with_kv = w_kv is not None
    row = lambda n: pl.BlockSpec((MLP_TM, n), lambda i: (i, 0))
    hbm = pl.BlockSpec(memory_space=pl.ANY)
    in_specs = [row(d), _const_spec((1, d)), hbm, hbm]
    operands = [h, g.reshape(1, d), w_in, w_out]
    if with_kv:
        nkv = w_kv.shape[1]
        in_specs += [_const_spec((1, d)), _const_spec(w_kv.shape), _const_spec((1, nkv))]
        operands += [g_kv.reshape(1, d), w_kv, b_kv.reshape(1, nkv)]
        out_shape = (jax.ShapeDtypeStruct((t, d), F32), jax.ShapeDtypeStruct((t, nkv), BF16))
        out_specs = (row(d), row(nkv))
    else:
        in_specs += [_const_spec((1, d))]
        operands += [g_fin.reshape(1, d)]
        out_shape = jax.ShapeDtypeStruct((t, d), F32)
        out_specs = row(d)
    return pl.pallas_call(
        functools.partial(_mlp_kernel, layer=layer, with_kv=with_kv),
        out_shape=out_shape,
        grid=(t // MLP_TM,),
        in_specs=in_specs,
        out_specs=out_specs,
        scratch_shapes=[pltpu.VMEM(w_in.shape[1:], F32),
                        pltpu.VMEM(w_out.shape[1:], F32),
                        pltpu.SemaphoreType.DMA((2, D_FF // MLP_FF_CHUNK))],
        compiler_params=pltpu.CompilerParams(
            dimension_semantics=("arbitrary",), vmem_limit_bytes=56 * MIB),
        name="mlp_kv" if with_kv else "mlp_final",
    )(*operands)


def _pair_tile_sources(a, g):
    head_lo = (2 * a) * Q_PER_KV + g
    head_hi = (2 * a + 1) * Q_PER_KV + g
    return (head_lo // 2, head_lo % 2), (head_hi // 2, head_hi % 2)


def _attn_kernel(sink_ref, h_ref, kvp_ref, kvc_ref, g_ref, wq_ref, bq_ref, wo_ref, bo_ref,
                 o_ref, wq_s, bq_s, wo_s, q_s, oh_s, *, tq):
    nsb = tq // WINDOW
    n = pl.program_id(1)
    n_kv_tiles = N_KV_HEADS // 2
    lo_row = lax.broadcasted_iota(jnp.int32, (1, LANES), 1) < HEAD_DIM

    @pl.when((pl.program_id(0) == 0) & (n == 0))
    def _():
        for a in range(n_kv_tiles):
            for g in range(Q_PER_KV):
                (t_lo, h_lo), (t_hi, h_hi) = _pair_tile_sources(a, g)
                dst = slice((a * Q_PER_KV + g) * LANES, (a * Q_PER_KV + g + 1) * LANES)

                def pair(ref):
                    src_lo = ref[:, t_lo * LANES:(t_lo + 1) * LANES]
                    src_hi = ref[:, t_hi * LANES:(t_hi + 1) * LANES]
                    if h_lo == 1:
                        src_lo = pltpu.roll(src_lo, HEAD_DIM, axis=1)
                    if h_hi == 0:
                        src_hi = pltpu.roll(src_hi, HEAD_DIM, axis=1)
                    return jnp.where(lo_row, src_lo, src_hi)

                wq_s[:, dst] = pair(wq_ref).astype(BF16)
                bq_s[:, dst] = pair(bq_ref)
                for half in range(2):
                    head = (2 * a + half) * Q_PER_KV + g
                    r0 = (a * Q_PER_KV + g) * LANES + half * HEAD_DIM
                    wo_s[r0:r0 + HEAD_DIM, :] = (
                        wo_ref[head * HEAD_DIM:(head + 1) * HEAD_DIM, :].astype(BF16))

    h = h_ref[0]
    hn = _rmsnorm(h, g_ref[...]).astype(BF16)
    q = jnp.dot(hn, wq_s[...], preferred_element_type=F32) + bq_s[...]
    q_s[...] = (q * (LOG2_E / math.sqrt(HEAD_DIM))).astype(BF16)

    lo = lax.broadcasted_iota(jnp.int32, (WINDOW, LANES), 1) < HEAD_DIM
    own = (lax.broadcasted_iota(jnp.int32, (WINDOW, WINDOW), 1)
           <= lax.broadcasted_iota(jnp.int32, (WINDOW, WINDOW), 0))
    zero = jnp.zeros((WINDOW, LANES), BF16)
    fzero = jnp.zeros((WINDOW, WINDOW), F32)

    for sb in range(nsb):
        rows = slice(sb * WINDOW, (sb + 1) * WINDOW)
        prev_bias = jnp.where(n == 0, -jnp.inf, 0.0).astype(F32) if sb == 0 else None
        for a in range(n_kv_tiles):
            kl = slice(a * LANES, (a + 1) * LANES)
            vl = slice((n_kv_tiles + a) * LANES, (n_kv_tiles + a + 1) * LANES)
            if sb == 0:
                kprev, vprev = kvp_ref[0, :, kl], kvp_ref[0, :, vl]
            else:
                prow = slice((sb - 1) * WINDOW, sb * WINDOW)
                kprev, vprev = kvc_ref[0, prow, kl], kvc_ref[0, prow, vl]
            kd = jnp.concatenate([kprev, kvc_ref[0, rows, kl]], axis=0)
            vd = jnp.concatenate([vprev, kvc_ref[0, rows, vl]], axis=0)
            vd1 = jnp.concatenate([vd, jnp.ones((2 * WINDOW, LANES), BF16)], axis=1)
            qt = [q_s[rows, (a * Q_PER_KV + g) * LANES:(a * Q_PER_KV + g + 1) * LANES]
                  for g in range(Q_PER_KV)]
            lhs = jnp.concatenate([jnp.where(lo, t, zero) for t in qt]
                                  + [jnp.where(lo, zero, t) for t in qt], axis=0)
            s = lax.dot_general(lhs, kd, (((1,), (1,)), ((), ())),
                                preferred_element_type=F32)
            ps, sink_terms = [], []
            for half in range(2):
                for g in range(Q_PER_KV):
                    blk = half * Q_PER_KV + g
                    s_prev = s[blk * WINDOW:(blk + 1) * WINDOW, :WINDOW]
                    s_own = s[blk * WINDOW:(blk + 1) * WINDOW, WINDOW:]
                    if prev_bias is not None:
                        s_prev = s_prev + prev_bias
                    sg = jnp.where(own, s_own, s_prev)
                    sink = sink_ref[(2 * a + half) * Q_PER_KV + g] * LOG2_E
                    m = jnp.maximum(jnp.max(sg, axis=-1, keepdims=True), sink)
                    p = jnp.exp2(sg - m)
                    ps.append(jnp.concatenate([jnp.where(own, fzero, p).astype(BF16),
                                               jnp.where(own, p, fzero).astype(BF16)], axis=1))
                    sink_terms.append(jnp.exp2(sink - m))
            od = jnp.dot(jnp.concatenate(ps, axis=0), vd1, preferred_element_type=F32)
            for g in range(Q_PER_KV):
                r_lo = slice(g * WINDOW, (g + 1) * WINDOW)
                r_hi = slice((Q_PER_KV + g) * WINDOW, (Q_PER_KV + g + 1) * WINDOW)
                num = jnp.where(lo, od[r_lo, :LANES], od[r_hi, :LANES])
                den = (jnp.where(lo, od[r_lo, LANES:], od[r_hi, LANES:])
                       + jnp.where(lo, sink_terms[g], sink_terms[Q_PER_KV + g]))
                oh_s[rows, (a * Q_PER_KV + g) * LANES:(a * Q_PER_KV + g + 1) * LANES] = (
                    (num * (1.0 / den)).astype(BF16))

    o_ref[0] = h + jnp.dot(oh_s[...], wo_s[...], preferred_element_type=F32) + bo_ref[...]


def _attn_layer(h, kv, sinks, g, w_q, b_q, w_o, b_o):
    bsz, seq, d = h.shape
    nkv = kv.shape[-1]
    tq = ATTN_TQ
    nsb = tq // WINDOW
    kernel = functools.partial(_attn_kernel, tq=tq)
    return pl.pallas_call(
        kernel,
        out_shape=jax.ShapeDtypeStruct((bsz, seq, d), F32),
        grid=(bsz, seq // tq),
        in_specs=[
            pl.BlockSpec(memory_space=pltpu.SMEM),
            pl.BlockSpec((1, tq, d), lambda b, n: (b, n, 0)),
            pl.BlockSpec((1, WINDOW, nkv), lambda b, n: (b, jnp.maximum(n * nsb - 1, 0), 0)),
            pl.BlockSpec((1, tq, nkv), lambda b, n: (b, n, 0)),
            _const_spec((1, d)),
            _const_spec(w_q.shape),
            _const_spec((1, d)),
            _const_spec(w_o.shape),
            _const_spec((1, d)),
        ],
        out_specs=pl.BlockSpec((1, tq, d), lambda b, n: (b, n, 0)),
        scratch_shapes=[pltpu.VMEM((d, d), BF16),
                        pltpu.VMEM((1, d), F32),
                        pltpu.VMEM((d, d), BF16),
                        pltpu.VMEM((tq, d), BF16),
                        pltpu.VMEM((tq, d), BF16)],
        compiler_params=pltpu.CompilerParams(
            dimension_semantics=("arbitrary", "arbitrary"), vmem_limit_bytes=40 * MIB),
        name="attn",
    )(sinks, h, kv, kv, g.reshape(1, d), w_q, b_q.reshape(1, d), w_o, b_o.reshape(1, d))


def kernel(x, norm_mix, norm_mlp, norm_kv, norm_final, s5_a_re, s5_a_im, s5_log_dt, s5_b_re, s5_b_im, s5_c_re, s5_c_im, s5_d, s5_w_glu, s5_b_glu, w_kv, b_kv, w_q, b_q, sinks, w_o, b_o, w_mlp_in, w_mlp_out):
    bsz, seq, d = x.shape

    w1, cx, lam2_re_t, lam2_im_t = _s5_params(s5_a_re[0], s5_a_im[0], s5_log_dt[0], s5_b_re[0],
                                              s5_b_im[0], s5_c_re[0], s5_c_im[0])
    h = _s5_layer(x, norm_mix[0], w1, lam2_re_t, lam2_im_t, cx, s5_d[0], s5_w_glu[0], s5_b_glu[0])

    h, kv = _mlp(h.reshape(bsz * seq, d), norm_mlp[0], w_mlp_in, w_mlp_out, 0,
                 g_kv=norm_kv, w_kv=w_kv, b_kv=b_kv)

    h = _attn_layer(h.reshape(bsz, seq, d), kv.reshape(bsz, seq, -1), sinks[0], norm_mix[1],
                    w_q[0], b_q[0], w_o[0], b_o[0])

    out = _mlp(h.reshape(bsz * seq, d), norm_mlp[1], w_mlp_in, w_mlp_out, 1, g_fin=norm_final)
    return out.reshape(bsz, seq, d)
```

```python
import functools
import math

import jax
import jax.numpy as jnp
from jax import lax
from jax.experimental import pallas as pl
from jax.experimental.pallas import tpu as pltpu

F32 = jnp.float32
BF16 = jnp.bfloat16

D_MODEL = 1024
BATCH = 8
SEQ = 2048
S5_GROUP = 16
S5_GROUPS = D_MODEL // S5_GROUP
S5_STATE = 64
LAMBDA_RE_MAX = -1e-4
HEAD_DIM = 64
N_Q_HEADS = D_MODEL // HEAD_DIM
N_KV_HEADS = 4
Q_PER_KV = N_Q_HEADS // N_KV_HEADS
WINDOW = 128
D_FF = 4 * D_MODEL
NORM_EPS = 1e-5
LOG2_E = math.log2(math.e)

LANES = 128
SUBLANES = 8
N_LANE_TILES = D_MODEL // LANES
GROUPS_PER_TILE = LANES // S5_GROUP
STATES_PER_TILE = GROUPS_PER_TILE * S5_STATE
MIB = 1024 * 1024

S5_TL = 64
S5_TILES_IN_FLIGHT = 2
MLP_TM = 512
MLP_FF_CHUNK = 1024
ATTN_TQ = 512


def _rmsnorm(x, g):
    return x * lax.rsqrt(jnp.mean(x * x, axis=-1, keepdims=True) + NORM_EPS) * g


def _gelu_tanh(x):
    c = math.sqrt(2.0 / math.pi)
    return 0.5 * x * (1.0 + jnp.tanh(c * (x + 0.044715 * (x * x * x))))


def _const_spec(shape):
    nd = len(shape)
    return pl.BlockSpec(shape, lambda *_: (0,) * nd, pipeline_mode=pl.Buffered(1))


def _s5_params_kernel(abl_ref, bt_ref, cre_ref, cim_ref, w1_ref, cx_ref, l2r_ref, l2i_ref):
    gpt, grp, nst = GROUPS_PER_TILE, S5_GROUP, S5_STATE
    ar = jnp.minimum(abl_ref[0], LAMBDA_RE_MAX)
    ai = abl_ref[1]
    dt = jnp.exp(abl_ref[2])
    mag = jnp.exp(ar * dt)
    ang = ai * dt
    lr = mag * jnp.cos(ang)
    li = mag * jnp.sin(ang)
    den = ar * ar + ai * ai
    cr = ((lr - 1.0) * ar + li * ai) / den
    ci = (li * ar - (lr - 1.0) * ai) / den
    l2r = lr * lr - li * li
    l2i = 2.0 * (lr * li)

    def per_row(v):
        return jnp.concatenate([jnp.broadcast_to(v[g:g + 1], (grp, nst)) for g in range(gpt)], axis=0)

    row_g = lax.broadcasted_iota(jnp.int32, (LANES, STATES_PER_TILE), 0) // grp
    col_g = lax.broadcasted_iota(jnp.int32, (LANES, STATES_PER_TILE), 1) // nst
    own = row_g == col_g

    def blockdiag(v):
        return jnp.where(own, jnp.concatenate([v] * gpt, axis=1), 0.0)

    lr_c, li_c, cr_c, ci_c = per_row(lr), per_row(li), per_row(cr), per_row(ci)
    l2r_c, l2i_c = per_row(l2r), per_row(l2i)
    btr, bti = bt_ref[0], bt_ref[1]
    bbr = cr_c * btr - ci_c * bti
    bbi = cr_c * bti + ci_c * btr
    wbr, wbi = blockdiag(bbr), blockdiag(bbi)
    w1r = blockdiag(lr_c * bbr - li_c * bbi)
    w1i = blockdiag(lr_c * bbi + li_c * bbr)
    ccr, cci = cre_ref[...], cim_ref[...]
    wcr, wci = blockdiag(ccr), blockdiag(cci)

    def hdot_nt(a, b):
        return lax.dot_general(a, b, (((1,), (1,)), ((), ())),
                               precision=lax.Precision.HIGHEST, preferred_element_type=F32)

    k0 = hdot_nt(wbr, wcr) - hdot_nt(wbi, wci)
    k1 = hdot_nt(w1r, wcr) - hdot_nt(w1i, wci)
    top = jnp.concatenate([k0, k1, w1r, w1i], axis=1)
    bot = jnp.concatenate([jnp.zeros_like(k0), k0, wbr, wbi], axis=1)
    w1_ref[...] = jnp.concatenate([top, bot], axis=0).astype(BF16)

    def readout_t(pr, pi):
        return jnp.concatenate([blockdiag(pr * ccr - pi * cci),
                                -blockdiag(pi * ccr + pr * cci)], axis=1)

    cx_t = jnp.concatenate([readout_t(lr_c, li_c), readout_t(l2r_c, l2i_c)], axis=0)
    cx_ref[...] = cx_t.T.astype(BF16)

    def flat_row(v):
        row = jnp.concatenate([v[g:g + 1] for g in range(gpt)], axis=1)
        return jnp.broadcast_to(row, (SUBLANES, STATES_PER_TILE))

    l2r_ref[...] = flat_row(l2r)
    l2i_ref[...] = flat_row(l2i)


def _s5_params(a_re, a_im, log_dt, b_re, b_im, c_re, c_im):
    nt, gpt, sp = N_LANE_TILES, GROUPS_PER_TILE, STATES_PER_TILE
    abl = jnp.stack([a_re, a_im, jnp.broadcast_to(log_dt[:, None], a_re.shape)])
    abl = abl.reshape(3, nt, gpt, S5_STATE)
    bt = jnp.swapaxes(jnp.stack([b_re, b_im]), -1, -2).reshape(2, nt, LANES, S5_STATE)
    c_tile = lambda c: c.reshape(nt, LANES, S5_STATE)
    return pl.pallas_call(
        _s5_params_kernel,
        out_shape=(jax.ShapeDtypeStruct((nt, 2 * LANES, 2 * LANES + 2 * sp), BF16),
                   jax.ShapeDtypeStruct((nt, 2 * sp, 2 * LANES), BF16),
                   jax.ShapeDtypeStruct((nt, SUBLANES, sp), F32),
                   jax.ShapeDtypeStruct((nt, SUBLANES, sp), F32)),
        grid=(nt,),
        in_specs=[pl.BlockSpec((3, None, gpt, S5_STATE), lambda j: (0, j, 0, 0)),
                  pl.BlockSpec((2, None, LANES, S5_STATE), lambda j: (0, j, 0, 0)),
                  pl.BlockSpec((None, LANES, S5_STATE), lambda j: (j, 0, 0)),
                  pl.BlockSpec((None, LANES, S5_STATE), lambda j: (j, 0, 0))],
        out_specs=(pl.BlockSpec((None, 2 * LANES, 2 * LANES + 2 * sp), lambda j: (j, 0, 0)),
                   pl.BlockSpec((None, 2 * sp, 2 * LANES), lambda j: (j, 0, 0)),
                   pl.BlockSpec((None, SUBLANES, sp), lambda j: (j, 0, 0)),
                   pl.BlockSpec((None, SUBLANES, sp), lambda j: (j, 0, 0))),
        compiler_params=pltpu.CompilerParams(dimension_semantics=("parallel",)),
        name="s5_params",
    )(abl, bt, c_tile(c_re), c_tile(c_im))


def _s5_kernel(x_ref, g_ref, w1_ref, l2r_ref, l2i_ref, cx_ref, d_ref, wglu_ref, bglu_ref,
               o_ref, hbm_s, hn_s, hnu_s, zy_s, bu_s, xs_s, y_s, st_s, res_s, *, tl):
    half = tl // 2
    prow = half * SUBLANES
    sp = STATES_PER_TILE
    pitch = tl + SUBLANES

    @pl.when(pl.program_id(0) == 0)
    def _():
        st_s[...] = jnp.zeros_like(st_s)

    for b in range(SUBLANES):
        hn = _rmsnorm(x_ref[b], g_ref[...])
        for j in range(N_LANE_TILES):
            hbm_s[j, b * pitch:b * pitch + tl, :] = hn[:, j * LANES:(j + 1) * LANES]

    for j in range(N_LANE_TILES):
        lanes = slice(j * LANES, (j + 1) * LANES)
        for m in range(tl // 4):
            v = [hbm_s[j, pl.ds(4 * m + i, SUBLANES, stride=pitch), :] for i in range(4)]
            for i in range(4):
                g = (i % 2) * half + 2 * m + i // 2
                hn_s[g * SUBLANES:(g + 1) * SUBLANES, lanes] = v[i]
            r16 = slice(2 * m * SUBLANES, (2 * m + 2) * SUBLANES)
            hnu_s[r16, 2 * j * LANES:(2 * j + 1) * LANES] = (
                jnp.concatenate([v[0], v[2]], axis=0).astype(BF16))
            hnu_s[r16, (2 * j + 1) * LANES:(2 * j + 2) * LANES] = (
                jnp.concatenate([v[1], v[3]], axis=0).astype(BF16))

    def project_in(j):
        z = jnp.dot(hnu_s[:, 2 * j * LANES:(2 * j + 2) * LANES], w1_ref[j],
                    preferred_element_type=F32)
        zy_s[j % S5_TILES_IN_FLIGHT] = z[:, :2 * LANES]
        bu_s[j % S5_TILES_IN_FLIGHT] = z[:, 2 * LANES:]

    ahead = S5_TILES_IN_FLIGHT - 1
    for j in range(ahead):
        project_in(j)
    for j in range(N_LANE_TILES):
        if j + ahead < N_LANE_TILES:
            project_in(j + ahead)
        p = j % S5_TILES_IN_FLIGHT
        lanes = slice(j * LANES, (j + 1) * LANES)
        ar = l2r_ref[j]
        ai = l2i_ref[j]
        xr = st_s[j, :, 0:sp]
        xi = st_s[j, :, sp:2 * sp]
        for k in range(half):
            r = slice(k * SUBLANES, (k + 1) * SUBLANES)
            xs_s[p, r, 0:sp] = xr
            xs_s[p, r, sp:2 * sp] = xi
            nxr = ar * xr - ai * xi + bu_s[p, r, 0:sp]
            nxi = ar * xi + ai * xr + bu_s[p, r, sp:2 * sp]
            xr, xi = nxr, nxi
        st_s[j, :, 0:sp] = xr
        st_s[j, :, sp:2 * sp] = xi
        y = zy_s[p] + jnp.dot(xs_s[p].astype(BF16), cx_ref[j], preferred_element_type=F32)
        y_s[0:prow, lanes] = y[:, :LANES]
        y_s[prow:2 * prow, lanes] = y[:, LANES:]

    for par in range(2):
        rs = slice(par * prow, (par + 1) * prow)
        y = y_s[rs, :] + d_ref[...] * hn_s[rs, :]
        z = jnp.dot(_gelu_tanh(y).astype(BF16), wglu_ref[...].astype(BF16),
                    preferred_element_type=F32) + bglu_ref[...]
        mix = z[:, :D_MODEL] * (1.0 / (1.0 + jnp.exp(-z[:, D_MODEL:])))
        for k in range(half):
            t = 2 * k + par
            for j in range(N_LANE_TILES):
                res_s[j, t * SUBLANES:(t + 1) * SUBLANES, :] = (
                    mix[k * SUBLANES:(k + 1) * SUBLANES, j * LANES:(j + 1) * LANES])

    for b in range(SUBLANES):
        for j in range(N_LANE_TILES):
            lanes = slice(j * LANES, (j + 1) * LANES)
            o_ref[b, :, lanes] = x_ref[b, :, lanes] + res_s[j, pl.ds(b, tl, stride=SUBLANES), :]


def _s5_layer(x, g_mix, w1, lam2_re_t, lam2_im_t, cx, d_skip, w_glu, b_glu):
    bsz, seq, d = x.shape
    tl = S5_TL
    rows = tl * bsz
    prow = rows // 2
    kernel = functools.partial(_s5_kernel, tl=tl)
    return pl.pallas_call(
        kernel,
        out_shape=jax.ShapeDtypeStruct((bsz, seq, d), F32),
        grid=(seq // tl,),
        in_specs=[
            pl.BlockSpec((bsz, tl, d), lambda i: (0, i, 0)),
            _const_spec((1, d)),
            _const_spec(w1.shape),
            _const_spec(lam2_re_t.shape),
            _const_spec(lam2_im_t.shape),
            _const_spec(cx.shape),
            _const_spec((1, d)),
            _const_spec(w_glu.shape),
            _const_spec((1, 2 * d)),
        ],
        out_specs=pl.BlockSpec((bsz, tl, d), lambda i: (0, i, 0)),
        scratch_shapes=[
            pltpu.VMEM((N_LANE_TILES, bsz * (tl + SUBLANES), LANES), F32),
            pltpu.VMEM((rows, d), F32),
            pltpu.VMEM((prow, 2 * d), BF16),
            pltpu.VMEM((S5_TILES_IN_FLIGHT, prow, 2 * LANES), F32),
            pltpu.VMEM((S5_TILES_IN_FLIGHT, prow, 2 * STATES_PER_TILE), F32),
            pltpu.VMEM((S5_TILES_IN_FLIGHT, prow, 2 * STATES_PER_TILE), F32),
            pltpu.VMEM((rows, d), F32),
            pltpu.VMEM((N_LANE_TILES, SUBLANES, 2 * STATES_PER_TILE), F32),
            pltpu.VMEM((N_LANE_TILES, rows, LANES), F32),
        ],
        compiler_params=pltpu.CompilerParams(
            dimension_semantics=("arbitrary",), vmem_limit_bytes=56 * MIB),
        name="s5_layer",
    )(x, g_mix.reshape(1, d), w1, lam2_re_t, lam2_im_t, cx, d_skip.reshape(1, d), w_glu,
      b_glu.reshape(1, 2 * d))


def _inv_rms(x):
    return lax.rsqrt(jnp.mean(x * x, axis=-1, keepdims=True) + NORM_EPS)


def _mlp_kernel(h_ref, g_ref, win_hbm, wout_hbm, *rest, layer, with_kv):
    if with_kv:
        gkv_ref, wkv_ref, bkv_ref, o_ref, kv_ref, win_s, wout_s, sem = rest
    else:
        gfin_ref, o_ref, win_s, wout_s, sem = rest
    n_chunks = D_FF // MLP_FF_CHUNK

    def weight_copies(c):
        cols = pl.ds(c * MLP_FF_CHUNK, MLP_FF_CHUNK)
        return (pltpu.make_async_copy(win_hbm.at[layer, :, cols], win_s.at[:, cols], sem.at[0, c]),
                pltpu.make_async_copy(wout_hbm.at[layer, cols, :], wout_s.at[cols, :], sem.at[1, c]))

    def body(first_step):
        h = h_ref[...]
        hg = (h * g_ref[...]).astype(BF16)
        r = _inv_rms(h)
        acc = jnp.zeros(h.shape, F32)
        for c in range(n_chunks):
            cols = slice(c * MLP_FF_CHUNK, (c + 1) * MLP_FF_CHUNK)
            if first_step:
                for copy in weight_copies(c):
                    copy.wait()
                if c + 1 < n_chunks:
                    for copy in weight_copies(c + 1):
                        copy.start()
            a = jnp.dot(hg, win_s[:, cols].astype(BF16), preferred_element_type=F32)
            a = jnp.square(jnp.maximum(a, 0.0)).astype(BF16)
            acc = acc + jnp.dot(a, wout_s[cols, :].astype(BF16), preferred_element_type=F32)
        out = h + (r * r) * acc
        if with_kv:
            o_ref[...] = out
            kv = jnp.dot((out * gkv_ref[...]).astype(BF16), wkv_ref[...].astype(BF16),
                         preferred_element_type=F32)
            kv_ref[...] = (_inv_rms(out) * kv + bkv_ref[...]).astype(BF16)
        else:
            o_ref[...] = out * _inv_rms(out) * gfin_ref[...]

    @pl.when(pl.program_id(0) == 0)
    def _():
        for copy in weight_copies(0):
            copy.start()
        body(True)

    @pl.when(pl.program_id(0) > 0)
    def _():
        body(False)


def _mlp(h, g, w_in, w_out, layer, *, g_kv=None, w_kv=None, b_kv=None, g_fin=None):
    t, d = h.shape
    with_kv = w_kv is not None
    row = lambda n: pl.BlockSpec((MLP_TM, n), lambda i: (i, 0))
    hbm = pl.BlockSpec(memory_space=pl.ANY)
    in_specs = [row(d), _const_spec((1, d)), hbm, hbm]
    operands = [h, g.reshape(1, d), w_in, w_out]
    if with_kv:
        nkv = w_kv.shape[1]
        in_specs += [_const_spec((1, d)), _const_spec(w_kv.shape), _const_spec((1, nkv))]
        operands += [g_kv.reshape(1, d), w_kv, b_kv.reshape(1, nkv)]
        out_shape = (jax.ShapeDtypeStruct((t, d), F32), jax.ShapeDtypeStruct((t, nkv), BF16))
        out_specs = (row(d), row(nkv))
    else:
        in_specs += [_const_spec((1, d))]
        operands += [g_fin.reshape(1, d)]
        out_shape = jax.ShapeDtypeStruct((t, d), F32)
        out_specs = row(d)
    return pl.pallas_call(
        functools.partial(_mlp_kernel, layer=layer, with_kv=with_kv),
        out_shape=out_shape,
        grid=(t // MLP_TM,),
        in_specs=in_specs,
        out_specs=out_specs,
        scratch_shapes=[pltpu.VMEM(w_in.shape[1:], F32),
                        pltpu.VMEM(w_out.shape[1:], F32),
                        pltpu.SemaphoreType.DMA((2, D_FF // MLP_FF_CHUNK))],
        compiler_params=pltpu.CompilerParams(
            dimension_semantics=("arbitrary",), vmem_limit_bytes=56 * MIB),
        name="mlp_kv" if with_kv else "mlp_final",
    )(*operands)


def _pair_tile_sources(a, g):
    head_lo = (2 * a) * Q_PER_KV + g
    head_hi = (2 * a + 1) * Q_PER_KV + g
    return (head_lo // 2, head_lo % 2), (head_hi // 2, head_hi % 2)


def _attn_kernel(sink_ref, h_ref, kvp_ref, kvc_ref, g_ref, wq_ref, bq_ref, wo_ref, bo_ref,
                 o_ref, wq_s, bq_s, wo_s, q_s, oh_s, *, tq):
    nsb = tq // WINDOW
    n = pl.program_id(1)
    n_kv_tiles = N_KV_HEADS // 2
    lo_row = lax.broadcasted_iota(jnp.int32, (1, LANES), 1) < HEAD_DIM

    @pl.when((pl.program_id(0) == 0) & (n == 0))
    def _():
        for a in range(n_kv_tiles):
            for g in range(Q_PER_KV):
                (t_lo, h_lo), (t_hi, h_hi) = _pair_tile_sources(a, g)
                dst = slice((a * Q_PER_KV + g) * LANES, (a * Q_PER_KV + g + 1) * LANES)

                def pair(ref):
                    src_lo = ref[:, t_lo * LANES:(t_lo + 1) * LANES]
                    src_hi = ref[:, t_hi * LANES:(t_hi + 1) * LANES]
                    if h_lo == 1:
                        src_lo = pltpu.roll(src_lo, HEAD_DIM, axis=1)
                    if h_hi == 0:
                        src_hi = pltpu.roll(src_hi, HEAD_DIM, axis=1)
                    return jnp.where(lo_row, src_lo, src_hi)

                wq_s[:, dst] = pair(wq_ref).astype(BF16)
                bq_s[:, dst] = pair(bq_ref)
                for half in range(2):
                    head = (2 * a + half) * Q_PER_KV + g
                    r0 = (a * Q_PER_KV + g) * LANES + half * HEAD_DIM
                    wo_s[r0:r0 + HEAD_DIM, :] = (
                        wo_ref[head * HEAD_DIM:(head + 1) * HEAD_DIM, :].astype(BF16))

    h = h_ref[0]
    hn = _rmsnorm(h, g_ref[...]).astype(BF16)
    q = jnp.dot(hn, wq_s[...], preferred_element_type=F32) + bq_s[...]
    q_s[...] = (q * (LOG2_E / math.sqrt(HEAD_DIM))).astype(BF16)

    lo = lax.broadcasted_iota(jnp.int32, (WINDOW, LANES), 1) < HEAD_DIM
    own = (lax.broadcasted_iota(jnp.int32, (WINDOW, WINDOW), 1)
           <= lax.broadcasted_iota(jnp.int32, (WINDOW, WINDOW), 0))
    zero = jnp.zeros((WINDOW, LANES), BF16)
    fzero = jnp.zeros((WINDOW, WINDOW), F32)

    for sb in range(nsb):
        rows = slice(sb * WINDOW, (sb + 1) * WINDOW)
        prev_bias = jnp.where(n == 0, -jnp.inf, 0.0).astype(F32) if sb == 0 else None
        for a in range(n_kv_tiles):
            kl = slice(a * LANES, (a + 1) * LANES)
            vl = slice((n_kv_tiles + a) * LANES, (n_kv_tiles + a + 1) * LANES)
            if sb == 0:
                kprev, vprev = kvp_ref[0, :, kl], kvp_ref[0, :, vl]
            else:
                prow = slice((sb - 1) * WINDOW, sb * WINDOW)
                kprev, vprev = kvc_ref[0, prow, kl], kvc_ref[0, prow, vl]
            kd = jnp.concatenate([kprev, kvc_ref[0, rows, kl]], axis=0)
            vd = jnp.concatenate([vprev, kvc_ref[0, rows, vl]], axis=0)
            vd1 = jnp.concatenate([vd, jnp.ones((2 * WINDOW, LANES), BF16)], axis=1)
            qt = [q_s[rows, (a * Q_PER_KV + g) * LANES:(a * Q_PER_KV + g + 1) * LANES]
                  for g in range(Q_PER_KV)]
            lhs = jnp.concatenate([jnp.where(lo, t, zero) for t in qt]
                                  + [jnp.where(lo, zero, t) for t in qt], axis=0)
            s = lax.dot_general(lhs, kd, (((1,), (1,)), ((), ())),
                                preferred_element_type=F32)
            ps, sink_terms = [], []
            for half in range(2):
                for g in range(Q_PER_KV):
                    blk = half * Q_PER_KV + g
                    s_prev = s[blk * WINDOW:(blk + 1) * WINDOW, :WINDOW]
                    s_own = s[blk * WINDOW:(blk + 1) * WINDOW, WINDOW:]
                    if prev_bias is not None:
                        s_prev = s_prev + prev_bias
                    sg = jnp.where(own, s_own, s_prev)
                    sink = sink_ref[(2 * a + half) * Q_PER_KV + g] * LOG2_E
                    m = jnp.maximum(jnp.max(sg, axis=-1, keepdims=True), sink)
                    p = jnp.exp2(sg - m)
                    ps.append(jnp.concatenate([jnp.where(own, fzero, p).astype(BF16),
                                               jnp.where(own, p, fzero).astype(BF16)], axis=1))
                    sink_terms.append(jnp.exp2(sink - m))
            od = jnp.dot(jnp.concatenate(ps, axis=0), vd1, preferred_element_type=F32)
            for g in range(Q_PER_KV):
                r_lo = slice(g * WINDOW, (g + 1) * WINDOW)
                r_hi = slice((Q_PER_KV + g) * WINDOW, (Q_PER_KV + g + 1) * WINDOW)
                num = jnp.where(lo, od[r_lo, :LANES], od[r_hi, :LANES])
                den = (jnp.where(lo, od[r_lo, LANES:], od[r_hi, LANES:])
                       + jnp.where(lo, sink_terms[g], sink_terms[Q_PER_KV + g]))
                oh_s[rows, (a * Q_PER_KV + g) * LANES:(a * Q_PER_KV + g + 1) * LANES] = (
                    (num * (1.0 / den)).astype(BF16))

    o_ref[0] = h + jnp.dot(oh_s[...], wo_s[...], preferred_element_type=F32) + bo_ref[...]


def _attn_layer(h, kv, sinks, g, w_q, b_q, w_o, b_o):
    bsz, seq, d = h.shape
    nkv = kv.shape[-1]
    tq = ATTN_TQ
    nsb = tq // WINDOW
    kernel = functools.partial(_attn_kernel, tq=tq)
    return pl.pallas_call(
        kernel,
        out_shape=jax.ShapeDtypeStruct((bsz, seq, d), F32),
        grid=(bsz, seq // tq),
        in_specs=[
            pl.BlockSpec(memory_space=pltpu.SMEM),
            pl.BlockSpec((1, tq, d), lambda b, n: (b, n, 0)),
            pl.BlockSpec((1, WINDOW, nkv), lambda b, n: (b, jnp.maximum(n * nsb - 1, 0), 0)),
            pl.BlockSpec((1, tq, nkv), lambda b, n: (b, n, 0)),
            _const_spec((1, d)),
            _const_spec(w_q.shape),
            _const_spec((1, d)),
            _const_spec(w_o.shape),
            _const_spec((1, d)),
        ],
        out_specs=pl.BlockSpec((1, tq, d), lambda b, n: (b, n, 0)),
        scratch_shapes=[pltpu.VMEM((d, d), BF16),
                        pltpu.VMEM((1, d), F32),
                        pltpu.VMEM((d, d), BF16),
                        pltpu.VMEM((tq, d), BF16),
                        pltpu.VMEM((tq, d), BF16)],
        compiler_params=pltpu.CompilerParams(
            dimension_semantics=("arbitrary", "arbitrary"), vmem_limit_bytes=40 * MIB),
        name="attn",
    )(sinks, h, kv, kv, g.reshape(1, d), w_q, b_q.reshape(1, d), w_o, b_o.reshape(1, d))


def kernel(x, norm_mix, norm_mlp, norm_kv, norm_final, s5_a_re, s5_a_im, s5_log_dt, s5_b_re, s5_b_im, s5_c_re, s5_c_im, s5_d, s5_w_glu, s5_b_glu, w_kv, b_kv, w_q, b_q, sinks, w_o, b_o, w_mlp_in, w_mlp_out):
    bsz, seq, d = x.shape

    w1, cx, lam2_re_t, lam2_im_t = _s5_params(s5_a_re[0], s5_a_im[0], s5_log_dt[0], s5_b_re[0],
                                              s5_b_im[0], s5_c_re[0], s5_c_im[0])
    h = _s5_layer(x, norm_mix[0], w1, lam2_re_t, lam2_im_t, cx, s5_d[0], s5_w_glu[0], s5_b_glu[0])

    h, kv = _mlp(h.reshape(bsz * seq, d), norm_mlp[0], w_mlp_in, w_mlp_out, 0,
                 g_kv=norm_kv, w_kv=w_kv, b_kv=b_kv)

    h = _attn_layer(h.reshape(bsz, seq, d), kv.reshape(bsz, seq, -1), sinks[0], norm_mix[1],
                    w_q[0], b_q[0], w_o[0], b_o[0])

    out = _mlp(h.reshape(bsz * seq, d), norm_mlp[1], w_mlp_in, w_mlp_out, 1, g_fin=norm_final)
    return out.reshape(bsz, seq, d)
```

```python
import functools
import math

import jax
import jax.numpy as jnp
from jax import lax
from jax.experimental import pallas as pl
from jax.experimental.pallas import tpu as pltpu

F32 = jnp.float32
BF16 = jnp.bfloat16

D_MODEL = 1024
BATCH = 8
SEQ = 2048
S5_GROUP = 16
S5_GROUPS = D_MODEL // S5_GROUP
S5_STATE = 64
LAMBDA_RE_MAX = -1e-4
HEAD_DIM = 64
N_Q_HEADS = D_MODEL // HEAD_DIM
N_KV_HEADS = 4
Q_PER_KV = N_Q_HEADS // N_KV_HEADS
WINDOW = 128
D_FF = 4 * D_MODEL
NORM_EPS = 1e-5
LOG2_E = math.log2(math.e)

LANES = 128
SUBLANES = 8
N_LANE_TILES = D_MODEL // LANES
GROUPS_PER_TILE = LANES // S5_GROUP
STATES_PER_TILE = GROUPS_PER_TILE * S5_STATE
MIB = 1024 * 1024

S5_TL = 64
S5_TILES_IN_FLIGHT = 2
MLP_TM = 512
MLP_FF_CHUNK = 1024
ATTN_TQ = 512


def _rmsnorm(x, g):
    return x * lax.rsqrt(jnp.mean(x * x, axis=-1, keepdims=True) + NORM_EPS) * g


def _gelu_tanh(x):
    c = math.sqrt(2.0 / math.pi)
    return 0.5 * x * (1.0 + jnp.tanh(c * (x + 0.044715 * (x * x * x))))


def _const_spec(shape):
    nd = len(shape)
    return pl.BlockSpec(shape, lambda *_: (0,) * nd, pipeline_mode=pl.Buffered(1))


def _s5_params_kernel(abl_ref, bt_ref, cre_ref, cim_ref, w1_ref, cx_ref, l2r_ref, l2i_ref):
    gpt, grp, nst = GROUPS_PER_TILE, S5_GROUP, S5_STATE
    ar = jnp.minimum(abl_ref[0], LAMBDA_RE_MAX)
    ai = abl_ref[1]
    dt = jnp.exp(abl_ref[2])
    mag = jnp.exp(ar * dt)
    ang = ai * dt
    lr = mag * jnp.cos(ang)
    li = mag * jnp.sin(ang)
    den = ar * ar + ai * ai
    cr = ((lr - 1.0) * ar + li * ai) / den
    ci = (li * ar - (lr - 1.0) * ai) / den
    l2r = lr * lr - li * li
    l2i = 2.0 * (lr * li)

    def per_row(v):
        return jnp.concatenate([jnp.broadcast_to(v[g:g + 1], (grp, nst)) for g in range(gpt)], axis=0)

    row_g = lax.broadcasted_iota(jnp.int32, (LANES, STATES_PER_TILE), 0) // grp
    col_g = lax.broadcasted_iota(jnp.int32, (LANES, STATES_PER_TILE), 1) // nst
    own = row_g == col_g

    def blockdiag(v):
        return jnp.where(own, jnp.concatenate([v] * gpt, axis=1), 0.0)

    lr_c, li_c, cr_c, ci_c = per_row(lr), per_row(li), per_row(cr), per_row(ci)
    l2r_c, l2i_c = per_row(l2r), per_row(l2i)
    btr, bti = bt_ref[0], bt_ref[1]
    bbr = cr_c * btr - ci_c * bti
    bbi = cr_c * bti + ci_c * btr
    wbr, wbi = blockdiag(bbr), blockdiag(bbi)
    w1r = blockdiag(lr_c * bbr - li_c * bbi)
    w1i = blockdiag(lr_c * bbi + li_c * bbr)
    ccr, cci = cre_ref[...], cim_ref[...]
    wcr, wci = blockdiag(ccr), blockdiag(cci)

    def hdot_nt(a, b):
        return lax.dot_general(a, b, (((1,), (1,)), ((), ())),
                               precision=lax.Precision.HIGHEST, preferred_element_type=F32)

    k0 = hdot_nt(wbr, wcr) - hdot_nt(wbi, wci)
    k1 = hdot_nt(w1r, wcr) - hdot_nt(w1i, wci)
    top = jnp.concatenate([k0, k1, w1r, w1i], axis=1)
    bot = jnp.concatenate([jnp.zeros_like(k0), k0, wbr, wbi], axis=1)
    w1_ref[...] = jnp.concatenate([top, bot], axis=0).astype(BF16)

    def readout_t(pr, pi):
        return jnp.concatenate([blockdiag(pr * ccr - pi * cci),
                                -blockdiag(pi * ccr + pr * cci)], axis=1)

    cx_t = jnp.concatenate([readout_t(lr_c, li_c), readout_t(l2r_c, l2i_c)], axis=0)
    cx_ref[...] = cx_t.T.astype(BF16)

    def flat_row(v):
        row = jnp.concatenate([v[g:g + 1] for g in range(gpt)], axis=1)
        return jnp.broadcast_to(row, (SUBLANES, STATES_PER_TILE))

    l2r_ref[...] = flat_row(l2r)
    l2i_ref[...] = flat_row(l2i)


def _s5_params(a_re, a_im, log_dt, b_re, b_im, c_re, c_im):
    nt, gpt, sp = N_LANE_TILES, GROUPS_PER_TILE, STATES_PER_TILE
    abl = jnp.stack([a_re, a_im, jnp.broadcast_to(log_dt[:, None], a_re.shape)])
    abl = abl.reshape(3, nt, gpt, S5_STATE)
    bt = jnp.swapaxes(jnp.stack([b_re, b_im]), -1, -2).reshape(2, nt, LANES, S5_STATE)
    c_tile = lambda c: c.reshape(nt, LANES, S5_STATE)
    return pl.pallas_call(
        _s5_params_kernel,
        out_shape=(jax.ShapeDtypeStruct((nt, 2 * LANES, 2 * LANES + 2 * sp), BF16),
                   jax.ShapeDtypeStruct((nt, 2 * sp, 2 * LANES), BF16),
                   jax.ShapeDtypeStruct((nt, SUBLANES, sp), F32),
                   jax.ShapeDtypeStruct((nt, SUBLANES, sp), F32)),
        grid=(nt,),
        in_specs=[pl.BlockSpec((3, None, gpt, S5_STATE), lambda j: (0, j, 0, 0)),
                  pl.BlockSpec((2, None, LANES, S5_STATE), lambda j: (0, j, 0, 0)),
                  pl.BlockSpec((None, LANES, S5_STATE), lambda j: (j, 0, 0)),
                  pl.BlockSpec((None, LANES, S5_STATE), lambda j: (j, 0, 0))],
        out_specs=(pl.BlockSpec((None, 2 * LANES, 2 * LANES + 2 * sp), lambda j: (j, 0, 0)),
                   pl.BlockSpec((None, 2 * sp, 2 * LANES), lambda j: (j, 0, 0)),
                   pl.BlockSpec((None, SUBLANES, sp), lambda j: (j, 0, 0)),
                   pl.BlockSpec((None, SUBLANES, sp), lambda j: (j, 0, 0))),
        compiler_params=pltpu.CompilerParams(dimension_semantics=("parallel",)),
        name="s5_params",
    )(abl, bt, c_tile(c_re), c_tile(c_im))


def _s5_kernel(x_hbm, g_ref, w1_ref, l2r_ref, l2i_ref, cx_ref, d_ref, wglu_ref, bglu_ref,
               o_hbm, xin_s, hn_s, hnu_s, zy_s, bu_s, xs_s, y_s, st_s, res_s, sem_in, sem_out,
               *, tl, n_steps):
    half = tl // 2
    prow = half * SUBLANES
    sp = STATES_PER_TILE
    i = pl.program_id(0)
    slot = i % 2

    def in_copies(step, sl):
        return [pltpu.make_async_copy(x_hbm.at[b, pl.ds(step * tl, tl), :],
                                      xin_s.at[sl, :, b, :], sem_in.at[sl, b])
                for b in range(SUBLANES)]

    def out_copies(step, sl):
        return [pltpu.make_async_copy(res_s.at[sl, :, b, :],
                                      o_hbm.at[b, pl.ds(step * tl, tl), :], sem_out.at[sl, b])
                for b in range(SUBLANES)]

    @pl.when(i == 0)
    def _():
        st_s[...] = jnp.zeros_like(st_s)
        for copy in in_copies(0, 0):
            copy.start()

    @pl.when(i + 1 < n_steps)
    def _():
        for copy in in_copies(i + 1, 1 - slot):
            copy.start()

    for copy in in_copies(i, slot):
        copy.wait()

    @pl.when(i >= 2)
    def _():
        for copy in out_copies(i - 2, slot):
            copy.wait()

    def row_group(v, t):
        return v[t * SUBLANES:(t + 1) * SUBLANES]

    hn = _rmsnorm(xin_s[slot].reshape(tl * SUBLANES, D_MODEL), g_ref[...])
    for t in range(tl):
        g = (t % 2) * half + t // 2
        hn_s[g * SUBLANES:(g + 1) * SUBLANES, :] = row_group(hn, t)
    for m in range(tl // 4):
        r16 = slice(2 * m * SUBLANES, (2 * m + 2) * SUBLANES)
        even = jnp.concatenate([row_group(hn, 4 * m), row_group(hn, 4 * m + 2)], axis=0).astype(BF16)
        odd = jnp.concatenate([row_group(hn, 4 * m + 1), row_group(hn, 4 * m + 3)], axis=0).astype(BF16)
        for j in range(N_LANE_TILES):
            lanes = slice(j * LANES, (j + 1) * LANES)
            hnu_s[r16, 2 * j * LANES:(2 * j + 1) * LANES] = even[:, lanes]
            hnu_s[r16, (2 * j + 1) * LANES:(2 * j + 2) * LANES] = odd[:, lanes]

    def project_in(j):
        z = jnp.dot(hnu_s[:, 2 * j * LANES:(2 * j + 2) * LANES], w1_ref[j],
                    preferred_element_type=F32)
        zy_s[j % S5_TILES_IN_FLIGHT] = z[:, :2 * LANES]
        bu_s[j % S5_TILES_IN_FLIGHT] = z[:, 2 * LANES:]

    ahead = S5_TILES_IN_FLIGHT - 1
    for j in range(ahead):
        project_in(j)
    for j in range(N_LANE_TILES):
        if j + ahead < N_LANE_TILES:
            project_in(j + ahead)
        p = j % S5_TILES_IN_FLIGHT
        lanes = slice(j * LANES, (j + 1) * LANES)
        ar = l2r_ref[j]
        ai = l2i_ref[j]
        xr = st_s[j, :, 0:sp]
        xi = st_s[j, :, sp:2 * sp]
        for k in range(half):
            r = slice(k * SUBLANES, (k + 1) * SUBLANES)
            xs_s[p, r, 0:sp] = xr
            xs_s[p, r, sp:2 * sp] = xi
            nxr = ar * xr - ai * xi + bu_s[p, r, 0:sp]
            nxi = ar * xi + ai * xr + bu_s[p, r, sp:2 * sp]
            xr, xi = nxr, nxi
        st_s[j, :, 0:sp] = xr
        st_s[j, :, sp:2 * sp] = xi
        y = zy_s[p] + jnp.dot(xs_s[p].astype(BF16), cx_ref[j], preferred_element_type=F32)
        y_s[0:prow, lanes] = y[:, :LANES]
        y_s[prow:2 * prow, lanes] = y[:, LANES:]

    for par in range(2):
        rs = slice(par * prow, (par + 1) * prow)
        y = y_s[rs, :] + d_ref[...] * hn_s[rs, :]
        z = jnp.dot(_gelu_tanh(y).astype(BF16), wglu_ref[...].astype(BF16),
                    preferred_element_type=F32) + bglu_ref[...]
        mix = z[:, :D_MODEL] * (1.0 / (1.0 + jnp.exp(-z[:, D_MODEL:])))
        for k in range(half):
            t = 2 * k + par
            res_s[slot, t] = xin_s[slot, t] + row_group(mix, k)

    for copy in out_copies(i, slot):
        copy.start()

    @pl.when(i == n_steps - 1)
    def _():
        for copy in out_copies(i - 1, 1 - slot) + out_copies(i, slot):
            copy.wait()


def _s5_layer(x, g_mix, w1, lam2_re_t, lam2_im_t, cx, d_skip, w_glu, b_glu):
    bsz, seq, d = x.shape
    tl = S5_TL
    n_steps = seq // tl
    assert bsz == SUBLANES and n_steps >= 2
    rows = tl * bsz
    prow = rows // 2
    kernel = functools.partial(_s5_kernel, tl=tl, n_steps=n_steps)
    hbm = pl.BlockSpec(memory_space=pl.ANY)
    return pl.pallas_call(
        kernel,
        out_shape=jax.ShapeDtypeStruct((bsz, seq, d), F32),
        grid=(n_steps,),
        in_specs=[
            hbm,
            _const_spec((1, d)),
            _const_spec(w1.shape),
            _const_spec(lam2_re_t.shape),
            _const_spec(lam2_im_t.shape),
            _const_spec(cx.shape),
            _const_spec((1, d)),
            _const_spec(w_glu.shape),
            _const_spec((1, 2 * d)),
        ],
        out_specs=hbm,
        scratch_shapes=[
            pltpu.VMEM((2, tl, bsz, d), F32),
            pltpu.VMEM((rows, d), F32),
            pltpu.VMEM((prow, 2 * d), BF16),
            pltpu.VMEM((S5_TILES_IN_FLIGHT, prow, 2 * LANES), F32),
            pltpu.VMEM((S5_TILES_IN_FLIGHT, prow, 2 * STATES_PER_TILE), F32),
            pltpu.VMEM((S5_TILES_IN_FLIGHT, prow, 2 * STATES_PER_TILE), F32),
            pltpu.VMEM((rows, d), F32),
            pltpu.VMEM((N_LANE_TILES, SUBLANES, 2 * STATES_PER_TILE), F32),
            pltpu.VMEM((2, tl, bsz, d), F32),
            pltpu.SemaphoreType.DMA((2, SUBLANES)),
            pltpu.SemaphoreType.DMA((2, SUBLANES)),
        ],
        compiler_params=pltpu.CompilerParams(
            dimension_semantics=("arbitrary",), vmem_limit_bytes=56 * MIB),
        name="s5_layer",
    )(x, g_mix.reshape(1, d), w1, lam2_re_t, lam2_im_t, cx, d_skip.reshape(1, d), w_glu,
      b_glu.reshape(1, 2 * d))


def _inv_rms(x):
    return lax.rsqrt(jnp.mean(x * x, axis=-1, keepdims=True) + NORM_EPS)


def _mlp_kernel(h_ref, g_ref, win_hbm, wout_hbm, *rest, layer, with_kv):
    if with_kv:
        gkv_ref, wkv_ref, bkv_ref, o_ref, kv_ref, win_s, wout_s, sem = rest
    else:
        gfin_ref, o_ref, win_s, wout_s, sem = rest
    n_chunks = D_FF // MLP_FF_CHUNK

    def weight_copies(c):
        cols = pl.ds(c * MLP_FF_CHUNK, MLP_FF_CHUNK)
        return (pltpu.make_async_copy(win_hbm.at[layer, :, cols], win_s.at[:, cols], sem.at[0, c]),
                pltpu.make_async_copy(wout_hbm.at[layer, cols, :], wout_s.at[cols, :], sem.at[1, c]))

    def body(first_step):
        h = h_ref[...]
        hg = (h * g_ref[...]).astype(BF16)
        r = _inv_rms(h)
        acc = jnp.zeros(h.shape, F32)
        for c in range(n_chunks):
            cols = slice(c * MLP_FF_CHUNK, (c + 1) * MLP_FF_CHUNK)
            if first_step:
                for copy in weight_copies(c):
                    copy.wait()
            a = jnp.dot(hg, win_s[:, cols].astype(BF16), preferred_element_type=F32)
            a = jnp.square(jnp.maximum(a, 0.0)).astype(BF16)
            acc = acc + jnp.dot(a, wout_s[cols, :].astype(BF16), preferred_element_type=F32)
        out = h + (r * r) * acc
        if with_kv:
            o_ref[...] = out
            kv = jnp.dot((out * gkv_ref[...]).astype(BF16), wkv_ref[...].astype(BF16),
                         preferred_element_type=F32)
            kv_ref[...] = (_inv_rms(out) * kv + bkv_ref[...]).astype(BF16)
        else:
            o_ref[...] = out * _inv_rms(out) * gfin_ref[...]

    @pl.when(pl.program_id(0) == 0)
    def _():
        for c in range(n_chunks):
            for copy in weight_copies(c):
                copy.start()
        body(True)

    @pl.when(pl.program_id(0) > 0)
    def _():
        body(False)


def _mlp(h, g, w_in, w_out, layer, *, g_kv=None, w_kv=None, b_kv=None, g_fin=None):
    t, d = h.shape
    with_kv = w_kv is not None
    row = lambda n: pl.BlockSpec((MLP_TM, n), lambda i: (i, 0))
    hbm = pl.BlockSpec(memory_space=pl.ANY)
    in_specs = [row(d), _const_spec((1, d)), hbm, hbm]
    operands = [h, g.reshape(1, d), w_in, w_out]
    if with_kv:
        nkv = w_kv.shape[1]
        in_specs += [_const_spec((1, d)), _const_spec(w_kv.shape), _const_spec((1, nkv))]
        operands += [g_kv.reshape(1, d), w_kv, b_kv.reshape(1, nkv)]
        out_shape = (jax.ShapeDtypeStruct((t, d), F32), jax.ShapeDtypeStruct((t, nkv), BF16))
        out_specs = (row(d), row(nkv))
    else:
        in_specs += [_const_spec((1, d))]
        operands += [g_fin.reshape(1, d)]
        out_shape = jax.ShapeDtypeStruct((t, d), F32)
        out_specs = row(d)
    return pl.pallas_call(
        functools.partial(_mlp_kernel, layer=layer, with_kv=with_kv),
        out_shape=out_shape,
        grid=(t // MLP_TM,),
        in_specs=in_specs,
        out_specs=out_specs,
        scratch_shapes=[pltpu.VMEM(w_in.shape[1:], F32),
                        pltpu.VMEM(w_out.shape[1:], F32),
                        pltpu.SemaphoreType.DMA((2, D_FF // MLP_FF_CHUNK))],
        compiler_params=pltpu.CompilerParams(
            dimension_semantics=("arbitrary",), vmem_limit_bytes=56 * MIB),
        name="mlp_kv" if with_kv else "mlp_final",
    )(*operands)


def _pair_tile_sources(a, g):
    head_lo = (2 * a) * Q_PER_KV + g
    head_hi = (2 * a + 1) * Q_PER_KV + g
    return (head_lo // 2, head_lo % 2), (head_hi // 2, head_hi % 2)


def _attn_kernel(sink_ref, h_ref, kvp_ref, kvc_ref, g_ref, wq_ref, bq_ref, wo_ref, bo_ref,
                 o_ref, wq_s, bq_s, wo_s, q_s, oh_s, *, tq):
    nsb = tq // WINDOW
    n = pl.program_id(1)
    n_kv_tiles = N_KV_HEADS // 2
    lo_row = lax.broadcasted_iota(jnp.int32, (1, LANES), 1) < HEAD_DIM

    @pl.when((pl.program_id(0) == 0) & (n == 0))
    def _():
        for a in range(n_kv_tiles):
            for g in range(Q_PER_KV):
                (t_lo, h_lo), (t_hi, h_hi) = _pair_tile_sources(a, g)
                dst = slice((a * Q_PER_KV + g) * LANES, (a * Q_PER_KV + g + 1) * LANES)

                def pair(ref):
                    src_lo = ref[:, t_lo * LANES:(t_lo + 1) * LANES]
                    src_hi = ref[:, t_hi * LANES:(t_hi + 1) * LANES]
                    if h_lo == 1:
                        src_lo = pltpu.roll(src_lo, HEAD_DIM, axis=1)
                    if h_hi == 0:
                        src_hi = pltpu.roll(src_hi, HEAD_DIM, axis=1)
                    return jnp.where(lo_row, src_lo, src_hi)

                wq_s[:, dst] = pair(wq_ref).astype(BF16)
                bq_s[:, dst] = pair(bq_ref)
                for half in range(2):
                    head = (2 * a + half) * Q_PER_KV + g
                    r0 = (a * Q_PER_KV + g) * LANES + half * HEAD_DIM
                    wo_s[r0:r0 + HEAD_DIM, :] = (
                        wo_ref[head * HEAD_DIM:(head + 1) * HEAD_DIM, :].astype(BF16))

    h = h_ref[0]
    hn = _rmsnorm(h, g_ref[...]).astype(BF16)
    q = jnp.dot(hn, wq_s[...], preferred_element_type=F32) + bq_s[...]
    q_s[...] = (q * (LOG2_E / math.sqrt(HEAD_DIM))).astype(BF16)

    lo = lax.broadcasted_iota(jnp.int32, (WINDOW, LANES), 1) < HEAD_DIM
    own = (lax.broadcasted_iota(jnp.int32, (WINDOW, WINDOW), 1)
           <= lax.broadcasted_iota(jnp.int32, (WINDOW, WINDOW), 0))
    zero = jnp.zeros((WINDOW, LANES), BF16)
    fzero = jnp.zeros((WINDOW, WINDOW), F32)

    for sb in range(nsb):
        rows = slice(sb * WINDOW, (sb + 1) * WINDOW)
        prev_bias = jnp.where(n == 0, -jnp.inf, 0.0).astype(F32) if sb == 0 else None
        for a in range(n_kv_tiles):
            kl = slice(a * LANES, (a + 1) * LANES)
            vl = slice((n_kv_tiles + a) * LANES, (n_kv_tiles + a + 1) * LANES)
            if sb == 0:
                kprev, vprev = kvp_ref[0, :, kl], kvp_ref[0, :, vl]
            else:
                prow = slice((sb - 1) * WINDOW, sb * WINDOW)
                kprev, vprev = kvc_ref[0, prow, kl], kvc_ref[0, prow, vl]
            kd = jnp.concatenate([kprev, kvc_ref[0, rows, kl]], axis=0)
            vd = jnp.concatenate([vprev, kvc_ref[0, rows, vl]], axis=0)
            vd1 = jnp.concatenate([vd, jnp.ones((2 * WINDOW, LANES), BF16)], axis=1)
            qt = [q_s[rows, (a * Q_PER_KV + g) * LANES:(a * Q_PER_KV + g + 1) * LANES]
                  for g in range(Q_PER_KV)]
            lhs = jnp.concatenate([jnp.where(lo, t, zero) for t in qt]
                                  + [jnp.where(lo, zero, t) for t in qt], axis=0)
            s = lax.dot_general(lhs, kd, (((1,), (1,)), ((), ())),
                                preferred_element_type=F32)
            ps, sink_terms = [], []
            for half in range(2):
                for g in range(Q_PER_KV):
                    blk = half * Q_PER_KV + g
                    s_prev = s[blk * WINDOW:(blk + 1) * WINDOW, :WINDOW]
                    s_own = s[blk * WINDOW:(blk + 1) * WINDOW, WINDOW:]
                    if prev_bias is not None:
                        s_prev = s_prev + prev_bias
                    sg = jnp.where(own, s_own, s_prev)
                    sink = sink_ref[(2 * a + half) * Q_PER_KV + g] * LOG2_E
                    m = jnp.maximum(jnp.max(sg, axis=-1, keepdims=True), sink)
                    p = jnp.exp2(sg - m)
                    ps.append(jnp.concatenate([jnp.where(own, fzero, p).astype(BF16),
                                               jnp.where(own, p, fzero).astype(BF16)], axis=1))
                    sink_terms.append(jnp.exp2(sink - m))
            od = jnp.dot(jnp.concatenate(ps, axis=0), vd1, preferred_element_type=F32)
            for g in range(Q_PER_KV):
                r_lo = slice(g * WINDOW, (g + 1) * WINDOW)
                r_hi = slice((Q_PER_KV + g) * WINDOW, (Q_PER_KV + g + 1) * WINDOW)
                num = jnp.where(lo, od[r_lo, :LANES], od[r_hi, :LANES])
                den = (jnp.where(lo, od[r_lo, LANES:], od[r_hi, LANES:])
                       + jnp.where(lo, sink_terms[g], sink_terms[Q_PER_KV + g]))
                oh_s[rows, (a * Q_PER_KV + g) * LANES:(a * Q_PER_KV + g + 1) * LANES] = (
                    (num * (1.0 / den)).astype(BF16))

    o_ref[0] = h + jnp.dot(oh_s[...], wo_s[...], preferred_element_type=F32) + bo_ref[...]


def _attn_layer(h, kv, sinks, g, w_q, b_q, w_o, b_o):
    bsz, seq, d = h.shape
    nkv = kv.shape[-1]
    tq = ATTN_TQ
    nsb = tq // WINDOW
    kernel = functools.partial(_attn_kernel, tq=tq)
    return pl.pallas_call(
        kernel,
        out_shape=jax.ShapeDtypeStruct((bsz, seq, d), F32),
        grid=(bsz, seq // tq),
        in_specs=[
            pl.BlockSpec(memory_space=pltpu.SMEM),
            pl.BlockSpec((1, tq, d), lambda b, n: (b, n, 0)),
            pl.BlockSpec((1, WINDOW, nkv), lambda b, n: (b, jnp.maximum(n * nsb - 1, 0), 0)),
            pl.BlockSpec((1, tq, nkv), lambda b, n: (b, n, 0)),
            _const_spec((1, d)),
            _const_spec(w_q.shape),
            _const_spec((1, d)),
            _const_spec(w_o.shape),
            _const_spec((1, d)),
        ],
        out_specs=pl.BlockSpec((1, tq, d), lambda b, n: (b, n, 0)),
        scratch_shapes=[pltpu.VMEM((d, d), BF16),
                        pltpu.VMEM((1, d), F32),
                        pltpu.VMEM((d, d), BF16),
                        pltpu.VMEM((tq, d), BF16),
                        pltpu.VMEM((tq, d), BF16)],
        compiler_params=pltpu.CompilerParams(
            dimension_semantics=("arbitrary", "arbitrary"), vmem_limit_bytes=40 * MIB),
        name="attn",
    )(sinks, h, kv, kv, g.reshape(1, d), w_q, b_q.reshape(1, d), w_o, b_o.reshape(1, d))


def kernel(x, norm_mix, norm_mlp, norm_kv, norm_final, s5_a_re, s5_a_im, s5_log_dt, s5_b_re, s5_b_im, s5_c_re, s5_c_im, s5_d, s5_w_glu, s5_b_glu, w_kv, b_kv, w_q, b_q, sinks, w_o, b_o, w_mlp_in, w_mlp_out):
    bsz, seq, d = x.shape

    w1, cx, lam2_re_t, lam2_im_t = _s5_params(s5_a_re[0], s5_a_im[0], s5_log_dt[0], s5_b_re[0],
                                              s5_b_im[0], s5_c_re[0], s5_c_im[0])
    h = _s5_layer(x, norm_mix[0], w1, lam2_re_t, lam2_im_t, cx, s5_d[0], s5_w_glu[0], s5_b_glu[0])

    h, kv = _mlp(h.reshape(bsz * seq, d), norm_mlp[0], w_mlp_in, w_mlp_out, 0,
                 g_kv=norm_kv, w_kv=w_kv, b_kv=b_kv)

    h = _attn_layer(h.reshape(bsz, seq, d), kv.reshape(bsz, seq, -1), sinks[0], norm_mix[1],
                    w_q[0], b_q[0], w_o[0], b_o[0])

    out = _mlp(h.reshape(bsz * seq, d), norm_mlp[1], w_mlp_in, w_mlp_out, 1, g_fin=norm_final)
    return out.reshape(bsz, seq, d)
```

```python
import functools
import math

import jax
import jax.numpy as jnp
from jax import lax
from jax.experimental import pallas as pl
from jax.experimental.pallas import tpu as pltpu

F32 = jnp.float32
BF16 = jnp.bfloat16

D_MODEL = 1024
BATCH = 8
SEQ = 2048
S5_GROUP = 16
S5_GROUPS = D_MODEL // S5_GROUP
S5_STATE = 64
LAMBDA_RE_MAX = -1e-4
HEAD_DIM = 64
N_Q_HEADS = D_MODEL // HEAD_DIM
N_KV_HEADS = 4
Q_PER_KV = N_Q_HEADS // N_KV_HEADS
WINDOW = 128
D_FF = 4 * D_MODEL
NORM_EPS = 1e-5
LOG2_E = math.log2(math.e)

LANES = 128
SUBLANES = 8
N_LANE_TILES = D_MODEL // LANES
GROUPS_PER_TILE = LANES // S5_GROUP
STATES_PER_TILE = GROUPS_PER_TILE * S5_STATE
MIB = 1024 * 1024

S5_TL = 128
S5_PARAM_TILES_PER_STEP = 4
S5_TILES_IN_FLIGHT = 2
MLP_TM = 512
MLP_FF_CHUNK = 1024
ATTN_TQ = 512


def _rmsnorm(x, g):
    return x * lax.rsqrt(jnp.mean(x * x, axis=-1, keepdims=True) + NORM_EPS) * g


def _gelu_tanh(x):
    c = math.sqrt(2.0 / math.pi)
    return 0.5 * x * (1.0 + jnp.tanh(c * (x + 0.044715 * (x * x * x))))


def _const_spec(shape):
    nd = len(shape)
    return pl.BlockSpec(shape, lambda *_: (0,) * nd, pipeline_mode=pl.Buffered(1))


def _s5_params_kernel(abl_ref, bt_ref, cre_ref, cim_ref, w1_ref, cx_ref, l2r_ref, l2i_ref):
    for tile in range(S5_PARAM_TILES_PER_STEP):
        _s5_params_tile(abl_ref.at[:, tile], bt_ref.at[:, tile], cre_ref.at[tile], cim_ref.at[tile],
                        w1_ref.at[tile], cx_ref.at[tile], l2r_ref.at[tile], l2i_ref.at[tile])


def _s5_params_tile(abl_ref, bt_ref, cre_ref, cim_ref, w1_ref, cx_ref, l2r_ref, l2i_ref):
    gpt, grp, nst = GROUPS_PER_TILE, S5_GROUP, S5_STATE
    ar = jnp.minimum(abl_ref[0], LAMBDA_RE_MAX)
    ai = abl_ref[1]
    dt = jnp.exp(abl_ref[2])
    mag = jnp.exp(ar * dt)
    ang = ai * dt
    lr = mag * jnp.cos(ang)
    li = mag * jnp.sin(ang)
    den = ar * ar + ai * ai
    cr = ((lr - 1.0) * ar + li * ai) / den
    ci = (li * ar - (lr - 1.0) * ai) / den
    l2r = lr * lr - li * li
    l2i = 2.0 * (lr * li)

    def per_row(v):
        return jnp.concatenate([jnp.broadcast_to(v[g:g + 1], (grp, nst)) for g in range(gpt)], axis=0)

    row_g = lax.broadcasted_iota(jnp.int32, (LANES, STATES_PER_TILE), 0) // grp
    col_g = lax.broadcasted_iota(jnp.int32, (LANES, STATES_PER_TILE), 1) // nst
    own = row_g == col_g

    def blockdiag(v):
        return jnp.where(own, jnp.concatenate([v] * gpt, axis=1), 0.0)

    lr_c, li_c, cr_c, ci_c = per_row(lr), per_row(li), per_row(cr), per_row(ci)
    l2r_c, l2i_c = per_row(l2r), per_row(l2i)
    btr, bti = bt_ref[0], bt_ref[1]
    bbr = cr_c * btr - ci_c * bti
    bbi = cr_c * bti + ci_c * btr
    wbr, wbi = blockdiag(bbr), blockdiag(bbi)
    w1r = blockdiag(lr_c * bbr - li_c * bbi)
    w1i = blockdiag(lr_c * bbi + li_c * bbr)
    ccr, cci = cre_ref[...], cim_ref[...]
    wcr, wci = blockdiag(ccr), blockdiag(cci)

    def hdot_nt(a, b):
        return lax.dot_general(a, b, (((1,), (1,)), ((), ())),
                               precision=lax.Precision.HIGHEST, preferred_element_type=F32)

    k0 = hdot_nt(wbr, wcr) - hdot_nt(wbi, wci)
    k1 = hdot_nt(w1r, wcr) - hdot_nt(w1i, wci)
    top = jnp.concatenate([k0, k1, w1r, w1i], axis=1)
    bot = jnp.concatenate([jnp.zeros_like(k0), k0, wbr, wbi], axis=1)
    w1_ref[...] = jnp.concatenate([top, bot], axis=0).astype(BF16)

    def readout_t(pr, pi):
        return jnp.concatenate([blockdiag(pr * ccr - pi * cci),
                                -blockdiag(pi * ccr + pr * cci)], axis=1)

    cx_t = jnp.concatenate([readout_t(lr_c, li_c), readout_t(l2r_c, l2i_c)], axis=0)
    cx_ref[...] = cx_t.T.astype(BF16)

    def flat_row(v):
        row = jnp.concatenate([v[g:g + 1] for g in range(gpt)], axis=1)
        return jnp.broadcast_to(row, (SUBLANES, STATES_PER_TILE))

    l2r_ref[...] = flat_row(l2r)
    l2i_ref[...] = flat_row(l2i)


def _s5_params(a_re, a_im, log_dt, b_re, b_im, c_re, c_im):
    nt, gpt, sp = N_LANE_TILES, GROUPS_PER_TILE, STATES_PER_TILE
    tps = S5_PARAM_TILES_PER_STEP
    abl = jnp.stack([a_re, a_im, jnp.broadcast_to(log_dt[:, None], a_re.shape)])
    abl = abl.reshape(3, nt, gpt, S5_STATE)
    bt = jnp.swapaxes(jnp.stack([b_re, b_im]), -1, -2).reshape(2, nt, LANES, S5_STATE)
    c_tile = lambda c: c.reshape(nt, LANES, S5_STATE)
    return pl.pallas_call(
        _s5_params_kernel,
        out_shape=(jax.ShapeDtypeStruct((nt, 2 * LANES, 2 * LANES + 2 * sp), BF16),
                   jax.ShapeDtypeStruct((nt, 2 * sp, 2 * LANES), BF16),
                   jax.ShapeDtypeStruct((nt, SUBLANES, sp), F32),
                   jax.ShapeDtypeStruct((nt, SUBLANES, sp), F32)),
        grid=(nt // tps,),
        in_specs=[pl.BlockSpec((3, tps, gpt, S5_STATE), lambda j: (0, j, 0, 0)),
                  pl.BlockSpec((2, tps, LANES, S5_STATE), lambda j: (0, j, 0, 0)),
                  pl.BlockSpec((tps, LANES, S5_STATE), lambda j: (j, 0, 0)),
                  pl.BlockSpec((tps, LANES, S5_STATE), lambda j: (j, 0, 0))],
        out_specs=(pl.BlockSpec((tps, 2 * LANES, 2 * LANES + 2 * sp), lambda j: (j, 0, 0)),
                   pl.BlockSpec((tps, 2 * sp, 2 * LANES), lambda j: (j, 0, 0)),
                   pl.BlockSpec((tps, SUBLANES, sp), lambda j: (j, 0, 0)),
                   pl.BlockSpec((tps, SUBLANES, sp), lambda j: (j, 0, 0))),
        compiler_params=pltpu.CompilerParams(dimension_semantics=("parallel",)),
        name="s5_params",
    )(abl, bt, c_tile(c_re), c_tile(c_im))


def _s5_kernel(x_hbm, g_ref, w1_ref, l2r_ref, l2i_ref, cx_ref, d_ref, wglu_ref, bglu_ref,
               o_hbm, xin_s, hn_s, hnu_s, zy_s, bu_s, xs_s, y_s, st_s, res_s, sem_in, sem_out,
               *, tl, n_steps):
    half = tl // 2
    prow = half * SUBLANES
    sp = STATES_PER_TILE
    i = pl.program_id(0)
    slot = i % 2

    def in_copies(step, sl):
        return [pltpu.make_async_copy(x_hbm.at[b, pl.ds(step * tl, tl), :],
                                      xin_s.at[sl, :, b, :], sem_in.at[sl, b])
                for b in range(SUBLANES)]

    def out_copies(step, sl):
        return [pltpu.make_async_copy(res_s.at[sl, :, b, :],
                                      o_hbm.at[b, pl.ds(step * tl, tl), :], sem_out.at[sl, b])
                for b in range(SUBLANES)]

    @pl.when(i == 0)
    def _():
        st_s[...] = jnp.zeros_like(st_s)
        for copy in in_copies(0, 0):
            copy.start()

    @pl.when(i + 1 < n_steps)
    def _():
        for copy in in_copies(i + 1, 1 - slot):
            copy.start()

    for copy in in_copies(i, slot):
        copy.wait()

    @pl.when(i >= 2)
    def _():
        for copy in out_copies(i - 2, slot):
            copy.wait()

    def row_group(v, t):
        return v[t * SUBLANES:(t + 1) * SUBLANES]

    hn = _rmsnorm(xin_s[slot].reshape(tl * SUBLANES, D_MODEL), g_ref[...])
    for t in range(tl):
        g = (t % 2) * half + t // 2
        hn_s[g * SUBLANES:(g + 1) * SUBLANES, :] = row_group(hn, t)
    for m in range(tl // 4):
        r16 = slice(2 * m * SUBLANES, (2 * m + 2) * SUBLANES)
        even = jnp.concatenate([row_group(hn, 4 * m), row_group(hn, 4 * m + 2)], axis=0).astype(BF16)
        odd = jnp.concatenate([row_group(hn, 4 * m + 1), row_group(hn, 4 * m + 3)], axis=0).astype(BF16)
        for j in range(N_LANE_TILES):
            lanes = slice(j * LANES, (j + 1) * LANES)
            hnu_s[r16, 2 * j * LANES:(2 * j + 1) * LANES] = even[:, lanes]
            hnu_s[r16, (2 * j + 1) * LANES:(2 * j + 2) * LANES] = odd[:, lanes]

    def project_in(j):
        z = jnp.dot(hnu_s[:, 2 * j * LANES:(2 * j + 2) * LANES], w1_ref[j],
                    preferred_element_type=F32)
        zy_s[j % S5_TILES_IN_FLIGHT] = z[:, :2 * LANES]
        bu_s[j % S5_TILES_IN_FLIGHT] = z[:, 2 * LANES:]

    ahead = S5_TILES_IN_FLIGHT - 1
    for j in range(ahead):
        project_in(j)
    for j in range(N_LANE_TILES):
        if j + ahead < N_LANE_TILES:
            project_in(j + ahead)
        p = j % S5_TILES_IN_FLIGHT
        lanes = slice(j * LANES, (j + 1) * LANES)
        ar = l2r_ref[j]
        ai = l2i_ref[j]
        xr = st_s[j, :, 0:sp]
        xi = st_s[j, :, sp:2 * sp]
        for k in range(half):
            r = slice(k * SUBLANES, (k + 1) * SUBLANES)
            xs_s[p, r, 0:sp] = xr
            xs_s[p, r, sp:2 * sp] = xi
            nxr = ar * xr - ai * xi + bu_s[p, r, 0:sp]
            nxi = ar * xi + ai * xr + bu_s[p, r, sp:2 * sp]
            xr, xi = nxr, nxi
        st_s[j, :, 0:sp] = xr
        st_s[j, :, sp:2 * sp] = xi
        y = zy_s[p] + jnp.dot(xs_s[p].astype(BF16), cx_ref[j], preferred_element_type=F32)
        y_s[0:prow, lanes] = y[:, :LANES]
        y_s[prow:2 * prow, lanes] = y[:, LANES:]

    for par in range(2):
        rs = slice(par * prow, (par + 1) * prow)
        y = y_s[rs, :] + d_ref[...] * hn_s[rs, :]
        z = jnp.dot(_gelu_tanh(y).astype(BF16), wglu_ref[...].astype(BF16),
                    preferred_element_type=F32) + bglu_ref[...]
        mix = z[:, :D_MODEL] * (1.0 / (1.0 + jnp.exp(-z[:, D_MODEL:])))
        for k in range(half):
            t = 2 * k + par
            res_s[slot, t] = xin_s[slot, t] + row_group(mix, k)

    for copy in out_copies(i, slot):
        copy.start()

    @pl.when(i == n_steps - 1)
    def _():
        for copy in out_copies(i - 1, 1 - slot) + out_copies(i, slot):
            copy.wait()


def _s5_layer(x, g_mix, w1, lam2_re_t, lam2_im_t, cx, d_skip, w_glu, b_glu):
    bsz, seq, d = x.shape
    tl = S5_TL
    n_steps = seq // tl
    assert bsz == SUBLANES and n_steps >= 2
    rows = tl * bsz
    prow = rows // 2
    kernel = functools.partial(_s5_kernel, tl=tl, n_steps=n_steps)
    hbm = pl.BlockSpec(memory_space=pl.ANY)
    return pl.pallas_call(
        kernel,
        out_shape=jax.ShapeDtypeStruct((bsz, seq, d), F32),
        grid=(n_steps,),
        in_specs=[
            hbm,
            _const_spec((1, d)),
            _const_spec(w1.shape),
            _const_spec(lam2_re_t.shape),
            _const_spec(lam2_im_t.shape),
            _const_spec(cx.shape),
            _const_spec((1, d)),
            _const_spec(w_glu.shape),
            _const_spec((1, 2 * d)),
        ],
        out_specs=hbm,
        scratch_shapes=[
            pltpu.VMEM((2, tl, bsz, d), F32),
            pltpu.VMEM((rows, d), F32),
            pltpu.VMEM((prow, 2 * d), BF16),
            pltpu.VMEM((S5_TILES_IN_FLIGHT, prow, 2 * LANES), F32),
            pltpu.VMEM((S5_TILES_IN_FLIGHT, prow, 2 * STATES_PER_TILE), F32),
            pltpu.VMEM((S5_TILES_IN_FLIGHT, prow, 2 * STATES_PER_TILE), F32),
            pltpu.VMEM((rows, d), F32),
            pltpu.VMEM((N_LANE_TILES, SUBLANES, 2 * STATES_PER_TILE), F32),
            pltpu.VMEM((2, tl, bsz, d), F32),
            pltpu.SemaphoreType.DMA((2, SUBLANES)),
            pltpu.SemaphoreType.DMA((2, SUBLANES)),
        ],
        compiler_params=pltpu.CompilerParams(
            dimension_semantics=("arbitrary",), vmem_limit_bytes=56 * MIB),
        name="s5_layer",
    )(x, g_mix.reshape(1, d), w1, lam2_re_t, lam2_im_t, cx, d_skip.reshape(1, d), w_glu,
      b_glu.reshape(1, 2 * d))


def _inv_rms(x):
    return lax.rsqrt(jnp.mean(x * x, axis=-1, keepdims=True) + NORM_EPS)


def _mlp_kernel(h_ref, g_ref, win_hbm, wout_hbm, *rest, layer, with_kv):
    if with_kv:
        gkv_ref, wkv_ref, bkv_ref, o_ref, kv_ref, win_s, wout_s, sem = rest
    else:
        gfin_ref, o_ref, win_s, wout_s, sem = rest
    n_chunks = D_FF // MLP_FF_CHUNK

    def weight_copies(c):
        cols = pl.ds(c * MLP_FF_CHUNK, MLP_FF_CHUNK)
        return (pltpu.make_async_copy(win_hbm.at[layer, :, cols], win_s.at[:, cols], sem.at[0, c]),
                pltpu.make_async_copy(wout_hbm.at[layer, cols, :], wout_s.at[cols, :], sem.at[1, c]))

    def body(first_step):
        h = h_ref[...]
        hg = (h * g_ref[...]).astype(BF16)
        r = _inv_rms(h)
        acc = jnp.zeros(h.shape, F32)
        for c in range(n_chunks):
            cols = slice(c * MLP_FF_CHUNK, (c + 1) * MLP_FF_CHUNK)
            if first_step:
                for copy in weight_copies(c):
                    copy.wait()
            a = jnp.dot(hg, win_s[:, cols].astype(BF16), preferred_element_type=F32)
            a = jnp.square(jnp.maximum(a, 0.0)).astype(BF16)
            acc = acc + jnp.dot(a, wout_s[cols, :].astype(BF16), preferred_element_type=F32)
        out = h + (r * r) * acc
        if with_kv:
            o_ref[...] = out
            kv = jnp.dot((out * gkv_ref[...]).astype(BF16), wkv_ref[...].astype(BF16),
                         preferred_element_type=F32)
            kv_ref[...] = (_inv_rms(out) * kv + bkv_ref[...]).astype(BF16)
        else:
            o_ref[...] = out * _inv_rms(out) * gfin_ref[...]

    @pl.when(pl.program_id(0) == 0)
    def _():
        for c in range(n_chunks):
            for copy in weight_copies(c):
                copy.start()
        body(True)

    @pl.when(pl.program_id(0) > 0)
    def _():
        body(False)


def _mlp(h, g, w_in, w_out, layer, *, g_kv=None, w_kv=None, b_kv=None, g_fin=None):
    t, d = h.shape
    with_kv = w_kv is not None
    row = lambda n: pl.BlockSpec((MLP_TM, n), lambda i: (i, 0))
    hbm = pl.BlockSpec(memory_space=pl.ANY)
    in_specs = [row(d), _const_spec((1, d)), hbm, hbm]
    operands = [h, g.reshape(1, d), w_in, w_out]
    if with_kv:
        nkv = w_kv.shape[1]
        in_specs += [_const_spec((1, d)), _const_spec(w_kv.shape), _const_spec((1, nkv))]
        operands += [g_kv.reshape(1, d), w_kv, b_kv.reshape(1, nkv)]
        out_shape = (jax.ShapeDtypeStruct((t, d), F32), jax.ShapeDtypeStruct((t, nkv), BF16))
        out_specs = (row(d), row(nkv))
    else:
        in_specs += [_const_spec((1, d))]
        operands += [g_fin.reshape(1, d)]
        out_shape = jax.ShapeDtypeStruct((t, d), F32)
        out_specs = row(d)
    return pl.pallas_call(
        functools.partial(_mlp_kernel, layer=layer, with_kv=with_kv),
        out_shape=out_shape,
        grid=(t // MLP_TM,),
        in_specs=in_specs,
        out_specs=out_specs,
        scratch_shapes=[pltpu.VMEM(w_in.shape[1:], F32),
                        pltpu.VMEM(w_out.shape[1:], F32),
                        pltpu.SemaphoreType.DMA((2, D_FF // MLP_FF_CHUNK))],
        compiler_params=pltpu.CompilerParams(
            dimension_semantics=("arbitrary",), vmem_limit_bytes=56 * MIB),
        name="mlp_kv" if with_kv else "mlp_final",
    )(*operands)


def _pair_tile_sources(a, g):
    head_lo = (2 * a) * Q_PER_KV + g
    head_hi = (2 * a + 1) * Q_PER_KV + g
    return (head_lo // 2, head_lo % 2), (head_hi // 2, head_hi % 2)


def _attn_kernel(sink_ref, h_ref, kvp_ref, kvc_ref, g_ref, wq_ref, bq_ref, wo_ref, bo_ref,
                 o_ref, wq_s, bq_s, wo_s, q_s, oh_s, *, tq):
    nsb = tq // WINDOW
    n = pl.program_id(1)
    n_kv_tiles = N_KV_HEADS // 2
    lo_row = lax.broadcasted_iota(jnp.int32, (1, LANES), 1) < HEAD_DIM

    @pl.when((pl.program_id(0) == 0) & (n == 0))
    def _():
        for a in range(n_kv_tiles):
            for g in range(Q_PER_KV):
                (t_lo, h_lo), (t_hi, h_hi) = _pair_tile_sources(a, g)
                dst = slice((a * Q_PER_KV + g) * LANES, (a * Q_PER_KV + g + 1) * LANES)

                def pair(ref):
                    src_lo = ref[:, t_lo * LANES:(t_lo + 1) * LANES]
                    src_hi = ref[:, t_hi * LANES:(t_hi + 1) * LANES]
                    if h_lo == 1:
                        src_lo = pltpu.roll(src_lo, HEAD_DIM, axis=1)
                    if h_hi == 0:
                        src_hi = pltpu.roll(src_hi, HEAD_DIM, axis=1)
                    return jnp.where(lo_row, src_lo, src_hi)

                wq_s[:, dst] = pair(wq_ref).astype(BF16)
                bq_s[:, dst] = pair(bq_ref)
                for half in range(2):
                    head = (2 * a + half) * Q_PER_KV + g
                    r0 = (a * Q_PER_KV + g) * LANES + half * HEAD_DIM
                    wo_s[r0:r0 + HEAD_DIM, :] = (
                        wo_ref[head * HEAD_DIM:(head + 1) * HEAD_DIM, :].astype(BF16))

    h = h_ref[0]
    hn = _rmsnorm(h, g_ref[...]).astype(BF16)
    q = jnp.dot(hn, wq_s[...], preferred_element_type=F32) + bq_s[...]
    q_s[...] = (q * (LOG2_E / math.sqrt(HEAD_DIM))).astype(BF16)

    lo = lax.broadcasted_iota(jnp.int32, (WINDOW, LANES), 1) < HEAD_DIM
    own = (lax.broadcasted_iota(jnp.int32, (WINDOW, WINDOW), 1)
           <= lax.broadcasted_iota(jnp.int32, (WINDOW, WINDOW), 0))
    zero = jnp.zeros((WINDOW, LANES), BF16)
    fzero = jnp.zeros((WINDOW, WINDOW), F32)

    for sb in range(nsb):
        rows = slice(sb * WINDOW, (sb + 1) * WINDOW)
        prev_bias = jnp.where(n == 0, -jnp.inf, 0.0).astype(F32) if sb == 0 else None
        for a in range(n_kv_tiles):
            kl = slice(a * LANES, (a + 1) * LANES)
            vl = slice((n_kv_tiles + a) * LANES, (n_kv_tiles + a + 1) * LANES)
            if sb == 0:
                kprev, vprev = kvp_ref[0, :, kl], kvp_ref[0, :, vl]
            else:
                prow = slice((sb - 1) * WINDOW, sb * WINDOW)
                kprev, vprev = kvc_ref[0, prow, kl], kvc_ref[0, prow, vl]
            kd = jnp.concatenate([kprev, kvc_ref[0, rows, kl]], axis=0)
            vd = jnp.concatenate([vprev, kvc_ref[0, rows, vl]], axis=0)
            vd1 = jnp.concatenate([vd, jnp.ones((2 * WINDOW, LANES), BF16)], axis=1)
            qt = [q_s[rows, (a * Q_PER_KV + g) * LANES:(a * Q_PER_KV + g + 1) * LANES]
                  for g in range(Q_PER_KV)]
            lhs = jnp.concatenate([jnp.where(lo, t, zero) for t in qt]
                                  + [jnp.where(lo, zero, t) for t in qt], axis=0)
            s = lax.dot_general(lhs, kd, (((1,), (1,)), ((), ())),
                                preferred_element_type=F32)
            ps, sink_terms = [], []
            for half in range(2):
                for g in range(Q_PER_KV):
                    blk = half * Q_PER_KV + g
                    s_prev = s[blk * WINDOW:(blk + 1) * WINDOW, :WINDOW]
                    s_own = s[blk * WINDOW:(blk + 1) * WINDOW, WINDOW:]
                    if prev_bias is not None:
                        s_prev = s_prev + prev_bias
                    sg = jnp.where(own, s_own, s_prev)
                    sink = sink_ref[(2 * a + half) * Q_PER_KV + g] * LOG2_E
                    m = jnp.maximum(jnp.max(sg, axis=-1, keepdims=True), sink)
                    p = jnp.exp2(sg - m)
                    ps.append(jnp.concatenate([jnp.where(own, fzero, p).astype(BF16),
                                               jnp.where(own, p, fzero).astype(BF16)], axis=1))
                    sink_terms.append(jnp.exp2(sink - m))
            od = jnp.dot(jnp.concatenate(ps, axis=0), vd1, preferred_element_type=F32)
            for g in range(Q_PER_KV):
                r_lo = slice(g * WINDOW, (g + 1) * WINDOW)
                r_hi = slice((Q_PER_KV + g) * WINDOW, (Q_PER_KV + g + 1) * WINDOW)
                num = jnp.where(lo, od[r_lo, :LANES], od[r_hi, :LANES])
                den = (jnp.where(lo, od[r_lo, LANES:], od[r_hi, LANES:])
                       + jnp.where(lo, sink_terms[g], sink_terms[Q_PER_KV + g]))
                oh_s[rows, (a * Q_PER_KV + g) * LANES:(a * Q_PER_KV + g + 1) * LANES] = (
                    (num * (1.0 / den)).astype(BF16))

    o_ref[0] = h + jnp.dot(oh_s[...], wo_s[...], preferred_element_type=F32) + bo_ref[...]


def _attn_layer(h, kv, sinks, g, w_q, b_q, w_o, b_o):
    bsz, seq, d = h.shape
    nkv = kv.shape[-1]
    tq = ATTN_TQ
    nsb = tq // WINDOW
    kernel = functools.partial(_attn_kernel, tq=tq)
    return pl.pallas_call(
        kernel,
        out_shape=jax.ShapeDtypeStruct((bsz, seq, d), F32),
        grid=(bsz, seq // tq),
        in_specs=[
            pl.BlockSpec(memory_space=pltpu.SMEM),
            pl.BlockSpec((1, tq, d), lambda b, n: (b, n, 0)),
            pl.BlockSpec((1, WINDOW, nkv), lambda b, n: (b, jnp.maximum(n * nsb - 1, 0), 0)),
            pl.BlockSpec((1, tq, nkv), lambda b, n: (b, n, 0)),
            _const_spec((1, d)),
            _const_spec(w_q.shape),
            _const_spec((1, d)),
            _const_spec(w_o.shape),
            _const_spec((1, d)),
        ],
        out_specs=pl.BlockSpec((1, tq, d), lambda b, n: (b, n, 0)),
        scratch_shapes=[pltpu.VMEM((d, d), BF16),
                        pltpu.VMEM((1, d), F32),
                        pltpu.VMEM((d, d), BF16),
                        pltpu.VMEM((tq, d), BF16),
                        pltpu.VMEM((tq, d), BF16)],
        compiler_params=pltpu.CompilerParams(
            dimension_semantics=("arbitrary", "arbitrary"), vmem_limit_bytes=40 * MIB),
        name="attn",
    )(sinks, h, kv, kv, g.reshape(1, d), w_q, b_q.reshape(1, d), w_o, b_o.reshape(1, d))


def kernel(x, norm_mix, norm_mlp, norm_kv, norm_final, s5_a_re, s5_a_im, s5_log_dt, s5_b_re, s5_b_im, s5_c_re, s5_c_im, s5_d, s5_w_glu, s5_b_glu, w_kv, b_kv, w_q, b_q, sinks, w_o, b_o, w_mlp_in, w_mlp_out):
    bsz, seq, d = x.shape

    w1, cx, lam2_re_t, lam2_im_t = _s5_params(s5_a_re[0], s5_a_im[0], s5_log_dt[0], s5_b_re[0],
                                              s5_b_im[0], s5_c_re[0], s5_c_im[0])
    h = _s5_layer(x, norm_mix[0], w1, lam2_re_t, lam2_im_t, cx, s5_d[0], s5_w_glu[0], s5_b_glu[0])

    h, kv = _mlp(h.reshape(bsz * seq, d), norm_mlp[0], w_mlp_in, w_mlp_out, 0,
                 g_kv=norm_kv, w_kv=w_kv, b_kv=b_kv)

    h = _attn_layer(h.reshape(bsz, seq, d), kv.reshape(bsz, seq, -1), sinks[0], norm_mix[1],
                    w_q[0], b_q[0], w_o[0], b_o[0])

    out = _mlp(h.reshape(bsz * seq, d), norm_mlp[1], w_mlp_in, w_mlp_out, 1, g_fin=norm_final)
    return out.reshape(bsz, seq, d)
```

```python
import functools
import math

import jax
import jax.numpy as jnp
from jax import lax
from jax.experimental import pallas as pl
from jax.experimental.pallas import tpu as pltpu

F32 = jnp.float32
BF16 = jnp.bfloat16

D_MODEL = 1024
BATCH = 8
SEQ = 2048
S5_GROUP = 16
S5_GROUPS = D_MODEL // S5_GROUP
S5_STATE = 64
LAMBDA_RE_MAX = -1e-4
HEAD_DIM = 64
N_Q_HEADS = D_MODEL // HEAD_DIM
N_KV_HEADS = 4
Q_PER_KV = N_Q_HEADS // N_KV_HEADS
WINDOW = 128
D_FF = 4 * D_MODEL
NORM_EPS = 1e-5
LOG2_E = math.log2(math.e)

LANES = 128
SUBLANES = 8
N_LANE_TILES = D_MODEL // LANES
GROUPS_PER_TILE = LANES // S5_GROUP
STATES_PER_TILE = GROUPS_PER_TILE * S5_STATE
MIB = 1024 * 1024

S5_TL = 64
S5_PARAM_TILES_PER_STEP = 4
S5_TILES_IN_FLIGHT = 2
MLP_TM = 512
MLP_FF_CHUNK = 1024
ATTN_TQ = 1024


def _rmsnorm(x, g):
    return x * lax.rsqrt(jnp.mean(x * x, axis=-1, keepdims=True) + NORM_EPS) * g


def _gelu_tanh(x):
    c = math.sqrt(2.0 / math.pi)
    return 0.5 * x * (1.0 + jnp.tanh(c * (x + 0.044715 * (x * x * x))))


def _const_spec(shape):
    nd = len(shape)
    return pl.BlockSpec(shape, lambda *_: (0,) * nd, pipeline_mode=pl.Buffered(1))


def _s5_params_kernel(abl_ref, bt_ref, cre_ref, cim_ref, w1_ref, cx_ref, l2r_ref, l2i_ref):
    for tile in range(S5_PARAM_TILES_PER_STEP):
        _s5_params_tile(abl_ref.at[:, tile], bt_ref.at[:, tile], cre_ref.at[tile], cim_ref.at[tile],
                        w1_ref.at[tile], cx_ref.at[tile], l2r_ref.at[tile], l2i_ref.at[tile])


def _s5_params_tile(abl_ref, bt_ref, cre_ref, cim_ref, w1_ref, cx_ref, l2r_ref, l2i_ref):
    gpt, grp, nst = GROUPS_PER_TILE, S5_GROUP, S5_STATE
    ar = jnp.minimum(abl_ref[0], LAMBDA_RE_MAX)
    ai = abl_ref[1]
    dt = jnp.exp(abl_ref[2])
    mag = jnp.exp(ar * dt)
    ang = ai * dt
    lr = mag * jnp.cos(ang)
    li = mag * jnp.sin(ang)
    den = ar * ar + ai * ai
    cr = ((lr - 1.0) * ar + li * ai) / den
    ci = (li * ar - (lr - 1.0) * ai) / den
    l2r = lr * lr - li * li
    l2i = 2.0 * (lr * li)

    def per_row(v):
        return jnp.concatenate([jnp.broadcast_to(v[g:g + 1], (grp, nst)) for g in range(gpt)], axis=0)

    row_g = lax.broadcasted_iota(jnp.int32, (LANES, STATES_PER_TILE), 0) // grp
    col_g = lax.broadcasted_iota(jnp.int32, (LANES, STATES_PER_TILE), 1) // nst
    own = row_g == col_g

    def blockdiag(v):
        return jnp.where(own, jnp.concatenate([v] * gpt, axis=1), 0.0)

    lr_c, li_c, cr_c, ci_c = per_row(lr), per_row(li), per_row(cr), per_row(ci)
    l2r_c, l2i_c = per_row(l2r), per_row(l2i)
    btr, bti = bt_ref[0], bt_ref[1]
    bbr = cr_c * btr - ci_c * bti
    bbi = cr_c * bti + ci_c * btr
    wbr, wbi = blockdiag(bbr), blockdiag(bbi)
    w1r = blockdiag(lr_c * bbr - li_c * bbi)
    w1i = blockdiag(lr_c * bbi + li_c * bbr)
    ccr, cci = cre_ref[...], cim_ref[...]
    wcr, wci = blockdiag(ccr), blockdiag(cci)

    def hdot_nt(a, b):
        return lax.dot_general(a, b, (((1,), (1,)), ((), ())),
                               precision=lax.Precision.HIGHEST, preferred_element_type=F32)

    k0 = hdot_nt(wbr, wcr) - hdot_nt(wbi, wci)
    k1 = hdot_nt(w1r, wcr) - hdot_nt(w1i, wci)
    top = jnp.concatenate([k0, k1, w1r, w1i], axis=1)
    bot = jnp.concatenate([jnp.zeros_like(k0), k0, wbr, wbi], axis=1)
    w1_ref[...] = jnp.concatenate([top, bot], axis=0).astype(BF16)

    def readout_t(pr, pi):
        return jnp.concatenate([blockdiag(pr * ccr - pi * cci),
                                -blockdiag(pi * ccr + pr * cci)], axis=1)

    cx_t = jnp.concatenate([readout_t(lr_c, li_c), readout_t(l2r_c, l2i_c)], axis=0)
    cx_ref[...] = cx_t.T.astype(BF16)

    def flat_row(v):
        row = jnp.concatenate([v[g:g + 1] for g in range(gpt)], axis=1)
        return jnp.broadcast_to(row, (SUBLANES, STATES_PER_TILE))

    l2r_ref[...] = flat_row(l2r)
    l2i_ref[...] = flat_row(l2i)


def _s5_params(a_re, a_im, log_dt, b_re, b_im, c_re, c_im):
    nt, gpt, sp = N_LANE_TILES, GROUPS_PER_TILE, STATES_PER_TILE
    tps = S5_PARAM_TILES_PER_STEP
    abl = jnp.stack([a_re, a_im, jnp.broadcast_to(log_dt[:, None], a_re.shape)])
    abl = abl.reshape(3, nt, gpt, S5_STATE)
    bt = jnp.swapaxes(jnp.stack([b_re, b_im]), -1, -2).reshape(2, nt, LANES, S5_STATE)
    c_tile = lambda c: c.reshape(nt, LANES, S5_STATE)
    return pl.pallas_call(
        _s5_params_kernel,
        out_shape=(jax.ShapeDtypeStruct((nt, 2 * LANES, 2 * LANES + 2 * sp), BF16),
                   jax.ShapeDtypeStruct((nt, 2 * sp, 2 * LANES), BF16),
                   jax.ShapeDtypeStruct((nt, SUBLANES, sp), F32),
                   jax.ShapeDtypeStruct((nt, SUBLANES, sp), F32)),
        grid=(nt // tps,),
        in_specs=[pl.BlockSpec((3, tps, gpt, S5_STATE), lambda j: (0, j, 0, 0)),
                  pl.BlockSpec((2, tps, LANES, S5_STATE), lambda j: (0, j, 0, 0)),
                  pl.BlockSpec((tps, LANES, S5_STATE), lambda j: (j, 0, 0)),
                  pl.BlockSpec((tps, LANES, S5_STATE), lambda j: (j, 0, 0))],
        out_specs=(pl.BlockSpec((tps, 2 * LANES, 2 * LANES + 2 * sp), lambda j: (j, 0, 0)),
                   pl.BlockSpec((tps, 2 * sp, 2 * LANES), lambda j: (j, 0, 0)),
                   pl.BlockSpec((tps, SUBLANES, sp), lambda j: (j, 0, 0)),
                   pl.BlockSpec((tps, SUBLANES, sp), lambda j: (j, 0, 0))),
        compiler_params=pltpu.CompilerParams(dimension_semantics=("parallel",)),
        name="s5_params",
    )(abl, bt, c_tile(c_re), c_tile(c_im))


def _s5_kernel(x_hbm, g_ref, w1_ref, l2r_ref, l2i_ref, cx_ref, d_ref, wglu_ref, bglu_ref,
               o_hbm, xin_s, hn_s, hnu_s, zy_s, bu_s, xs_s, y_s, st_s, res_s, sem_in, sem_out,
               *, tl, n_steps):
    half = tl // 2
    prow = half * SUBLANES
    sp = STATES_PER_TILE
    i = pl.program_id(0)
    slot = i % 2

    def in_copies(step, sl):
        return [pltpu.make_async_copy(x_hbm.at[b, pl.ds(step * tl, tl), :],
                                      xin_s.at[sl, :, b, :], sem_in.at[sl, b])
                for b in range(SUBLANES)]

    def out_copies(step, sl):
        return [pltpu.make_async_copy(res_s.at[sl, :, b, :],
                                      o_hbm.at[b, pl.ds(step * tl, tl), :], sem_out.at[sl, b])
                for b in range(SUBLANES)]

    @pl.when(i == 0)
    def _():
        st_s[...] = jnp.zeros_like(st_s)
        for copy in in_copies(0, 0):
            copy.start()

    @pl.when(i + 1 < n_steps)
    def _():
        for copy in in_copies(i + 1, 1 - slot):
            copy.start()

    for copy in in_copies(i, slot):
        copy.wait()

    @pl.when(i >= 2)
    def _():
        for copy in out_copies(i - 2, slot):
            copy.wait()

    def row_group(v, t):
        return v[t * SUBLANES:(t + 1) * SUBLANES]

    hn = _rmsnorm(xin_s[slot].reshape(tl * SUBLANES, D_MODEL), g_ref[...])
    for t in range(tl):
        g = (t % 2) * half + t // 2
        hn_s[g * SUBLANES:(g + 1) * SUBLANES, :] = row_group(hn, t)
    for m in range(tl // 4):
        r16 = slice(2 * m * SUBLANES, (2 * m + 2) * SUBLANES)
        even = jnp.concatenate([row_group(hn, 4 * m), row_group(hn, 4 * m + 2)], axis=0).astype(BF16)
        odd = jnp.concatenate([row_group(hn, 4 * m + 1), row_group(hn, 4 * m + 3)], axis=0).astype(BF16)
        for j in range(N_LANE_TILES):
            lanes = slice(j * LANES, (j + 1) * LANES)
            hnu_s[r16, 2 * j * LANES:(2 * j + 1) * LANES] = even[:, lanes]
            hnu_s[r16, (2 * j + 1) * LANES:(2 * j + 2) * LANES] = odd[:, lanes]

    def project_in(j):
        z = jnp.dot(hnu_s[:, 2 * j * LANES:(2 * j + 2) * LANES], w1_ref[j],
                    preferred_element_type=F32)
        zy_s[j % S5_TILES_IN_FLIGHT] = z[:, :2 * LANES]
        bu_s[j % S5_TILES_IN_FLIGHT] = z[:, 2 * LANES:]

    ahead = S5_TILES_IN_FLIGHT - 1
    for j in range(ahead):
        project_in(j)
    for j in range(N_LANE_TILES):
        if j + ahead < N_LANE_TILES:
            project_in(j + ahead)
        p = j % S5_TILES_IN_FLIGHT
        lanes = slice(j * LANES, (j + 1) * LANES)
        ar = l2r_ref[j]
        ai = l2i_ref[j]
        xr = st_s[j, :, 0:sp]
        xi = st_s[j, :, sp:2 * sp]
        for k in range(half):
            r = slice(k * SUBLANES, (k + 1) * SUBLANES)
            xs_s[p, r, 0:sp] = xr
            xs_s[p, r, sp:2 * sp] = xi
            nxr = ar * xr - ai * xi + bu_s[p, r, 0:sp]
            nxi = ar * xi + ai * xr + bu_s[p, r, sp:2 * sp]
            xr, xi = nxr, nxi
        st_s[j, :, 0:sp] = xr
        st_s[j, :, sp:2 * sp] = xi
        y = zy_s[p] + jnp.dot(xs_s[p].astype(BF16), cx_ref[j], preferred_element_type=F32)
        y_s[0:prow, lanes] = y[:, :LANES]
        y_s[prow:2 * prow, lanes] = y[:, LANES:]

    for par in range(2):
        rs = slice(par * prow, (par + 1) * prow)
        y = y_s[rs, :] + d_ref[...] * hn_s[rs, :]
        z = jnp.dot(_gelu_tanh(y).astype(BF16), wglu_ref[...].astype(BF16),
                    preferred_element_type=F32) + bglu_ref[...]
        mix = z[:, :D_MODEL] * (1.0 / (1.0 + jnp.exp(-z[:, D_MODEL:])))
        for k in range(half):
            t = 2 * k + par
            res_s[slot, t] = xin_s[slot, t] + row_group(mix, k)

    for copy in out_copies(i, slot):
        copy.start()

    @pl.when(i == n_steps - 1)
    def _():
        for copy in out_copies(i - 1, 1 - slot) + out_copies(i, slot):
            copy.wait()


def _s5_layer(x, g_mix, w1, lam2_re_t, lam2_im_t, cx, d_skip, w_glu, b_glu):
    bsz, seq, d = x.shape
    tl = S5_TL
    n_steps = seq // tl
    assert bsz == SUBLANES and n_steps >= 2
    rows = tl * bsz
    prow = rows // 2
    kernel = functools.partial(_s5_kernel, tl=tl, n_steps=n_steps)
    hbm = pl.BlockSpec(memory_space=pl.ANY)
    return pl.pallas_call(
        kernel,
        out_shape=jax.ShapeDtypeStruct((bsz, seq, d), F32),
        grid=(n_steps,),
        in_specs=[
            hbm,
            _const_spec((1, d)),
            _const_spec(w1.shape),
            _const_spec(lam2_re_t.shape),
            _const_spec(lam2_im_t.shape),
            _const_spec(cx.shape),
            _const_spec((1, d)),
            _const_spec(w_glu.shape),
            _const_spec((1, 2 * d)),
        ],
        out_specs=hbm,
        scratch_shapes=[
            pltpu.VMEM((2, tl, bsz, d), F32),
            pltpu.VMEM((rows, d), F32),
            pltpu.VMEM((prow, 2 * d), BF16),
            pltpu.VMEM((S5_TILES_IN_FLIGHT, prow, 2 * LANES), F32),
            pltpu.VMEM((S5_TILES_IN_FLIGHT, prow, 2 * STATES_PER_TILE), F32),
            pltpu.VMEM((S5_TILES_IN_FLIGHT, prow, 2 * STATES_PER_TILE), F32),
            pltpu.VMEM((rows, d), F32),
            pltpu.VMEM((N_LANE_TILES, SUBLANES, 2 * STATES_PER_TILE), F32),
            pltpu.VMEM((2, tl, bsz, d), F32),
            pltpu.SemaphoreType.DMA((2, SUBLANES)),
            pltpu.SemaphoreType.DMA((2, SUBLANES)),
        ],
        compiler_params=pltpu.CompilerParams(
            dimension_semantics=("arbitrary",), vmem_limit_bytes=56 * MIB),
        name="s5_layer",
    )(x, g_mix.reshape(1, d), w1, lam2_re_t, lam2_im_t, cx, d_skip.reshape(1, d), w_glu,
      b_glu.reshape(1, 2 * d))


def _inv_rms(x):
    return lax.rsqrt(jnp.mean(x * x, axis=-1, keepdims=True) + NORM_EPS)


def _mlp_kernel(h_ref, g_ref, win_hbm, wout_hbm, *rest, layer, with_kv):
    if with_kv:
        gkv_ref, wkv_ref, bkv_ref, o_ref, kv_ref, win_s, wout_s, sem = rest
    else:
        gfin_ref, o_ref, win_s, wout_s, sem = rest
    n_chunks = D_FF // MLP_FF_CHUNK

    def weight_copies(c):
        cols = pl.ds(c * MLP_FF_CHUNK, MLP_FF_CHUNK)
        return (pltpu.make_async_copy(win_hbm.at[layer, :, cols], win_s.at[:, cols], sem.at[0, c]),
                pltpu.make_async_copy(wout_hbm.at[layer, cols, :], wout_s.at[cols, :], sem.at[1, c]))

    def body(first_step):
        h = h_ref[...]
        hg = (h * g_ref[...]).astype(BF16)
        r = _inv_rms(h)
        acc = jnp.zeros(h.shape, F32)
        for c in range(n_chunks):
            cols = slice(c * MLP_FF_CHUNK, (c + 1) * MLP_FF_CHUNK)
            if first_step:
                for copy in weight_copies(c):
                    copy.wait()
            a = jnp.dot(hg, win_s[:, cols].astype(BF16), preferred_element_type=F32)
            a = jnp.square(jnp.maximum(a, 0.0)).astype(BF16)
            acc = acc + jnp.dot(a, wout_s[cols, :].astype(BF16), preferred_element_type=F32)
        out = h + (r * r) * acc
        if with_kv:
            o_ref[...] = out
            kv = jnp.dot((out * gkv_ref[...]).astype(BF16), wkv_ref[...].astype(BF16),
                         preferred_element_type=F32)
            kv_ref[...] = (_inv_rms(out) * kv + bkv_ref[...]).astype(BF16)
        else:
            o_ref[...] = out * _inv_rms(out) * gfin_ref[...]

    @pl.when(pl.program_id(0) == 0)
    def _():
        for c in range(n_chunks):
            for copy in weight_copies(c):
                copy.start()
        body(True)

    @pl.when(pl.program_id(0) > 0)
    def _():
        body(False)


def _mlp(h, g, w_in, w_out, layer, *, g_kv=None, w_kv=None, b_kv=None, g_fin=None):
    t, d = h.shape
    with_kv = w_kv is not None
    row = lambda n: pl.BlockSpec((MLP_TM, n), lambda i: (i, 0))
    hbm = pl.BlockSpec(memory_space=pl.ANY)
    in_specs = [row(d), _const_spec((1, d)), hbm, hbm]
    operands = [h, g.reshape(1, d), w_in, w_out]
    if with_kv:
        nkv = w_kv.shape[1]
        in_specs += [_const_spec((1, d)), _const_spec(w_kv.shape), _const_spec((1, nkv))]
        operands += [g_kv.reshape(1, d), w_kv, b_kv.reshape(1, nkv)]
        out_shape = (jax.ShapeDtypeStruct((t, d), F32), jax.ShapeDtypeStruct((t, nkv), BF16))
        out_specs = (row(d), row(nkv))
    else:
        in_specs += [_const_spec((1, d))]
        operands += [g_fin.reshape(1, d)]
        out_shape = jax.ShapeDtypeStruct((t, d), F32)
        out_specs = row(d)
    return pl.pallas_call(
        functools.partial(_mlp_kernel, layer=layer, with_kv=with_kv),
        out_shape=out_shape,
        grid=(t // MLP_TM,),
        in_specs=in_specs,
        out_specs=out_specs,
        scratch_shapes=[pltpu.VMEM(w_in.shape[1:], F32),
                        pltpu.VMEM(w_out.shape[1:], F32),
                        pltpu.SemaphoreType.DMA((2, D_FF // MLP_FF_CHUNK))],
        compiler_params=pltpu.CompilerParams(
            dimension_semantics=("arbitrary",), vmem_limit_bytes=56 * MIB),
        name="mlp_kv" if with_kv else "mlp_final",
    )(*operands)


def _pair_tile_sources(a, g):
    head_lo = (2 * a) * Q_PER_KV + g
    head_hi = (2 * a + 1) * Q_PER_KV + g
    return (head_lo // 2, head_lo % 2), (head_hi // 2, head_hi % 2)


def _attn_kernel(sink_ref, h_ref, kvp_ref, kvc_ref, g_ref, wq_ref, bq_ref, wo_ref, bo_ref,
                 o_ref, wq_s, bq_s, wo_s, q_s, oh_s, *, tq):
    nsb = tq // WINDOW
    n = pl.program_id(1)
    n_kv_tiles = N_KV_HEADS // 2
    lo_row = lax.broadcasted_iota(jnp.int32, (1, LANES), 1) < HEAD_DIM

    @pl.when((pl.program_id(0) == 0) & (n == 0))
    def _():
        for a in range(n_kv_tiles):
            for g in range(Q_PER_KV):
                (t_lo, h_lo), (t_hi, h_hi) = _pair_tile_sources(a, g)
                dst = slice((a * Q_PER_KV + g) * LANES, (a * Q_PER_KV + g + 1) * LANES)

                def pair(ref):
                    src_lo = ref[:, t_lo * LANES:(t_lo + 1) * LANES]
                    src_hi = ref[:, t_hi * LANES:(t_hi + 1) * LANES]
                    if h_lo == 1:
                        src_lo = pltpu.roll(src_lo, HEAD_DIM, axis=1)
                    if h_hi == 0:
                        src_hi = pltpu.roll(src_hi, HEAD_DIM, axis=1)
                    return jnp.where(lo_row, src_lo, src_hi)

                wq_s[:, dst] = pair(wq_ref).astype(BF16)
                bq_s[:, dst] = pair(bq_ref)
                for half in range(2):
                    head = (2 * a + half) * Q_PER_KV + g
                    r0 = (a * Q_PER_KV + g) * LANES + half * HEAD_DIM
                    wo_s[r0:r0 + HEAD_DIM, :] = (
                        wo_ref[head * HEAD_DIM:(head + 1) * HEAD_DIM, :].astype(BF16))

    h = h_ref[0]
    hn = _rmsnorm(h, g_ref[...]).astype(BF16)
    q = jnp.dot(hn, wq_s[...], preferred_element_type=F32) + bq_s[...]
    q_s[...] = (q * (LOG2_E / math.sqrt(HEAD_DIM))).astype(BF16)

    lo = lax.broadcasted_iota(jnp.int32, (WINDOW, LANES), 1) < HEAD_DIM
    own = (lax.broadcasted_iota(jnp.int32, (WINDOW, WINDOW), 1)
           <= lax.broadcasted_iota(jnp.int32, (WINDOW, WINDOW), 0))
    zero = jnp.zeros((WINDOW, LANES), BF16)
    fzero = jnp.zeros((WINDOW, WINDOW), F32)

    for sb in range(nsb):
        rows = slice(sb * WINDOW, (sb + 1) * WINDOW)
        prev_bias = jnp.where(n == 0, -jnp.inf, 0.0).astype(F32) if sb == 0 else None
        for a in range(n_kv_tiles):
            kl = slice(a * LANES, (a + 1) * LANES)
            vl = slice((n_kv_tiles + a) * LANES, (n_kv_tiles + a + 1) * LANES)
            if sb == 0:
                kprev, vprev = kvp_ref[0, :, kl], kvp_ref[0, :, vl]
            else:
                prow = slice((sb - 1) * WINDOW, sb * WINDOW)
                kprev, vprev = kvc_ref[0, prow, kl], kvc_ref[0, prow, vl]
            kd = jnp.concatenate([kprev, kvc_ref[0, rows, kl]], axis=0)
            vd = jnp.concatenate([vprev, kvc_ref[0, rows, vl]], axis=0)
            vd1 = jnp.concatenate([vd, jnp.ones((2 * WINDOW, LANES), BF16)], axis=1)
            qt = [q_s[rows, (a * Q_PER_KV + g) * LANES:(a * Q_PER_KV + g + 1) * LANES]
                  for g in range(Q_PER_KV)]
            lhs = jnp.concatenate([jnp.where(lo, t, zero) for t in qt]
                                  + [jnp.where(lo, zero, t) for t in qt], axis=0)
            s = lax.dot_general(lhs, kd, (((1,), (1,)), ((), ())),
                                preferred_element_type=F32)
            ps, sink_terms = [], []
            for half in range(2):
                for g in range(Q_PER_KV):
                    blk = half * Q_PER_KV + g
                    s_prev = s[blk * WINDOW:(blk + 1) * WINDOW, :WINDOW]
                    s_own = s[blk * WINDOW:(blk + 1) * WINDOW, WINDOW:]
                    if prev_bias is not None:
                        s_prev = s_prev + prev_bias
                    sg = jnp.where(own, s_own, s_prev)
                    sink = sink_ref[(2 * a + half) * Q_PER_KV + g] * LOG2_E
                    m = jnp.maximum(jnp.max(sg, axis=-1, keepdims=True), sink)
                    p = jnp.exp2(sg - m)
                    ps.append(jnp.concatenate([jnp.where(own, fzero, p).astype(BF16),
                                               jnp.where(own, p, fzero).astype(BF16)], axis=1))
                    sink_terms.append(jnp.exp2(sink - m))
            od = jnp.dot(jnp.concatenate(ps, axis=0), vd1, preferred_element_type=F32)
            for g in range(Q_PER_KV):
                r_lo = slice(g * WINDOW, (g + 1) * WINDOW)
                r_hi = slice((Q_PER_KV + g) * WINDOW, (Q_PER_KV + g + 1) * WINDOW)
                num = jnp.where(lo, od[r_lo, :LANES], od[r_hi, :LANES])
                den = (jnp.where(lo, od[r_lo, LANES:], od[r_hi, LANES:])
                       + jnp.where(lo, sink_terms[g], sink_terms[Q_PER_KV + g]))
                oh_s[rows, (a * Q_PER_KV + g) * LANES:(a * Q_PER_KV + g + 1) * LANES] = (
                    (num * (1.0 / den)).astype(BF16))

    o_ref[0] = h + jnp.dot(oh_s[...], wo_s[...], preferred_element_type=F32) + bo_ref[...]


def _attn_layer(h, kv, sinks, g, w_q, b_q, w_o, b_o):
    bsz, seq, d = h.shape
    nkv = kv.shape[-1]
    tq = ATTN_TQ
    nsb = tq // WINDOW
    kernel = functools.partial(_attn_kernel, tq=tq)
    return pl.pallas_call(
        kernel,
        out_shape=jax.ShapeDtypeStruct((bsz, seq, d), F32),
        grid=(bsz, seq // tq),
        in_specs=[
            pl.BlockSpec(memory_space=pltpu.SMEM),
            pl.BlockSpec((1, tq, d), lambda b, n: (b, n, 0)),
            pl.BlockSpec((1, WINDOW, nkv), lambda b, n: (b, jnp.maximum(n * nsb - 1, 0), 0)),
            pl.BlockSpec((1, tq, nkv), lambda b, n: (b, n, 0)),
            _const_spec((1, d)),
            _const_spec(w_q.shape),
            _const_spec((1, d)),
            _const_spec(w_o.shape),
            _const_spec((1, d)),
        ],
        out_specs=pl.BlockSpec((1, tq, d), lambda b, n: (b, n, 0)),
        scratch_shapes=[pltpu.VMEM((d, d), BF16),
                        pltpu.VMEM((1, d), F32),
                        pltpu.VMEM((d, d), BF16),
                        pltpu.VMEM((tq, d), BF16),
                        pltpu.VMEM((tq, d), BF16)],
        compiler_params=pltpu.CompilerParams(
            dimension_semantics=("arbitrary", "arbitrary"), vmem_limit_bytes=40 * MIB),
        name="attn",
    )(sinks, h, kv, kv, g.reshape(1, d), w_q, b_q.reshape(1, d), w_o, b_o.reshape(1, d))


def kernel(x, norm_mix, norm_mlp, norm_kv, norm_final, s5_a_re, s5_a_im, s5_log_dt, s5_b_re, s5_b_im, s5_c_re, s5_c_im, s5_d, s5_w_glu, s5_b_glu, w_kv, b_kv, w_q, b_q, sinks, w_o, b_o, w_mlp_in, w_mlp_out):
    bsz, seq, d = x.shape

    w1, cx, lam2_re_t, lam2_im_t = _s5_params(s5_a_re[0], s5_a_im[0], s5_log_dt[0], s5_b_re[0],
                                              s5_b_im[0], s5_c_re[0], s5_c_im[0])
    h = _s5_layer(x, norm_mix[0], w1, lam2_re_t, lam2_im_t, cx, s5_d[0], s5_w_glu[0], s5_b_glu[0])

    h, kv = _mlp(h.reshape(bsz * seq, d), norm_mlp[0], w_mlp_in, w_mlp_out, 0,
                 g_kv=norm_kv, w_kv=w_kv, b_kv=b_kv)

    h = _attn_layer(h.reshape(bsz, seq, d), kv.reshape(bsz, seq, -1), sinks[0], norm_mix[1],
                    w_q[0], b_q[0], w_o[0], b_o[0])

    out = _mlp(h.reshape(bsz * seq, d), norm_mlp[1], w_mlp_in, w_mlp_out, 1, g_fin=norm_final)
    return out.reshape(bsz, seq, d)
```

```python
import functools
import math

import jax
import jax.numpy as jnp
from jax import lax
from jax.experimental import pallas as pl
from jax.experimental.pallas import tpu as pltpu

F32 = jnp.float32
BF16 = jnp.bfloat16

D_MODEL = 1024
S5_GROUP = 16
S5_STATE = 64
LAMBDA_RE_MAX = -1e-4
HEAD_DIM = 64
N_Q_HEADS = D_MODEL // HEAD_DIM
N_KV_HEADS = 4
Q_PER_KV = N_Q_HEADS // N_KV_HEADS
WINDOW = 128
D_FF = 4 * D_MODEL
NORM_EPS = 1e-5
LOG2_E = math.log2(math.e)

LANES = 128
SUBLANES = 8
N_LANE_TILES = D_MODEL // LANES
GROUPS_PER_TILE = LANES // S5_GROUP
STATES_PER_TILE = GROUPS_PER_TILE * S5_STATE
MIB = 1024 * 1024
VMEM_LIMIT_BYTES = 56 * MIB

S5_TL = 64
S5_PARAM_TILES_PER_STEP = 4
S5_TILES_IN_FLIGHT = 2
MLP_TM = 512
MLP_FF_CHUNK = 1024
ATTN_TQ = 1024


def _rmsnorm(x, g):
    return x * lax.rsqrt(jnp.mean(x * x, axis=-1, keepdims=True) + NORM_EPS) * g


def _gelu_tanh(x):
    c = math.sqrt(2.0 / math.pi)
    return 0.5 * x * (1.0 + jnp.tanh(c * (x + 0.044715 * (x * x * x))))


def _const_spec(shape):
    nd = len(shape)
    return pl.BlockSpec(shape, lambda *_: (0,) * nd, pipeline_mode=pl.Buffered(1))


def _s5_params_kernel(abl_ref, bt_ref, cre_ref, cim_ref, w1_ref, cx_ref, l2r_ref, l2i_ref):
    for tile in range(S5_PARAM_TILES_PER_STEP):
        _s5_params_tile(abl_ref.at[:, tile], bt_ref.at[:, tile], cre_ref.at[tile], cim_ref.at[tile],
                        w1_ref.at[tile], cx_ref.at[tile], l2r_ref.at[tile], l2i_ref.at[tile])


def _s5_params_tile(abl_ref, bt_ref, cre_ref, cim_ref, w1_ref, cx_ref, l2r_ref, l2i_ref):
    gpt, grp, nst = GROUPS_PER_TILE, S5_GROUP, S5_STATE
    ar = jnp.minimum(abl_ref[0], LAMBDA_RE_MAX)
    ai = abl_ref[1]
    dt = jnp.exp(abl_ref[2])
    mag = jnp.exp(ar * dt)
    ang = ai * dt
    lr = mag * jnp.cos(ang)
    li = mag * jnp.sin(ang)
    den = ar * ar + ai * ai
    cr = ((lr - 1.0) * ar + li * ai) / den
    ci = (li * ar - (lr - 1.0) * ai) / den
    l2r = lr * lr - li * li
    l2i = 2.0 * (lr * li)

    def per_row(v):
        return jnp.concatenate([jnp.broadcast_to(v[g:g + 1], (grp, nst)) for g in range(gpt)], axis=0)

    row_g = lax.broadcasted_iota(jnp.int32, (LANES, STATES_PER_TILE), 0) // grp
    col_g = lax.broadcasted_iota(jnp.int32, (LANES, STATES_PER_TILE), 1) // nst
    own = row_g == col_g

    def blockdiag(v):
        return jnp.where(own, jnp.concatenate([v] * gpt, axis=1), 0.0)

    lr_c, li_c, cr_c, ci_c = per_row(lr), per_row(li), per_row(cr), per_row(ci)
    l2r_c, l2i_c = per_row(l2r), per_row(l2i)
    btr, bti = bt_ref[0], bt_ref[1]
    bbr = cr_c * btr - ci_c * bti
    bbi = cr_c * bti + ci_c * btr
    wbr, wbi = blockdiag(bbr), blockdiag(bbi)
    w1r = blockdiag(lr_c * bbr - li_c * bbi)
    w1i = blockdiag(lr_c * bbi + li_c * bbr)
    ccr, cci = cre_ref[...], cim_ref[...]
    wcr, wci = blockdiag(ccr), blockdiag(cci)

    def hdot_nt(a, b):
        return lax.dot_general(a, b, (((1,), (1,)), ((), ())),
                               precision=lax.Precision.HIGHEST, preferred_element_type=F32)

    k0 = hdot_nt(wbr, wcr) - hdot_nt(wbi, wci)
    k1 = hdot_nt(w1r, wcr) - hdot_nt(w1i, wci)
    top = jnp.concatenate([k0, k1, w1r, w1i], axis=1)
    bot = jnp.concatenate([jnp.zeros_like(k0), k0, wbr, wbi], axis=1)
    w1_ref[...] = jnp.concatenate([top, bot], axis=0).astype(BF16)

    def readout_t(pr, pi):
        return jnp.concatenate([blockdiag(pr * ccr - pi * cci),
                                -blockdiag(pi * ccr + pr * cci)], axis=1)

    cx_t = jnp.concatenate([readout_t(lr_c, li_c), readout_t(l2r_c, l2i_c)], axis=0)
    cx_ref[...] = cx_t.T.astype(BF16)

    def flat_row(v):
        row = jnp.concatenate([v[g:g + 1] for g in range(gpt)], axis=1)
        return jnp.broadcast_to(row, (SUBLANES, STATES_PER_TILE))

    l2r_ref[...] = flat_row(l2r)
    l2i_ref[...] = flat_row(l2i)


def _s5_params(a_re, a_im, log_dt, b_re, b_im, c_re, c_im):
    nt, gpt, sp = N_LANE_TILES, GROUPS_PER_TILE, STATES_PER_TILE
    tps = S5_PARAM_TILES_PER_STEP
    abl = jnp.stack([a_re, a_im, jnp.broadcast_to(log_dt[:, None], a_re.shape)])
    abl = abl.reshape(3, nt, gpt, S5_STATE)
    bt = jnp.swapaxes(jnp.stack([b_re, b_im]), -1, -2).reshape(2, nt, LANES, S5_STATE)
    c_tile = lambda c: c.reshape(nt, LANES, S5_STATE)
    return pl.pallas_call(
        _s5_params_kernel,
        out_shape=(jax.ShapeDtypeStruct((nt, 2 * LANES, 2 * LANES + 2 * sp), BF16),
                   jax.ShapeDtypeStruct((nt, 2 * sp, 2 * LANES), BF16),
                   jax.ShapeDtypeStruct((nt, SUBLANES, sp), F32),
                   jax.ShapeDtypeStruct((nt, SUBLANES, sp), F32)),
        grid=(nt // tps,),
        in_specs=[pl.BlockSpec((3, tps, gpt, S5_STATE), lambda j: (0, j, 0, 0)),
                  pl.BlockSpec((2, tps, LANES, S5_STATE), lambda j: (0, j, 0, 0)),
                  pl.BlockSpec((tps, LANES, S5_STATE), lambda j: (j, 0, 0)),
                  pl.BlockSpec((tps, LANES, S5_STATE), lambda j: (j, 0, 0))],
        out_specs=(pl.BlockSpec((tps, 2 * LANES, 2 * LANES + 2 * sp), lambda j: (j, 0, 0)),
                   pl.BlockSpec((tps, 2 * sp, 2 * LANES), lambda j: (j, 0, 0)),
                   pl.BlockSpec((tps, SUBLANES, sp), lambda j: (j, 0, 0)),
                   pl.BlockSpec((tps, SUBLANES, sp), lambda j: (j, 0, 0))),
        compiler_params=pltpu.CompilerParams(dimension_semantics=("parallel",)),
        name="s5_params",
    )(abl, bt, c_tile(c_re), c_tile(c_im))


def _s5_kernel(x_hbm, g_ref, w1_ref, l2r_ref, l2i_ref, cx_ref, d_ref, wglu_ref, bglu_ref,
               o_hbm, xin_s, hn_s, hnu_s, zy_s, bu_s, xs_s, y_s, st_s, res_s, sem_in, sem_out,
               *, tl, n_steps):
    half = tl // 2
    prow = half * SUBLANES
    sp = STATES_PER_TILE
    i = pl.program_id(0)
    slot = i % 2

    def in_copies(step, sl):
        return [pltpu.make_async_copy(x_hbm.at[b, pl.ds(step * tl, tl), :],
                                      xin_s.at[sl, :, b, :], sem_in.at[sl, b])
                for b in range(SUBLANES)]

    def out_copies(step, sl):
        return [pltpu.make_async_copy(res_s.at[sl, :, b, :],
                                      o_hbm.at[b, pl.ds(step * tl, tl), :], sem_out.at[sl, b])
                for b in range(SUBLANES)]

    @pl.when(i == 0)
    def _():
        st_s[...] = jnp.zeros_like(st_s)
        for copy in in_copies(0, 0):
            copy.start()

    @pl.when(i + 1 < n_steps)
    def _():
        for copy in in_copies(i + 1, 1 - slot):
            copy.start()

    for copy in in_copies(i, slot):
        copy.wait()

    @pl.when(i >= 2)
    def _():
        for copy in out_copies(i - 2, slot):
            copy.wait()

    def row_group(v, t):
        return v[t * SUBLANES:(t + 1) * SUBLANES]

    hn = _rmsnorm(xin_s[slot].reshape(tl * SUBLANES, D_MODEL), g_ref[...])
    for t in range(tl):
        g = (t % 2) * half + t // 2
        hn_s[g * SUBLANES:(g + 1) * SUBLANES, :] = row_group(hn, t)
    for m in range(tl // 4):
        r16 = slice(2 * m * SUBLANES, (2 * m + 2) * SUBLANES)
        even = jnp.concatenate([row_group(hn, 4 * m), row_group(hn, 4 * m + 2)], axis=0).astype(BF16)
        odd = jnp.concatenate([row_group(hn, 4 * m + 1), row_group(hn, 4 * m + 3)], axis=0).astype(BF16)
        for j in range(N_LANE_TILES):
            lanes = slice(j * LANES, (j + 1) * LANES)
            hnu_s[r16, 2 * j * LANES:(2 * j + 1) * LANES] = even[:, lanes]
            hnu_s[r16, (2 * j + 1) * LANES:(2 * j + 2) * LANES] = odd[:, lanes]

    def project_in(j):
        z = jnp.dot(hnu_s[:, 2 * j * LANES:(2 * j + 2) * LANES], w1_ref[j],
                    preferred_element_type=F32)
        zy_s[j % S5_TILES_IN_FLIGHT] = z[:, :2 * LANES]
        bu_s[j % S5_TILES_IN_FLIGHT] = z[:, 2 * LANES:]

    ahead = S5_TILES_IN_FLIGHT - 1
    for j in range(ahead):
        project_in(j)
    for j in range(N_LANE_TILES):
        if j + ahead < N_LANE_TILES:
            project_in(j + ahead)
        p = j % S5_TILES_IN_FLIGHT
        lanes = slice(j * LANES, (j + 1) * LANES)
        ar = l2r_ref[j]
        ai = l2i_ref[j]
        xr = st_s[j, :, 0:sp]
        xi = st_s[j, :, sp:2 * sp]
        for k in range(half):
            r = slice(k * SUBLANES, (k + 1) * SUBLANES)
            xs_s[p, r, 0:sp] = xr
            xs_s[p, r, sp:2 * sp] = xi
            nxr = ar * xr - ai * xi + bu_s[p, r, 0:sp]
            nxi = ar * xi + ai * xr + bu_s[p, r, sp:2 * sp]
            xr, xi = nxr, nxi
        st_s[j, :, 0:sp] = xr
        st_s[j, :, sp:2 * sp] = xi
        y = zy_s[p] + jnp.dot(xs_s[p].astype(BF16), cx_ref[j], preferred_element_type=F32)
        y_s[0:prow, lanes] = y[:, :LANES]
        y_s[prow:2 * prow, lanes] = y[:, LANES:]

    for par in range(2):
        rs = slice(par * prow, (par + 1) * prow)
        y = y_s[rs, :] + d_ref[...] * hn_s[rs, :]
        z = jnp.dot(_gelu_tanh(y).astype(BF16), wglu_ref[...].astype(BF16),
                    preferred_element_type=F32) + bglu_ref[...]
        mix = z[:, :D_MODEL] * (1.0 / (1.0 + jnp.exp(-z[:, D_MODEL:])))
        for k in range(half):
            t = 2 * k + par
            res_s[slot, t] = xin_s[slot, t] + row_group(mix, k)

    for copy in out_copies(i, slot):
        copy.start()

    @pl.when(i == n_steps - 1)
    def _():
        for copy in out_copies(i - 1, 1 - slot) + out_copies(i, slot):
            copy.wait()


def _s5_layer(x, g_mix, w1, lam2_re_t, lam2_im_t, cx, d_skip, w_glu, b_glu):
    bsz, seq, d = x.shape
    tl = S5_TL
    n_steps = seq // tl
    assert bsz == SUBLANES and n_steps >= 2
    rows = tl * bsz
    prow = rows // 2
    kernel = functools.partial(_s5_kernel, tl=tl, n_steps=n_steps)
    hbm = pl.BlockSpec(memory_space=pl.ANY)
    return pl.pallas_call(
        kernel,
        out_shape=jax.ShapeDtypeStruct((bsz, seq, d), F32),
        grid=(n_steps,),
        in_specs=[
            hbm,
            _const_spec((1, d)),
            _const_spec(w1.shape),
            _const_spec(lam2_re_t.shape),
            _const_spec(lam2_im_t.shape),
            _const_spec(cx.shape),
            _const_spec((1, d)),
            _const_spec(w_glu.shape),
            _const_spec((1, 2 * d)),
        ],
        out_specs=hbm,
        scratch_shapes=[
            pltpu.VMEM((2, tl, bsz, d), F32),
            pltpu.VMEM((rows, d), F32),
            pltpu.VMEM((prow, 2 * d), BF16),
            pltpu.VMEM((S5_TILES_IN_FLIGHT, prow, 2 * LANES), F32),
            pltpu.VMEM((S5_TILES_IN_FLIGHT, prow, 2 * STATES_PER_TILE), F32),
            pltpu.VMEM((S5_TILES_IN_FLIGHT, prow, 2 * STATES_PER_TILE), F32),
            pltpu.VMEM((rows, d), F32),
            pltpu.VMEM((N_LANE_TILES, SUBLANES, 2 * STATES_PER_TILE), F32),
            pltpu.VMEM((2, tl, bsz, d), F32),
            pltpu.SemaphoreType.DMA((2, SUBLANES)),
            pltpu.SemaphoreType.DMA((2, SUBLANES)),
        ],
        compiler_params=pltpu.CompilerParams(
            dimension_semantics=("arbitrary",), vmem_limit_bytes=VMEM_LIMIT_BYTES),
        name="s5_layer",
    )(x, g_mix.reshape(1, d), w1, lam2_re_t, lam2_im_t, cx, d_skip.reshape(1, d), w_glu,
      b_glu.reshape(1, 2 * d))


def _inv_rms(x):
    return lax.rsqrt(jnp.mean(x * x, axis=-1, keepdims=True) + NORM_EPS)


def _mlp_kernel(h_ref, g_ref, win_hbm, wout_hbm, *rest, layer, with_kv):
    if with_kv:
        gkv_ref, wkv_ref, bkv_ref, o_ref, kv_ref, win_s, wout_s, sem = rest
    else:
        gfin_ref, o_ref, win_s, wout_s, sem = rest
    n_chunks = D_FF // MLP_FF_CHUNK

    def weight_copies(c):
        cols = pl.ds(c * MLP_FF_CHUNK, MLP_FF_CHUNK)
        return (pltpu.make_async_copy(win_hbm.at[layer, :, cols], win_s.at[:, cols], sem.at[0, c]),
                pltpu.make_async_copy(wout_hbm.at[layer, cols, :], wout_s.at[cols, :], sem.at[1, c]))

    def body(first_step):
        h = h_ref[...]
        hg = (h * g_ref[...]).astype(BF16)
        r = _inv_rms(h)
        acts = []
        for c in range(n_chunks):
            cols = slice(c * MLP_FF_CHUNK, (c + 1) * MLP_FF_CHUNK)
            if first_step:
                for copy in weight_copies(c):
                    copy.wait()
            a = jnp.dot(hg, win_s[:, cols].astype(BF16), preferred_element_type=F32)
            acts.append(jnp.square(jnp.maximum(a, 0.0)).astype(BF16))
        acc = jnp.dot(jnp.concatenate(acts, axis=1), wout_s[...].astype(BF16),
                      preferred_element_type=F32)
        out = h + (r * r) * acc
        if with_kv:
            o_ref[...] = out
            kv = jnp.dot((out * gkv_ref[...]).astype(BF16), wkv_ref[...].astype(BF16),
                         preferred_element_type=F32)
            kv_ref[...] = (_inv_rms(out) * kv + bkv_ref[...]).astype(BF16)
        else:
            o_ref[...] = out * _inv_rms(out) * gfin_ref[...]

    @pl.when(pl.program_id(0) == 0)
    def _():
        for c in range(n_chunks):
            for copy in weight_copies(c):
                copy.start()
        body(True)

    @pl.when(pl.program_id(0) > 0)
    def _():
        body(False)


def _mlp(h, g, w_in, w_out, layer, *, g_kv=None, w_kv=None, b_kv=None, g_fin=None):
    t, d = h.shape
    with_kv = w_kv is not None
    row = lambda n: pl.BlockSpec((MLP_TM, n), lambda i: (i, 0))
    hbm = pl.BlockSpec(memory_space=pl.ANY)
    in_specs = [row(d), _const_spec((1, d)), hbm, hbm]
    operands = [h, g.reshape(1, d), w_in, w_out]
    if with_kv:
        nkv = w_kv.shape[1]
        in_specs += [_const_spec((1, d)), _const_spec(w_kv.shape), _const_spec((1, nkv))]
        operands += [g_kv.reshape(1, d), w_kv, b_kv.reshape(1, nkv)]
        out_shape = (jax.ShapeDtypeStruct((t, d), F32), jax.ShapeDtypeStruct((t, nkv), BF16))
        out_specs = (row(d), row(nkv))
    else:
        in_specs += [_const_spec((1, d))]
        operands += [g_fin.reshape(1, d)]
        out_shape = jax.ShapeDtypeStruct((t, d), F32)
        out_specs = row(d)
    return pl.pallas_call(
        functools.partial(_mlp_kernel, layer=layer, with_kv=with_kv),
        out_shape=out_shape,
        grid=(t // MLP_TM,),
        in_specs=in_specs,
        out_specs=out_specs,
        scratch_shapes=[pltpu.VMEM(w_in.shape[1:], F32),
                        pltpu.VMEM(w_out.shape[1:], F32),
                        pltpu.SemaphoreType.DMA((2, D_FF // MLP_FF_CHUNK))],
        compiler_params=pltpu.CompilerParams(
            dimension_semantics=("arbitrary",), vmem_limit_bytes=VMEM_LIMIT_BYTES),
        name="mlp_kv" if with_kv else "mlp_final",
    )(*operands)


def _pair_tile_sources(a, g):
    head_lo = (2 * a) * Q_PER_KV + g
    head_hi = (2 * a + 1) * Q_PER_KV + g
    return (head_lo // 2, head_lo % 2), (head_hi // 2, head_hi % 2)


def _attn_kernel(sink_ref, h_ref, kvp_ref, kvc_ref, g_ref, wq_ref, bq_ref, wo_ref, bo_ref,
                 o_ref, wq_s, bq_s, wo_s, q_s, oh_s, *, tq):
    nsb = tq // WINDOW
    n = pl.program_id(1)
    n_kv_tiles = N_KV_HEADS // 2
    lo_row = lax.broadcasted_iota(jnp.int32, (1, LANES), 1) < HEAD_DIM

    @pl.when((pl.program_id(0) == 0) & (n == 0))
    def _():
        for a in range(n_kv_tiles):
            for g in range(Q_PER_KV):
                (t_lo, h_lo), (t_hi, h_hi) = _pair_tile_sources(a, g)
                dst = slice((a * Q_PER_KV + g) * LANES, (a * Q_PER_KV + g + 1) * LANES)

                def pair(ref):
                    src_lo = ref[:, t_lo * LANES:(t_lo + 1) * LANES]
                    src_hi = ref[:, t_hi * LANES:(t_hi + 1) * LANES]
                    if h_lo == 1:
                        src_lo = pltpu.roll(src_lo, HEAD_DIM, axis=1)
                    if h_hi == 0:
                        src_hi = pltpu.roll(src_hi, HEAD_DIM, axis=1)
                    return jnp.where(lo_row, src_lo, src_hi)

                wq_s[:, dst] = pair(wq_ref).astype(BF16)
                bq_s[:, dst] = pair(bq_ref)
                for half in range(2):
                    head = (2 * a + half) * Q_PER_KV + g
                    r0 = (a * Q_PER_KV + g) * LANES + half * HEAD_DIM
                    wo_s[r0:r0 + HEAD_DIM, :] = (
                        wo_ref[head * HEAD_DIM:(head + 1) * HEAD_DIM, :].astype(BF16))

    h = h_ref[0]
    hn = _rmsnorm(h, g_ref[...]).astype(BF16)
    q = jnp.dot(hn, wq_s[...], preferred_element_type=F32) + bq_s[...]
    q_s[...] = (q * (LOG2_E / math.sqrt(HEAD_DIM))).astype(BF16)

    lo = lax.broadcasted_iota(jnp.int32, (WINDOW, LANES), 1) < HEAD_DIM
    own = (lax.broadcasted_iota(jnp.int32, (WINDOW, WINDOW), 1)
           <= lax.broadcasted_iota(jnp.int32, (WINDOW, WINDOW), 0))
    zero = jnp.zeros((WINDOW, LANES), BF16)
    fzero = jnp.zeros((WINDOW, WINDOW), F32)

    for sb in range(nsb):
        rows = slice(sb * WINDOW, (sb + 1) * WINDOW)
        prev_bias = jnp.where(n == 0, -jnp.inf, 0.0).astype(F32) if sb == 0 else None
        for a in range(n_kv_tiles):
            kl = slice(a * LANES, (a + 1) * LANES)
            vl = slice((n_kv_tiles + a) * LANES, (n_kv_tiles + a + 1) * LANES)
            if sb == 0:
                kprev, vprev = kvp_ref[0, :, kl], kvp_ref[0, :, vl]
            else:
                prow = slice((sb - 1) * WINDOW, sb * WINDOW)
                kprev, vprev = kvc_ref[0, prow, kl], kvc_ref[0, prow, vl]
            kd = jnp.concatenate([kprev, kvc_ref[0, rows, kl]], axis=0)
            vd = jnp.concatenate([vprev, kvc_ref[0, rows, vl]], axis=0)
            vd1 = jnp.concatenate([vd, jnp.ones((2 * WINDOW, LANES), BF16)], axis=1)
            qt = [q_s[rows, (a * Q_PER_KV + g) * LANES:(a * Q_PER_KV + g + 1) * LANES]
                  for g in range(Q_PER_KV)]
            lhs = jnp.concatenate([jnp.where(lo, t, zero) for t in qt]
                                  + [jnp.where(lo, zero, t) for t in qt], axis=0)
            s = lax.dot_general(lhs, kd, (((1,), (1,)), ((), ())),
                                preferred_element_type=F32)
            ps, sink_terms = [], []
            for half in range(2):
                for g in range(Q_PER_KV):
                    blk = half * Q_PER_KV + g
                    s_prev = s[blk * WINDOW:(blk + 1) * WINDOW, :WINDOW]
                    s_own = s[blk * WINDOW:(blk + 1) * WINDOW, WINDOW:]
                    if prev_bias is not None:
                        s_prev = s_prev + prev_bias
                    sg = jnp.where(own, s_own, s_prev)
                    sink = sink_ref[(2 * a + half) * Q_PER_KV + g] * LOG2_E
                    m = jnp.maximum(jnp.max(sg, axis=-1, keepdims=True), sink)
                    p = jnp.exp2(sg - m)
                    ps.append(jnp.concatenate([jnp.where(own, fzero, p).astype(BF16),
                                               jnp.where(own, p, fzero).astype(BF16)], axis=1))
                    sink_terms.append(jnp.exp2(sink - m))
            od = jnp.dot(jnp.concatenate(ps, axis=0), vd1, preferred_element_type=F32)
            for g in range(Q_PER_KV):
                r_lo = slice(g * WINDOW, (g + 1) * WINDOW)
                r_hi = slice((Q_PER_KV + g) * WINDOW, (Q_PER_KV + g + 1) * WINDOW)
                num = jnp.where(lo, od[r_lo, :LANES], od[r_hi, :LANES])
                den = (jnp.where(lo, od[r_lo, LANES:], od[r_hi, LANES:])
                       + jnp.where(lo, sink_terms[g], sink_terms[Q_PER_KV + g]))
                oh_s[rows, (a * Q_PER_KV + g) * LANES:(a * Q_PER_KV + g + 1) * LANES] = (
                    (num * (1.0 / den)).astype(BF16))

    o_ref[0] = h + jnp.dot(oh_s[...], wo_s[...], preferred_element_type=F32) + bo_ref[...]


def _attn_layer(h, kv, sinks, g, w_q, b_q, w_o, b_o):
    bsz, seq, d = h.shape
    nkv = kv.shape[-1]
    tq = ATTN_TQ
    nsb = tq // WINDOW
    kernel = functools.partial(_attn_kernel, tq=tq)
    return pl.pallas_call(
        kernel,
        out_shape=jax.ShapeDtypeStruct((bsz, seq, d), F32),
        grid=(bsz, seq // tq),
        in_specs=[
            pl.BlockSpec(memory_space=pltpu.SMEM),
            pl.BlockSpec((1, tq, d), lambda b, n: (b, n, 0)),
            pl.BlockSpec((1, WINDOW, nkv), lambda b, n: (b, jnp.maximum(n * nsb - 1, 0), 0)),
            pl.BlockSpec((1, tq, nkv), lambda b, n: (b, n, 0)),
            _const_spec((1, d)),
            _const_spec(w_q.shape),
            _const_spec((1, d)),
            _const_spec(w_o.shape),
            _const_spec((1, d)),
        ],
        out_specs=pl.BlockSpec((1, tq, d), lambda b, n: (b, n, 0)),
        scratch_shapes=[pltpu.VMEM((d, d), BF16),
                        pltpu.VMEM((1, d), F32),
                        pltpu.VMEM((d, d), BF16),
                        pltpu.VMEM((tq, d), BF16),
                        pltpu.VMEM((tq, d), BF16)],
        compiler_params=pltpu.CompilerParams(
            dimension_semantics=("arbitrary", "arbitrary"), vmem_limit_bytes=VMEM_LIMIT_BYTES),
        name="attn",
    )(sinks, h, kv, kv, g.reshape(1, d), w_q, b_q.reshape(1, d), w_o, b_o.reshape(1, d))


def kernel(x, norm_mix, norm_mlp, norm_kv, norm_final, s5_a_re, s5_a_im, s5_log_dt, s5_b_re, s5_b_im, s5_c_re, s5_c_im, s5_d, s5_w_glu, s5_b_glu, w_kv, b_kv, w_q, b_q, sinks, w_o, b_o, w_mlp_in, w_mlp_out):
    bsz, seq, d = x.shape

    w1, cx, lam2_re_t, lam2_im_t = _s5_params(s5_a_re[0], s5_a_im[0], s5_log_dt[0], s5_b_re[0],
                                              s5_b_im[0], s5_c_re[0], s5_c_im[0])
    h = _s5_layer(x, norm_mix[0], w1, lam2_re_t, lam2_im_t, cx, s5_d[0], s5_w_glu[0], s5_b_glu[0])

    h, kv = _mlp(h.reshape(bsz * seq, d), norm_mlp[0], w_mlp_in, w_mlp_out, 0,
                 g_kv=norm_kv, w_kv=w_kv, b_kv=b_kv)

    h = _attn_layer(h.reshape(bsz, seq, d), kv.reshape(bsz, seq, -1), sinks[0], norm_mix[1],
                    w_q[0], b_q[0], w_o[0], b_o[0])

    out = _mlp(h.reshape(bsz * seq, d), norm_mlp[1], w_mlp_in, w_mlp_out, 1, g_fin=norm_final)
    return out.reshape(bsz, seq, d)
```

```python
import functools
import math

import jax
import jax.numpy as jnp
from jax import lax
from jax.experimental import pallas as pl
from jax.experimental.pallas import tpu as pltpu

F32 = jnp.float32
BF16 = jnp.bfloat16

D_MODEL = 1024
S5_GROUP = 16
S5_STATE = 64
LAMBDA_RE_MAX = -1e-4
HEAD_DIM = 64
N_Q_HEADS = D_MODEL // HEAD_DIM
N_KV_HEADS = 4
Q_PER_KV = N_Q_HEADS // N_KV_HEADS
WINDOW = 128
D_FF = 4 * D_MODEL
NORM_EPS = 1e-5
LOG2_E = math.log2(math.e)

LANES = 128
SUBLANES = 8
N_LANE_TILES = D_MODEL // LANES
GROUPS_PER_TILE = LANES // S5_GROUP
STATES_PER_TILE = GROUPS_PER_TILE * S5_STATE
MIB = 1024 * 1024
VMEM_LIMIT_BYTES = 56 * MIB

S5_TL = 64
S5_PARAM_TILES_PER_STEP = 4
S5_TILES_IN_FLIGHT = 2
MLP_TM = 512
MLP_FF_CHUNK = 1024
MLP_DMA_CHUNK = 512
ATTN_TQ = 1024


def _rmsnorm(x, g):
    return x * lax.rsqrt(jnp.mean(x * x, axis=-1, keepdims=True) + NORM_EPS) * g


def _gelu_tanh(x):
    c = math.sqrt(2.0 / math.pi)
    return 0.5 * x * (1.0 + jnp.tanh(c * (x + 0.044715 * (x * x * x))))


def _const_spec(shape):
    nd = len(shape)
    return pl.BlockSpec(shape, lambda *_: (0,) * nd, pipeline_mode=pl.Buffered(1))


def _s5_params_kernel(abl_ref, bt_ref, cre_ref, cim_ref, w1_ref, cx_ref, l2r_ref, l2i_ref):
    for tile in range(S5_PARAM_TILES_PER_STEP):
        _s5_params_tile(abl_ref.at[:, tile], bt_ref.at[:, tile], cre_ref.at[tile], cim_ref.at[tile],
                        w1_ref.at[tile], cx_ref.at[tile], l2r_ref.at[tile], l2i_ref.at[tile])


def _s5_params_tile(abl_ref, bt_ref, cre_ref, cim_ref, w1_ref, cx_ref, l2r_ref, l2i_ref):
    gpt, grp, nst = GROUPS_PER_TILE, S5_GROUP, S5_STATE
    ar = jnp.minimum(abl_ref[0], LAMBDA_RE_MAX)
    ai = abl_ref[1]
    dt = jnp.exp(abl_ref[2])
    mag = jnp.exp(ar * dt)
    ang = ai * dt
    lr = mag * jnp.cos(ang)
    li = mag * jnp.sin(ang)
    den = ar * ar + ai * ai
    cr = ((lr - 1.0) * ar + li * ai) / den
    ci = (li * ar - (lr - 1.0) * ai) / den
    l2r = lr * lr - li * li
    l2i = 2.0 * (lr * li)

    def per_row(v):
        return jnp.concatenate([jnp.broadcast_to(v[g:g + 1], (grp, nst)) for g in range(gpt)], axis=0)

    row_g = lax.broadcasted_iota(jnp.int32, (LANES, STATES_PER_TILE), 0) // grp
    col_g = lax.broadcasted_iota(jnp.int32, (LANES, STATES_PER_TILE), 1) // nst
    own = row_g == col_g

    def blockdiag(v):
        return jnp.where(own, jnp.concatenate([v] * gpt, axis=1), 0.0)

    lr_c, li_c, cr_c, ci_c = per_row(lr), per_row(li), per_row(cr), per_row(ci)
    l2r_c, l2i_c = per_row(l2r), per_row(l2i)
    btr, bti = bt_ref[0], bt_ref[1]
    bbr = cr_c * btr - ci_c * bti
    bbi = cr_c * bti + ci_c * btr
    wbr, wbi = blockdiag(bbr), blockdiag(bbi)
    w1r = blockdiag(lr_c * bbr - li_c * bbi)
    w1i = blockdiag(lr_c * bbi + li_c * bbr)
    ccr, cci = cre_ref[...], cim_ref[...]
    wcr, wci = blockdiag(ccr), blockdiag(cci)

    def hdot_nt(a, b):
        return lax.dot_general(a, b, (((1,), (1,)), ((), ())),
                               precision=lax.Precision.HIGHEST, preferred_element_type=F32)

    k0 = hdot_nt(wbr, wcr) - hdot_nt(wbi, wci)
    k1 = hdot_nt(w1r, wcr) - hdot_nt(w1i, wci)
    top = jnp.concatenate([k0, k1, w1r, w1i], axis=1)
    bot = jnp.concatenate([jnp.zeros_like(k0), k0, wbr, wbi], axis=1)
    w1_ref[...] = jnp.concatenate([top, bot], axis=0).astype(BF16)

    def readout_t(pr, pi):
        return jnp.concatenate([blockdiag(pr * ccr - pi * cci),
                                -blockdiag(pi * ccr + pr * cci)], axis=1)

    cx_t = jnp.concatenate([readout_t(lr_c, li_c), readout_t(l2r_c, l2i_c)], axis=0)
    cx_ref[...] = cx_t.T.astype(BF16)

    def flat_row(v):
        row = jnp.concatenate([v[g:g + 1] for g in range(gpt)], axis=1)
        return jnp.broadcast_to(row, (SUBLANES, STATES_PER_TILE))

    l2r_ref[...] = flat_row(l2r)
    l2i_ref[...] = flat_row(l2i)


def _s5_params(a_re, a_im, log_dt, b_re, b_im, c_re, c_im):
    nt, gpt, sp = N_LANE_TILES, GROUPS_PER_TILE, STATES_PER_TILE
    tps = S5_PARAM_TILES_PER_STEP
    abl = jnp.stack([a_re, a_im, jnp.broadcast_to(log_dt[:, None], a_re.shape)])
    abl = abl.reshape(3, nt, gpt, S5_STATE)
    bt = jnp.swapaxes(jnp.stack([b_re, b_im]), -1, -2).reshape(2, nt, LANES, S5_STATE)
    c_tile = lambda c: c.reshape(nt, LANES, S5_STATE)
    return pl.pallas_call(
        _s5_params_kernel,
        out_shape=(jax.ShapeDtypeStruct((nt, 2 * LANES, 2 * LANES + 2 * sp), BF16),
                   jax.ShapeDtypeStruct((nt, 2 * sp, 2 * LANES), BF16),
                   jax.ShapeDtypeStruct((nt, SUBLANES, sp), F32),
                   jax.ShapeDtypeStruct((nt, SUBLANES, sp), F32)),
        grid=(nt // tps,),
        in_specs=[pl.BlockSpec((3, tps, gpt, S5_STATE), lambda j: (0, j, 0, 0)),
                  pl.BlockSpec((2, tps, LANES, S5_STATE), lambda j: (0, j, 0, 0)),
                  pl.BlockSpec((tps, LANES, S5_STATE), lambda j: (j, 0, 0)),
                  pl.BlockSpec((tps, LANES, S5_STATE), lambda j: (j, 0, 0))],
        out_specs=(pl.BlockSpec((tps, 2 * LANES, 2 * LANES + 2 * sp), lambda j: (j, 0, 0)),
                   pl.BlockSpec((tps, 2 * sp, 2 * LANES), lambda j: (j, 0, 0)),
                   pl.BlockSpec((tps, SUBLANES, sp), lambda j: (j, 0, 0)),
                   pl.BlockSpec((tps, SUBLANES, sp), lambda j: (j, 0, 0))),
        compiler_params=pltpu.CompilerParams(dimension_semantics=("parallel",)),
        name="s5_params",
    )(abl, bt, c_tile(c_re), c_tile(c_im))


def _s5_kernel(x_hbm, g_ref, w1_ref, l2r_ref, l2i_ref, cx_ref, d_ref, wglu_ref, bglu_ref,
               o_hbm, xin_s, hn_s, hnu_s, zy_s, bu_s, xs_s, y_s, st_s, res_s, sem_in, sem_out,
               *, tl, n_steps):
    half = tl // 2
    prow = half * SUBLANES
    sp = STATES_PER_TILE
    i = pl.program_id(0)
    slot = i % 2

    def in_copies(step, sl):
        return [pltpu.make_async_copy(x_hbm.at[b, pl.ds(step * tl, tl), :],
                                      xin_s.at[sl, :, b, :], sem_in.at[sl, b])
                for b in range(SUBLANES)]

    def out_copies(step, sl):
        return [pltpu.make_async_copy(res_s.at[sl, :, b, :],
                                      o_hbm.at[b, pl.ds(step * tl, tl), :], sem_out.at[sl, b])
                for b in range(SUBLANES)]

    @pl.when(i == 0)
    def _():
        st_s[...] = jnp.zeros_like(st_s)
        for copy in in_copies(0, 0):
            copy.start()

    @pl.when(i + 1 < n_steps)
    def _():
        for copy in in_copies(i + 1, 1 - slot):
            copy.start()

    for copy in in_copies(i, slot):
        copy.wait()

    @pl.when(i >= 2)
    def _():
        for copy in out_copies(i - 2, slot):
            copy.wait()

    def row_group(v, t):
        return v[t * SUBLANES:(t + 1) * SUBLANES]

    hn = _rmsnorm(xin_s[slot].reshape(tl * SUBLANES, D_MODEL), g_ref[...])
    for t in range(tl):
        g = (t % 2) * half + t // 2
        hn_s[g * SUBLANES:(g + 1) * SUBLANES, :] = row_group(hn, t)
    for m in range(tl // 4):
        r16 = slice(2 * m * SUBLANES, (2 * m + 2) * SUBLANES)
        even = jnp.concatenate([row_group(hn, 4 * m), row_group(hn, 4 * m + 2)], axis=0).astype(BF16)
        odd = jnp.concatenate([row_group(hn, 4 * m + 1), row_group(hn, 4 * m + 3)], axis=0).astype(BF16)
        for j in range(N_LANE_TILES):
            lanes = slice(j * LANES, (j + 1) * LANES)
            hnu_s[r16, 2 * j * LANES:(2 * j + 1) * LANES] = even[:, lanes]
            hnu_s[r16, (2 * j + 1) * LANES:(2 * j + 2) * LANES] = odd[:, lanes]

    def project_in(j):
        z = jnp.dot(hnu_s[:, 2 * j * LANES:(2 * j + 2) * LANES], w1_ref[j],
                    preferred_element_type=F32)
        zy_s[j % S5_TILES_IN_FLIGHT] = z[:, :2 * LANES]
        bu_s[j % S5_TILES_IN_FLIGHT] = z[:, 2 * LANES:]

    ahead = S5_TILES_IN_FLIGHT - 1
    for j in range(ahead):
        project_in(j)
    for j in range(N_LANE_TILES):
        if j + ahead < N_LANE_TILES:
            project_in(j + ahead)
        p = j % S5_TILES_IN_FLIGHT
        lanes = slice(j * LANES, (j + 1) * LANES)
        ar = l2r_ref[j]
        ai = l2i_ref[j]
        xr = st_s[j, :, 0:sp]
        xi = st_s[j, :, sp:2 * sp]
        for k in range(half):
            r = slice(k * SUBLANES, (k + 1) * SUBLANES)
            xs_s[p, r, 0:sp] = xr
            xs_s[p, r, sp:2 * sp] = xi
            nxr = ar * xr - ai * xi + bu_s[p, r, 0:sp]
            nxi = ar * xi + ai * xr + bu_s[p, r, sp:2 * sp]
            xr, xi = nxr, nxi
        st_s[j, :, 0:sp] = xr
        st_s[j, :, sp:2 * sp] = xi
        y = zy_s[p] + jnp.dot(xs_s[p].astype(BF16), cx_ref[j], preferred_element_type=F32)
        y_s[0:prow, lanes] = y[:, :LANES]
        y_s[prow:2 * prow, lanes] = y[:, LANES:]

    for par in range(2):
        rs = slice(par * prow, (par + 1) * prow)
        y = y_s[rs, :] + d_ref[...] * hn_s[rs, :]
        z = jnp.dot(_gelu_tanh(y).astype(BF16), wglu_ref[...].astype(BF16),
                    preferred_element_type=F32) + bglu_ref[...]
        mix = z[:, :D_MODEL] * (1.0 / (1.0 + jnp.exp(-z[:, D_MODEL:])))
        for k in range(half):
            t = 2 * k + par
            res_s[slot, t] = xin_s[slot, t] + row_group(mix, k)

    for copy in out_copies(i, slot):
        copy.start()

    @pl.when(i == n_steps - 1)
    def _():
        for copy in out_copies(i - 1, 1 - slot) + out_copies(i, slot):
            copy.wait()


def _s5_layer(x, g_mix, w1, lam2_re_t, lam2_im_t, cx, d_skip, w_glu, b_glu):
    bsz, seq, d = x.shape
    tl = S5_TL
    n_steps = seq // tl
    assert bsz == SUBLANES and n_steps >= 2
    rows = tl * bsz
    prow = rows // 2
    kernel = functools.partial(_s5_kernel, tl=tl, n_steps=n_steps)
    hbm = pl.BlockSpec(memory_space=pl.ANY)
    return pl.pallas_call(
        kernel,
        out_shape=jax.ShapeDtypeStruct((bsz, seq, d), F32),
        grid=(n_steps,),
        in_specs=[
            hbm,
            _const_spec((1, d)),
            _const_spec(w1.shape),
            _const_spec(lam2_re_t.shape),
            _const_spec(lam2_im_t.shape),
            _const_spec(cx.shape),
            _const_spec((1, d)),
            _const_spec(w_glu.shape),
            _const_spec((1, 2 * d)),
        ],
        out_specs=hbm,
        scratch_shapes=[
            pltpu.VMEM((2, tl, bsz, d), F32),
            pltpu.VMEM((rows, d), F32),
            pltpu.VMEM((prow, 2 * d), BF16),
            pltpu.VMEM((S5_TILES_IN_FLIGHT, prow, 2 * LANES), F32),
            pltpu.VMEM((S5_TILES_IN_FLIGHT, prow, 2 * STATES_PER_TILE), F32),
            pltpu.VMEM((S5_TILES_IN_FLIGHT, prow, 2 * STATES_PER_TILE), F32),
            pltpu.VMEM((rows, d), F32),
            pltpu.VMEM((N_LANE_TILES, SUBLANES, 2 * STATES_PER_TILE), F32),
            pltpu.VMEM((2, tl, bsz, d), F32),
            pltpu.SemaphoreType.DMA((2, SUBLANES)),
            pltpu.SemaphoreType.DMA((2, SUBLANES)),
        ],
        compiler_params=pltpu.CompilerParams(
            dimension_semantics=("arbitrary",), vmem_limit_bytes=VMEM_LIMIT_BYTES),
        name="s5_layer",
    )(x, g_mix.reshape(1, d), w1, lam2_re_t, lam2_im_t, cx, d_skip.reshape(1, d), w_glu,
      b_glu.reshape(1, 2 * d))


def _inv_rms(x):
    return lax.rsqrt(jnp.mean(x * x, axis=-1, keepdims=True) + NORM_EPS)


def _mlp_kernel(h_ref, g_ref, win_hbm, wout_hbm, *rest, layer, with_kv):
    if with_kv:
        gkv_ref, wkv_ref, bkv_ref, o_ref, kv_ref, win_s, wout_s, stage_in, stage_out, sem = rest
    else:
        gfin_ref, o_ref, win_s, wout_s, stage_in, stage_out, sem = rest
    n_chunks = D_FF // MLP_FF_CHUNK
    n_copies = D_FF // MLP_DMA_CHUNK
    copies_per_chunk = MLP_FF_CHUNK // MLP_DMA_CHUNK

    def weight_copies(k):
        cols = pl.ds(k * MLP_DMA_CHUNK, MLP_DMA_CHUNK)
        return (pltpu.make_async_copy(win_hbm.at[layer, :, cols], stage_in.at[k % 2], sem.at[0, k % 2]),
                pltpu.make_async_copy(wout_hbm.at[layer, cols, :], stage_out.at[k % 2], sem.at[1, k % 2]))

    def land(k):
        cols = slice(k * MLP_DMA_CHUNK, (k + 1) * MLP_DMA_CHUNK)
        copy_in, copy_out = weight_copies(k)
        copy_in.wait()
        win_s[:, cols] = stage_in[k % 2].astype(BF16)
        copy_out.wait()
        wout_s[cols, :] = stage_out[k % 2].astype(BF16)
        if k + 2 < n_copies:
            for copy in weight_copies(k + 2):
                copy.start()

    def body(first_step):
        h = h_ref[...]
        hg = (h * g_ref[...]).astype(BF16)
        r = _inv_rms(h)
        acts = []
        for c in range(n_chunks):
            cols = slice(c * MLP_FF_CHUNK, (c + 1) * MLP_FF_CHUNK)
            if first_step:
                for k in range(c * copies_per_chunk, (c + 1) * copies_per_chunk):
                    land(k)
            a = jnp.dot(hg, win_s[:, cols], preferred_element_type=F32)
            acts.append(jnp.square(jnp.maximum(a, 0.0)).astype(BF16))
        acc = jnp.dot(jnp.concatenate(acts, axis=1), wout_s[...], preferred_element_type=F32)
        out = h + (r * r) * acc
        if with_kv:
            o_ref[...] = out
            kv = jnp.dot((out * gkv_ref[...]).astype(BF16), wkv_ref[...].astype(BF16),
                         preferred_element_type=F32)
            kv_ref[...] = (_inv_rms(out) * kv + bkv_ref[...]).astype(BF16)
        else:
            o_ref[...] = out * _inv_rms(out) * gfin_ref[...]

    @pl.when(pl.program_id(0) == 0)
    def _():
        for k in range(2):
            for copy in weight_copies(k):
                copy.start()
        body(True)

    @pl.when(pl.program_id(0) > 0)
    def _():
        body(False)


def _mlp(h, g, w_in, w_out, layer, *, g_kv=None, w_kv=None, b_kv=None, g_fin=None):
    t, d = h.shape
    with_kv = w_kv is not None
    row = lambda n: pl.BlockSpec((MLP_TM, n), lambda i: (i, 0))
    hbm = pl.BlockSpec(memory_space=pl.ANY)
    in_specs = [row(d), _const_spec((1, d)), hbm, hbm]
    operands = [h, g.reshape(1, d), w_in, w_out]
    if with_kv:
        nkv = w_kv.shape[1]
        in_specs += [_const_spec((1, d)), _const_spec(w_kv.shape), _const_spec((1, nkv))]
        operands += [g_kv.reshape(1, d), w_kv, b_kv.reshape(1, nkv)]
        out_shape = (jax.ShapeDtypeStruct((t, d), F32), jax.ShapeDtypeStruct((t, nkv), BF16))
        out_specs = (row(d), row(nkv))
    else:
        in_specs += [_const_spec((1, d))]
        operands += [g_fin.reshape(1, d)]
        out_shape = jax.ShapeDtypeStruct((t, d), F32)
        out_specs = row(d)
    return pl.pallas_call(
        functools.partial(_mlp_kernel, layer=layer, with_kv=with_kv),
        out_shape=out_shape,
        grid=(t // MLP_TM,),
        in_specs=in_specs,
        out_specs=out_specs,
        scratch_shapes=[pltpu.VMEM(w_in.shape[1:], BF16),
                        pltpu.VMEM(w_out.shape[1:], BF16),
                        pltpu.VMEM((2, d, MLP_DMA_CHUNK), F32),
                        pltpu.VMEM((2, MLP_DMA_CHUNK, d), F32),
                        pltpu.SemaphoreType.DMA((2, 2))],
        compiler_params=pltpu.CompilerParams(
            dimension_semantics=("arbitrary",), vmem_limit_bytes=VMEM_LIMIT_BYTES),
        name="mlp_kv" if with_kv else "mlp_final",
    )(*operands)


def _pair_tile_sources(a, g):
    head_lo = (2 * a) * Q_PER_KV + g
    head_hi = (2 * a + 1) * Q_PER_KV + g
    return (head_lo // 2, head_lo % 2), (head_hi // 2, head_hi % 2)


def _attn_kernel(sink_ref, h_ref, kvp_ref, kvc_ref, g_ref, wq_ref, bq_ref, wo_ref, bo_ref,
                 o_ref, wq_s, bq_s, wo_s, q_s, oh_s, *, tq):
    nsb = tq // WINDOW
    n = pl.program_id(1)
    n_kv_tiles = N_KV_HEADS // 2
    lo_row = lax.broadcasted_iota(jnp.int32, (1, LANES), 1) < HEAD_DIM

    @pl.when((pl.program_id(0) == 0) & (n == 0))
    def _():
        for a in range(n_kv_tiles):
            for g in range(Q_PER_KV):
                (t_lo, h_lo), (t_hi, h_hi) = _pair_tile_sources(a, g)
                dst = slice((a * Q_PER_KV + g) * LANES, (a * Q_PER_KV + g + 1) * LANES)

                def pair(ref):
                    src_lo = ref[:, t_lo * LANES:(t_lo + 1) * LANES]
                    src_hi = ref[:, t_hi * LANES:(t_hi + 1) * LANES]
                    if h_lo == 1:
                        src_lo = pltpu.roll(src_lo, HEAD_DIM, axis=1)
                    if h_hi == 0:
                        src_hi = pltpu.roll(src_hi, HEAD_DIM, axis=1)
                    return jnp.where(lo_row, src_lo, src_hi)

                wq_s[:, dst] = pair(wq_ref).astype(BF16)
                bq_s[:, dst] = pair(bq_ref)
                for half in range(2):
                    head = (2 * a + half) * Q_PER_KV + g
                    r0 = (a * Q_PER_KV + g) * LANES + half * HEAD_DIM
                    wo_s[r0:r0 + HEAD_DIM, :] = (
                        wo_ref[head * HEAD_DIM:(head + 1) * HEAD_DIM, :].astype(BF16))

    h = h_ref[0]
    hn = _rmsnorm(h, g_ref[...]).astype(BF16)
    q = jnp.dot(hn, wq_s[...], preferred_element_type=F32) + bq_s[...]
    q_s[...] = (q * (LOG2_E / math.sqrt(HEAD_DIM))).astype(BF16)

    lo = lax.broadcasted_iota(jnp.int32, (WINDOW, LANES), 1) < HEAD_DIM
    own = (lax.broadcasted_iota(jnp.int32, (WINDOW, WINDOW), 1)
           <= lax.broadcasted_iota(jnp.int32, (WINDOW, WINDOW), 0))
    zero = jnp.zeros((WINDOW, LANES), BF16)
    fzero = jnp.zeros((WINDOW, WINDOW), F32)

    for sb in range(nsb):
        rows = slice(sb * WINDOW, (sb + 1) * WINDOW)
        prev_bias = jnp.where(n == 0, -jnp.inf, 0.0).astype(F32) if sb == 0 else None
        for a in range(n_kv_tiles):
            kl = slice(a * LANES, (a + 1) * LANES)
            vl = slice((n_kv_tiles + a) * LANES, (n_kv_tiles + a + 1) * LANES)
            if sb == 0:
                kprev, vprev = kvp_ref[0, :, kl], kvp_ref[0, :, vl]
            else:
                prow = slice((sb - 1) * WINDOW, sb * WINDOW)
                kprev, vprev = kvc_ref[0, prow, kl], kvc_ref[0, prow, vl]
            kd = jnp.concatenate([kprev, kvc_ref[0, rows, kl]], axis=0)
            vd = jnp.concatenate([vprev, kvc_ref[0, rows, vl]], axis=0)
            vd1 = jnp.concatenate([vd, jnp.ones((2 * WINDOW, LANES), BF16)], axis=1)
            qt = [q_s[rows, (a * Q_PER_KV + g) * LANES:(a * Q_PER_KV + g + 1) * LANES]
                  for g in range(Q_PER_KV)]
            lhs = jnp.concatenate([jnp.where(lo, t, zero) for t in qt]
                                  + [jnp.where(lo, zero, t) for t in qt], axis=0)
            s = lax.dot_general(lhs, kd, (((1,), (1,)), ((), ())),
                                preferred_element_type=F32)
            ps, sink_terms = [], []
            for half in range(2):
                for g in range(Q_PER_KV):
                    blk = half * Q_PER_KV + g
                    s_prev = s[blk * WINDOW:(blk + 1) * WINDOW, :WINDOW]
                    s_own = s[blk * WINDOW:(blk + 1) * WINDOW, WINDOW:]
                    if prev_bias is not None:
                        s_prev = s_prev + prev_bias
                    sg = jnp.where(own, s_own, s_prev)
                    sink = sink_ref[(2 * a + half) * Q_PER_KV + g] * LOG2_E
                    m = jnp.maximum(jnp.max(sg, axis=-1, keepdims=True), sink)
                    p = jnp.exp2(sg - m)
                    ps.append(jnp.concatenate([jnp.where(own, fzero, p).astype(BF16),
                                               jnp.where(own, p, fzero).astype(BF16)], axis=1))
                    sink_terms.append(jnp.exp2(sink - m))
            od = jnp.dot(jnp.concatenate(ps, axis=0), vd1, preferred_element_type=F32)
            for g in range(Q_PER_KV):
                r_lo = slice(g * WINDOW, (g + 1) * WINDOW)
                r_hi = slice((Q_PER_KV + g) * WINDOW, (Q_PER_KV + g + 1) * WINDOW)
                num = jnp.where(lo, od[r_lo, :LANES], od[r_hi, :LANES])
                den = (jnp.where(lo, od[r_lo, LANES:], od[r_hi, LANES:])
                       + jnp.where(lo, sink_terms[g], sink_terms[Q_PER_KV + g]))
                oh_s[rows, (a * Q_PER_KV + g) * LANES:(a * Q_PER_KV + g + 1) * LANES] = (
                    (num * (1.0 / den)).astype(BF16))

    o_ref[0] = h + jnp.dot(oh_s[...], wo_s[...], preferred_element_type=F32) + bo_ref[...]


def _attn_layer(h, kv, sinks, g, w_q, b_q, w_o, b_o):
    bsz, seq, d = h.shape
    nkv = kv.shape[-1]
    tq = ATTN_TQ
    nsb = tq // WINDOW
    kernel = functools.partial(_attn_kernel, tq=tq)
    return pl.pallas_call(
        kernel,
        out_shape=jax.ShapeDtypeStruct((bsz, seq, d), F32),
        grid=(bsz, seq // tq),
        in_specs=[
            pl.BlockSpec(memory_space=pltpu.SMEM),
            pl.BlockSpec((1, tq, d), lambda b, n: (b, n, 0)),
            pl.BlockSpec((1, WINDOW, nkv), lambda b, n: (b, jnp.maximum(n * nsb - 1, 0), 0)),
            pl.BlockSpec((1, tq, nkv), lambda b, n: (b, n, 0)),
            _const_spec((1, d)),
            _const_spec(w_q.shape),
            _const_spec((1, d)),
            _const_spec(w_o.shape),
            _const_spec((1, d)),
        ],
        out_specs=pl.BlockSpec((1, tq, d), lambda b, n: (b, n, 0)),
        scratch_shapes=[pltpu.VMEM((d, d), BF16),
                        pltpu.VMEM((1, d), F32),
                        pltpu.VMEM((d, d), BF16),
                        pltpu.VMEM((tq, d), BF16),
                        pltpu.VMEM((tq, d), BF16)],
        compiler_params=pltpu.CompilerParams(
            dimension_semantics=("arbitrary", "arbitrary"), vmem_limit_bytes=VMEM_LIMIT_BYTES),
        name="attn",
    )(sinks, h, kv, kv, g.reshape(1, d), w_q, b_q.reshape(1, d), w_o, b_o.reshape(1, d))


def kernel(x, norm_mix, norm_mlp, norm_kv, norm_final, s5_a_re, s5_a_im, s5_log_dt, s5_b_re, s5_b_im, s5_c_re, s5_c_im, s5_d, s5_w_glu, s5_b_glu, w_kv, b_kv, w_q, b_q, sinks, w_o, b_o, w_mlp_in, w_mlp_out):
    bsz, seq, d = x.shape

    w1, cx, lam2_re_t, lam2_im_t = _s5_params(s5_a_re[0], s5_a_im[0], s5_log_dt[0], s5_b_re[0],
                                              s5_b_im[0], s5_c_re[0], s5_c_im[0])
    h = _s5_layer(x, norm_mix[0], w1, lam2_re_t, lam2_im_t, cx, s5_d[0], s5_w_glu[0], s5_b_glu[0])

    h, kv = _mlp(h.reshape(bsz * seq, d), norm_mlp[0], w_mlp_in, w_mlp_out, 0,
                 g_kv=norm_kv, w_kv=w_kv, b_kv=b_kv)

    h = _attn_layer(h.reshape(bsz, seq, d), kv.reshape(bsz, seq, -1), sinks[0], norm_mix[1],
                    w_q[0], b_q[0], w_o[0], b_o[0])

    out = _mlp(h.reshape(bsz * seq, d), norm_mlp[1], w_mlp_in, w_mlp_out, 1, g_fin=norm_final)
    return out.reshape(bsz, seq, d)
```

```python
import functools
import math

import jax
import jax.numpy as jnp
from jax import lax
from jax.experimental import pallas as pl
from jax.experimental.pallas import tpu as pltpu

F32 = jnp.float32
BF16 = jnp.bfloat16

D_MODEL = 1024
S5_GROUP = 16
S5_STATE = 64
LAMBDA_RE_MAX = -1e-4
HEAD_DIM = 64
N_Q_HEADS = D_MODEL // HEAD_DIM
N_KV_HEADS = 4
Q_PER_KV = N_Q_HEADS // N_KV_HEADS
WINDOW = 128
D_FF = 4 * D_MODEL
NORM_EPS = 1e-5
LOG2_E = math.log2(math.e)

LANES = 128
SUBLANES = 8
N_LANE_TILES = D_MODEL // LANES
GROUPS_PER_TILE = LANES // S5_GROUP
STATES_PER_TILE = GROUPS_PER_TILE * S5_STATE
MIB = 1024 * 1024
VMEM_LIMIT_BYTES = 56 * MIB

S5_TL = 64
S5_PARAM_TILES_PER_STEP = 4
S5_TILES_IN_FLIGHT = 2
MLP_TM = 1024
MLP_FF_CHUNK = 1024
MLP_DMA_CHUNK = 256
ATTN_TQ = 1024


def _rmsnorm(x, g):
    return x * lax.rsqrt(jnp.mean(x * x, axis=-1, keepdims=True) + NORM_EPS) * g


def _gelu_tanh(x):
    c = math.sqrt(2.0 / math.pi)
    return 0.5 * x * (1.0 + jnp.tanh(c * (x + 0.044715 * (x * x * x))))


def _const_spec(shape):
    nd = len(shape)
    return pl.BlockSpec(shape, lambda *_: (0,) * nd, pipeline_mode=pl.Buffered(1))


def _s5_params_kernel(abl_ref, bt_ref, cre_ref, cim_ref, w1_ref, cx_ref, l2r_ref, l2i_ref):
    for tile in range(S5_PARAM_TILES_PER_STEP):
        _s5_params_tile(abl_ref.at[:, tile], bt_ref.at[:, tile], cre_ref.at[tile], cim_ref.at[tile],
                        w1_ref.at[tile], cx_ref.at[tile], l2r_ref.at[tile], l2i_ref.at[tile])


def _s5_params_tile(abl_ref, bt_ref, cre_ref, cim_ref, w1_ref, cx_ref, l2r_ref, l2i_ref):
    gpt, grp, nst = GROUPS_PER_TILE, S5_GROUP, S5_STATE
    ar = jnp.minimum(abl_ref[0], LAMBDA_RE_MAX)
    ai = abl_ref[1]
    dt = jnp.exp(abl_ref[2])
    mag = jnp.exp(ar * dt)
    ang = ai * dt
    lr = mag * jnp.cos(ang)
    li = mag * jnp.sin(ang)
    den = ar * ar + ai * ai
    cr = ((lr - 1.0) * ar + li * ai) / den
    ci = (li * ar - (lr - 1.0) * ai) / den
    l2r = lr * lr - li * li
    l2i = 2.0 * (lr * li)

    def per_row(v):
        return jnp.concatenate([jnp.broadcast_to(v[g:g + 1], (grp, nst)) for g in range(gpt)], axis=0)

    row_g = lax.broadcasted_iota(jnp.int32, (LANES, STATES_PER_TILE), 0) // grp
    col_g = lax.broadcasted_iota(jnp.int32, (LANES, STATES_PER_TILE), 1) // nst
    own = row_g == col_g

    def blockdiag(v):
        return jnp.where(own, jnp.concatenate([v] * gpt, axis=1), 0.0)

    lr_c, li_c, cr_c, ci_c = per_row(lr), per_row(li), per_row(cr), per_row(ci)
    l2r_c, l2i_c = per_row(l2r), per_row(l2i)
    btr, bti = bt_ref[0], bt_ref[1]
    bbr = cr_c * btr - ci_c * bti
    bbi = cr_c * bti + ci_c * btr
    wbr, wbi = blockdiag(bbr), blockdiag(bbi)
    w1r = blockdiag(lr_c * bbr - li_c * bbi)
    w1i = blockdiag(lr_c * bbi + li_c * bbr)
    ccr, cci = cre_ref[...], cim_ref[...]
    wcr, wci = blockdiag(ccr), blockdiag(cci)

    def hdot_nt(a, b):
        return lax.dot_general(a, b, (((1,), (1,)), ((), ())),
                               precision=lax.Precision.HIGHEST, preferred_element_type=F32)

    k0 = hdot_nt(wbr, wcr) - hdot_nt(wbi, wci)
    k1 = hdot_nt(w1r, wcr) - hdot_nt(w1i, wci)
    top = jnp.concatenate([k0, k1, w1r, w1i], axis=1)
    bot = jnp.concatenate([jnp.zeros_like(k0), k0, wbr, wbi], axis=1)
    w1_ref[...] = jnp.concatenate([top, bot], axis=0).astype(BF16)

    def readout_t(pr, pi):
        return jnp.concatenate([blockdiag(pr * ccr - pi * cci),
                                -blockdiag(pi * ccr + pr * cci)], axis=1)

    cx_t = jnp.concatenate([readout_t(lr_c, li_c), readout_t(l2r_c, l2i_c)], axis=0)
    cx_ref[...] = cx_t.T.astype(BF16)

    def flat_row(v):
        row = jnp.concatenate([v[g:g + 1] for g in range(gpt)], axis=1)
        return jnp.broadcast_to(row, (SUBLANES, STATES_PER_TILE))

    l2r_ref[...] = flat_row(l2r)
    l2i_ref[...] = flat_row(l2i)


def _s5_params(a_re, a_im, log_dt, b_re, b_im, c_re, c_im):
    nt, gpt, sp = N_LANE_TILES, GROUPS_PER_TILE, STATES_PER_TILE
    tps = S5_PARAM_TILES_PER_STEP
    abl = jnp.stack([a_re, a_im, jnp.broadcast_to(log_dt[:, None], a_re.shape)])
    abl = abl.reshape(3, nt, gpt, S5_STATE)
    bt = jnp.swapaxes(jnp.stack([b_re, b_im]), -1, -2).reshape(2, nt, LANES, S5_STATE)
    c_tile = lambda c: c.reshape(nt, LANES, S5_STATE)
    return pl.pallas_call(
        _s5_params_kernel,
        out_shape=(jax.ShapeDtypeStruct((nt, 2 * LANES, 2 * LANES + 2 * sp), BF16),
                   jax.ShapeDtypeStruct((nt, 2 * sp, 2 * LANES), BF16),
                   jax.ShapeDtypeStruct((nt, SUBLANES, sp), F32),
                   jax.ShapeDtypeStruct((nt, SUBLANES, sp), F32)),
        grid=(nt // tps,),
        in_specs=[pl.BlockSpec((3, tps, gpt, S5_STATE), lambda j: (0, j, 0, 0)),
                  pl.BlockSpec((2, tps, LANES, S5_STATE), lambda j: (0, j, 0, 0)),
                  pl.BlockSpec((tps, LANES, S5_STATE), lambda j: (j, 0, 0)),
                  pl.BlockSpec((tps, LANES, S5_STATE), lambda j: (j, 0, 0))],
        out_specs=(pl.BlockSpec((tps, 2 * LANES, 2 * LANES + 2 * sp), lambda j: (j, 0, 0)),
                   pl.BlockSpec((tps, 2 * sp, 2 * LANES), lambda j: (j, 0, 0)),
                   pl.BlockSpec((tps, SUBLANES, sp), lambda j: (j, 0, 0)),
                   pl.BlockSpec((tps, SUBLANES, sp), lambda j: (j, 0, 0))),
        compiler_params=pltpu.CompilerParams(dimension_semantics=("parallel",)),
        name="s5_params",
    )(abl, bt, c_tile(c_re), c_tile(c_im))


def _s5_kernel(x_hbm, g_ref, w1_ref, l2r_ref, l2i_ref, cx_ref, d_ref, wglu_ref, bglu_ref,
               o_hbm, xin_s, hn_s, hnu_s, zy_s, bu_s, xs_s, y_s, st_s, res_s, sem_in, sem_out,
               *, tl, n_steps):
    half = tl // 2
    prow = half * SUBLANES
    sp = STATES_PER_TILE
    i = pl.program_id(0)
    slot = i % 2

    def in_copies(step, sl):
        return [pltpu.make_async_copy(x_hbm.at[b, pl.ds(step * tl, tl), :],
                                      xin_s.at[sl, :, b, :], sem_in.at[sl, b])
                for b in range(SUBLANES)]

    def out_copies(step, sl):
        return [pltpu.make_async_copy(res_s.at[sl, :, b, :],
                                      o_hbm.at[b, pl.ds(step * tl, tl), :], sem_out.at[sl, b])
                for b in range(SUBLANES)]

    @pl.when(i == 0)
    def _():
        st_s[...] = jnp.zeros_like(st_s)
        for copy in in_copies(0, 0):
            copy.start()

    @pl.when(i + 1 < n_steps)
    def _():
        for copy in in_copies(i + 1, 1 - slot):
            copy.start()

    for copy in in_copies(i, slot):
        copy.wait()

    @pl.when(i >= 2)
    def _():
        for copy in out_copies(i - 2, slot):
            copy.wait()

    def row_group(v, t):
        return v[t * SUBLANES:(t + 1) * SUBLANES]

    hn = _rmsnorm(xin_s[slot].reshape(tl * SUBLANES, D_MODEL), g_ref[...])
    for t in range(tl):
        g = (t % 2) * half + t // 2
        hn_s[g * SUBLANES:(g + 1) * SUBLANES, :] = row_group(hn, t)
    for m in range(tl // 4):
        r16 = slice(2 * m * SUBLANES, (2 * m + 2) * SUBLANES)
        even = jnp.concatenate([row_group(hn, 4 * m), row_group(hn, 4 * m + 2)], axis=0).astype(BF16)
        odd = jnp.concatenate([row_group(hn, 4 * m + 1), row_group(hn, 4 * m + 3)], axis=0).astype(BF16)
        for j in range(N_LANE_TILES):
            lanes = slice(j * LANES, (j + 1) * LANES)
            hnu_s[r16, 2 * j * LANES:(2 * j + 1) * LANES] = even[:, lanes]
            hnu_s[r16, (2 * j + 1) * LANES:(2 * j + 2) * LANES] = odd[:, lanes]

    def project_in(j):
        z = jnp.dot(hnu_s[:, 2 * j * LANES:(2 * j + 2) * LANES], w1_ref[j],
                    preferred_element_type=F32)
        zy_s[j % S5_TILES_IN_FLIGHT] = z[:, :2 * LANES]
        bu_s[j % S5_TILES_IN_FLIGHT] = z[:, 2 * LANES:]

    ahead = S5_TILES_IN_FLIGHT - 1
    for j in range(ahead):
        project_in(j)
    for j in range(N_LANE_TILES):
        if j + ahead < N_LANE_TILES:
            project_in(j + ahead)
        p = j % S5_TILES_IN_FLIGHT
        lanes = slice(j * LANES, (j + 1) * LANES)
        ar = l2r_ref[j]
        ai = l2i_ref[j]
        xr = st_s[j, :, 0:sp]
        xi = st_s[j, :, sp:2 * sp]
        for k in range(half):
            r = slice(k * SUBLANES, (k + 1) * SUBLANES)
            xs_s[p, r, 0:sp] = xr
            xs_s[p, r, sp:2 * sp] = xi
            nxr = ar * xr - ai * xi + bu_s[p, r, 0:sp]
            nxi = ar * xi + ai * xr + bu_s[p, r, sp:2 * sp]
            xr, xi = nxr, nxi
        st_s[j, :, 0:sp] = xr
        st_s[j, :, sp:2 * sp] = xi
        y = zy_s[p] + jnp.dot(xs_s[p].astype(BF16), cx_ref[j], preferred_element_type=F32)
        y_s[0:prow, lanes] = y[:, :LANES]
        y_s[prow:2 * prow, lanes] = y[:, LANES:]

    for par in range(2):
        rs = slice(par * prow, (par + 1) * prow)
        y = y_s[rs, :] + d_ref[...] * hn_s[rs, :]
        z = jnp.dot(_gelu_tanh(y).astype(BF16), wglu_ref[...].astype(BF16),
                    preferred_element_type=F32) + bglu_ref[...]
        mix = z[:, :D_MODEL] * (1.0 / (1.0 + jnp.exp(-z[:, D_MODEL:])))
        for k in range(half):
            t = 2 * k + par
            res_s[slot, t] = xin_s[slot, t] + row_group(mix, k)

    for copy in out_copies(i, slot):
        copy.start()

    @pl.when(i == n_steps - 1)
    def _():
        for copy in out_copies(i - 1, 1 - slot) + out_copies(i, slot):
            copy.wait()


def _s5_layer(x, g_mix, w1, lam2_re_t, lam2_im_t, cx, d_skip, w_glu, b_glu):
    bsz, seq, d = x.shape
    tl = S5_TL
    n_steps = seq // tl
    assert bsz == SUBLANES and n_steps >= 2
    rows = tl * bsz
    prow = rows // 2
    kernel = functools.partial(_s5_kernel, tl=tl, n_steps=n_steps)
    hbm = pl.BlockSpec(memory_space=pl.ANY)
    return pl.pallas_call(
        kernel,
        out_shape=jax.ShapeDtypeStruct((bsz, seq, d), F32),
        grid=(n_steps,),
        in_specs=[
            hbm,
            _const_spec((1, d)),
            _const_spec(w1.shape),
            _const_spec(lam2_re_t.shape),
            _const_spec(lam2_im_t.shape),
            _const_spec(cx.shape),
            _const_spec((1, d)),
            _const_spec(w_glu.shape),
            _const_spec((1, 2 * d)),
        ],
        out_specs=hbm,
        scratch_shapes=[
            pltpu.VMEM((2, tl, bsz, d), F32),
            pltpu.VMEM((rows, d), F32),
            pltpu.VMEM((prow, 2 * d), BF16),
            pltpu.VMEM((S5_TILES_IN_FLIGHT, prow, 2 * LANES), F32),
            pltpu.VMEM((S5_TILES_IN_FLIGHT, prow, 2 * STATES_PER_TILE), F32),
            pltpu.VMEM((S5_TILES_IN_FLIGHT, prow, 2 * STATES_PER_TILE), F32),
            pltpu.VMEM((rows, d), F32),
            pltpu.VMEM((N_LANE_TILES, SUBLANES, 2 * STATES_PER_TILE), F32),
            pltpu.VMEM((2, tl, bsz, d), F32),
            pltpu.SemaphoreType.DMA((2, SUBLANES)),
            pltpu.SemaphoreType.DMA((2, SUBLANES)),
        ],
        compiler_params=pltpu.CompilerParams(
            dimension_semantics=("arbitrary",), vmem_limit_bytes=VMEM_LIMIT_BYTES),
        name="s5_layer",
    )(x, g_mix.reshape(1, d), w1, lam2_re_t, lam2_im_t, cx, d_skip.reshape(1, d), w_glu,
      b_glu.reshape(1, 2 * d))


def _inv_rms(x):
    return lax.rsqrt(jnp.mean(x * x, axis=-1, keepdims=True) + NORM_EPS)


def _mlp_kernel(h_ref, g_ref, win_hbm, wout_hbm, *rest, layer, with_kv):
    if with_kv:
        gkv_ref, wkv_ref, bkv_ref, o_ref, kv_ref, win_s, wout_s, stage_in, stage_out, sem = rest
    else:
        gfin_ref, o_ref, win_s, wout_s, stage_in, stage_out, sem = rest
    n_chunks = D_FF // MLP_FF_CHUNK
    n_copies = D_FF // MLP_DMA_CHUNK
    copies_per_chunk = MLP_FF_CHUNK // MLP_DMA_CHUNK

    def weight_copies(k):
        cols = pl.ds(k * MLP_DMA_CHUNK, MLP_DMA_CHUNK)
        return (pltpu.make_async_copy(win_hbm.at[layer, :, cols], stage_in.at[k % 2], sem.at[0, k % 2]),
                pltpu.make_async_copy(wout_hbm.at[layer, cols, :], stage_out.at[k % 2], sem.at[1, k % 2]))

    def land(k):
        cols = slice(k * MLP_DMA_CHUNK, (k + 1) * MLP_DMA_CHUNK)
        copy_in, copy_out = weight_copies(k)
        copy_in.wait()
        win_s[:, cols] = stage_in[k % 2].astype(BF16)
        copy_out.wait()
        wout_s[cols, :] = stage_out[k % 2].astype(BF16)
        if k + 2 < n_copies:
            for copy in weight_copies(k + 2):
                copy.start()

    def body(first_step):
        h = h_ref[...]
        hg = (h * g_ref[...]).astype(BF16)
        r = _inv_rms(h)
        acts = []
        for c in range(n_chunks):
            cols = slice(c * MLP_FF_CHUNK, (c + 1) * MLP_FF_CHUNK)
            if first_step:
                for k in range(c * copies_per_chunk, (c + 1) * copies_per_chunk):
                    land(k)
            a = jnp.dot(hg, win_s[:, cols], preferred_element_type=F32)
            acts.append(jnp.square(jnp.maximum(a, 0.0)).astype(BF16))
        acc = jnp.dot(jnp.concatenate(acts, axis=1), wout_s[...], preferred_element_type=F32)
        out = h + (r * r) * acc
        if with_kv:
            o_ref[...] = out
            kv = jnp.dot((out * gkv_ref[...]).astype(BF16), wkv_ref[...].astype(BF16),
                         preferred_element_type=F32)
            kv_ref[...] = (_inv_rms(out) * kv + bkv_ref[...]).astype(BF16)
        else:
            o_ref[...] = out * _inv_rms(out) * gfin_ref[...]

    @pl.when(pl.program_id(0) == 0)
    def _():
        for k in range(2):
            for copy in weight_copies(k):
                copy.start()
        body(True)

    @pl.when(pl.program_id(0) > 0)
    def _():
        body(False)


def _mlp(h, g, w_in, w_out, layer, *, g_kv=None, w_kv=None, b_kv=None, g_fin=None):
    t, d = h.shape
    with_kv = w_kv is not None
    row = lambda n: pl.BlockSpec((MLP_TM, n), lambda i: (i, 0))
    hbm = pl.BlockSpec(memory_space=pl.ANY)
    in_specs = [row(d), _const_spec((1, d)), hbm, hbm]
    operands = [h, g.reshape(1, d), w_in, w_out]
    if with_kv:
        nkv = w_kv.shape[1]
        in_specs += [_const_spec((1, d)), _const_spec(w_kv.shape), _const_spec((1, nkv))]
        operands += [g_kv.reshape(1, d), w_kv, b_kv.reshape(1, nkv)]
        out_shape = (jax.ShapeDtypeStruct((t, d), F32), jax.ShapeDtypeStruct((t, nkv), BF16))
        out_specs = (row(d), row(nkv))
    else:
        in_specs += [_const_spec((1, d))]
        operands += [g_fin.reshape(1, d)]
        out_shape = jax.ShapeDtypeStruct((t, d), F32)
        out_specs = row(d)
    return pl.pallas_call(
        functools.partial(_mlp_kernel, layer=layer, with_kv=with_kv),
        out_shape=out_shape,
        grid=(t // MLP_TM,),
        in_specs=in_specs,
        out_specs=out_specs,
        scratch_shapes=[pltpu.VMEM(w_in.shape[1:], BF16),
                        pltpu.VMEM(w_out.shape[1:], BF16),
                        pltpu.VMEM((2, d, MLP_DMA_CHUNK), F32),
                        pltpu.VMEM((2, MLP_DMA_CHUNK, d), F32),
                        pltpu.SemaphoreType.DMA((2, 2))],
        compiler_params=pltpu.CompilerParams(
            dimension_semantics=("arbitrary",), vmem_limit_bytes=VMEM_LIMIT_BYTES),
        name="mlp_kv" if with_kv else "mlp_final",
    )(*operands)


def _pair_tile_sources(a, g):
    head_lo = (2 * a) * Q_PER_KV + g
    head_hi = (2 * a + 1) * Q_PER_KV + g
    return (head_lo // 2, head_lo % 2), (head_hi // 2, head_hi % 2)


def _attn_kernel(sink_ref, h_ref, kvp_ref, kvc_ref, g_ref, wq_ref, bq_ref, wo_ref, bo_ref,
                 o_ref, wq_s, bq_s, wo_s, q_s, oh_s, *, tq):
    nsb = tq // WINDOW
    n = pl.program_id(1)
    n_kv_tiles = N_KV_HEADS // 2
    lo_row = lax.broadcasted_iota(jnp.int32, (1, LANES), 1) < HEAD_DIM

    @pl.when((pl.program_id(0) == 0) & (n == 0))
    def _():
        for a in range(n_kv_tiles):
            for g in range(Q_PER_KV):
                (t_lo, h_lo), (t_hi, h_hi) = _pair_tile_sources(a, g)
                dst = slice((a * Q_PER_KV + g) * LANES, (a * Q_PER_KV + g + 1) * LANES)

                def pair(ref):
                    src_lo = ref[:, t_lo * LANES:(t_lo + 1) * LANES]
                    src_hi = ref[:, t_hi * LANES:(t_hi + 1) * LANES]
                    if h_lo == 1:
                        src_lo = pltpu.roll(src_lo, HEAD_DIM, axis=1)
                    if h_hi == 0:
                        src_hi = pltpu.roll(src_hi, HEAD_DIM, axis=1)
                    return jnp.where(lo_row, src_lo, src_hi)

                wq_s[:, dst] = pair(wq_ref).astype(BF16)
                bq_s[:, dst] = pair(bq_ref)
                for half in range(2):
                    head = (2 * a + half) * Q_PER_KV + g
                    r0 = (a * Q_PER_KV + g) * LANES + half * HEAD_DIM
                    wo_s[r0:r0 + HEAD_DIM, :] = (
                        wo_ref[head * HEAD_DIM:(head + 1) * HEAD_DIM, :].astype(BF16))

    h = h_ref[0]
    hn = _rmsnorm(h, g_ref[...]).astype(BF16)
    q = jnp.dot(hn, wq_s[...], preferred_element_type=F32) + bq_s[...]
    q_s[...] = (q * (LOG2_E / math.sqrt(HEAD_DIM))).astype(BF16)

    lo = lax.broadcasted_iota(jnp.int32, (WINDOW, LANES), 1) < HEAD_DIM
    own = (lax.broadcasted_iota(jnp.int32, (WINDOW, WINDOW), 1)
           <= lax.broadcasted_iota(jnp.int32, (WINDOW, WINDOW), 0))
    zero = jnp.zeros((WINDOW, LANES), BF16)
    fzero = jnp.zeros((WINDOW, WINDOW), F32)

    for sb in range(nsb):
        rows = slice(sb * WINDOW, (sb + 1) * WINDOW)
        prev_bias = jnp.where(n == 0, -jnp.inf, 0.0).astype(F32) if sb == 0 else None
        for a in range(n_kv_tiles):
            kl = slice(a * LANES, (a + 1) * LANES)
            vl = slice((n_kv_tiles + a) * LANES, (n_kv_tiles + a + 1) * LANES)
            if sb == 0:
                kprev, vprev = kvp_ref[0, :, kl], kvp_ref[0, :, vl]
            else:
                prow = slice((sb - 1) * WINDOW, sb * WINDOW)
                kprev, vprev = kvc_ref[0, prow, kl], kvc_ref[0, prow, vl]
            kd = jnp.concatenate([kprev, kvc_ref[0, rows, kl]], axis=0)
            vd = jnp.concatenate([vprev, kvc_ref[0, rows, vl]], axis=0)
            vd1 = jnp.concatenate([vd, jnp.ones((2 * WINDOW, LANES), BF16)], axis=1)
            qt = [q_s[rows, (a * Q_PER_KV + g) * LANES:(a * Q_PER_KV + g + 1) * LANES]
                  for g in range(Q_PER_KV)]
            lhs = jnp.concatenate([jnp.where(lo, t, zero) for t in qt]
                                  + [jnp.where(lo, zero, t) for t in qt], axis=0)
            s = lax.dot_general(lhs, kd, (((1,), (1,)), ((), ())),
                                preferred_element_type=F32)
            ps, sink_terms = [], []
            for half in range(2):
                for g in range(Q_PER_KV):
                    blk = half * Q_PER_KV + g
                    s_prev = s[blk * WINDOW:(blk + 1) * WINDOW, :WINDOW]
                    s_own = s[blk * WINDOW:(blk + 1) * WINDOW, WINDOW:]
                    if prev_bias is not None:
                        s_prev = s_prev + prev_bias
                    sg = jnp.where(own, s_own, s_prev)
                    sink = sink_ref[(2 * a + half) * Q_PER_KV + g] * LOG2_E
                    m = jnp.maximum(jnp.max(sg, axis=-1, keepdims=True), sink)
                    p = jnp.exp2(sg - m)
                    ps.append(jnp.concatenate([jnp.where(own, fzero, p).astype(BF16),
                                               jnp.where(own, p, fzero).astype(BF16)], axis=1))
                    sink_terms.append(jnp.exp2(sink - m))
            od = jnp.dot(jnp.concatenate(ps, axis=0), vd1, preferred_element_type=F32)
            for g in range(Q_PER_KV):
                r_lo = slice(g * WINDOW, (g + 1) * WINDOW)
                r_hi = slice((Q_PER_KV + g) * WINDOW, (Q_PER_KV + g + 1) * WINDOW)
                num = jnp.where(lo, od[r_lo, :LANES], od[r_hi, :LANES])
                den = (jnp.where(lo, od[r_lo, LANES:], od[r_hi, LANES:])
                       + jnp.where(lo, sink_terms[g], sink_terms[Q_PER_KV + g]))
                oh_s[rows, (a * Q_PER_KV + g) * LANES:(a * Q_PER_KV + g + 1) * LANES] = (
                    (num * (1.0 / den)).astype(BF16))

    o_ref[0] = h + jnp.dot(oh_s[...], wo_s[...], preferred_element_type=F32) + bo_ref[...]


def _attn_layer(h, kv, sinks, g, w_q, b_q, w_o, b_o):
    bsz, seq, d = h.shape
    nkv = kv.shape[-1]
    tq = ATTN_TQ
    nsb = tq // WINDOW
    kernel = functools.partial(_attn_kernel, tq=tq)
    return pl.pallas_call(
        kernel,
        out_shape=jax.ShapeDtypeStruct((bsz, seq, d), F32),
        grid=(bsz, seq // tq),
        in_specs=[
            pl.BlockSpec(memory_space=pltpu.SMEM),
            pl.BlockSpec((1, tq, d), lambda b, n: (b, n, 0)),
            pl.BlockSpec((1, WINDOW, nkv), lambda b, n: (b, jnp.maximum(n * nsb - 1, 0), 0)),
            pl.BlockSpec((1, tq, nkv), lambda b, n: (b, n, 0)),
            _const_spec((1, d)),
            _const_spec(w_q.shape),
            _const_spec((1, d)),
            _const_spec(w_o.shape),
            _const_spec((1, d)),
        ],
        out_specs=pl.BlockSpec((1, tq, d), lambda b, n: (b, n, 0)),
        scratch_shapes=[pltpu.VMEM((d, d), BF16),
                        pltpu.VMEM((1, d), F32),
                        pltpu.VMEM((d, d), BF16),
                        pltpu.VMEM((tq, d), BF16),
                        pltpu.VMEM((tq, d), BF16)],
        compiler_params=pltpu.CompilerParams(
            dimension_semantics=("arbitrary", "arbitrary"), vmem_limit_bytes=VMEM_LIMIT_BYTES),
        name="attn",
    )(sinks, h, kv, kv, g.reshape(1, d), w_q, b_q.reshape(1, d), w_o, b_o.reshape(1, d))


def kernel(x, norm_mix, norm_mlp, norm_kv, norm_final, s5_a_re, s5_a_im, s5_log_dt, s5_b_re, s5_b_im, s5_c_re, s5_c_im, s5_d, s5_w_glu, s5_b_glu, w_kv, b_kv, w_q, b_q, sinks, w_o, b_o, w_mlp_in, w_mlp_out):
    bsz, seq, d = x.shape

    w1, cx, lam2_re_t, lam2_im_t = _s5_params(s5_a_re[0], s5_a_im[0], s5_log_dt[0], s5_b_re[0],
                                              s5_b_im[0], s5_c_re[0], s5_c_im[0])
    h = _s5_layer(x, norm_mix[0], w1, lam2_re_t, lam2_im_t, cx, s5_d[0], s5_w_glu[0], s5_b_glu[0])

    h, kv = _mlp(h.reshape(bsz * seq, d), norm_mlp[0], w_mlp_in, w_mlp_out, 0,
                 g_kv=norm_kv, w_kv=w_kv, b_kv=b_kv)

    h = _attn_layer(h.reshape(bsz, seq, d), kv.reshape(bsz, seq, -1), sinks[0], norm_mix[1],
                    w_q[0], b_q[0], w_o[0], b_o[0])

    out = _mlp(h.reshape(bsz * seq, d), norm_mlp[1], w_mlp_in, w_mlp_out, 1, g_fin=norm_final)
    return out.reshape(bsz, seq, d)
```

```python
import functools
import math

import jax
import jax.numpy as jnp
from jax import lax
from jax.experimental import pallas as pl
from jax.experimental.pallas import tpu as pltpu

F32 = jnp.float32
BF16 = jnp.bfloat16

D_MODEL = 1024
S5_GROUP = 16
S5_STATE = 64
LAMBDA_RE_MAX = -1e-4
HEAD_DIM = 64
N_Q_HEADS = D_MODEL // HEAD_DIM
N_KV_HEADS = 4
Q_PER_KV = N_Q_HEADS // N_KV_HEADS
WINDOW = 128
D_FF = 4 * D_MODEL
NORM_EPS = 1e-5
LOG2_E = math.log2(math.e)

LANES = 128
SUBLANES = 8
N_LANE_TILES = D_MODEL // LANES
GROUPS_PER_TILE = LANES // S5_GROUP
STATES_PER_TILE = GROUPS_PER_TILE * S5_STATE
MIB = 1024 * 1024
VMEM_LIMIT_BYTES = 56 * MIB

S5_TL = 64
S5_PARAM_TILES_PER_STEP = 4
S5_TILES_IN_FLIGHT = 2
MLP_TM = 512
MLP_FF_CHUNK = 1024
ATTN_TQ = 1024


def _rmsnorm(x, g):
    return x * lax.rsqrt(jnp.mean(x * x, axis=-1, keepdims=True) + NORM_EPS) * g


def _gelu_tanh(x):
    c = math.sqrt(2.0 / math.pi)
    return 0.5 * x * (1.0 + jnp.tanh(c * (x + 0.044715 * (x * x * x))))


def _const_spec(shape):
    nd = len(shape)
    return pl.BlockSpec(shape, lambda *_: (0,) * nd, pipeline_mode=pl.Buffered(1))


def _s5_params_kernel(abl_ref, bt_ref, cre_ref, cim_ref, w1_ref, cx_ref, l2r_ref, l2i_ref):
    for tile in range(S5_PARAM_TILES_PER_STEP):
        _s5_params_tile(abl_ref.at[:, tile], bt_ref.at[:, tile], cre_ref.at[tile], cim_ref.at[tile],
                        w1_ref.at[tile], cx_ref.at[tile], l2r_ref.at[tile], l2i_ref.at[tile])


def _s5_params_tile(abl_ref, bt_ref, cre_ref, cim_ref, w1_ref, cx_ref, l2r_ref, l2i_ref):
    gpt, grp, nst = GROUPS_PER_TILE, S5_GROUP, S5_STATE
    ar = jnp.minimum(abl_ref[0], LAMBDA_RE_MAX)
    ai = abl_ref[1]
    dt = jnp.exp(abl_ref[2])
    mag = jnp.exp(ar * dt)
    ang = ai * dt
    lr = mag * jnp.cos(ang)
    li = mag * jnp.sin(ang)
    den = ar * ar + ai * ai
    cr = ((lr - 1.0) * ar + li * ai) / den
    ci = (li * ar - (lr - 1.0) * ai) / den
    l2r = lr * lr - li * li
    l2i = 2.0 * (lr * li)

    def per_row(v):
        return jnp.concatenate([jnp.broadcast_to(v[g:g + 1], (grp, nst)) for g in range(gpt)], axis=0)

    row_g = lax.broadcasted_iota(jnp.int32, (LANES, STATES_PER_TILE), 0) // grp
    col_g = lax.broadcasted_iota(jnp.int32, (LANES, STATES_PER_TILE), 1) // nst
    own = row_g == col_g

    def blockdiag(v):
        return jnp.where(own, jnp.concatenate([v] * gpt, axis=1), 0.0)

    lr_c, li_c, cr_c, ci_c = per_row(lr), per_row(li), per_row(cr), per_row(ci)
    l2r_c, l2i_c = per_row(l2r), per_row(l2i)
    btr, bti = bt_ref[0], bt_ref[1]
    bbr = cr_c * btr - ci_c * bti
    bbi = cr_c * bti + ci_c * btr
    wbr, wbi = blockdiag(bbr), blockdiag(bbi)
    w1r = blockdiag(lr_c * bbr - li_c * bbi)
    w1i = blockdiag(lr_c * bbi + li_c * bbr)
    ccr, cci = cre_ref[...], cim_ref[...]
    wcr, wci = blockdiag(ccr), blockdiag(cci)

    def hdot_nt(a, b):
        return lax.dot_general(a, b, (((1,), (1,)), ((), ())),
                               precision=lax.Precision.HIGHEST, preferred_element_type=F32)

    k0 = hdot_nt(wbr, wcr) - hdot_nt(wbi, wci)
    k1 = hdot_nt(w1r, wcr) - hdot_nt(w1i, wci)
    top = jnp.concatenate([k0, k1, w1r, w1i], axis=1)
    bot = jnp.concatenate([jnp.zeros_like(k0), k0, wbr, wbi], axis=1)
    w1_ref[...] = jnp.concatenate([top, bot], axis=0).astype(BF16)

    def readout_t(pr, pi):
        return jnp.concatenate([blockdiag(pr * ccr - pi * cci),
                                -blockdiag(pi * ccr + pr * cci)], axis=1)

    cx_t = jnp.concatenate([readout_t(lr_c, li_c), readout_t(l2r_c, l2i_c)], axis=0)
    cx_ref[...] = cx_t.T.astype(BF16)

    def flat_row(v):
        row = jnp.concatenate([v[g:g + 1] for g in range(gpt)], axis=1)
        return jnp.broadcast_to(row, (SUBLANES, STATES_PER_TILE))

    l2r_ref[...] = flat_row(l2r)
    l2i_ref[...] = flat_row(l2i)


def _s5_params(a_re, a_im, log_dt, b_re, b_im, c_re, c_im):
    nt, gpt, sp = N_LANE_TILES, GROUPS_PER_TILE, STATES_PER_TILE
    tps = S5_PARAM_TILES_PER_STEP
    abl = jnp.stack([a_re, a_im, jnp.broadcast_to(log_dt[:, None], a_re.shape)])
    abl = abl.reshape(3, nt, gpt, S5_STATE)
    bt = jnp.swapaxes(jnp.stack([b_re, b_im]), -1, -2).reshape(2, nt, LANES, S5_STATE)
    c_tile = lambda c: c.reshape(nt, LANES, S5_STATE)
    return pl.pallas_call(
        _s5_params_kernel,
        out_shape=(jax.ShapeDtypeStruct((nt, 2 * LANES, 2 * LANES + 2 * sp), BF16),
                   jax.ShapeDtypeStruct((nt, 2 * sp, 2 * LANES), BF16),
                   jax.ShapeDtypeStruct((nt, SUBLANES, sp), F32),
                   jax.ShapeDtypeStruct((nt, SUBLANES, sp), F32)),
        grid=(nt // tps,),
        in_specs=[pl.BlockSpec((3, tps, gpt, S5_STATE), lambda j: (0, j, 0, 0)),
                  pl.BlockSpec((2, tps, LANES, S5_STATE), lambda j: (0, j, 0, 0)),
                  pl.BlockSpec((tps, LANES, S5_STATE), lambda j: (j, 0, 0)),
                  pl.BlockSpec((tps, LANES, S5_STATE), lambda j: (j, 0, 0))],
        out_specs=(pl.BlockSpec((tps, 2 * LANES, 2 * LANES + 2 * sp), lambda j: (j, 0, 0)),
                   pl.BlockSpec((tps, 2 * sp, 2 * LANES), lambda j: (j, 0, 0)),
                   pl.BlockSpec((tps, SUBLANES, sp), lambda j: (j, 0, 0)),
                   pl.BlockSpec((tps, SUBLANES, sp), lambda j: (j, 0, 0))),
        compiler_params=pltpu.CompilerParams(dimension_semantics=("parallel",)),
        name="s5_params",
    )(abl, bt, c_tile(c_re), c_tile(c_im))


def _s5_kernel(x_hbm, g_ref, w1_ref, l2r_ref, l2i_ref, cx_ref, d_ref, wglu_ref, bglu_ref,
               o_hbm, xin_s, hn_s, hnu_s, zy_s, bu_s, xs_s, y_s, st_s, res_s, sem_in, sem_out,
               *, tl, n_chunks):
    half = tl // 2
    prow = half * SUBLANES
    sp = STATES_PER_TILE
    i = pl.program_id(0)

    def in_copies(chunk):
        return [pltpu.make_async_copy(x_hbm.at[b, pl.ds(chunk * tl, tl), :],
                                      xin_s.at[chunk % 3, :, b, :], sem_in.at[chunk % 3, b])
                for b in range(SUBLANES)]

    def out_copies(chunk):
        return [pltpu.make_async_copy(res_s.at[chunk % 2, :, b, :],
                                      o_hbm.at[b, pl.ds(chunk * tl, tl), :], sem_out.at[chunk % 2, b])
                for b in range(SUBLANES)]

    @pl.when(i == 0)
    def _():
        st_s[...] = jnp.zeros_like(st_s)
        for copy in in_copies(0):
            copy.start()

    @pl.when(i + 1 < n_chunks)
    def _():
        for copy in in_copies(i + 1):
            copy.start()

    @pl.when(i < n_chunks)
    def _():
        for copy in in_copies(i):
            copy.wait()

    @pl.when(i >= 3)
    def _():
        for copy in out_copies(i - 3):
            copy.wait()

    def row_group(v, t):
        return v[t * SUBLANES:(t + 1) * SUBLANES]

    def recurrence_part(par):
        hn = _rmsnorm(xin_s[i % 3].reshape(tl * SUBLANES, D_MODEL), g_ref[...])
        for t in range(tl):
            g = (t % 2) * half + t // 2
            hn_s[par, g * SUBLANES:(g + 1) * SUBLANES, :] = row_group(hn, t)
        for m in range(tl // 4):
            r16 = slice(2 * m * SUBLANES, (2 * m + 2) * SUBLANES)
            even = jnp.concatenate([row_group(hn, 4 * m), row_group(hn, 4 * m + 2)], axis=0).astype(BF16)
            odd = jnp.concatenate([row_group(hn, 4 * m + 1), row_group(hn, 4 * m + 3)], axis=0).astype(BF16)
            for j in range(N_LANE_TILES):
                lanes = slice(j * LANES, (j + 1) * LANES)
                hnu_s[r16, 2 * j * LANES:(2 * j + 1) * LANES] = even[:, lanes]
                hnu_s[r16, (2 * j + 1) * LANES:(2 * j + 2) * LANES] = odd[:, lanes]

        def project_in(j):
            z = jnp.dot(hnu_s[:, 2 * j * LANES:(2 * j + 2) * LANES], w1_ref[j],
                        preferred_element_type=F32)
            zy_s[j % S5_TILES_IN_FLIGHT] = z[:, :2 * LANES]
            bu_s[j % S5_TILES_IN_FLIGHT] = z[:, 2 * LANES:]

        ahead = S5_TILES_IN_FLIGHT - 1
        for j in range(ahead):
            project_in(j)
        for j in range(N_LANE_TILES):
            if j + ahead < N_LANE_TILES:
                project_in(j + ahead)
            p = j % S5_TILES_IN_FLIGHT
            lanes = slice(j * LANES, (j + 1) * LANES)
            ar = l2r_ref[j]
            ai = l2i_ref[j]
            xr = st_s[j, :, 0:sp]
            xi = st_s[j, :, sp:2 * sp]
            for k in range(half):
                r = slice(k * SUBLANES, (k + 1) * SUBLANES)
                xs_s[p, r, 0:sp] = xr
                xs_s[p, r, sp:2 * sp] = xi
                nxr = ar * xr - ai * xi + bu_s[p, r, 0:sp]
                nxi = ar * xi + ai * xr + bu_s[p, r, sp:2 * sp]
                xr, xi = nxr, nxi
            st_s[j, :, 0:sp] = xr
            st_s[j, :, sp:2 * sp] = xi
            y = zy_s[p] + jnp.dot(xs_s[p].astype(BF16), cx_ref[j], preferred_element_type=F32)
            y_s[par, 0:prow, lanes] = y[:, :LANES]
            y_s[par, prow:2 * prow, lanes] = y[:, LANES:]

    def output_part(par):
        for tpar in range(2):
            rs = slice(tpar * prow, (tpar + 1) * prow)
            y = y_s[par, rs, :] + d_ref[...] * hn_s[par, rs, :]
            z = jnp.dot(_gelu_tanh(y).astype(BF16), wglu_ref[...].astype(BF16),
                        preferred_element_type=F32) + bglu_ref[...]
            mix = z[:, :D_MODEL] * (1.0 / (1.0 + jnp.exp(-z[:, D_MODEL:])))
            for k in range(half):
                t = 2 * k + tpar
                res_s[(i - 1) % 2, t] = xin_s[(i - 1) % 3, t] + row_group(mix, k)
        for copy in out_copies(i - 1):
            copy.start()

    @pl.when(i == 0)
    def _():
        recurrence_part(0)

    for par in range(2):
        @pl.when((i >= 1) & (i < n_chunks) & (i % 2 == par))
        def _():
            recurrence_part(par)
            output_part(1 - par)

    @pl.when(i == n_chunks)
    def _():
        output_part((n_chunks - 1) % 2)
        for copy in out_copies(n_chunks - 2) + out_copies(n_chunks - 1):
            copy.wait()


def _s5_layer(x, g_mix, w1, lam2_re_t, lam2_im_t, cx, d_skip, w_glu, b_glu):
    bsz, seq, d = x.shape
    tl = S5_TL
    n_chunks = seq // tl
    assert bsz == SUBLANES and n_chunks >= 3
    rows = tl * bsz
    prow = rows // 2
    kernel = functools.partial(_s5_kernel, tl=tl, n_chunks=n_chunks)
    hbm = pl.BlockSpec(memory_space=pl.ANY)
    return pl.pallas_call(
        kernel,
        out_shape=jax.ShapeDtypeStruct((bsz, seq, d), F32),
        grid=(n_chunks + 1,),
        in_specs=[
            hbm,
            _const_spec((1, d)),
            _const_spec(w1.shape),
            _const_spec(lam2_re_t.shape),
            _const_spec(lam2_im_t.shape),
            _const_spec(cx.shape),
            _const_spec((1, d)),
            _const_spec(w_glu.shape),
            _const_spec((1, 2 * d)),
        ],
        out_specs=hbm,
        scratch_shapes=[
            pltpu.VMEM((3, tl, bsz, d), F32),
            pltpu.VMEM((2, rows, d), F32),
            pltpu.VMEM((prow, 2 * d), BF16),
            pltpu.VMEM((S5_TILES_IN_FLIGHT, prow, 2 * LANES), F32),
            pltpu.VMEM((S5_TILES_IN_FLIGHT, prow, 2 * STATES_PER_TILE), F32),
            pltpu.VMEM((S5_TILES_IN_FLIGHT, prow, 2 * STATES_PER_TILE), F32),
            pltpu.VMEM((2, rows, d), F32),
            pltpu.VMEM((N_LANE_TILES, SUBLANES, 2 * STATES_PER_TILE), F32),
            pltpu.VMEM((2, tl, bsz, d), F32),
            pltpu.SemaphoreType.DMA((3, SUBLANES)),
            pltpu.SemaphoreType.DMA((2, SUBLANES)),
        ],
        compiler_params=pltpu.CompilerParams(
            dimension_semantics=("arbitrary",), vmem_limit_bytes=VMEM_LIMIT_BYTES),
        name="s5_layer",
    )(x, g_mix.reshape(1, d), w1, lam2_re_t, lam2_im_t, cx, d_skip.reshape(1, d), w_glu,
      b_glu.reshape(1, 2 * d))


def _inv_rms(x):
    return lax.rsqrt(jnp.mean(x * x, axis=-1, keepdims=True) + NORM_EPS)


def _mlp_kernel(h_ref, g_ref, win_hbm, wout_hbm, *rest, layer, with_kv):
    if with_kv:
        gkv_ref, wkv_ref, bkv_ref, o_ref, kv_ref, win_s, wout_s, sem = rest
    else:
        gfin_ref, o_ref, win_s, wout_s, sem = rest
    n_chunks = D_FF // MLP_FF_CHUNK

    def weight_copies(c):
        cols = pl.ds(c * MLP_FF_CHUNK, MLP_FF_CHUNK)
        return (pltpu.make_async_copy(win_hbm.at[layer, :, cols], win_s.at[:, cols], sem.at[0, c]),
                pltpu.make_async_copy(wout_hbm.at[layer, cols, :], wout_s.at[cols, :], sem.at[1, c]))

    def body(first_step):
        h = h_ref[...]
        hg = (h * g_ref[...]).astype(BF16)
        r = _inv_rms(h)
        acts = []
        for c in range(n_chunks):
            cols = slice(c * MLP_FF_CHUNK, (c + 1) * MLP_FF_CHUNK)
            if first_step:
                for copy in weight_copies(c):
                    copy.wait()
            a = jnp.dot(hg, win_s[:, cols].astype(BF16), preferred_element_type=F32)
            acts.append(jnp.square(jnp.maximum(a, 0.0)).astype(BF16))
        acc = jnp.dot(jnp.concatenate(acts, axis=1), wout_s[...].astype(BF16),
                      preferred_element_type=F32)
        out = h + (r * r) * acc
        if with_kv:
            o_ref[...] = out
            kv = jnp.dot((out * gkv_ref[...]).astype(BF16), wkv_ref[...].astype(BF16),
                         preferred_element_type=F32)
            kv_ref[...] = (_inv_rms(out) * kv + bkv_ref[...]).astype(BF16)
        else:
            o_ref[...] = out * _inv_rms(out) * gfin_ref[...]

    @pl.when(pl.program_id(0) == 0)
    def _():
        for c in range(n_chunks):
            for copy in weight_copies(c):
                copy.start()
        body(True)

    @pl.when(pl.program_id(0) > 0)
    def _():
        body(False)


def _mlp(h, g, w_in, w_out, layer, *, g_kv=None, w_kv=None, b_kv=None, g_fin=None):
    t, d = h.shape
    with_kv = w_kv is not None
    row = lambda n: pl.BlockSpec((MLP_TM, n), lambda i: (i, 0))
    hbm = pl.BlockSpec(memory_space=pl.ANY)
    in_specs = [row(d), _const_spec((1, d)), hbm, hbm]
    operands = [h, g.reshape(1, d), w_in, w_out]
    if with_kv:
        nkv = w_kv.shape[1]
        in_specs += [_const_spec((1, d)), _const_spec(w_kv.shape), _const_spec((1, nkv))]
        operands += [g_kv.reshape(1, d), w_kv, b_kv.reshape(1, nkv)]
        out_shape = (jax.ShapeDtypeStruct((t, d), F32), jax.ShapeDtypeStruct((t, nkv), BF16))
        out_specs = (row(d), row(nkv))
    else:
        in_specs += [_const_spec((1, d))]
        operands += [g_fin.reshape(1, d)]
        out_shape = jax.ShapeDtypeStruct((t, d), F32)
        out_specs = row(d)
    return pl.pallas_call(
        functools.partial(_mlp_kernel, layer=layer, with_kv=with_kv),
        out_shape=out_shape,
        grid=(t // MLP_TM,),
        in_specs=in_specs,
        out_specs=out_specs,
        scratch_shapes=[pltpu.VMEM(w_in.shape[1:], F32),
                        pltpu.VMEM(w_out.shape[1:], F32),
                        pltpu.SemaphoreType.DMA((2, D_FF // MLP_FF_CHUNK))],
        compiler_params=pltpu.CompilerParams(
            dimension_semantics=("arbitrary",), vmem_limit_bytes=VMEM_LIMIT_BYTES),
        name="mlp_kv" if with_kv else "mlp_final",
    )(*operands)


def _pair_tile_sources(a, g):
    head_lo = (2 * a) * Q_PER_KV + g
    head_hi = (2 * a + 1) * Q_PER_KV + g
    return (head_lo // 2, head_lo % 2), (head_hi // 2, head_hi % 2)


def _attn_kernel(sink_ref, h_ref, kvp_ref, kvc_ref, g_ref, wq_ref, bq_ref, wo_ref, bo_ref,
                 o_ref, wq_s, bq_s, wo_s, q_s, oh_s, *, tq):
    nsb = tq // WINDOW
    n = pl.program_id(1)
    n_kv_tiles = N_KV_HEADS // 2
    lo_row = lax.broadcasted_iota(jnp.int32, (1, LANES), 1) < HEAD_DIM

    @pl.when((pl.program_id(0) == 0) & (n == 0))
    def _():
        for a in range(n_kv_tiles):
            for g in range(Q_PER_KV):
                (t_lo, h_lo), (t_hi, h_hi) = _pair_tile_sources(a, g)
                dst = slice((a * Q_PER_KV + g) * LANES, (a * Q_PER_KV + g + 1) * LANES)

                def pair(ref):
                    src_lo = ref[:, t_lo * LANES:(t_lo + 1) * LANES]
                    src_hi = ref[:, t_hi * LANES:(t_hi + 1) * LANES]
                    if h_lo == 1:
                        src_lo = pltpu.roll(src_lo, HEAD_DIM, axis=1)
                    if h_hi == 0:
                        src_hi = pltpu.roll(src_hi, HEAD_DIM, axis=1)
                    return jnp.where(lo_row, src_lo, src_hi)

                wq_s[:, dst] = pair(wq_ref).astype(BF16)
                bq_s[:, dst] = pair(bq_ref)
                for half in range(2):
                    head = (2 * a + half) * Q_PER_KV + g
                    r0 = (a * Q_PER_KV + g) * LANES + half * HEAD_DIM
                    wo_s[r0:r0 + HEAD_DIM, :] = (
                        wo_ref[head * HEAD_DIM:(head + 1) * HEAD_DIM, :].astype(BF16))

    h = h_ref[0]
    hn = _rmsnorm(h, g_ref[...]).astype(BF16)
    q = jnp.dot(hn, wq_s[...], preferred_element_type=F32) + bq_s[...]
    q_s[...] = (q * (LOG2_E / math.sqrt(HEAD_DIM))).astype(BF16)

    lo = lax.broadcasted_iota(jnp.int32, (WINDOW, LANES), 1) < HEAD_DIM
    own = (lax.broadcasted_iota(jnp.int32, (WINDOW, WINDOW), 1)
           <= lax.broadcasted_iota(jnp.int32, (WINDOW, WINDOW), 0))
    zero = jnp.zeros((WINDOW, LANES), BF16)
    fzero = jnp.zeros((WINDOW, WINDOW), F32)

    for sb in range(nsb):
        rows = slice(sb * WINDOW, (sb + 1) * WINDOW)
        prev_bias = jnp.where(n == 0, -jnp.inf, 0.0).astype(F32) if sb == 0 else None
        for a in range(n_kv_tiles):
            kl = slice(a * LANES, (a + 1) * LANES)
            vl = slice((n_kv_tiles + a) * LANES, (n_kv_tiles + a + 1) * LANES)
            if sb == 0:
                kprev, vprev = kvp_ref[0, :, kl], kvp_ref[0, :, vl]
            else:
                prow = slice((sb - 1) * WINDOW, sb * WINDOW)
                kprev, vprev = kvc_ref[0, prow, kl], kvc_ref[0, prow, vl]
            kd = jnp.concatenate([kprev, kvc_ref[0, rows, kl]], axis=0)
            vd = jnp.concatenate([vprev, kvc_ref[0, rows, vl]], axis=0)
            vd1 = jnp.concatenate([vd, jnp.ones((2 * WINDOW, LANES), BF16)], axis=1)
            qt = [q_s[rows, (a * Q_PER_KV + g) * LANES:(a * Q_PER_KV + g + 1) * LANES]
                  for g in range(Q_PER_KV)]
            lhs = jnp.concatenate([jnp.where(lo, t, zero) for t in qt]
                                  + [jnp.where(lo, zero, t) for t in qt], axis=0)
            s = lax.dot_general(lhs, kd, (((1,), (1,)), ((), ())),
                                preferred_element_type=F32)
            ps, sink_terms = [], []
            for half in range(2):
                for g in range(Q_PER_KV):
                    blk = half * Q_PER_KV + g
                    s_prev = s[blk * WINDOW:(blk + 1) * WINDOW, :WINDOW]
                    s_own = s[blk * WINDOW:(blk + 1) * WINDOW, WINDOW:]
                    if prev_bias is not None:
                        s_prev = s_prev + prev_bias
                    sg = jnp.where(own, s_own, s_prev)
                    sink = sink_ref[(2 * a + half) * Q_PER_KV + g] * LOG2_E
                    m = jnp.maximum(jnp.max(sg, axis=-1, keepdims=True), sink)
                    p = jnp.exp2(sg - m)
                    ps.append(jnp.concatenate([jnp.where(own, fzero, p).astype(BF16),
                                               jnp.where(own, p, fzero).astype(BF16)], axis=1))
                    sink_terms.append(jnp.exp2(sink - m))
            od = jnp.dot(jnp.concatenate(ps, axis=0), vd1, preferred_element_type=F32)
            for g in range(Q_PER_KV):
                r_lo = slice(g * WINDOW, (g + 1) * WINDOW)
                r_hi = slice((Q_PER_KV + g) * WINDOW, (Q_PER_KV + g + 1) * WINDOW)
                num = jnp.where(lo, od[r_lo, :LANES], od[r_hi, :LANES])
                den = (jnp.where(lo, od[r_lo, LANES:], od[r_hi, LANES:])
                       + jnp.where(lo, sink_terms[g], sink_terms[Q_PER_KV + g]))
                oh_s[rows, (a * Q_PER_KV + g) * LANES:(a * Q_PER_KV + g + 1) * LANES] = (
                    (num * (1.0 / den)).astype(BF16))

    o_ref[0] = h + jnp.dot(oh_s[...], wo_s[...], preferred_element_type=F32) + bo_ref[...]


def _attn_layer(h, kv, sinks, g, w_q, b_q, w_o, b_o):
    bsz, seq, d = h.shape
    nkv = kv.shape[-1]
    tq = ATTN_TQ
    nsb = tq // WINDOW
    kernel = functools.partial(_attn_kernel, tq=tq)
    return pl.pallas_call(
        kernel,
        out_shape=jax.ShapeDtypeStruct((bsz, seq, d), F32),
        grid=(bsz, seq // tq),
        in_specs=[
            pl.BlockSpec(memory_space=pltpu.SMEM),
            pl.BlockSpec((1, tq, d), lambda b, n: (b, n, 0)),
            pl.BlockSpec((1, WINDOW, nkv), lambda b, n: (b, jnp.maximum(n * nsb - 1, 0), 0)),
            pl.BlockSpec((1, tq, nkv), lambda b, n: (b, n, 0)),
            _const_spec((1, d)),
            _const_spec(w_q.shape),
            _const_spec((1, d)),
            _const_spec(w_o.shape),
            _const_spec((1, d)),
        ],
        out_specs=pl.BlockSpec((1, tq, d), lambda b, n: (b, n, 0)),
        scratch_shapes=[pltpu.VMEM((d, d), BF16),
                        pltpu.VMEM((1, d), F32),
                        pltpu.VMEM((d, d), BF16),
                        pltpu.VMEM((tq, d), BF16),
                        pltpu.VMEM((tq, d), BF16)],
        compiler_params=pltpu.CompilerParams(
            dimension_semantics=("arbitrary", "arbitrary"), vmem_limit_bytes=VMEM_LIMIT_BYTES),
        name="attn",
    )(sinks, h, kv, kv, g.reshape(1, d), w_q, b_q.reshape(1, d), w_o, b_o.reshape(1, d))


def kernel(x, norm_mix, norm_mlp, norm_kv, norm_final, s5_a_re, s5_a_im, s5_log_dt, s5_b_re, s5_b_im, s5_c_re, s5_c_im, s5_d, s5_w_glu, s5_b_glu, w_kv, b_kv, w_q, b_q, sinks, w_o, b_o, w_mlp_in, w_mlp_out):
    bsz, seq, d = x.shape

    w1, cx, lam2_re_t, lam2_im_t = _s5_params(s5_a_re[0], s5_a_im[0], s5_log_dt[0], s5_b_re[0],
                                              s5_b_im[0], s5_c_re[0], s5_c_im[0])
    h = _s5_layer(x, norm_mix[0], w1, lam2_re_t, lam2_im_t, cx, s5_d[0], s5_w_glu[0], s5_b_glu[0])

    h, kv = _mlp(h.reshape(bsz * seq, d), norm_mlp[0], w_mlp_in, w_mlp_out, 0,
                 g_kv=norm_kv, w_kv=w_kv, b_kv=b_kv)

    h = _attn_layer(h.reshape(bsz, seq, d), kv.reshape(bsz, seq, -1), sinks[0], norm_mix[1],
                    w_q[0], b_q[0], w_o[0], b_o[0])

    out = _mlp(h.reshape(bsz * seq, d), norm_mlp[1], w_mlp_in, w_mlp_out, 1, g_fin=norm_final)
    return out.reshape(bsz, seq, d)
```

```python
import functools
import math

import jax
import jax.numpy as jnp
from jax import lax
from jax.experimental import pallas as pl
from jax.experimental.pallas import tpu as pltpu

F32 = jnp.float32
BF16 = jnp.bfloat16

D_MODEL = 1024
S5_GROUP = 16
S5_STATE = 64
LAMBDA_RE_MAX = -1e-4
HEAD_DIM = 64
N_Q_HEADS = D_MODEL // HEAD_DIM
N_KV_HEADS = 4
Q_PER_KV = N_Q_HEADS // N_KV_HEADS
WINDOW = 128
D_FF = 4 * D_MODEL
NORM_EPS = 1e-5
LOG2_E = math.log2(math.e)

LANES = 128
SUBLANES = 8
N_LANE_TILES = D_MODEL // LANES
GROUPS_PER_TILE = LANES // S5_GROUP
STATES_PER_TILE = GROUPS_PER_TILE * S5_STATE
MIB = 1024 * 1024
VMEM_LIMIT_BYTES = 56 * MIB

S5_TL = 64
S5_PARAM_TILES_PER_STEP = 4
S5_TILES_IN_FLIGHT = 2
MLP_TM = 512
MLP_FF_CHUNK = 1024
ATTN_TQ = 1024


def _rmsnorm(x, g):
    return x * lax.rsqrt(jnp.mean(x * x, axis=-1, keepdims=True) + NORM_EPS) * g


def _gelu_tanh(x):
    c = math.sqrt(2.0 / math.pi)
    return 0.5 * x * (1.0 + jnp.tanh(c * (x + 0.044715 * (x * x * x))))


def _const_spec(shape):
    nd = len(shape)
    return pl.BlockSpec(shape, lambda *_: (0,) * nd, pipeline_mode=pl.Buffered(1))


def _s5_params_kernel(are_ref, aim_ref, ldt_ref, bt_ref, cre_ref, cim_ref,
                      w1_ref, cx_ref, l2r_ref, l2i_ref):
    for tile in range(S5_PARAM_TILES_PER_STEP):
        _s5_params_tile(are_ref.at[tile], aim_ref.at[tile], ldt_ref.at[tile], bt_ref.at[:, tile],
                        cre_ref.at[tile], cim_ref.at[tile],
                        w1_ref.at[tile], cx_ref.at[tile], l2r_ref.at[tile], l2i_ref.at[tile])


def _s5_params_tile(are_ref, aim_ref, ldt_ref, bt_ref, cre_ref, cim_ref,
                    w1_ref, cx_ref, l2r_ref, l2i_ref):
    gpt, grp, nst = GROUPS_PER_TILE, S5_GROUP, S5_STATE
    ar = jnp.minimum(are_ref[...], LAMBDA_RE_MAX)
    ai = aim_ref[...]
    dt = jnp.exp(ldt_ref[...])
    mag = jnp.exp(ar * dt)
    ang = ai * dt
    lr = mag * jnp.cos(ang)
    li = mag * jnp.sin(ang)
    den = ar * ar + ai * ai
    cr = ((lr - 1.0) * ar + li * ai) / den
    ci = (li * ar - (lr - 1.0) * ai) / den
    l2r = lr * lr - li * li
    l2i = 2.0 * (lr * li)

    def per_row(v):
        return jnp.concatenate([jnp.broadcast_to(v[g:g + 1], (grp, nst)) for g in range(gpt)], axis=0)

    row_g = lax.broadcasted_iota(jnp.int32, (LANES, STATES_PER_TILE), 0) // grp
    col_g = lax.broadcasted_iota(jnp.int32, (LANES, STATES_PER_TILE), 1) // nst
    own = row_g == col_g

    def blockdiag(v):
        return jnp.where(own, jnp.concatenate([v] * gpt, axis=1), 0.0)

    lr_c, li_c, cr_c, ci_c = per_row(lr), per_row(li), per_row(cr), per_row(ci)
    l2r_c, l2i_c = per_row(l2r), per_row(l2i)
    btr, bti = bt_ref[0], bt_ref[1]
    bbr = cr_c * btr - ci_c * bti
    bbi = cr_c * bti + ci_c * btr
    wbr, wbi = blockdiag(bbr), blockdiag(bbi)
    w1r = blockdiag(lr_c * bbr - li_c * bbi)
    w1i = blockdiag(lr_c * bbi + li_c * bbr)
    ccr, cci = cre_ref[...], cim_ref[...]
    wcr, wci = blockdiag(ccr), blockdiag(cci)

    def hdot_nt(a, b):
        return lax.dot_general(a, b, (((1,), (1,)), ((), ())),
                               precision=lax.Precision.HIGHEST, preferred_element_type=F32)

    k0 = hdot_nt(wbr, wcr) - hdot_nt(wbi, wci)
    k1 = hdot_nt(w1r, wcr) - hdot_nt(w1i, wci)
    top = jnp.concatenate([k0, k1, w1r, w1i], axis=1)
    bot = jnp.concatenate([jnp.zeros_like(k0), k0, wbr, wbi], axis=1)
    w1_ref[...] = jnp.concatenate([top, bot], axis=0).astype(BF16)

    def readout_t(pr, pi):
        return jnp.concatenate([blockdiag(pr * ccr - pi * cci),
                                -blockdiag(pi * ccr + pr * cci)], axis=1)

    cx_t = jnp.concatenate([readout_t(lr_c, li_c), readout_t(l2r_c, l2i_c)], axis=0)
    cx_ref[...] = cx_t.T.astype(BF16)

    def flat_row(v):
        row = jnp.concatenate([v[g:g + 1] for g in range(gpt)], axis=1)
        return jnp.broadcast_to(row, (SUBLANES, STATES_PER_TILE))

    l2r_ref[...] = flat_row(l2r)
    l2i_ref[...] = flat_row(l2i)


def _s5_params(a_re, a_im, log_dt, b_re, b_im, c_re, c_im):
    nt, gpt, sp = N_LANE_TILES, GROUPS_PER_TILE, STATES_PER_TILE
    tps = S5_PARAM_TILES_PER_STEP
    per_group = lambda a: a.reshape(nt, gpt, S5_STATE)
    log_dt = jnp.broadcast_to(log_dt[:, None], a_re.shape)
    bt = jnp.swapaxes(jnp.stack([b_re, b_im]), -1, -2).reshape(2, nt, LANES, S5_STATE)
    c_tile = lambda c: c.reshape(nt, LANES, S5_STATE)
    return pl.pallas_call(
        _s5_params_kernel,
        out_shape=(jax.ShapeDtypeStruct((nt, 2 * LANES, 2 * LANES + 2 * sp), BF16),
                   jax.ShapeDtypeStruct((nt, 2 * sp, 2 * LANES), BF16),
                   jax.ShapeDtypeStruct((nt, SUBLANES, sp), F32),
                   jax.ShapeDtypeStruct((nt, SUBLANES, sp), F32)),
        grid=(nt // tps,),
        in_specs=[pl.BlockSpec((tps, gpt, S5_STATE), lambda j: (j, 0, 0)),
                  pl.BlockSpec((tps, gpt, S5_STATE), lambda j: (j, 0, 0)),
                  pl.BlockSpec((tps, gpt, S5_STATE), lambda j: (j, 0, 0)),
                  pl.BlockSpec((2, tps, LANES, S5_STATE), lambda j: (0, j, 0, 0)),
                  pl.BlockSpec((tps, LANES, S5_STATE), lambda j: (j, 0, 0)),
                  pl.BlockSpec((tps, LANES, S5_STATE), lambda j: (j, 0, 0))],
        out_specs=(pl.BlockSpec((tps, 2 * LANES, 2 * LANES + 2 * sp), lambda j: (j, 0, 0)),
                   pl.BlockSpec((tps, 2 * sp, 2 * LANES), lambda j: (j, 0, 0)),
                   pl.BlockSpec((tps, SUBLANES, sp), lambda j: (j, 0, 0)),
                   pl.BlockSpec((tps, SUBLANES, sp), lambda j: (j, 0, 0))),
        compiler_params=pltpu.CompilerParams(dimension_semantics=("parallel",)),
        name="s5_params",
    )(per_group(a_re), per_group(a_im), per_group(log_dt), bt, c_tile(c_re), c_tile(c_im))


def _s5_kernel(x_hbm, g_ref, w1_ref, l2r_ref, l2i_ref, cx_ref, d_ref, wglu_ref, bglu_ref,
               o_hbm, xin_s, hn_s, hnu_s, zy_s, bu_s, xs_s, y_s, st_s, res_s, sem_in, sem_out,
               *, tl, n_steps):
    half = tl // 2
    prow = half * SUBLANES
    sp = STATES_PER_TILE
    i = pl.program_id(0)
    slot = i % 2

    def in_copies(step, sl):
        return [pltpu.make_async_copy(x_hbm.at[b, pl.ds(step * tl, tl), :],
                                      xin_s.at[sl, :, b, :], sem_in.at[sl, b])
                for b in range(SUBLANES)]

    def out_copies(step, sl):
        return [pltpu.make_async_copy(res_s.at[sl, :, b, :],
                                      o_hbm.at[b, pl.ds(step * tl, tl), :], sem_out.at[sl, b])
                for b in range(SUBLANES)]

    def start_all(copies):
        for n, copy in enumerate(copies):
            copy.start(priority=n % 2)

    @pl.when(i == 0)
    def _():
        st_s[...] = jnp.zeros_like(st_s)
        start_all(in_copies(0, 0))

    @pl.when(i + 1 < n_steps)
    def _():
        start_all(in_copies(i + 1, 1 - slot))

    for copy in in_copies(i, slot):
        copy.wait()

    @pl.when(i >= 2)
    def _():
        for copy in out_copies(i - 2, slot):
            copy.wait()

    def row_group(v, t):
        return v[t * SUBLANES:(t + 1) * SUBLANES]

    hn = _rmsnorm(xin_s[slot].reshape(tl * SUBLANES, D_MODEL), g_ref[...])
    for t in range(tl):
        g = (t % 2) * half + t // 2
        hn_s[g * SUBLANES:(g + 1) * SUBLANES, :] = row_group(hn, t)
    for m in range(tl // 4):
        r16 = slice(2 * m * SUBLANES, (2 * m + 2) * SUBLANES)
        even = jnp.concatenate([row_group(hn, 4 * m), row_group(hn, 4 * m + 2)], axis=0).astype(BF16)
        odd = jnp.concatenate([row_group(hn, 4 * m + 1), row_group(hn, 4 * m + 3)], axis=0).astype(BF16)
        for j in range(N_LANE_TILES):
            lanes = slice(j * LANES, (j + 1) * LANES)
            hnu_s[r16, 2 * j * LANES:(2 * j + 1) * LANES] = even[:, lanes]
            hnu_s[r16, (2 * j + 1) * LANES:(2 * j + 2) * LANES] = odd[:, lanes]

    def project_in(j):
        z = jnp.dot(hnu_s[:, 2 * j * LANES:(2 * j + 2) * LANES], w1_ref[j],
                    preferred_element_type=F32)
        zy_s[j % S5_TILES_IN_FLIGHT] = z[:, :2 * LANES]
        bu_s[j % S5_TILES_IN_FLIGHT] = z[:, 2 * LANES:]

    ahead = S5_TILES_IN_FLIGHT - 1
    for j in range(ahead):
        project_in(j)
    for j in range(N_LANE_TILES):
        if j + ahead < N_LANE_TILES:
            project_in(j + ahead)
        p = j % S5_TILES_IN_FLIGHT
        lanes = slice(j * LANES, (j + 1) * LANES)
        ar = l2r_ref[j]
        ai = l2i_ref[j]
        xr = st_s[j, :, 0:sp]
        xi = st_s[j, :, sp:2 * sp]
        for k in range(half):
            r = slice(k * SUBLANES, (k + 1) * SUBLANES)
            xs_s[p, r, 0:sp] = xr
            xs_s[p, r, sp:2 * sp] = xi
            nxr = ar * xr - ai * xi + bu_s[p, r, 0:sp]
            nxi = ar * xi + ai * xr + bu_s[p, r, sp:2 * sp]
            xr, xi = nxr, nxi
        st_s[j, :, 0:sp] = xr
        st_s[j, :, sp:2 * sp] = xi
        y = zy_s[p] + jnp.dot(xs_s[p].astype(BF16), cx_ref[j], preferred_element_type=F32)
        y_s[0:prow, lanes] = y[:, :LANES]
        y_s[prow:2 * prow, lanes] = y[:, LANES:]

    for par in range(2):
        rs = slice(par * prow, (par + 1) * prow)
        y = y_s[rs, :] + d_ref[...] * hn_s[rs, :]
        z = jnp.dot(_gelu_tanh(y).astype(BF16), wglu_ref[...].astype(BF16),
                    preferred_element_type=F32) + bglu_ref[...]
        mix = z[:, :D_MODEL] * (1.0 / (1.0 + jnp.exp(-z[:, D_MODEL:])))
        for k in range(half):
            t = 2 * k + par
            res_s[slot, t] = xin_s[slot, t] + row_group(mix, k)

    start_all(out_copies(i, slot))

    @pl.when(i == n_steps - 1)
    def _():
        for copy in out_copies(i - 1, 1 - slot) + out_copies(i, slot):
            copy.wait()


def _s5_layer(x, g_mix, w1, lam2_re_t, lam2_im_t, cx, d_skip, w_glu, b_glu):
    bsz, seq, d = x.shape
    tl = S5_TL
    n_steps = seq // tl
    assert bsz == SUBLANES and n_steps >= 2
    rows = tl * bsz
    prow = rows // 2
    kernel = functools.partial(_s5_kernel, tl=tl, n_steps=n_steps)
    hbm = pl.BlockSpec(memory_space=pl.ANY)
    return pl.pallas_call(
        kernel,
        out_shape=jax.ShapeDtypeStruct((bsz, seq, d), F32),
        grid=(n_steps,),
        in_specs=[
            hbm,
            _const_spec((1, d)),
            _const_spec(w1.shape),
            _const_spec(lam2_re_t.shape),
            _const_spec(lam2_im_t.shape),
            _const_spec(cx.shape),
            _const_spec((1, d)),
            _const_spec(w_glu.shape),
            _const_spec((1, 2 * d)),
        ],
        out_specs=hbm,
        scratch_shapes=[
            pltpu.VMEM((2, tl, bsz, d), F32),
            pltpu.VMEM((rows, d), F32),
            pltpu.VMEM((prow, 2 * d), BF16),
            pltpu.VMEM((S5_TILES_IN_FLIGHT, prow, 2 * LANES), F32),
            pltpu.VMEM((S5_TILES_IN_FLIGHT, prow, 2 * STATES_PER_TILE), F32),
            pltpu.VMEM((S5_TILES_IN_FLIGHT, prow, 2 * STATES_PER_TILE), F32),
            pltpu.VMEM((rows, d), F32),
            pltpu.VMEM((N_LANE_TILES, SUBLANES, 2 * STATES_PER_TILE), F32),
            pltpu.VMEM((2, tl, bsz, d), F32),
            pltpu.SemaphoreType.DMA((2, SUBLANES)),
            pltpu.SemaphoreType.DMA((2, SUBLANES)),
        ],
        compiler_params=pltpu.CompilerParams(
            dimension_semantics=("arbitrary",), vmem_limit_bytes=VMEM_LIMIT_BYTES),
        name="s5_layer",
    )(x, g_mix.reshape(1, d), w1, lam2_re_t, lam2_im_t, cx, d_skip.reshape(1, d), w_glu,
      b_glu.reshape(1, 2 * d))


def _inv_rms(x):
    return lax.rsqrt(jnp.mean(x * x, axis=-1, keepdims=True) + NORM_EPS)


def _mlp_kernel(h_ref, g_ref, win_hbm, wout_hbm, *rest, layer, with_kv):
    if with_kv:
        gkv_ref, wkv_ref, bkv_ref, o_ref, kv_ref, win_s, wout_s, sem = rest
    else:
        gfin_ref, o_ref, win_s, wout_s, sem = rest
    n_chunks = D_FF // MLP_FF_CHUNK

    def weight_copies(c):
        cols = pl.ds(c * MLP_FF_CHUNK, MLP_FF_CHUNK)
        return (pltpu.make_async_copy(win_hbm.at[layer, :, cols], win_s.at[:, cols], sem.at[0, c]),
                pltpu.make_async_copy(wout_hbm.at[layer, cols, :], wout_s.at[cols, :], sem.at[1, c]))

    def body(first_step):
        h = h_ref[...]
        hg = (h * g_ref[...]).astype(BF16)
        r = _inv_rms(h)
        acts = []
        for c in range(n_chunks):
            cols = slice(c * MLP_FF_CHUNK, (c + 1) * MLP_FF_CHUNK)
            if first_step:
                for copy in weight_copies(c):
                    copy.wait()
            a = jnp.dot(hg, win_s[:, cols].astype(BF16), preferred_element_type=F32)
            acts.append(jnp.square(jnp.maximum(a, 0.0)).astype(BF16))
        acc = jnp.dot(jnp.concatenate(acts, axis=1), wout_s[...].astype(BF16),
                      preferred_element_type=F32)
        out = h + (r * r) * acc
        if with_kv:
            o_ref[...] = out
            kv = jnp.dot((out * gkv_ref[...]).astype(BF16), wkv_ref[...].astype(BF16),
                         preferred_element_type=F32)
            kv_ref[...] = (_inv_rms(out) * kv + bkv_ref[...]).astype(BF16)
        else:
            o_ref[...] = out * _inv_rms(out) * gfin_ref[...]

    @pl.when(pl.program_id(0) == 0)
    def _():
        for c in range(n_chunks):
            for copy in weight_copies(c):
                copy.start()
        body(True)

    @pl.when(pl.program_id(0) > 0)
    def _():
        body(False)


def _mlp(h, g, w_in, w_out, layer, *, g_kv=None, w_kv=None, b_kv=None, g_fin=None):
    t, d = h.shape
    with_kv = w_kv is not None
    row = lambda n: pl.BlockSpec((MLP_TM, n), lambda i: (i, 0))
    hbm = pl.BlockSpec(memory_space=pl.ANY)
    in_specs = [row(d), _const_spec((1, d)), hbm, hbm]
    operands = [h, g.reshape(1, d), w_in, w_out]
    if with_kv:
        nkv = w_kv.shape[1]
        in_specs += [_const_spec((1, d)), _const_spec(w_kv.shape), _const_spec((1, nkv))]
        operands += [g_kv.reshape(1, d), w_kv, b_kv.reshape(1, nkv)]
        out_shape = (jax.ShapeDtypeStruct((t, d), F32), jax.ShapeDtypeStruct((t, nkv), BF16))
        out_specs = (row(d), row(nkv))
    else:
        in_specs += [_const_spec((1, d))]
        operands += [g_fin.reshape(1, d)]
        out_shape = jax.ShapeDtypeStruct((t, d), F32)
        out_specs = row(d)
    return pl.pallas_call(
        functools.partial(_mlp_kernel, layer=layer, with_kv=with_kv),
        out_shape=out_shape,
        grid=(t // MLP_TM,),
        in_specs=in_specs,
        out_specs=out_specs,
        scratch_shapes=[pltpu.VMEM(w_in.shape[1:], F32),
                        pltpu.VMEM(w_out.shape[1:], F32),
                        pltpu.SemaphoreType.DMA((2, D_FF // MLP_FF_CHUNK))],
        compiler_params=pltpu.CompilerParams(
            dimension_semantics=("arbitrary",), vmem_limit_bytes=VMEM_LIMIT_BYTES),
        name="mlp_kv" if with_kv else "mlp_final",
    )(*operands)


def _pair_tile_sources(a, g):
    head_lo = (2 * a) * Q_PER_KV + g
    head_hi = (2 * a + 1) * Q_PER_KV + g
    return (head_lo // 2, head_lo % 2), (head_hi // 2, head_hi % 2)


def _attn_kernel(sink_ref, h_ref, kvp_ref, kvc_ref, g_ref, wq_ref, bq_ref, wo_ref, bo_ref,
                 o_ref, wq_s, bq_s, wo_s, q_s, oh_s, *, tq):
    nsb = tq // WINDOW
    n = pl.program_id(1)
    n_kv_tiles = N_KV_HEADS // 2
    lo_row = lax.broadcasted_iota(jnp.int32, (1, LANES), 1) < HEAD_DIM

    @pl.when((pl.program_id(0) == 0) & (n == 0))
    def _():
        for a in range(n_kv_tiles):
            for g in range(Q_PER_KV):
                (t_lo, h_lo), (t_hi, h_hi) = _pair_tile_sources(a, g)
                dst = slice((a * Q_PER_KV + g) * LANES, (a * Q_PER_KV + g + 1) * LANES)

                def pair(ref):
                    src_lo = ref[:, t_lo * LANES:(t_lo + 1) * LANES]
                    src_hi = ref[:, t_hi * LANES:(t_hi + 1) * LANES]
                    if h_lo == 1:
                        src_lo = pltpu.roll(src_lo, HEAD_DIM, axis=1)
                    if h_hi == 0:
                        src_hi = pltpu.roll(src_hi, HEAD_DIM, axis=1)
                    return jnp.where(lo_row, src_lo, src_hi)

                wq_s[:, dst] = pair(wq_ref).astype(BF16)
                bq_s[:, dst] = pair(bq_ref)
                for half in range(2):
                    head = (2 * a + half) * Q_PER_KV + g
                    r0 = (a * Q_PER_KV + g) * LANES + half * HEAD_DIM
                    wo_s[r0:r0 + HEAD_DIM, :] = (
                        wo_ref[head * HEAD_DIM:(head + 1) * HEAD_DIM, :].astype(BF16))

    h = h_ref[0]
    hn = _rmsnorm(h, g_ref[...]).astype(BF16)
    q = jnp.dot(hn, wq_s[...], preferred_element_type=F32) + bq_s[...]
    q_s[...] = (q * (LOG2_E / math.sqrt(HEAD_DIM))).astype(BF16)

    lo = lax.broadcasted_iota(jnp.int32, (WINDOW, LANES), 1) < HEAD_DIM
    own = (lax.broadcasted_iota(jnp.int32, (WINDOW, WINDOW), 1)
           <= lax.broadcasted_iota(jnp.int32, (WINDOW, WINDOW), 0))
    zero = jnp.zeros((WINDOW, LANES), BF16)
    fzero = jnp.zeros((WINDOW, WINDOW), F32)

    for sb in range(nsb):
        rows = slice(sb * WINDOW, (sb + 1) * WINDOW)
        prev_bias = jnp.where(n == 0, -jnp.inf, 0.0).astype(F32) if sb == 0 else None
        for a in range(n_kv_tiles):
            kl = slice(a * LANES, (a + 1) * LANES)
            vl = slice((n_kv_tiles + a) * LANES, (n_kv_tiles + a + 1) * LANES)
            if sb == 0:
                kprev, vprev = kvp_ref[0, :, kl], kvp_ref[0, :, vl]
            else:
                prow = slice((sb - 1) * WINDOW, sb * WINDOW)
                kprev, vprev = kvc_ref[0, prow, kl], kvc_ref[0, prow, vl]
            kd = jnp.concatenate([kprev, kvc_ref[0, rows, kl]], axis=0)
            vd = jnp.concatenate([vprev, kvc_ref[0, rows, vl]], axis=0)
            vd1 = jnp.concatenate([vd, jnp.ones((2 * WINDOW, LANES), BF16)], axis=1)
            qt = [q_s[rows, (a * Q_PER_KV + g) * LANES:(a * Q_PER_KV + g + 1) * LANES]
                  for g in range(Q_PER_KV)]
            lhs = jnp.concatenate([jnp.where(lo, t, zero) for t in qt]
                                  + [jnp.where(lo, zero, t) for t in qt], axis=0)
            s = lax.dot_general(lhs, kd, (((1,), (1,)), ((), ())),
                                preferred_element_type=F32)
            ps, sink_terms = [], []
            for half in range(2):
                for g in range(Q_PER_KV):
                    blk = half * Q_PER_KV + g
                    s_prev = s[blk * WINDOW:(blk + 1) * WINDOW, :WINDOW]
                    s_own = s[blk * WINDOW:(blk + 1) * WINDOW, WINDOW:]
                    if prev_bias is not None:
                        s_prev = s_prev + prev_bias
                    sg = jnp.where(own, s_own, s_prev)
                    sink = sink_ref[(2 * a + half) * Q_PER_KV + g] * LOG2_E
                    m = jnp.maximum(jnp.max(sg, axis=-1, keepdims=True), sink)
                    p = jnp.exp2(sg - m)
                    ps.append(jnp.concatenate([jnp.where(own, fzero, p).astype(BF16),
                                               jnp.where(own, p, fzero).astype(BF16)], axis=1))
                    sink_terms.append(jnp.exp2(sink - m))
            od = jnp.dot(jnp.concatenate(ps, axis=0), vd1, preferred_element_type=F32)
            for g in range(Q_PER_KV):
                r_lo = slice(g * WINDOW, (g + 1) * WINDOW)
                r_hi = slice((Q_PER_KV + g) * WINDOW, (Q_PER_KV + g + 1) * WINDOW)
                num = jnp.where(lo, od[r_lo, :LANES], od[r_hi, :LANES])
                den = (jnp.where(lo, od[r_lo, LANES:], od[r_hi, LANES:])
                       + jnp.where(lo, sink_terms[g], sink_terms[Q_PER_KV + g]))
                oh_s[rows, (a * Q_PER_KV + g) * LANES:(a * Q_PER_KV + g + 1) * LANES] = (
                    (num * (1.0 / den)).astype(BF16))

    o_ref[0] = h + jnp.dot(oh_s[...], wo_s[...], preferred_element_type=F32) + bo_ref[...]


def _attn_layer(h, kv, sinks, g, w_q, b_q, w_o, b_o):
    bsz, seq, d = h.shape
    nkv = kv.shape[-1]
    tq = ATTN_TQ
    nsb = tq // WINDOW
    kernel = functools.partial(_attn_kernel, tq=tq)
    return pl.pallas_call(
        kernel,
        out_shape=jax.ShapeDtypeStruct((bsz, seq, d), F32),
        grid=(bsz, seq // tq),
        in_specs=[
            pl.BlockSpec(memory_space=pltpu.SMEM),
            pl.BlockSpec((1, tq, d), lambda b, n: (b, n, 0)),
            pl.BlockSpec((1, WINDOW, nkv), lambda b, n: (b, jnp.maximum(n * nsb - 1, 0), 0)),
            pl.BlockSpec((1, tq, nkv), lambda b, n: (b, n, 0)),
            _const_spec((1, d)),
            _const_spec(w_q.shape),
            _const_spec((1, d)),
            _const_spec(w_o.shape),
            _const_spec((1, d)),
        ],
        out_specs=pl.BlockSpec((1, tq, d), lambda b, n: (b, n, 0)),
        scratch_shapes=[pltpu.VMEM((d, d), BF16),
                        pltpu.VMEM((1, d), F32),
                        pltpu.VMEM((d, d), BF16),
                        pltpu.VMEM((tq, d), BF16),
                        pltpu.VMEM((tq, d), BF16)],
        compiler_params=pltpu.CompilerParams(
            dimension_semantics=("arbitrary", "arbitrary"), vmem_limit_bytes=VMEM_LIMIT_BYTES),
        name="attn",
    )(sinks, h, kv, kv, g.reshape(1, d), w_q, b_q.reshape(1, d), w_o, b_o.reshape(1, d))


def kernel(x, norm_mix, norm_mlp, norm_kv, norm_final, s5_a_re, s5_a_im, s5_log_dt, s5_b_re, s5_b_im, s5_c_re, s5_c_im, s5_d, s5_w_glu, s5_b_glu, w_kv, b_kv, w_q, b_q, sinks, w_o, b_o, w_mlp_in, w_mlp_out):
    bsz, seq, d = x.shape

    w1, cx, lam2_re_t, lam2_im_t = _s5_params(s5_a_re[0], s5_a_im[0], s5_log_dt[0], s5_b_re[0],
                                              s5_b_im[0], s5_c_re[0], s5_c_im[0])
    h = _s5_layer(x, norm_mix[0], w1, lam2_re_t, lam2_im_t, cx, s5_d[0], s5_w_glu[0], s5_b_glu[0])

    h, kv = _mlp(h.reshape(bsz * seq, d), norm_mlp[0], w_mlp_in, w_mlp_out, 0,
                 g_kv=norm_kv, w_kv=w_kv, b_kv=b_kv)

    h = _attn_layer(h.reshape(bsz, seq, d), kv.reshape(bsz, seq, -1), sinks[0], norm_mix[1],
                    w_q[0], b_q[0], w_o[0], b_o[0])

    out = _mlp(h.reshape(bsz * seq, d), norm_mlp[1], w_mlp_in, w_mlp_out, 1, g_fin=norm_final)
    return out.reshape(bsz, seq, d)
```

```python
import functools
import math

import jax
import jax.numpy as jnp
from jax import lax
from jax.experimental import pallas as pl
from jax.experimental.pallas import tpu as pltpu

F32 = jnp.float32
BF16 = jnp.bfloat16

D_MODEL = 1024
S5_GROUP = 16
S5_STATE = 64
LAMBDA_RE_MAX = -1e-4
HEAD_DIM = 64
N_Q_HEADS = D_MODEL // HEAD_DIM
N_KV_HEADS = 4
Q_PER_KV = N_Q_HEADS // N_KV_HEADS
WINDOW = 128
D_FF = 4 * D_MODEL
NORM_EPS = 1e-5
LOG2_E = math.log2(math.e)

LANES = 128
SUBLANES = 8
N_LANE_TILES = D_MODEL // LANES
GROUPS_PER_TILE = LANES // S5_GROUP
STATES_PER_TILE = GROUPS_PER_TILE * S5_STATE
MIB = 1024 * 1024
S5_VMEM_LIMIT_BYTES = 36 * MIB
MLP_VMEM_LIMIT_BYTES = 56 * MIB
ATTN_VMEM_LIMIT_BYTES = 40 * MIB

S5_TL = 64
S5_PARAM_TILES_PER_STEP = 4
S5_TILES_IN_FLIGHT = 2
MLP_TM = 512
MLP_FF_CHUNK = 1024
ATTN_TQ = 1024


def _rmsnorm(x, g):
    return x * lax.rsqrt(jnp.mean(x * x, axis=-1, keepdims=True) + NORM_EPS) * g


def _gelu_tanh(x):
    c = math.sqrt(2.0 / math.pi)
    return 0.5 * x * (1.0 + jnp.tanh(c * (x + 0.044715 * (x * x * x))))


def _const_spec(shape):
    nd = len(shape)
    return pl.BlockSpec(shape, lambda *_: (0,) * nd, pipeline_mode=pl.Buffered(1))


def _s5_params_kernel(abl_ref, bt_ref, cre_ref, cim_ref, w1_ref, cx_ref, l2r_ref, l2i_ref):
    for tile in range(S5_PARAM_TILES_PER_STEP):
        _s5_params_tile(abl_ref.at[:, tile], bt_ref.at[:, tile], cre_ref.at[tile], cim_ref.at[tile],
                        w1_ref.at[tile], cx_ref.at[tile], l2r_ref.at[tile], l2i_ref.at[tile])


def _s5_params_tile(abl_ref, bt_ref, cre_ref, cim_ref, w1_ref, cx_ref, l2r_ref, l2i_ref):
    gpt, grp, nst = GROUPS_PER_TILE, S5_GROUP, S5_STATE
    ar = jnp.minimum(abl_ref[0], LAMBDA_RE_MAX)
    ai = abl_ref[1]
    dt = jnp.exp(abl_ref[2])
    mag = jnp.exp(ar * dt)
    ang = ai * dt
    lr = mag * jnp.cos(ang)
    li = mag * jnp.sin(ang)
    den = ar * ar + ai * ai
    cr = ((lr - 1.0) * ar + li * ai) / den
    ci = (li * ar - (lr - 1.0) * ai) / den
    l2r = lr * lr - li * li
    l2i = 2.0 * (lr * li)

    def per_row(v):
        return jnp.concatenate([jnp.broadcast_to(v[g:g + 1], (grp, nst)) for g in range(gpt)], axis=0)

    row_g = lax.broadcasted_iota(jnp.int32, (LANES, STATES_PER_TILE), 0) // grp
    col_g = lax.broadcasted_iota(jnp.int32, (LANES, STATES_PER_TILE), 1) // nst
    own = row_g == col_g

    def blockdiag(v):
        return jnp.where(own, jnp.concatenate([v] * gpt, axis=1), 0.0)

    lr_c, li_c, cr_c, ci_c = per_row(lr), per_row(li), per_row(cr), per_row(ci)
    l2r_c, l2i_c = per_row(l2r), per_row(l2i)
    btr, bti = bt_ref[0], bt_ref[1]
    bbr = cr_c * btr - ci_c * bti
    bbi = cr_c * bti + ci_c * btr
    wbr, wbi = blockdiag(bbr), blockdiag(bbi)
    w1r = blockdiag(lr_c * bbr - li_c * bbi)
    w1i = blockdiag(lr_c * bbi + li_c * bbr)
    ccr, cci = cre_ref[...], cim_ref[...]
    wcr, wci = blockdiag(ccr), blockdiag(cci)

    def hdot_nt(a, b):
        return lax.dot_general(a, b, (((1,), (1,)), ((), ())),
                               precision=lax.Precision.HIGHEST, preferred_element_type=F32)

    k0 = hdot_nt(wbr, wcr) - hdot_nt(wbi, wci)
    k1 = hdot_nt(w1r, wcr) - hdot_nt(w1i, wci)
    top = jnp.concatenate([k0, k1, w1r, w1i], axis=1)
    bot = jnp.concatenate([jnp.zeros_like(k0), k0, wbr, wbi], axis=1)
    w1_ref[...] = jnp.concatenate([top, bot], axis=0).astype(BF16)

    def readout_t(pr, pi):
        return jnp.concatenate([blockdiag(pr * ccr - pi * cci),
                                -blockdiag(pi * ccr + pr * cci)], axis=1)

    cx_t = jnp.concatenate([readout_t(lr_c, li_c), readout_t(l2r_c, l2i_c)], axis=0)
    cx_ref[...] = cx_t.T.astype(BF16)

    def flat_row(v):
        row = jnp.concatenate([v[g:g + 1] for g in range(gpt)], axis=1)
        return jnp.broadcast_to(row, (SUBLANES, STATES_PER_TILE))

    l2r_ref[...] = flat_row(l2r)
    l2i_ref[...] = flat_row(l2i)


def _s5_params(a_re, a_im, log_dt, b_re, b_im, c_re, c_im):
    nt, gpt, sp = N_LANE_TILES, GROUPS_PER_TILE, STATES_PER_TILE
    tps = S5_PARAM_TILES_PER_STEP
    abl = jnp.stack([a_re, a_im, jnp.broadcast_to(log_dt[:, None], a_re.shape)])
    abl = abl.reshape(3, nt, gpt, S5_STATE)
    bt = jnp.swapaxes(jnp.stack([b_re, b_im]), -1, -2).reshape(2, nt, LANES, S5_STATE)
    c_tile = lambda c: c.reshape(nt, LANES, S5_STATE)
    return pl.pallas_call(
        _s5_params_kernel,
        out_shape=(jax.ShapeDtypeStruct((nt, 2 * LANES, 2 * LANES + 2 * sp), BF16),
                   jax.ShapeDtypeStruct((nt, 2 * sp, 2 * LANES), BF16),
                   jax.ShapeDtypeStruct((nt, SUBLANES, sp), F32),
                   jax.ShapeDtypeStruct((nt, SUBLANES, sp), F32)),
        grid=(nt // tps,),
        in_specs=[pl.BlockSpec((3, tps, gpt, S5_STATE), lambda j: (0, j, 0, 0)),
                  pl.BlockSpec((2, tps, LANES, S5_STATE), lambda j: (0, j, 0, 0)),
                  pl.BlockSpec((tps, LANES, S5_STATE), lambda j: (j, 0, 0)),
                  pl.BlockSpec((tps, LANES, S5_STATE), lambda j: (j, 0, 0))],
        out_specs=(pl.BlockSpec((tps, 2 * LANES, 2 * LANES + 2 * sp), lambda j: (j, 0, 0)),
                   pl.BlockSpec((tps, 2 * sp, 2 * LANES), lambda j: (j, 0, 0)),
                   pl.BlockSpec((tps, SUBLANES, sp), lambda j: (j, 0, 0)),
                   pl.BlockSpec((tps, SUBLANES, sp), lambda j: (j, 0, 0))),
        compiler_params=pltpu.CompilerParams(dimension_semantics=("parallel",)),
        name="s5_params",
    )(abl, bt, c_tile(c_re), c_tile(c_im))


def _s5_kernel(x_hbm, g_ref, w1_ref, l2r_ref, l2i_ref, cx_ref, d_ref, wglu_ref, bglu_ref,
               o_hbm, xin_s, hn_s, hnu_s, zy_s, bu_s, xs_s, y_s, st_s, res_s, sem_in, sem_out,
               *, tl, n_steps):
    half = tl // 2
    prow = half * SUBLANES
    sp = STATES_PER_TILE
    i = pl.program_id(0)
    slot = i % 2

    def in_copies(step, sl):
        return [pltpu.make_async_copy(x_hbm.at[b, pl.ds(step * tl, tl), :],
                                      xin_s.at[sl, :, b, :], sem_in.at[sl, b])
                for b in range(SUBLANES)]

    def out_copies(step, sl):
        return [pltpu.make_async_copy(res_s.at[sl, :, b, :],
                                      o_hbm.at[b, pl.ds(step * tl, tl), :], sem_out.at[sl, b])
                for b in range(SUBLANES)]

    @pl.when(i == 0)
    def _():
        st_s[...] = jnp.zeros_like(st_s)
        for copy in in_copies(0, 0):
            copy.start()

    @pl.when(i + 1 < n_steps)
    def _():
        for copy in in_copies(i + 1, 1 - slot):
            copy.start()

    for copy in in_copies(i, slot):
        copy.wait()

    @pl.when(i >= 2)
    def _():
        for copy in out_copies(i - 2, slot):
            copy.wait()

    def row_group(v, t):
        return v[t * SUBLANES:(t + 1) * SUBLANES]

    hn = _rmsnorm(xin_s[slot].reshape(tl * SUBLANES, D_MODEL), g_ref[...])
    for t in range(tl):
        g = (t % 2) * half + t // 2
        hn_s[g * SUBLANES:(g + 1) * SUBLANES, :] = row_group(hn, t)
    for m in range(tl // 4):
        r16 = slice(2 * m * SUBLANES, (2 * m + 2) * SUBLANES)
        even = jnp.concatenate([row_group(hn, 4 * m), row_group(hn, 4 * m + 2)], axis=0).astype(BF16)
        odd = jnp.concatenate([row_group(hn, 4 * m + 1), row_group(hn, 4 * m + 3)], axis=0).astype(BF16)
        for j in range(N_LANE_TILES):
            lanes = slice(j * LANES, (j + 1) * LANES)
            hnu_s[r16, 2 * j * LANES:(2 * j + 1) * LANES] = even[:, lanes]
            hnu_s[r16, (2 * j + 1) * LANES:(2 * j + 2) * LANES] = odd[:, lanes]

    def project_in(j):
        z = jnp.dot(hnu_s[:, 2 * j * LANES:(2 * j + 2) * LANES], w1_ref[j],
                    preferred_element_type=F32)
        zy_s[j % S5_TILES_IN_FLIGHT] = z[:, :2 * LANES]
        bu_s[j % S5_TILES_IN_FLIGHT] = z[:, 2 * LANES:]

    ahead = S5_TILES_IN_FLIGHT - 1
    for j in range(ahead):
        project_in(j)
    for j in range(N_LANE_TILES):
        if j + ahead < N_LANE_TILES:
            project_in(j + ahead)
        p = j % S5_TILES_IN_FLIGHT
        lanes = slice(j * LANES, (j + 1) * LANES)
        ar = l2r_ref[j]
        ai = l2i_ref[j]
        xr = st_s[j, :, 0:sp]
        xi = st_s[j, :, sp:2 * sp]
        for k in range(half):
            r = slice(k * SUBLANES, (k + 1) * SUBLANES)
            xs_s[p, r, 0:sp] = xr
            xs_s[p, r, sp:2 * sp] = xi
            nxr = ar * xr - ai * xi + bu_s[p, r, 0:sp]
            nxi = ar * xi + ai * xr + bu_s[p, r, sp:2 * sp]
            xr, xi = nxr, nxi
        st_s[j, :, 0:sp] = xr
        st_s[j, :, sp:2 * sp] = xi
        y = zy_s[p] + jnp.dot(xs_s[p].astype(BF16), cx_ref[j], preferred_element_type=F32)
        y_s[0:prow, lanes] = y[:, :LANES]
        y_s[prow:2 * prow, lanes] = y[:, LANES:]

    for par in range(2):
        rs = slice(par * prow, (par + 1) * prow)
        y = y_s[rs, :] + d_ref[...] * hn_s[rs, :]
        z = jnp.dot(_gelu_tanh(y).astype(BF16), wglu_ref[...].astype(BF16),
                    preferred_element_type=F32) + bglu_ref[...]
        mix = z[:, :D_MODEL] * (1.0 / (1.0 + jnp.exp(-z[:, D_MODEL:])))
        for k in range(half):
            t = 2 * k + par
            res_s[slot, t] = xin_s[slot, t] + row_group(mix, k)

    for copy in out_copies(i, slot):
        copy.start()

    @pl.when(i == n_steps - 1)
    def _():
        for copy in out_copies(i - 1, 1 - slot) + out_copies(i, slot):
            copy.wait()


def _s5_layer(x, g_mix, w1, lam2_re_t, lam2_im_t, cx, d_skip, w_glu, b_glu):
    bsz, seq, d = x.shape
    tl = S5_TL
    n_steps = seq // tl
    assert bsz == SUBLANES and n_steps >= 2
    rows = tl * bsz
    prow = rows // 2
    kernel = functools.partial(_s5_kernel, tl=tl, n_steps=n_steps)
    hbm = pl.BlockSpec(memory_space=pl.ANY)
    return pl.pallas_call(
        kernel,
        out_shape=jax.ShapeDtypeStruct((bsz, seq, d), F32),
        grid=(n_steps,),
        in_specs=[
            hbm,
            _const_spec((1, d)),
            _const_spec(w1.shape),
            _const_spec(lam2_re_t.shape),
            _const_spec(lam2_im_t.shape),
            _const_spec(cx.shape),
            _const_spec((1, d)),
            _const_spec(w_glu.shape),
            _const_spec((1, 2 * d)),
        ],
        out_specs=hbm,
        scratch_shapes=[
            pltpu.VMEM((2, tl, bsz, d), F32),
            pltpu.VMEM((rows, d), F32),
            pltpu.VMEM((prow, 2 * d), BF16),
            pltpu.VMEM((S5_TILES_IN_FLIGHT, prow, 2 * LANES), F32),
            pltpu.VMEM((S5_TILES_IN_FLIGHT, prow, 2 * STATES_PER_TILE), F32),
            pltpu.VMEM((S5_TILES_IN_FLIGHT, prow, 2 * STATES_PER_TILE), F32),
            pltpu.VMEM((rows, d), F32),
            pltpu.VMEM((N_LANE_TILES, SUBLANES, 2 * STATES_PER_TILE), F32),
            pltpu.VMEM((2, tl, bsz, d), F32),
            pltpu.SemaphoreType.DMA((2, SUBLANES)),
            pltpu.SemaphoreType.DMA((2, SUBLANES)),
        ],
        compiler_params=pltpu.CompilerParams(
            dimension_semantics=("arbitrary",), vmem_limit_bytes=S5_VMEM_LIMIT_BYTES),
        name="s5_layer",
    )(x, g_mix.reshape(1, d), w1, lam2_re_t, lam2_im_t, cx, d_skip.reshape(1, d), w_glu,
      b_glu.reshape(1, 2 * d))


def _inv_rms(x):
    return lax.rsqrt(jnp.mean(x * x, axis=-1, keepdims=True) + NORM_EPS)


def _mlp_kernel(h_ref, g_ref, win_hbm, wout_hbm, *rest, layer, with_kv):
    if with_kv:
        gkv_ref, wkv_ref, bkv_ref, o_ref, kv_ref, win_s, wout_s, sem = rest
    else:
        gfin_ref, o_ref, win_s, wout_s, sem = rest
    n_chunks = D_FF // MLP_FF_CHUNK

    def weight_copies(c):
        cols = pl.ds(c * MLP_FF_CHUNK, MLP_FF_CHUNK)
        return (pltpu.make_async_copy(win_hbm.at[layer, :, cols], win_s.at[:, cols], sem.at[0, c]),
                pltpu.make_async_copy(wout_hbm.at[layer, cols, :], wout_s.at[cols, :], sem.at[1, c]))

    def body(first_step):
        h = h_ref[...]
        hg = (h * g_ref[...]).astype(BF16)
        r = _inv_rms(h)
        acts = []
        for c in range(n_chunks):
            cols = slice(c * MLP_FF_CHUNK, (c + 1) * MLP_FF_CHUNK)
            if first_step:
                for copy in weight_copies(c):
                    copy.wait()
            a = jnp.dot(hg, win_s[:, cols].astype(BF16), preferred_element_type=F32)
            acts.append(jnp.square(jnp.maximum(a, 0.0)).astype(BF16))
        acc = jnp.dot(jnp.concatenate(acts, axis=1), wout_s[...].astype(BF16),
                      preferred_element_type=F32)
        out = h + (r * r) * acc
        if with_kv:
            o_ref[...] = out
            kv = jnp.dot((out * gkv_ref[...]).astype(BF16), wkv_ref[...].astype(BF16),
                         preferred_element_type=F32)
            kv_ref[...] = (_inv_rms(out) * kv + bkv_ref[...]).astype(BF16)
        else:
            o_ref[...] = out * _inv_rms(out) * gfin_ref[...]

    @pl.when(pl.program_id(0) == 0)
    def _():
        for c in range(n_chunks):
            for copy in weight_copies(c):
                copy.start()
        body(True)

    @pl.when(pl.program_id(0) > 0)
    def _():
        body(False)


def _mlp(h, g, w_in, w_out, layer, *, g_kv=None, w_kv=None, b_kv=None, g_fin=None):
    t, d = h.shape
    with_kv = w_kv is not None
    row = lambda n: pl.BlockSpec((MLP_TM, n), lambda i: (i, 0))
    hbm = pl.BlockSpec(memory_space=pl.ANY)
    in_specs = [row(d), _const_spec((1, d)), hbm, hbm]
    operands = [h, g.reshape(1, d), w_in, w_out]
    if with_kv:
        nkv = w_kv.shape[1]
        in_specs += [_const_spec((1, d)), _const_spec(w_kv.shape), _const_spec((1, nkv))]
        operands += [g_kv.reshape(1, d), w_kv, b_kv.reshape(1, nkv)]
        out_shape = (jax.ShapeDtypeStruct((t, d), F32), jax.ShapeDtypeStruct((t, nkv), BF16))
        out_specs = (row(d), row(nkv))
    else:
        in_specs += [_const_spec((1, d))]
        operands += [g_fin.reshape(1, d)]
        out_shape = jax.ShapeDtypeStruct((t, d), F32)
        out_specs = row(d)
    return pl.pallas_call(
        functools.partial(_mlp_kernel, layer=layer, with_kv=with_kv),
        out_shape=out_shape,
        grid=(t // MLP_TM,),
        in_specs=in_specs,
        out_specs=out_specs,
        scratch_shapes=[pltpu.VMEM(w_in.shape[1:], F32),
                        pltpu.VMEM(w_out.shape[1:], F32),
                        pltpu.SemaphoreType.DMA((2, D_FF // MLP_FF_CHUNK))],
        compiler_params=pltpu.CompilerParams(
            dimension_semantics=("arbitrary",), vmem_limit_bytes=MLP_VMEM_LIMIT_BYTES),
        name="mlp_kv" if with_kv else "mlp_final",
    )(*operands)


def _pair_tile_sources(a, g):
    head_lo = (2 * a) * Q_PER_KV + g
    head_hi = (2 * a + 1) * Q_PER_KV + g
    return (head_lo // 2, head_lo % 2), (head_hi // 2, head_hi % 2)


def _attn_kernel(sink_ref, h_ref, kvp_ref, kvc_ref, g_ref, wq_ref, bq_ref, wo_ref, bo_ref,
                 o_ref, wq_s, bq_s, wo_s, q_s, oh_s, *, tq):
    nsb = tq // WINDOW
    n = pl.program_id(1)
    n_kv_tiles = N_KV_HEADS // 2
    lo_row = lax.broadcasted_iota(jnp.int32, (1, LANES), 1) < HEAD_DIM

    @pl.when((pl.program_id(0) == 0) & (n == 0))
    def _():
        for a in range(n_kv_tiles):
            for g in range(Q_PER_KV):
                (t_lo, h_lo), (t_hi, h_hi) = _pair_tile_sources(a, g)
                dst = slice((a * Q_PER_KV + g) * LANES, (a * Q_PER_KV + g + 1) * LANES)

                def pair(ref):
                    src_lo = ref[:, t_lo * LANES:(t_lo + 1) * LANES]
                    src_hi = ref[:, t_hi * LANES:(t_hi + 1) * LANES]
                    if h_lo == 1:
                        src_lo = pltpu.roll(src_lo, HEAD_DIM, axis=1)
                    if h_hi == 0:
                        src_hi = pltpu.roll(src_hi, HEAD_DIM, axis=1)
                    return jnp.where(lo_row, src_lo, src_hi)

                wq_s[:, dst] = pair(wq_ref).astype(BF16)
                bq_s[:, dst] = pair(bq_ref)
                for half in range(2):
                    head = (2 * a + half) * Q_PER_KV + g
                    r0 = (a * Q_PER_KV + g) * LANES + half * HEAD_DIM
                    wo_s[r0:r0 + HEAD_DIM, :] = (
                        wo_ref[head * HEAD_DIM:(head + 1) * HEAD_DIM, :].astype(BF16))

    h = h_ref[0]
    hn = _rmsnorm(h, g_ref[...]).astype(BF16)
    q = jnp.dot(hn, wq_s[...], preferred_element_type=F32) + bq_s[...]
    q_s[...] = (q * (LOG2_E / math.sqrt(HEAD_DIM))).astype(BF16)

    lo = lax.broadcasted_iota(jnp.int32, (WINDOW, LANES), 1) < HEAD_DIM
    own = (lax.broadcasted_iota(jnp.int32, (WINDOW, WINDOW), 1)
           <= lax.broadcasted_iota(jnp.int32, (WINDOW, WINDOW), 0))
    zero = jnp.zeros((WINDOW, LANES), BF16)
    fzero = jnp.zeros((WINDOW, WINDOW), F32)

    for sb in range(nsb):
        rows = slice(sb * WINDOW, (sb + 1) * WINDOW)
        prev_bias = jnp.where(n == 0, -jnp.inf, 0.0).astype(F32) if sb == 0 else None
        for a in range(n_kv_tiles):
            kl = slice(a * LANES, (a + 1) * LANES)
            vl = slice((n_kv_tiles + a) * LANES, (n_kv_tiles + a + 1) * LANES)
            if sb == 0:
                kprev, vprev = kvp_ref[0, :, kl], kvp_ref[0, :, vl]
            else:
                prow = slice((sb - 1) * WINDOW, sb * WINDOW)
                kprev, vprev = kvc_ref[0, prow, kl], kvc_ref[0, prow, vl]
            kd = jnp.concatenate([kprev, kvc_ref[0, rows, kl]], axis=0)
            vd = jnp.concatenate([vprev, kvc_ref[0, rows, vl]], axis=0)
            vd1 = jnp.concatenate([vd, jnp.ones((2 * WINDOW, LANES), BF16)], axis=1)
            qt = [q_s[rows, (a * Q_PER_KV + g) * LANES:(a * Q_PER_KV + g + 1) * LANES]
                  for g in range(Q_PER_KV)]
            lhs = jnp.concatenate([jnp.where(lo, t, zero) for t in qt]
                                  + [jnp.where(lo, zero, t) for t in qt], axis=0)
            s = lax.dot_general(lhs, kd, (((1,), (1,)), ((), ())),
                                preferred_element_type=F32)
            ps, sink_terms = [], []
            for half in range(2):
                for g in range(Q_PER_KV):
                    blk = half * Q_PER_KV + g
                    s_prev = s[blk * WINDOW:(blk + 1) * WINDOW, :WINDOW]
                    s_own = s[blk * WINDOW:(blk + 1) * WINDOW, WINDOW:]
                    if prev_bias is not None:
                        s_prev = s_prev + prev_bias
                    sg = jnp.where(own, s_own, s_prev)
                    sink = sink_ref[(2 * a + half) * Q_PER_KV + g] * LOG2_E
                    m = jnp.maximum(jnp.max(sg, axis=-1, keepdims=True), sink)
                    p = jnp.exp2(sg - m)
                    ps.append(jnp.concatenate([jnp.where(own, fzero, p).astype(BF16),
                                               jnp.where(own, p, fzero).astype(BF16)], axis=1))
                    sink_terms.append(jnp.exp2(sink - m))
            od = jnp.dot(jnp.concatenate(ps, axis=0), vd1, preferred_element_type=F32)
            for g in range(Q_PER_KV):
                r_lo = slice(g * WINDOW, (g + 1) * WINDOW)
                r_hi = slice((Q_PER_KV + g) * WINDOW, (Q_PER_KV + g + 1) * WINDOW)
                num = jnp.where(lo, od[r_lo, :LANES], od[r_hi, :LANES])
                den = (jnp.where(lo, od[r_lo, LANES:], od[r_hi, LANES:])
                       + jnp.where(lo, sink_terms[g], sink_terms[Q_PER_KV + g]))
                oh_s[rows, (a * Q_PER_KV + g) * LANES:(a * Q_PER_KV + g + 1) * LANES] = (
                    (num * (1.0 / den)).astype(BF16))

    o_ref[0] = h + jnp.dot(oh_s[...], wo_s[...], preferred_element_type=F32) + bo_ref[...]


def _attn_layer(h, kv, sinks, g, w_q, b_q, w_o, b_o):
    bsz, seq, d = h.shape
    nkv = kv.shape[-1]
    tq = ATTN_TQ
    nsb = tq // WINDOW
    kernel = functools.partial(_attn_kernel, tq=tq)
    return pl.pallas_call(
        kernel,
        out_shape=jax.ShapeDtypeStruct((bsz, seq, d), F32),
        grid=(bsz, seq // tq),
        in_specs=[
            pl.BlockSpec(memory_space=pltpu.SMEM),
            pl.BlockSpec((1, tq, d), lambda b, n: (b, n, 0)),
            pl.BlockSpec((1, WINDOW, nkv), lambda b, n: (b, jnp.maximum(n * nsb - 1, 0), 0)),
            pl.BlockSpec((1, tq, nkv), lambda b, n: (b, n, 0)),
            _const_spec((1, d)),
            _const_spec(w_q.shape),
            _const_spec((1, d)),
            _const_spec(w_o.shape),
            _const_spec((1, d)),
        ],
        out_specs=pl.BlockSpec((1, tq, d), lambda b, n: (b, n, 0)),
        scratch_shapes=[pltpu.VMEM((d, d), BF16),
                        pltpu.VMEM((1, d), F32),
                        pltpu.VMEM((d, d), BF16),
                        pltpu.VMEM((tq, d), BF16),
                        pltpu.VMEM((tq, d), BF16)],
        compiler_params=pltpu.CompilerParams(
            dimension_semantics=("arbitrary", "arbitrary"), vmem_limit_bytes=ATTN_VMEM_LIMIT_BYTES),
        name="attn",
    )(sinks, h, kv, kv, g.reshape(1, d), w_q, b_q.reshape(1, d), w_o, b_o.reshape(1, d))


def kernel(x, norm_mix, norm_mlp, norm_kv, norm_final, s5_a_re, s5_a_im, s5_log_dt, s5_b_re, s5_b_im, s5_c_re, s5_c_im, s5_d, s5_w_glu, s5_b_glu, w_kv, b_kv, w_q, b_q, sinks, w_o, b_o, w_mlp_in, w_mlp_out):
    bsz, seq, d = x.shape

    w1, cx, lam2_re_t, lam2_im_t = _s5_params(s5_a_re[0], s5_a_im[0], s5_log_dt[0], s5_b_re[0],
                                              s5_b_im[0], s5_c_re[0], s5_c_im[0])
    h = _s5_layer(x, norm_mix[0], w1, lam2_re_t, lam2_im_t, cx, s5_d[0], s5_w_glu[0], s5_b_glu[0])

    h, kv = _mlp(h.reshape(bsz * seq, d), norm_mlp[0], w_mlp_in, w_mlp_out, 0,
                 g_kv=norm_kv, w_kv=w_kv, b_kv=b_kv)

    h = _attn_layer(h.reshape(bsz, seq, d), kv.reshape(bsz, seq, -1), sinks[0], norm_mix[1],
                    w_q[0], b_q[0], w_o[0], b_o[0])

    out = _mlp(h.reshape(bsz * seq, d), norm_mlp[1], w_mlp_in, w_mlp_out, 1, g_fin=norm_final)
    return out.reshape(bsz, seq, d)
```

```python
import functools
import math

import jax
import jax.numpy as jnp
from jax import lax
from jax.experimental import pallas as pl
from jax.experimental.pallas import tpu as pltpu

F32 = jnp.float32
BF16 = jnp.bfloat16

D_MODEL = 1024
S5_GROUP = 16
S5_STATE = 64
LAMBDA_RE_MAX = -1e-4
HEAD_DIM = 64
N_Q_HEADS = D_MODEL // HEAD_DIM
N_KV_HEADS = 4
Q_PER_KV = N_Q_HEADS // N_KV_HEADS
WINDOW = 128
D_FF = 4 * D_MODEL
NORM_EPS = 1e-5
LOG2_E = math.log2(math.e)

LANES = 128
SUBLANES = 8
N_LANE_TILES = D_MODEL // LANES
GROUPS_PER_TILE = LANES // S5_GROUP
STATES_PER_TILE = GROUPS_PER_TILE * S5_STATE
MIB = 1024 * 1024
S5_VMEM_LIMIT_BYTES = 56 * MIB
MLP_VMEM_LIMIT_BYTES = 56 * MIB
ATTN_VMEM_LIMIT_BYTES = 40 * MIB

S5_TL = 64
S5_PARAM_TILES_PER_STEP = 4
S5_TILES_IN_FLIGHT = 2
MLP_TM = 512
MLP_FF_CHUNK = 1024
ATTN_TQ = 1024


def _rmsnorm(x, g):
    return x * lax.rsqrt(jnp.mean(x * x, axis=-1, keepdims=True) + NORM_EPS) * g


def _gelu_tanh(x):
    c = math.sqrt(2.0 / math.pi)
    return 0.5 * x * (1.0 + jnp.tanh(c * (x + 0.044715 * (x * x * x))))


def _const_spec(shape):
    nd = len(shape)
    return pl.BlockSpec(shape, lambda *_: (0,) * nd, pipeline_mode=pl.Buffered(1))


def _s5_params_kernel(abl_ref, bt_ref, cre_ref, cim_ref, w1_ref, cx_ref, l2r_ref, l2i_ref):
    for tile in range(S5_PARAM_TILES_PER_STEP):
        _s5_params_tile(abl_ref.at[:, tile], bt_ref.at[:, tile], cre_ref.at[tile], cim_ref.at[tile],
                        w1_ref.at[tile], cx_ref.at[tile], l2r_ref.at[tile], l2i_ref.at[tile])


def _s5_params_tile(abl_ref, bt_ref, cre_ref, cim_ref, w1_ref, cx_ref, l2r_ref, l2i_ref):
    gpt, grp, nst = GROUPS_PER_TILE, S5_GROUP, S5_STATE
    ar = jnp.minimum(abl_ref[0], LAMBDA_RE_MAX)
    ai = abl_ref[1]
    dt = jnp.exp(abl_ref[2])
    mag = jnp.exp(ar * dt)
    ang = ai * dt
    lr = mag * jnp.cos(ang)
    li = mag * jnp.sin(ang)
    den = ar * ar + ai * ai
    cr = ((lr - 1.0) * ar + li * ai) / den
    ci = (li * ar - (lr - 1.0) * ai) / den
    l2r = lr * lr - li * li
    l2i = 2.0 * (lr * li)

    def per_row(v):
        return jnp.concatenate([jnp.broadcast_to(v[g:g + 1], (grp, nst)) for g in range(gpt)], axis=0)

    row_g = lax.broadcasted_iota(jnp.int32, (LANES, STATES_PER_TILE), 0) // grp
    col_g = lax.broadcasted_iota(jnp.int32, (LANES, STATES_PER_TILE), 1) // nst
    own = row_g == col_g

    def blockdiag(v):
        return jnp.where(own, jnp.concatenate([v] * gpt, axis=1), 0.0)

    lr_c, li_c, cr_c, ci_c = per_row(lr), per_row(li), per_row(cr), per_row(ci)
    l2r_c, l2i_c = per_row(l2r), per_row(l2i)
    btr, bti = bt_ref[0], bt_ref[1]
    bbr = cr_c * btr - ci_c * bti
    bbi = cr_c * bti + ci_c * btr
    wbr, wbi = blockdiag(bbr), blockdiag(bbi)
    w1r = blockdiag(lr_c * bbr - li_c * bbi)
    w1i = blockdiag(lr_c * bbi + li_c * bbr)
    ccr, cci = cre_ref[...], cim_ref[...]
    wcr, wci = blockdiag(ccr), blockdiag(cci)

    def hdot_nt(a, b):
        return lax.dot_general(a, b, (((1,), (1,)), ((), ())),
                               precision=lax.Precision.HIGHEST, preferred_element_type=F32)

    k0 = hdot_nt(wbr, wcr) - hdot_nt(wbi, wci)
    k1 = hdot_nt(w1r, wcr) - hdot_nt(w1i, wci)
    top = jnp.concatenate([k0, k1, w1r, w1i], axis=1)
    bot = jnp.concatenate([jnp.zeros_like(k0), k0, wbr, wbi], axis=1)
    w1_ref[...] = jnp.concatenate([top, bot], axis=0).astype(BF16)

    def readout_t(pr, pi):
        return jnp.concatenate([blockdiag(pr * ccr - pi * cci),
                                -blockdiag(pi * ccr + pr * cci)], axis=1)

    cx_t = jnp.concatenate([readout_t(lr_c, li_c), readout_t(l2r_c, l2i_c)], axis=0)
    cx_ref[...] = cx_t.T.astype(BF16)

    def flat_row(v):
        row = jnp.concatenate([v[g:g + 1] for g in range(gpt)], axis=1)
        return jnp.broadcast_to(row, (SUBLANES, STATES_PER_TILE))

    l2r_ref[...] = flat_row(l2r)
    l2i_ref[...] = flat_row(l2i)


def _s5_params(a_re, a_im, log_dt, b_re, b_im, c_re, c_im):
    nt, gpt, sp = N_LANE_TILES, GROUPS_PER_TILE, STATES_PER_TILE
    tps = S5_PARAM_TILES_PER_STEP
    abl = jnp.stack([a_re, a_im, jnp.broadcast_to(log_dt[:, None], a_re.shape)])
    abl = abl.reshape(3, nt, gpt, S5_STATE)
    bt = jnp.swapaxes(jnp.stack([b_re, b_im]), -1, -2).reshape(2, nt, LANES, S5_STATE)
    c_tile = lambda c: c.reshape(nt, LANES, S5_STATE)
    return pl.pallas_call(
        _s5_params_kernel,
        out_shape=(jax.ShapeDtypeStruct((nt, 2 * LANES, 2 * LANES + 2 * sp), BF16),
                   jax.ShapeDtypeStruct((nt, 2 * sp, 2 * LANES), BF16),
                   jax.ShapeDtypeStruct((nt, SUBLANES, sp), F32),
                   jax.ShapeDtypeStruct((nt, SUBLANES, sp), F32)),
        grid=(nt // tps,),
        in_specs=[pl.BlockSpec((3, tps, gpt, S5_STATE), lambda j: (0, j, 0, 0)),
                  pl.BlockSpec((2, tps, LANES, S5_STATE), lambda j: (0, j, 0, 0)),
                  pl.BlockSpec((tps, LANES, S5_STATE), lambda j: (j, 0, 0)),
                  pl.BlockSpec((tps, LANES, S5_STATE), lambda j: (j, 0, 0))],
        out_specs=(pl.BlockSpec((tps, 2 * LANES, 2 * LANES + 2 * sp), lambda j: (j, 0, 0)),
                   pl.BlockSpec((tps, 2 * sp, 2 * LANES), lambda j: (j, 0, 0)),
                   pl.BlockSpec((tps, SUBLANES, sp), lambda j: (j, 0, 0)),
                   pl.BlockSpec((tps, SUBLANES, sp), lambda j: (j, 0, 0))),
        compiler_params=pltpu.CompilerParams(dimension_semantics=("parallel",)),
        name="s5_params",
    )(abl, bt, c_tile(c_re), c_tile(c_im))


def _s5_kernel(x_hbm, g_ref, w1_ref, l2r_ref, l2i_ref, cx_ref, d_ref, wglu_ref, bglu_ref,
               o_hbm, xin_s, hn_s, hnu_s, zy_s, bu_s, xs_s, y_s, st_s, res_s, sem_in, sem_out,
               *, tl, n_steps):
    half = tl // 2
    prow = half * SUBLANES
    sp = STATES_PER_TILE
    i = pl.program_id(0)
    slot = i % 2

    def in_copies(step, sl):
        return [pltpu.make_async_copy(x_hbm.at[b, pl.ds(step * tl, tl), :],
                                      xin_s.at[sl, :, b, :], sem_in.at[sl, b])
                for b in range(SUBLANES)]

    def out_copies(step, sl):
        return [pltpu.make_async_copy(res_s.at[sl, :, b, :],
                                      o_hbm.at[b, pl.ds(step * tl, tl), :], sem_out.at[sl, b])
                for b in range(SUBLANES)]

    @pl.when(i == 0)
    def _():
        st_s[...] = jnp.zeros_like(st_s)
        for copy in in_copies(0, 0):
            copy.start()

    @pl.when(i + 1 < n_steps)
    def _():
        for copy in in_copies(i + 1, 1 - slot):
            copy.start()

    for copy in in_copies(i, slot):
        copy.wait()

    @pl.when(i >= 2)
    def _():
        for copy in out_copies(i - 2, slot):
            copy.wait()

    def row_group(v, t):
        return v[t * SUBLANES:(t + 1) * SUBLANES]

    hn = _rmsnorm(xin_s[slot].reshape(tl * SUBLANES, D_MODEL), g_ref[...])
    for t in range(tl):
        g = (t % 2) * half + t // 2
        hn_s[g * SUBLANES:(g + 1) * SUBLANES, :] = row_group(hn, t)
    for m in range(tl // 4):
        r16 = slice(2 * m * SUBLANES, (2 * m + 2) * SUBLANES)
        even = jnp.concatenate([row_group(hn, 4 * m), row_group(hn, 4 * m + 2)], axis=0).astype(BF16)
        odd = jnp.concatenate([row_group(hn, 4 * m + 1), row_group(hn, 4 * m + 3)], axis=0).astype(BF16)
        for j in range(N_LANE_TILES):
            lanes = slice(j * LANES, (j + 1) * LANES)
            hnu_s[r16, 2 * j * LANES:(2 * j + 1) * LANES] = even[:, lanes]
            hnu_s[r16, (2 * j + 1) * LANES:(2 * j + 2) * LANES] = odd[:, lanes]

    def project_in(j):
        z = jnp.dot(hnu_s[:, 2 * j * LANES:(2 * j + 2) * LANES], w1_ref[j],
                    preferred_element_type=F32)
        zy_s[j % S5_TILES_IN_FLIGHT] = z[:, :2 * LANES]
        bu_s[j % S5_TILES_IN_FLIGHT] = z[:, 2 * LANES:]

    ahead = S5_TILES_IN_FLIGHT - 1
    for j in range(ahead):
        project_in(j)
    for j in range(N_LANE_TILES):
        if j + ahead < N_LANE_TILES:
            project_in(j + ahead)
        p = j % S5_TILES_IN_FLIGHT
        lanes = slice(j * LANES, (j + 1) * LANES)
        ar = l2r_ref[j]
        ai = l2i_ref[j]
        xr = st_s[j, :, 0:sp]
        xi = st_s[j, :, sp:2 * sp]
        for k in range(half):
            r = slice(k * SUBLANES, (k + 1) * SUBLANES)
            xs_s[p, r, 0:sp] = xr
            xs_s[p, r, sp:2 * sp] = xi
            nxr = ar * xr - ai * xi + bu_s[p, r, 0:sp]
            nxi = ar * xi + ai * xr + bu_s[p, r, sp:2 * sp]
            xr, xi = nxr, nxi
        st_s[j, :, 0:sp] = xr
        st_s[j, :, sp:2 * sp] = xi
        y = zy_s[p] + jnp.dot(xs_s[p].astype(BF16), cx_ref[j], preferred_element_type=F32)
        y_s[0:prow, lanes] = y[:, :LANES]
        y_s[prow:2 * prow, lanes] = y[:, LANES:]

    for par in range(2):
        rs = slice(par * prow, (par + 1) * prow)
        y = y_s[rs, :] + d_ref[...] * hn_s[rs, :]
        z = jnp.dot(_gelu_tanh(y).astype(BF16), wglu_ref[...].astype(BF16),
                    preferred_element_type=F32) + bglu_ref[...]
        mix = z[:, :D_MODEL] * (1.0 / (1.0 + jnp.exp(-z[:, D_MODEL:])))
        for k in range(half):
            t = 2 * k + par
            res_s[slot, t] = xin_s[slot, t] + row_group(mix, k)

    for copy in out_copies(i, slot):
        copy.start()

    @pl.when(i == n_steps - 1)
    def _():
        for copy in out_copies(i - 1, 1 - slot) + out_copies(i, slot):
            copy.wait()


def _s5_layer(x, g_mix, w1, lam2_re_t, lam2_im_t, cx, d_skip, w_glu, b_glu):
    bsz, seq, d = x.shape
    tl = S5_TL
    n_steps = seq // tl
    assert bsz == SUBLANES and n_steps >= 2
    rows = tl * bsz
    prow = rows // 2
    kernel = functools.partial(_s5_kernel, tl=tl, n_steps=n_steps)
    hbm = pl.BlockSpec(memory_space=pl.ANY)
    return pl.pallas_call(
        kernel,
        out_shape=jax.ShapeDtypeStruct((bsz, seq, d), F32),
        grid=(n_steps,),
        in_specs=[
            hbm,
            _const_spec((1, d)),
            _const_spec(w1.shape),
            _const_spec(lam2_re_t.shape),
            _const_spec(lam2_im_t.shape),
            _const_spec(cx.shape),
            _const_spec((1, d)),
            _const_spec(w_glu.shape),
            _const_spec((1, 2 * d)),
        ],
        out_specs=hbm,
        scratch_shapes=[
            pltpu.VMEM((2, tl, bsz, d), F32),
            pltpu.VMEM((rows, d), F32),
            pltpu.VMEM((prow, 2 * d), BF16),
            pltpu.VMEM((S5_TILES_IN_FLIGHT, prow, 2 * LANES), F32),
            pltpu.VMEM((S5_TILES_IN_FLIGHT, prow, 2 * STATES_PER_TILE), F32),
            pltpu.VMEM((S5_TILES_IN_FLIGHT, prow, 2 * STATES_PER_TILE), F32),
            pltpu.VMEM((rows, d), F32),
            pltpu.VMEM((N_LANE_TILES, SUBLANES, 2 * STATES_PER_TILE), F32),
            pltpu.VMEM((2, tl, bsz, d), F32),
            pltpu.SemaphoreType.DMA((2, SUBLANES)),
            pltpu.SemaphoreType.DMA((2, SUBLANES)),
        ],
        compiler_params=pltpu.CompilerParams(
            dimension_semantics=("arbitrary",), vmem_limit_bytes=S5_VMEM_LIMIT_BYTES),
        name="s5_layer",
    )(x, g_mix.reshape(1, d), w1, lam2_re_t, lam2_im_t, cx, d_skip.reshape(1, d), w_glu,
      b_glu.reshape(1, 2 * d))


def _inv_rms(x):
    return lax.rsqrt(jnp.mean(x * x, axis=-1, keepdims=True) + NORM_EPS)


def _mlp_kernel(h_ref, g_ref, win_hbm, wout_hbm, *rest, layer, with_kv):
    if with_kv:
        gkv_ref, wkv_ref, bkv_ref, o_ref, kv_ref, win_s, wout_s, sem = rest
    else:
        gfin_ref, o_ref, win_s, wout_s, sem = rest
    n_chunks = D_FF // MLP_FF_CHUNK

    def weight_copies(c):
        cols = pl.ds(c * MLP_FF_CHUNK, MLP_FF_CHUNK)
        return (pltpu.make_async_copy(win_hbm.at[layer, :, cols], win_s.at[:, cols], sem.at[0, c]),
                pltpu.make_async_copy(wout_hbm.at[layer, cols, :], wout_s.at[cols, :], sem.at[1, c]))

    def body(first_step):
        h = h_ref[...]
        hg = (h * g_ref[...]).astype(BF16)
        r = _inv_rms(h)
        acts = []
        for c in range(n_chunks):
            cols = slice(c * MLP_FF_CHUNK, (c + 1) * MLP_FF_CHUNK)
            if first_step:
                for copy in weight_copies(c):
                    copy.wait()
            a = jnp.dot(hg, win_s[:, cols].astype(BF16), preferred_element_type=F32)
            acts.append(jnp.square(jnp.maximum(a, 0.0)).astype(BF16))
        acc = jnp.dot(jnp.concatenate(acts, axis=1), wout_s[...].astype(BF16),
                      preferred_element_type=F32)
        out = h + (r * r) * acc
        if with_kv:
            o_ref[...] = out
            kv = jnp.dot((out * gkv_ref[...]).astype(BF16), wkv_ref[...].astype(BF16),
                         preferred_element_type=F32)
            kv_ref[...] = (_inv_rms(out) * kv + bkv_ref[...]).astype(BF16)
        else:
            o_ref[...] = out * _inv_rms(out) * gfin_ref[...]

    @pl.when(pl.program_id(0) == 0)
    def _():
        for c in range(n_chunks):
            for copy in weight_copies(c):
                copy.start()
        body(True)

    @pl.when(pl.program_id(0) > 0)
    def _():
        body(False)


def _mlp(h, g, w_in, w_out, layer, *, g_kv=None, w_kv=None, b_kv=None, g_fin=None):
    t, d = h.shape
    with_kv = w_kv is not None
    row = lambda n: pl.BlockSpec((MLP_TM, n), lambda i: (i, 0))
    hbm = pl.BlockSpec(memory_space=pl.ANY)
    in_specs = [row(d), _const_spec((1, d)), hbm, hbm]
    operands = [h, g.reshape(1, d), w_in, w_out]
    if with_kv:
        nkv = w_kv.shape[1]
        in_specs += [_const_spec((1, d)), _const_spec(w_kv.shape), _const_spec((1, nkv))]
        operands += [g_kv.reshape(1, d), w_kv, b_kv.reshape(1, nkv)]
        out_shape = (jax.ShapeDtypeStruct((t, d), F32), jax.ShapeDtypeStruct((t, nkv), BF16))
        out_specs = (row(d), row(nkv))
    else:
        in_specs += [_const_spec((1, d))]
        operands += [g_fin.reshape(1, d)]
        out_shape = jax.ShapeDtypeStruct((t, d), F32)
        out_specs = row(d)
    return pl.pallas_call(
        functools.partial(_mlp_kernel, layer=layer, with_kv=with_kv),
        out_shape=out_shape,
        grid=(t // MLP_TM,),
        in_specs=in_specs,
        out_specs=out_specs,
        scratch_shapes=[pltpu.VMEM(w_in.shape[1:], F32),
                        pltpu.VMEM(w_out.shape[1:], F32),
                        pltpu.SemaphoreType.DMA((2, D_FF // MLP_FF_CHUNK))],
        compiler_params=pltpu.CompilerParams(
            dimension_semantics=("arbitrary",), vmem_limit_bytes=MLP_VMEM_LIMIT_BYTES),
        name="mlp_kv" if with_kv else "mlp_final",
    )(*operands)


def _pair_tile_sources(a, g):
    head_lo = (2 * a) * Q_PER_KV + g
    head_hi = (2 * a + 1) * Q_PER_KV + g
    return (head_lo // 2, head_lo % 2), (head_hi // 2, head_hi % 2)


def _attn_kernel(sink_ref, h_ref, kvp_ref, kvc_ref, g_ref, wq_ref, bq_ref, wo_ref, bo_ref,
                 o_ref, wq_s, bq_s, wo_s, q_s, oh_s, *, tq):
    nsb = tq // WINDOW
    n = pl.program_id(1)
    n_kv_tiles = N_KV_HEADS // 2
    lo_row = lax.broadcasted_iota(jnp.int32, (1, LANES), 1) < HEAD_DIM

    @pl.when((pl.program_id(0) == 0) & (n == 0))
    def _():
        for a in range(n_kv_tiles):
            for g in range(Q_PER_KV):
                (t_lo, h_lo), (t_hi, h_hi) = _pair_tile_sources(a, g)
                dst = slice((a * Q_PER_KV + g) * LANES, (a * Q_PER_KV + g + 1) * LANES)

                def pair(ref):
                    src_lo = ref[:, t_lo * LANES:(t_lo + 1) * LANES]
                    src_hi = ref[:, t_hi * LANES:(t_hi + 1) * LANES]
                    if h_lo == 1:
                        src_lo = pltpu.roll(src_lo, HEAD_DIM, axis=1)
                    if h_hi == 0:
                        src_hi = pltpu.roll(src_hi, HEAD_DIM, axis=1)
                    return jnp.where(lo_row, src_lo, src_hi)

                wq_s[:, dst] = pair(wq_ref).astype(BF16)
                bq_s[:, dst] = pair(bq_ref)
                for half in range(2):
                    head = (2 * a + half) * Q_PER_KV + g
                    r0 = (a * Q_PER_KV + g) * LANES + half * HEAD_DIM
                    wo_s[r0:r0 + HEAD_DIM, :] = (
                        wo_ref[head * HEAD_DIM:(head + 1) * HEAD_DIM, :].astype(BF16))

    h = h_ref[0]
    hn = _rmsnorm(h, g_ref[...]).astype(BF16)
    q = jnp.dot(hn, wq_s[...], preferred_element_type=F32) + bq_s[...]
    q_s[...] = (q * (LOG2_E / math.sqrt(HEAD_DIM))).astype(BF16)

    lo = lax.broadcasted_iota(jnp.int32, (WINDOW, LANES), 1) < HEAD_DIM
    own = (lax.broadcasted_iota(jnp.int32, (WINDOW, WINDOW), 1)
           <= lax.broadcasted_iota(jnp.int32, (WINDOW, WINDOW), 0))
    zero = jnp.zeros((WINDOW, LANES), BF16)
    fzero = jnp.zeros((WINDOW, WINDOW), F32)

    for sb in range(nsb):
        rows = slice(sb * WINDOW, (sb + 1) * WINDOW)
        prev_bias = jnp.where(n == 0, -jnp.inf, 0.0).astype(F32) if sb == 0 else None
        for a in range(n_kv_tiles):
            kl = slice(a * LANES, (a + 1) * LANES)
            vl = slice((n_kv_tiles + a) * LANES, (n_kv_tiles + a + 1) * LANES)
            if sb == 0:
                kprev, vprev = kvp_ref[0, :, kl], kvp_ref[0, :, vl]
            else:
                prow = slice((sb - 1) * WINDOW, sb * WINDOW)
                kprev, vprev = kvc_ref[0, prow, kl], kvc_ref[0, prow, vl]
            kd = jnp.concatenate([kprev, kvc_ref[0, rows, kl]], axis=0)
            vd = jnp.concatenate([vprev, kvc_ref[0, rows, vl]], axis=0)
            vd1 = jnp.concatenate([vd, jnp.ones((2 * WINDOW, LANES), BF16)], axis=1)
            qt = [q_s[rows, (a * Q_PER_KV + g) * LANES:(a * Q_PER_KV + g + 1) * LANES]
                  for g in range(Q_PER_KV)]
            lhs = jnp.concatenate([jnp.where(lo, t, zero) for t in qt]
                                  + [jnp.where(lo, zero, t) for t in qt], axis=0)
            s = lax.dot_general(lhs, kd, (((1,), (1,)), ((), ())),
                                preferred_element_type=F32)
            ps, sink_terms = [], []
            for half in range(2):
                for g in range(Q_PER_KV):
                    blk = half * Q_PER_KV + g
                    s_prev = s[blk * WINDOW:(blk + 1) * WINDOW, :WINDOW]
                    s_own = s[blk * WINDOW:(blk + 1) * WINDOW, WINDOW:]
                    if prev_bias is not None:
                        s_prev = s_prev + prev_bias
                    sg = jnp.where(own, s_own, s_prev)
                    sink = sink_ref[(2 * a + half) * Q_PER_KV + g] * LOG2_E
                    m = jnp.maximum(jnp.max(sg, axis=-1, keepdims=True), sink)
                    p = jnp.exp2(sg - m)
                    ps.append(jnp.concatenate([jnp.where(own, fzero, p).astype(BF16),
                                               jnp.where(own, p, fzero).astype(BF16)], axis=1))
                    sink_terms.append(jnp.exp2(sink - m))
            od = jnp.dot(jnp.concatenate(ps, axis=0), vd1, preferred_element_type=F32)
            for g in range(Q_PER_KV):
                r_lo = slice(g * WINDOW, (g + 1) * WINDOW)
                r_hi = slice((Q_PER_KV + g) * WINDOW, (Q_PER_KV + g + 1) * WINDOW)
                num = jnp.where(lo, od[r_lo, :LANES], od[r_hi, :LANES])
                den = (jnp.where(lo, od[r_lo, LANES:], od[r_hi, LANES:])
                       + jnp.where(lo, sink_terms[g], sink_terms[Q_PER_KV + g]))
                oh_s[rows, (a * Q_PER_KV + g) * LANES:(a * Q_PER_KV + g + 1) * LANES] = (
                    (num * (1.0 / den)).astype(BF16))

    o_ref[0] = h + jnp.dot(oh_s[...], wo_s[...], preferred_element_type=F32) + bo_ref[...]


def _attn_layer(h, kv, sinks, g, w_q, b_q, w_o, b_o):
    bsz, seq, d = h.shape
    nkv = kv.shape[-1]
    tq = ATTN_TQ
    nsb = tq // WINDOW
    kernel = functools.partial(_attn_kernel, tq=tq)
    return pl.pallas_call(
        kernel,
        out_shape=jax.ShapeDtypeStruct((bsz, seq, d), F32),
        grid=(bsz, seq // tq),
        in_specs=[
            pl.BlockSpec(memory_space=pltpu.SMEM),
            pl.BlockSpec((1, tq, d), lambda b, n: (b, n, 0)),
            pl.BlockSpec((1, WINDOW, nkv), lambda b, n: (b, jnp.maximum(n * nsb - 1, 0), 0)),
            pl.BlockSpec((1, tq, nkv), lambda b, n: (b, n, 0)),
            _const_spec((1, d)),
            _const_spec(w_q.shape),
            _const_spec((1, d)),
            _const_spec(w_o.shape),
            _const_spec((1, d)),
        ],
        out_specs=pl.BlockSpec((1, tq, d), lambda b, n: (b, n, 0)),
        scratch_shapes=[pltpu.VMEM((d, d), BF16),
                        pltpu.VMEM((1, d), F32),
                        pltpu.VMEM((d, d), BF16),
                        pltpu.VMEM((tq, d), BF16),
                        pltpu.VMEM((tq, d), BF16)],
        compiler_params=pltpu.CompilerParams(
            dimension_semantics=("arbitrary", "arbitrary"), vmem_limit_bytes=ATTN_VMEM_LIMIT_BYTES),
        name="attn",
    )(sinks, h, kv, kv, g.reshape(1, d), w_q, b_q.reshape(1, d), w_o, b_o.reshape(1, d))


def kernel(x, norm_mix, norm_mlp, norm_kv, norm_final, s5_a_re, s5_a_im, s5_log_dt, s5_b_re, s5_b_im, s5_c_re, s5_c_im, s5_d, s5_w_glu, s5_b_glu, w_kv, b_kv, w_q, b_q, sinks, w_o, b_o, w_mlp_in, w_mlp_out):
    bsz, seq, d = x.shape

    w1, cx, lam2_re_t, lam2_im_t = _s5_params(s5_a_re[0], s5_a_im[0], s5_log_dt[0], s5_b_re[0],
                                              s5_b_im[0], s5_c_re[0], s5_c_im[0])
    h = _s5_layer(x, norm_mix[0], w1, lam2_re_t, lam2_im_t, cx, s5_d[0], s5_w_glu[0], s5_b_glu[0])

    h, kv = _mlp(h.reshape(bsz * seq, d), norm_mlp[0], w_mlp_in, w_mlp_out, 0,
                 g_kv=norm_kv, w_kv=w_kv, b_kv=b_kv)

    h = _attn_layer(h.reshape(bsz, seq, d), kv.reshape(bsz, seq, -1), sinks[0], norm_mix[1],
                    w_q[0], b_q[0], w_o[0], b_o[0])

    out = _mlp(h.reshape(bsz * seq, d), norm_mlp[1], w_mlp_in, w_mlp_out, 1, g_fin=norm_final)
    return out.reshape(bsz, seq, d)
```

```python
import functools
import math

import jax
import jax.numpy as jnp
from jax import lax
from jax.experimental import pallas as pl
from jax.experimental.pallas import tpu as pltpu

F32 = jnp.float32
BF16 = jnp.bfloat16

D_MODEL = 1024
S5_GROUP = 16
S5_STATE = 64
LAMBDA_RE_MAX = -1e-4
HEAD_DIM = 64
N_Q_HEADS = D_MODEL // HEAD_DIM
N_KV_HEADS = 4
Q_PER_KV = N_Q_HEADS // N_KV_HEADS
WINDOW = 128
D_FF = 4 * D_MODEL
NORM_EPS = 1e-5
LOG2_E = math.log2(math.e)

LANES = 128
SUBLANES = 8
N_LANE_TILES = D_MODEL // LANES
GROUPS_PER_TILE = LANES // S5_GROUP
STATES_PER_TILE = GROUPS_PER_TILE * S5_STATE
MIB = 1024 * 1024
S5_VMEM_LIMIT_BYTES = 56 * MIB
MLP_VMEM_LIMIT_BYTES = 52 * MIB
ATTN_VMEM_LIMIT_BYTES = 40 * MIB

S5_TL = 64
S5_PARAM_TILES_PER_STEP = 4
S5_TILES_IN_FLIGHT = 2
MLP_TM = 512
MLP_FF_CHUNK = 1024
ATTN_TQ = 1024


def _rmsnorm(x, g):
    return x * lax.rsqrt(jnp.mean(x * x, axis=-1, keepdims=True) + NORM_EPS) * g


def _gelu_tanh(x):
    c = math.sqrt(2.0 / math.pi)
    return 0.5 * x * (1.0 + jnp.tanh(c * (x + 0.044715 * (x * x * x))))


def _const_spec(shape):
    nd = len(shape)
    return pl.BlockSpec(shape, lambda *_: (0,) * nd, pipeline_mode=pl.Buffered(1))


def _s5_params_kernel(abl_ref, bt_ref, cre_ref, cim_ref, w1_ref, cx_ref, l2r_ref, l2i_ref):
    for tile in range(S5_PARAM_TILES_PER_STEP):
        _s5_params_tile(abl_ref.at[:, tile], bt_ref.at[:, tile], cre_ref.at[tile], cim_ref.at[tile],
                        w1_ref.at[tile], cx_ref.at[tile], l2r_ref.at[tile], l2i_ref.at[tile])


def _s5_params_tile(abl_ref, bt_ref, cre_ref, cim_ref, w1_ref, cx_ref, l2r_ref, l2i_ref):
    gpt, grp, nst = GROUPS_PER_TILE, S5_GROUP, S5_STATE
    ar = jnp.minimum(abl_ref[0], LAMBDA_RE_MAX)
    ai = abl_ref[1]
    dt = jnp.exp(abl_ref[2])
    mag = jnp.exp(ar * dt)
    ang = ai * dt
    lr = mag * jnp.cos(ang)
    li = mag * jnp.sin(ang)
    den = ar * ar + ai * ai
    cr = ((lr - 1.0) * ar + li * ai) / den
    ci = (li * ar - (lr - 1.0) * ai) / den
    l2r = lr * lr - li * li
    l2i = 2.0 * (lr * li)

    def per_row(v):
        return jnp.concatenate([jnp.broadcast_to(v[g:g + 1], (grp, nst)) for g in range(gpt)], axis=0)

    row_g = lax.broadcasted_iota(jnp.int32, (LANES, STATES_PER_TILE), 0) // grp
    col_g = lax.broadcasted_iota(jnp.int32, (LANES, STATES_PER_TILE), 1) // nst
    own = row_g == col_g

    def blockdiag(v):
        return jnp.where(own, jnp.concatenate([v] * gpt, axis=1), 0.0)

    lr_c, li_c, cr_c, ci_c = per_row(lr), per_row(li), per_row(cr), per_row(ci)
    l2r_c, l2i_c = per_row(l2r), per_row(l2i)
    btr, bti = bt_ref[0], bt_ref[1]
    bbr = cr_c * btr - ci_c * bti
    bbi = cr_c * bti + ci_c * btr
    wbr, wbi = blockdiag(bbr), blockdiag(bbi)
    w1r = blockdiag(lr_c * bbr - li_c * bbi)
    w1i = blockdiag(lr_c * bbi + li_c * bbr)
    ccr, cci = cre_ref[...], cim_ref[...]
    wcr, wci = blockdiag(ccr), blockdiag(cci)

    def hdot_nt(a, b):
        return lax.dot_general(a, b, (((1,), (1,)), ((), ())),
                               precision=lax.Precision.HIGHEST, preferred_element_type=F32)

    k0 = hdot_nt(wbr, wcr) - hdot_nt(wbi, wci)
    k1 = hdot_nt(w1r, wcr) - hdot_nt(w1i, wci)
    top = jnp.concatenate([k0, k1, w1r, w1i], axis=1)
    bot = jnp.concatenate([jnp.zeros_like(k0), k0, wbr, wbi], axis=1)
    w1_ref[...] = jnp.concatenate([top, bot], axis=0).astype(BF16)

    def readout_t(pr, pi):
        return jnp.concatenate([blockdiag(pr * ccr - pi * cci),
                                -blockdiag(pi * ccr + pr * cci)], axis=1)

    cx_t = jnp.concatenate([readout_t(lr_c, li_c), readout_t(l2r_c, l2i_c)], axis=0)
    cx_ref[...] = cx_t.T.astype(BF16)

    def flat_row(v):
        row = jnp.concatenate([v[g:g + 1] for g in range(gpt)], axis=1)
        return jnp.broadcast_to(row, (SUBLANES, STATES_PER_TILE))

    l2r_ref[...] = flat_row(l2r)
    l2i_ref[...] = flat_row(l2i)


def _s5_params(a_re, a_im, log_dt, b_re, b_im, c_re, c_im):
    nt, gpt, sp = N_LANE_TILES, GROUPS_PER_TILE, STATES_PER_TILE
    tps = S5_PARAM_TILES_PER_STEP
    abl = jnp.stack([a_re, a_im, jnp.broadcast_to(log_dt[:, None], a_re.shape)])
    abl = abl.reshape(3, nt, gpt, S5_STATE)
    bt = jnp.swapaxes(jnp.stack([b_re, b_im]), -1, -2).reshape(2, nt, LANES, S5_STATE)
    c_tile = lambda c: c.reshape(nt, LANES, S5_STATE)
    return pl.pallas_call(
        _s5_params_kernel,
        out_shape=(jax.ShapeDtypeStruct((nt, 2 * LANES, 2 * LANES + 2 * sp), BF16),
                   jax.ShapeDtypeStruct((nt, 2 * sp, 2 * LANES), BF16),
                   jax.ShapeDtypeStruct((nt, SUBLANES, sp), F32),
                   jax.ShapeDtypeStruct((nt, SUBLANES, sp), F32)),
        grid=(nt // tps,),
        in_specs=[pl.BlockSpec((3, tps, gpt, S5_STATE), lambda j: (0, j, 0, 0)),
                  pl.BlockSpec((2, tps, LANES, S5_STATE), lambda j: (0, j, 0, 0)),
                  pl.BlockSpec((tps, LANES, S5_STATE), lambda j: (j, 0, 0)),
                  pl.BlockSpec((tps, LANES, S5_STATE), lambda j: (j, 0, 0))],
        out_specs=(pl.BlockSpec((tps, 2 * LANES, 2 * LANES + 2 * sp), lambda j: (j, 0, 0)),
                   pl.BlockSpec((tps, 2 * sp, 2 * LANES), lambda j: (j, 0, 0)),
                   pl.BlockSpec((tps, SUBLANES, sp), lambda j: (j, 0, 0)),
                   pl.BlockSpec((tps, SUBLANES, sp), lambda j: (j, 0, 0))),
        compiler_params=pltpu.CompilerParams(dimension_semantics=("parallel",)),
        name="s5_params",
    )(abl, bt, c_tile(c_re), c_tile(c_im))


def _s5_kernel(x_hbm, g_ref, w1_ref, l2r_ref, l2i_ref, cx_ref, d_ref, wglu_ref, bglu_ref,
               o_hbm, xin_s, hn_s, hnu_s, zy_s, bu_s, xs_s, y_s, st_s, res_s, sem_in, sem_out,
               *, tl, n_steps):
    half = tl // 2
    prow = half * SUBLANES
    sp = STATES_PER_TILE
    i = pl.program_id(0)
    slot = i % 2

    def in_copies(step, sl):
        return [pltpu.make_async_copy(x_hbm.at[b, pl.ds(step * tl, tl), :],
                                      xin_s.at[sl, :, b, :], sem_in.at[sl, b])
                for b in range(SUBLANES)]

    def out_copies(step, sl):
        return [pltpu.make_async_copy(res_s.at[sl, :, b, :],
                                      o_hbm.at[b, pl.ds(step * tl, tl), :], sem_out.at[sl, b])
                for b in range(SUBLANES)]

    @pl.when(i == 0)
    def _():
        st_s[...] = jnp.zeros_like(st_s)
        for copy in in_copies(0, 0):
            copy.start()

    @pl.when(i + 1 < n_steps)
    def _():
        for copy in in_copies(i + 1, 1 - slot):
            copy.start()

    for copy in in_copies(i, slot):
        copy.wait()

    @pl.when(i >= 2)
    def _():
        for copy in out_copies(i - 2, slot):
            copy.wait()

    def row_group(v, t):
        return v[t * SUBLANES:(t + 1) * SUBLANES]

    hn = _rmsnorm(xin_s[slot].reshape(tl * SUBLANES, D_MODEL), g_ref[...])
    for t in range(tl):
        g = (t % 2) * half + t // 2
        hn_s[g * SUBLANES:(g + 1) * SUBLANES, :] = row_group(hn, t)
    for m in range(tl // 4):
        r16 = slice(2 * m * SUBLANES, (2 * m + 2) * SUBLANES)
        even = jnp.concatenate([row_group(hn, 4 * m), row_group(hn, 4 * m + 2)], axis=0).astype(BF16)
        odd = jnp.concatenate([row_group(hn, 4 * m + 1), row_group(hn, 4 * m + 3)], axis=0).astype(BF16)
        for j in range(N_LANE_TILES):
            lanes = slice(j * LANES, (j + 1) * LANES)
            hnu_s[r16, 2 * j * LANES:(2 * j + 1) * LANES] = even[:, lanes]
            hnu_s[r16, (2 * j + 1) * LANES:(2 * j + 2) * LANES] = odd[:, lanes]

    def project_in(j):
        z = jnp.dot(hnu_s[:, 2 * j * LANES:(2 * j + 2) * LANES], w1_ref[j],
                    preferred_element_type=F32)
        zy_s[j % S5_TILES_IN_FLIGHT] = z[:, :2 * LANES]
        bu_s[j % S5_TILES_IN_FLIGHT] = z[:, 2 * LANES:]

    ahead = S5_TILES_IN_FLIGHT - 1
    for j in range(ahead):
        project_in(j)
    for j in range(N_LANE_TILES):
        if j + ahead < N_LANE_TILES:
            project_in(j + ahead)
        p = j % S5_TILES_IN_FLIGHT
        lanes = slice(j * LANES, (j + 1) * LANES)
        ar = l2r_ref[j]
        ai = l2i_ref[j]
        xr = st_s[j, :, 0:sp]
        xi = st_s[j, :, sp:2 * sp]
        for k in range(half):
            r = slice(k * SUBLANES, (k + 1) * SUBLANES)
            xs_s[p, r, 0:sp] = xr
            xs_s[p, r, sp:2 * sp] = xi
            nxr = ar * xr - ai * xi + bu_s[p, r, 0:sp]
            nxi = ar * xi + ai * xr + bu_s[p, r, sp:2 * sp]
            xr, xi = nxr, nxi
        st_s[j, :, 0:sp] = xr
        st_s[j, :, sp:2 * sp] = xi
        y = zy_s[p] + jnp.dot(xs_s[p].astype(BF16), cx_ref[j], preferred_element_type=F32)
        y_s[0:prow, lanes] = y[:, :LANES]
        y_s[prow:2 * prow, lanes] = y[:, LANES:]

    for par in range(2):
        rs = slice(par * prow, (par + 1) * prow)
        y = y_s[rs, :] + d_ref[...] * hn_s[rs, :]
        z = jnp.dot(_gelu_tanh(y).astype(BF16), wglu_ref[...].astype(BF16),
                    preferred_element_type=F32) + bglu_ref[...]
        mix = z[:, :D_MODEL] * (1.0 / (1.0 + jnp.exp(-z[:, D_MODEL:])))
        for k in range(half):
            t = 2 * k + par
            res_s[slot, t] = xin_s[slot, t] + row_group(mix, k)

    for copy in out_copies(i, slot):
        copy.start()

    @pl.when(i == n_steps - 1)
    def _():
        for copy in out_copies(i - 1, 1 - slot) + out_copies(i, slot):
            copy.wait()


def _s5_layer(x, g_mix, w1, lam2_re_t, lam2_im_t, cx, d_skip, w_glu, b_glu):
    bsz, seq, d = x.shape
    tl = S5_TL
    n_steps = seq // tl
    assert bsz == SUBLANES and n_steps >= 2
    rows = tl * bsz
    prow = rows // 2
    kernel = functools.partial(_s5_kernel, tl=tl, n_steps=n_steps)
    hbm = pl.BlockSpec(memory_space=pl.ANY)
    return pl.pallas_call(
        kernel,
        out_shape=jax.ShapeDtypeStruct((bsz, seq, d), F32),
        grid=(n_steps,),
        in_specs=[
            hbm,
            _const_spec((1, d)),
            _const_spec(w1.shape),
            _const_spec(lam2_re_t.shape),
            _const_spec(lam2_im_t.shape),
            _const_spec(cx.shape),
            _const_spec((1, d)),
            _const_spec(w_glu.shape),
            _const_spec((1, 2 * d)),
        ],
        out_specs=hbm,
        scratch_shapes=[
            pltpu.VMEM((2, tl, bsz, d), F32),
            pltpu.VMEM((rows, d), F32),
            pltpu.VMEM((prow, 2 * d), BF16),
            pltpu.VMEM((S5_TILES_IN_FLIGHT, prow, 2 * LANES), F32),
            pltpu.VMEM((S5_TILES_IN_FLIGHT, prow, 2 * STATES_PER_TILE), F32),
            pltpu.VMEM((S5_TILES_IN_FLIGHT, prow, 2 * STATES_PER_TILE), F32),
            pltpu.VMEM((rows, d), F32),
            pltpu.VMEM((N_LANE_TILES, SUBLANES, 2 * STATES_PER_TILE), F32),
            pltpu.VMEM((2, tl, bsz, d), F32),
            pltpu.SemaphoreType.DMA((2, SUBLANES)),
            pltpu.SemaphoreType.DMA((2, SUBLANES)),
        ],
        compiler_params=pltpu.CompilerParams(
            dimension_semantics=("arbitrary",), vmem_limit_bytes=S5_VMEM_LIMIT_BYTES),
        name="s5_layer",
    )(x, g_mix.reshape(1, d), w1, lam2_re_t, lam2_im_t, cx, d_skip.reshape(1, d), w_glu,
      b_glu.reshape(1, 2 * d))


def _inv_rms(x):
    return lax.rsqrt(jnp.mean(x * x, axis=-1, keepdims=True) + NORM_EPS)


def _mlp_kernel(h_ref, g_ref, win_hbm, wout_hbm, *rest, layer, with_kv):
    if with_kv:
        gkv_ref, wkv_ref, bkv_ref, o_ref, kv_ref, win_s, wout_s, sem = rest
    else:
        gfin_ref, o_ref, win_s, wout_s, sem = rest
    n_chunks = D_FF // MLP_FF_CHUNK

    def weight_copies(c):
        cols = pl.ds(c * MLP_FF_CHUNK, MLP_FF_CHUNK)
        return (pltpu.make_async_copy(win_hbm.at[layer, :, cols], win_s.at[:, cols], sem.at[0, c]),
                pltpu.make_async_copy(wout_hbm.at[layer, cols, :], wout_s.at[cols, :], sem.at[1, c]))

    def body(first_step):
        h = h_ref[...]
        hg = (h * g_ref[...]).astype(BF16)
        r = _inv_rms(h)
        acts = []
        for c in range(n_chunks):
            cols = slice(c * MLP_FF_CHUNK, (c + 1) * MLP_FF_CHUNK)
            if first_step:
                for copy in weight_copies(c):
                    copy.wait()
            a = jnp.dot(hg, win_s[:, cols].astype(BF16), preferred_element_type=F32)
            acts.append(jnp.square(jnp.maximum(a, 0.0)).astype(BF16))
        acc = jnp.dot(jnp.concatenate(acts, axis=1), wout_s[...].astype(BF16),
                      preferred_element_type=F32)
        out = h + (r * r) * acc
        if with_kv:
            o_ref[...] = out
            kv = jnp.dot((out * gkv_ref[...]).astype(BF16), wkv_ref[...].astype(BF16),
                         preferred_element_type=F32)
            kv_ref[...] = (_inv_rms(out) * kv + bkv_ref[...]).astype(BF16)
        else:
            o_ref[...] = out * _inv_rms(out) * gfin_ref[...]

    @pl.when(pl.program_id(0) == 0)
    def _():
        for c in range(n_chunks):
            for copy in weight_copies(c):
                copy.start()
        body(True)

    @pl.when(pl.program_id(0) > 0)
    def _():
        body(False)


def _mlp(h, g, w_in, w_out, layer, *, g_kv=None, w_kv=None, b_kv=None, g_fin=None):
    t, d = h.shape
    with_kv = w_kv is not None
    row = lambda n: pl.BlockSpec((MLP_TM, n), lambda i: (i, 0))
    hbm = pl.BlockSpec(memory_space=pl.ANY)
    in_specs = [row(d), _const_spec((1, d)), hbm, hbm]
    operands = [h, g.reshape(1, d), w_in, w_out]
    if with_kv:
        nkv = w_kv.shape[1]
        in_specs += [_const_spec((1, d)), _const_spec(w_kv.shape), _const_spec((1, nkv))]
        operands += [g_kv.reshape(1, d), w_kv, b_kv.reshape(1, nkv)]
        out_shape = (jax.ShapeDtypeStruct((t, d), F32), jax.ShapeDtypeStruct((t, nkv), BF16))
        out_specs = (row(d), row(nkv))
    else:
        in_specs += [_const_spec((1, d))]
        operands += [g_fin.reshape(1, d)]
        out_shape = jax.ShapeDtypeStruct((t, d), F32)
        out_specs = row(d)
    return pl.pallas_call(
        functools.partial(_mlp_kernel, layer=layer, with_kv=with_kv),
        out_shape=out_shape,
        grid=(t // MLP_TM,),
        in_specs=in_specs,
        out_specs=out_specs,
        scratch_shapes=[pltpu.VMEM(w_in.shape[1:], F32),
                        pltpu.VMEM(w_out.shape[1:], F32),
                        pltpu.SemaphoreType.DMA((2, D_FF // MLP_FF_CHUNK))],
        compiler_params=pltpu.CompilerParams(
            dimension_semantics=("arbitrary",), vmem_limit_bytes=MLP_VMEM_LIMIT_BYTES),
        name="mlp_kv" if with_kv else "mlp_final",
    )(*operands)


def _pair_tile_sources(a, g):
    head_lo = (2 * a) * Q_PER_KV + g
    head_hi = (2 * a + 1) * Q_PER_KV + g
    return (head_lo // 2, head_lo % 2), (head_hi // 2, head_hi % 2)


def _attn_kernel(sink_ref, h_ref, kvp_ref, kvc_ref, g_ref, wq_ref, bq_ref, wo_ref, bo_ref,
                 o_ref, wq_s, bq_s, wo_s, q_s, oh_s, *, tq):
    nsb = tq // WINDOW
    n = pl.program_id(1)
    n_kv_tiles = N_KV_HEADS // 2
    lo_row = lax.broadcasted_iota(jnp.int32, (1, LANES), 1) < HEAD_DIM

    @pl.when((pl.program_id(0) == 0) & (n == 0))
    def _():
        for a in range(n_kv_tiles):
            for g in range(Q_PER_KV):
                (t_lo, h_lo), (t_hi, h_hi) = _pair_tile_sources(a, g)
                dst = slice((a * Q_PER_KV + g) * LANES, (a * Q_PER_KV + g + 1) * LANES)

                def pair(ref):
                    src_lo = ref[:, t_lo * LANES:(t_lo + 1) * LANES]
                    src_hi = ref[:, t_hi * LANES:(t_hi + 1) * LANES]
                    if h_lo == 1:
                        src_lo = pltpu.roll(src_lo, HEAD_DIM, axis=1)
                    if h_hi == 0:
                        src_hi = pltpu.roll(src_hi, HEAD_DIM, axis=1)
                    return jnp.where(lo_row, src_lo, src_hi)

                wq_s[:, dst] = pair(wq_ref).astype(BF16)
                bq_s[:, dst] = pair(bq_ref)
                for half in range(2):
                    head = (2 * a + half) * Q_PER_KV + g
                    r0 = (a * Q_PER_KV + g) * LANES + half * HEAD_DIM
                    wo_s[r0:r0 + HEAD_DIM, :] = (
                        wo_ref[head * HEAD_DIM:(head + 1) * HEAD_DIM, :].astype(BF16))

    h = h_ref[0]
    hn = _rmsnorm(h, g_ref[...]).astype(BF16)
    q = jnp.dot(hn, wq_s[...], preferred_element_type=F32) + bq_s[...]
    q_s[...] = (q * (LOG2_E / math.sqrt(HEAD_DIM))).astype(BF16)

    lo = lax.broadcasted_iota(jnp.int32, (WINDOW, LANES), 1) < HEAD_DIM
    own = (lax.broadcasted_iota(jnp.int32, (WINDOW, WINDOW), 1)
           <= lax.broadcasted_iota(jnp.int32, (WINDOW, WINDOW), 0))
    zero = jnp.zeros((WINDOW, LANES), BF16)
    fzero = jnp.zeros((WINDOW, WINDOW), F32)

    for sb in range(nsb):
        rows = slice(sb * WINDOW, (sb + 1) * WINDOW)
        prev_bias = jnp.where(n == 0, -jnp.inf, 0.0).astype(F32) if sb == 0 else None
        for a in range(n_kv_tiles):
            kl = slice(a * LANES, (a + 1) * LANES)
            vl = slice((n_kv_tiles + a) * LANES, (n_kv_tiles + a + 1) * LANES)
            if sb == 0:
                kprev, vprev = kvp_ref[0, :, kl], kvp_ref[0, :, vl]
            else:
                prow = slice((sb - 1) * WINDOW, sb * WINDOW)
                kprev, vprev = kvc_ref[0, prow, kl], kvc_ref[0, prow, vl]
            kd = jnp.concatenate([kprev, kvc_ref[0, rows, kl]], axis=0)
            vd = jnp.concatenate([vprev, kvc_ref[0, rows, vl]], axis=0)
            vd1 = jnp.concatenate([vd, jnp.ones((2 * WINDOW, LANES), BF16)], axis=1)
            qt = [q_s[rows, (a * Q_PER_KV + g) * LANES:(a * Q_PER_KV + g + 1) * LANES]
                  for g in range(Q_PER_KV)]
            lhs = jnp.concatenate([jnp.where(lo, t, zero) for t in qt]
                                  + [jnp.where(lo, zero, t) for t in qt], axis=0)
            s = lax.dot_general(lhs, kd, (((1,), (1,)), ((), ())),
                                preferred_element_type=F32)
            ps, sink_terms = [], []
            for half in range(2):
                for g in range(Q_PER_KV):
                    blk = half * Q_PER_KV + g
                    s_prev = s[blk * WINDOW:(blk + 1) * WINDOW, :WINDOW]
                    s_own = s[blk * WINDOW:(blk + 1) * WINDOW, WINDOW:]
                    if prev_bias is not None:
                        s_prev = s_prev + prev_bias
                    sg = jnp.where(own, s_own, s_prev)
                    sink = sink_ref[(2 * a + half) * Q_PER_KV + g] * LOG2_E
                    m = jnp.maximum(jnp.max(sg, axis=-1, keepdims=True), sink)
                    p = jnp.exp2(sg - m)
                    ps.append(jnp.concatenate([jnp.where(own, fzero, p).astype(BF16),
                                               jnp.where(own, p, fzero).astype(BF16)], axis=1))
                    sink_terms.append(jnp.exp2(sink - m))
            od = jnp.dot(jnp.concatenate(ps, axis=0), vd1, preferred_element_type=F32)
            for g in range(Q_PER_KV):
                r_lo = slice(g * WINDOW, (g + 1) * WINDOW)
                r_hi = slice((Q_PER_KV + g) * WINDOW, (Q_PER_KV + g + 1) * WINDOW)
                num = jnp.where(lo, od[r_lo, :LANES], od[r_hi, :LANES])
                den = (jnp.where(lo, od[r_lo, LANES:], od[r_hi, LANES:])
                       + jnp.where(lo, sink_terms[g], sink_terms[Q_PER_KV + g]))
                oh_s[rows, (a * Q_PER_KV + g) * LANES:(a * Q_PER_KV + g + 1) * LANES] = (
                    (num * (1.0 / den)).astype(BF16))

    o_ref[0] = h + jnp.dot(oh_s[...], wo_s[...], preferred_element_type=F32) + bo_ref[...]


def _attn_layer(h, kv, sinks, g, w_q, b_q, w_o, b_o):
    bsz, seq, d = h.shape
    nkv = kv.shape[-1]
    tq = ATTN_TQ
    nsb = tq // WINDOW
    kernel = functools.partial(_attn_kernel, tq=tq)
    return pl.pallas_call(
        kernel,
        out_shape=jax.ShapeDtypeStruct((bsz, seq, d), F32),
        grid=(bsz, seq // tq),
        in_specs=[
            pl.BlockSpec(memory_space=pltpu.SMEM),
            pl.BlockSpec((1, tq, d), lambda b, n: (b, n, 0)),
            pl.BlockSpec((1, WINDOW, nkv), lambda b, n: (b, jnp.maximum(n * nsb - 1, 0), 0)),
            pl.BlockSpec((1, tq, nkv), lambda b, n: (b, n, 0)),
            _const_spec((1, d)),
            _const_spec(w_q.shape),
            _const_spec((1, d)),
            _const_spec(w_o.shape),
            _const_spec((1, d)),
        ],
        out_specs=pl.BlockSpec((1, tq, d), lambda b, n: (b, n, 0)),
        scratch_shapes=[pltpu.VMEM((d, d), BF16),
                        pltpu.VMEM((1, d), F32),
                        pltpu.VMEM((d, d), BF16),
                        pltpu.VMEM((tq, d), BF16),
                        pltpu.VMEM((tq, d), BF16)],
        compiler_params=pltpu.CompilerParams(
            dimension_semantics=("arbitrary", "arbitrary"), vmem_limit_bytes=ATTN_VMEM_LIMIT_BYTES),
        name="attn",
    )(sinks, h, kv, kv, g.reshape(1, d), w_q, b_q.reshape(1, d), w_o, b_o.reshape(1, d))


def kernel(x, norm_mix, norm_mlp, norm_kv, norm_final, s5_a_re, s5_a_im, s5_log_dt, s5_b_re, s5_b_im, s5_c_re, s5_c_im, s5_d, s5_w_glu, s5_b_glu, w_kv, b_kv, w_q, b_q, sinks, w_o, b_o, w_mlp_in, w_mlp_out):
    bsz, seq, d = x.shape

    w1, cx, lam2_re_t, lam2_im_t = _s5_params(s5_a_re[0], s5_a_im[0], s5_log_dt[0], s5_b_re[0],
                                              s5_b_im[0], s5_c_re[0], s5_c_im[0])
    h = _s5_layer(x, norm_mix[0], w1, lam2_re_t, lam2_im_t, cx, s5_d[0], s5_w_glu[0], s5_b_glu[0])

    h, kv = _mlp(h.reshape(bsz * seq, d), norm_mlp[0], w_mlp_in, w_mlp_out, 0,
                 g_kv=norm_kv, w_kv=w_kv, b_kv=b_kv)

    h = _attn_layer(h.reshape(bsz, seq, d), kv.reshape(bsz, seq, -1), sinks[0], norm_mix[1],
                    w_q[0], b_q[0], w_o[0], b_o[0])

    out = _mlp(h.reshape(bsz * seq, d), norm_mlp[1], w_mlp_in, w_mlp_out, 1, g_fin=norm_final)
    return out.reshape(bsz, seq, d)
```

```python
import functools
import math

import jax
import jax.numpy as jnp
from jax import lax
from jax.experimental import pallas as pl
from jax.experimental.pallas import tpu as pltpu

F32 = jnp.float32
BF16 = jnp.bfloat16

D_MODEL = 1024
S5_GROUP = 16
S5_STATE = 64
LAMBDA_RE_MAX = -1e-4
HEAD_DIM = 64
N_Q_HEADS = D_MODEL // HEAD_DIM
N_KV_HEADS = 4
Q_PER_KV = N_Q_HEADS // N_KV_HEADS
WINDOW = 128
D_FF = 4 * D_MODEL
NORM_EPS = 1e-5
LOG2_E = math.log2(math.e)

LANES = 128
SUBLANES = 8
N_LANE_TILES = D_MODEL // LANES
GROUPS_PER_TILE = LANES // S5_GROUP
STATES_PER_TILE = GROUPS_PER_TILE * S5_STATE
MIB = 1024 * 1024
S5_VMEM_LIMIT_BYTES = 56 * MIB
MLP_VMEM_LIMIT_BYTES = 52 * MIB
ATTN_VMEM_LIMIT_BYTES = 40 * MIB

S5_TL = 64
S5_PARAM_TILES_PER_STEP = 4
S5_TILES_IN_FLIGHT = 2
MLP_TM = 512
MLP_FF_CHUNK = 1024
ATTN_TQ = 1024


def _rmsnorm(x, g):
    return x * lax.rsqrt(jnp.mean(x * x, axis=-1, keepdims=True) + NORM_EPS) * g


def _gelu_tanh(x):
    c = math.sqrt(2.0 / math.pi)
    return 0.5 * x * (1.0 + jnp.tanh(c * (x + 0.044715 * (x * x * x))))


def _const_spec(shape):
    nd = len(shape)
    return pl.BlockSpec(shape, lambda *_: (0,) * nd, pipeline_mode=pl.Buffered(1))


def _s5_params_kernel(abl_ref, bt_ref, cre_ref, cim_ref, w1_ref, cx_ref, l2r_ref, l2i_ref):
    for tile in range(S5_PARAM_TILES_PER_STEP):
        _s5_params_tile(abl_ref.at[:, tile], bt_ref.at[:, tile], cre_ref.at[tile], cim_ref.at[tile],
                        w1_ref.at[tile], cx_ref.at[tile], l2r_ref.at[tile], l2i_ref.at[tile])


def _s5_params_tile(abl_ref, bt_ref, cre_ref, cim_ref, w1_ref, cx_ref, l2r_ref, l2i_ref):
    gpt, grp, nst = GROUPS_PER_TILE, S5_GROUP, S5_STATE
    ar = jnp.minimum(abl_ref[0], LAMBDA_RE_MAX)
    ai = abl_ref[1]
    dt = jnp.exp(abl_ref[2])
    mag = jnp.exp(ar * dt)
    ang = ai * dt
    lr = mag * jnp.cos(ang)
    li = mag * jnp.sin(ang)
    den = ar * ar + ai * ai
    cr = ((lr - 1.0) * ar + li * ai) / den
    ci = (li * ar - (lr - 1.0) * ai) / den
    l2r = lr * lr - li * li
    l2i = 2.0 * (lr * li)

    def per_row(v):
        return jnp.concatenate([jnp.broadcast_to(v[g:g + 1], (grp, nst)) for g in range(gpt)], axis=0)

    row_g = lax.broadcasted_iota(jnp.int32, (LANES, STATES_PER_TILE), 0) // grp
    col_g = lax.broadcasted_iota(jnp.int32, (LANES, STATES_PER_TILE), 1) // nst
    own = row_g == col_g

    def blockdiag(v):
        return jnp.where(own, jnp.concatenate([v] * gpt, axis=1), 0.0)

    lr_c, li_c, cr_c, ci_c = per_row(lr), per_row(li), per_row(cr), per_row(ci)
    l2r_c, l2i_c = per_row(l2r), per_row(l2i)
    btr, bti = bt_ref[0], bt_ref[1]
    bbr = cr_c * btr - ci_c * bti
    bbi = cr_c * bti + ci_c * btr
    wbr, wbi = blockdiag(bbr), blockdiag(bbi)
    w1r = blockdiag(lr_c * bbr - li_c * bbi)
    w1i = blockdiag(lr_c * bbi + li_c * bbr)
    ccr, cci = cre_ref[...], cim_ref[...]
    wcr, wci = blockdiag(ccr), blockdiag(cci)

    def hdot_nt(a, b):
        return lax.dot_general(a, b, (((1,), (1,)), ((), ())),
                               precision=lax.Precision.HIGHEST, preferred_element_type=F32)

    k0 = hdot_nt(wbr, wcr) - hdot_nt(wbi, wci)
    k1 = hdot_nt(w1r, wcr) - hdot_nt(w1i, wci)
    top = jnp.concatenate([k0, k1, w1r, w1i], axis=1)
    bot = jnp.concatenate([jnp.zeros_like(k0), k0, wbr, wbi], axis=1)
    w1_ref[...] = jnp.concatenate([top, bot], axis=0).astype(BF16)

    def readout_t(pr, pi):
        return jnp.concatenate([blockdiag(pr * ccr - pi * cci),
                                -blockdiag(pi * ccr + pr * cci)], axis=1)

    cx_t = jnp.concatenate([readout_t(lr_c, li_c), readout_t(l2r_c, l2i_c)], axis=0)
    cx_ref[...] = cx_t.T.astype(BF16)

    def flat_row(v):
        row = jnp.concatenate([v[g:g + 1] for g in range(gpt)], axis=1)
        return jnp.broadcast_to(row, (SUBLANES, STATES_PER_TILE))

    l2r_ref[...] = flat_row(l2r)
    l2i_ref[...] = flat_row(l2i)


def _s5_params(a_re, a_im, log_dt, b_re, b_im, c_re, c_im):
    nt, gpt, sp = N_LANE_TILES, GROUPS_PER_TILE, STATES_PER_TILE
    tps = S5_PARAM_TILES_PER_STEP
    abl = jnp.stack([a_re, a_im, jnp.broadcast_to(log_dt[:, None], a_re.shape)])
    abl = abl.reshape(3, nt, gpt, S5_STATE)
    bt = jnp.swapaxes(jnp.stack([b_re, b_im]), -1, -2).reshape(2, nt, LANES, S5_STATE)
    c_tile = lambda c: c.reshape(nt, LANES, S5_STATE)
    return pl.pallas_call(
        _s5_params_kernel,
        out_shape=(jax.ShapeDtypeStruct((nt, 2 * LANES, 2 * LANES + 2 * sp), BF16),
                   jax.ShapeDtypeStruct((nt, 2 * sp, 2 * LANES), BF16),
                   jax.ShapeDtypeStruct((nt, SUBLANES, sp), F32),
                   jax.ShapeDtypeStruct((nt, SUBLANES, sp), F32)),
        grid=(nt // tps,),
        in_specs=[pl.BlockSpec((3, tps, gpt, S5_STATE), lambda j: (0, j, 0, 0)),
                  pl.BlockSpec((2, tps, LANES, S5_STATE), lambda j: (0, j, 0, 0)),
                  pl.BlockSpec((tps, LANES, S5_STATE), lambda j: (j, 0, 0)),
                  pl.BlockSpec((tps, LANES, S5_STATE), lambda j: (j, 0, 0))],
        out_specs=(pl.BlockSpec((tps, 2 * LANES, 2 * LANES + 2 * sp), lambda j: (j, 0, 0)),
                   pl.BlockSpec((tps, 2 * sp, 2 * LANES), lambda j: (j, 0, 0)),
                   pl.BlockSpec((tps, SUBLANES, sp), lambda j: (j, 0, 0)),
                   pl.BlockSpec((tps, SUBLANES, sp), lambda j: (j, 0, 0))),
        compiler_params=pltpu.CompilerParams(dimension_semantics=("parallel",)),
        name="s5_params",
    )(abl, bt, c_tile(c_re), c_tile(c_im))


def _s5_kernel(x_hbm, g_ref, w1_hbm, l2r_ref, l2i_ref, cx_hbm, d_ref, wglu_hbm, bglu_ref,
               o_hbm, xin_s, hn_s, hnu_s, zy_s, bu_s, xs_s, y_s, st_s, res_s, w1_s, cx_s, wglu_s,
               sem_in, sem_out, sem_w, *, tl, n_steps):
    half = tl // 2
    prow = half * SUBLANES
    sp = STATES_PER_TILE
    i = pl.program_id(0)
    slot = i % 2

    def in_copies(step, sl):
        return [pltpu.make_async_copy(x_hbm.at[b, pl.ds(step * tl, tl), :],
                                      xin_s.at[sl, :, b, :], sem_in.at[sl, b])
                for b in range(SUBLANES)]

    def out_copies(step, sl):
        return [pltpu.make_async_copy(res_s.at[sl, :, b, :],
                                      o_hbm.at[b, pl.ds(step * tl, tl), :], sem_out.at[sl, b])
                for b in range(SUBLANES)]

    in_slot = i % 3

    @pl.when(i == 0)
    def _():
        weight_copies = [pltpu.make_async_copy(src, dst, sem_w.at[k]) for k, (src, dst) in
                         enumerate(((w1_hbm, w1_s), (cx_hbm, cx_s), (wglu_hbm, wglu_s)))]
        for copy in in_copies(0, 0) + weight_copies + in_copies(1, 1):
            copy.start()
        st_s[...] = jnp.zeros_like(st_s)
        for copy in weight_copies:
            copy.wait()

    @pl.when(i + 2 < n_steps)
    def _():
        for copy in in_copies(i + 2, (i + 2) % 3):
            copy.start()

    for copy in in_copies(i, in_slot):
        copy.wait()

    @pl.when(i >= 2)
    def _():
        for copy in out_copies(i - 2, slot):
            copy.wait()

    def row_group(v, t):
        return v[t * SUBLANES:(t + 1) * SUBLANES]

    hn = _rmsnorm(xin_s[in_slot].reshape(tl * SUBLANES, D_MODEL), g_ref[...])
    for t in range(tl):
        g = (t % 2) * half + t // 2
        hn_s[g * SUBLANES:(g + 1) * SUBLANES, :] = row_group(hn, t)
    for m in range(tl // 4):
        r16 = slice(2 * m * SUBLANES, (2 * m + 2) * SUBLANES)
        even = jnp.concatenate([row_group(hn, 4 * m), row_group(hn, 4 * m + 2)], axis=0).astype(BF16)
        odd = jnp.concatenate([row_group(hn, 4 * m + 1), row_group(hn, 4 * m + 3)], axis=0).astype(BF16)
        for j in range(N_LANE_TILES):
            lanes = slice(j * LANES, (j + 1) * LANES)
            hnu_s[r16, 2 * j * LANES:(2 * j + 1) * LANES] = even[:, lanes]
            hnu_s[r16, (2 * j + 1) * LANES:(2 * j + 2) * LANES] = odd[:, lanes]

    def project_in(j):
        z = jnp.dot(hnu_s[:, 2 * j * LANES:(2 * j + 2) * LANES], w1_s[j],
                    preferred_element_type=F32)
        zy_s[j % S5_TILES_IN_FLIGHT] = z[:, :2 * LANES]
        bu_s[j % S5_TILES_IN_FLIGHT] = z[:, 2 * LANES:]

    ahead = S5_TILES_IN_FLIGHT - 1
    for j in range(ahead):
        project_in(j)
    for j in range(N_LANE_TILES):
        if j + ahead < N_LANE_TILES:
            project_in(j + ahead)
        p = j % S5_TILES_IN_FLIGHT
        lanes = slice(j * LANES, (j + 1) * LANES)
        ar = l2r_ref[j]
        ai = l2i_ref[j]
        xr = st_s[j, :, 0:sp]
        xi = st_s[j, :, sp:2 * sp]
        for k in range(half):
            r = slice(k * SUBLANES, (k + 1) * SUBLANES)
            xs_s[p, r, 0:sp] = xr
            xs_s[p, r, sp:2 * sp] = xi
            nxr = ar * xr - ai * xi + bu_s[p, r, 0:sp]
            nxi = ar * xi + ai * xr + bu_s[p, r, sp:2 * sp]
            xr, xi = nxr, nxi
        st_s[j, :, 0:sp] = xr
        st_s[j, :, sp:2 * sp] = xi
        y = zy_s[p] + jnp.dot(xs_s[p].astype(BF16), cx_s[j], preferred_element_type=F32)
        y_s[0:prow, lanes] = y[:, :LANES]
        y_s[prow:2 * prow, lanes] = y[:, LANES:]

    for par in range(2):
        rs = slice(par * prow, (par + 1) * prow)
        y = y_s[rs, :] + d_ref[...] * hn_s[rs, :]
        z = jnp.dot(_gelu_tanh(y).astype(BF16), wglu_s[...].astype(BF16),
                    preferred_element_type=F32) + bglu_ref[...]
        mix = z[:, :D_MODEL] * (1.0 / (1.0 + jnp.exp(-z[:, D_MODEL:])))
        for k in range(half):
            t = 2 * k + par
            res_s[slot, t] = xin_s[in_slot, t] + row_group(mix, k)

    for copy in out_copies(i, slot):
        copy.start()

    @pl.when(i == n_steps - 1)
    def _():
        for copy in out_copies(i - 1, 1 - slot) + out_copies(i, slot):
            copy.wait()


def _s5_layer(x, g_mix, w1, lam2_re_t, lam2_im_t, cx, d_skip, w_glu, b_glu):
    bsz, seq, d = x.shape
    tl = S5_TL
    n_steps = seq // tl
    assert bsz == SUBLANES and n_steps >= 3
    rows = tl * bsz
    prow = rows // 2
    kernel = functools.partial(_s5_kernel, tl=tl, n_steps=n_steps)
    hbm = pl.BlockSpec(memory_space=pl.ANY)
    return pl.pallas_call(
        kernel,
        out_shape=jax.ShapeDtypeStruct((bsz, seq, d), F32),
        grid=(n_steps,),
        in_specs=[
            hbm,
            _const_spec((1, d)),
            hbm,
            _const_spec(lam2_re_t.shape),
            _const_spec(lam2_im_t.shape),
            hbm,
            _const_spec((1, d)),
            hbm,
            _const_spec((1, 2 * d)),
        ],
        out_specs=hbm,
        scratch_shapes=[
            pltpu.VMEM((3, tl, bsz, d), F32),
            pltpu.VMEM((rows, d), F32),
            pltpu.VMEM((prow, 2 * d), BF16),
            pltpu.VMEM((S5_TILES_IN_FLIGHT, prow, 2 * LANES), F32),
            pltpu.VMEM((S5_TILES_IN_FLIGHT, prow, 2 * STATES_PER_TILE), F32),
            pltpu.VMEM((S5_TILES_IN_FLIGHT, prow, 2 * STATES_PER_TILE), F32),
            pltpu.VMEM((rows, d), F32),
            pltpu.VMEM((N_LANE_TILES, SUBLANES, 2 * STATES_PER_TILE), F32),
            pltpu.VMEM((2, tl, bsz, d), F32),
            pltpu.VMEM(w1.shape, w1.dtype),
            pltpu.VMEM(cx.shape, cx.dtype),
            pltpu.VMEM(w_glu.shape, w_glu.dtype),
            pltpu.SemaphoreType.DMA((3, SUBLANES)),
            pltpu.SemaphoreType.DMA((2, SUBLANES)),
            pltpu.SemaphoreType.DMA((3,)),
        ],
        compiler_params=pltpu.CompilerParams(
            dimension_semantics=("arbitrary",), vmem_limit_bytes=S5_VMEM_LIMIT_BYTES),
        name="s5_layer",
    )(x, g_mix.reshape(1, d), w1, lam2_re_t, lam2_im_t, cx, d_skip.reshape(1, d), w_glu,
      b_glu.reshape(1, 2 * d))


def _inv_rms(x):
    return lax.rsqrt(jnp.mean(x * x, axis=-1, keepdims=True) + NORM_EPS)


def _mlp_kernel(h_ref, g_ref, win_hbm, wout_hbm, *rest, layer, with_kv):
    if with_kv:
        gkv_ref, wkv_ref, bkv_ref, o_ref, kv_ref, win_s, wout_s, sem = rest
    else:
        gfin_ref, o_ref, win_s, wout_s, sem = rest
    n_chunks = D_FF // MLP_FF_CHUNK

    def weight_copies(c):
        cols = pl.ds(c * MLP_FF_CHUNK, MLP_FF_CHUNK)
        return (pltpu.make_async_copy(win_hbm.at[layer, :, cols], win_s.at[:, cols], sem.at[0, c]),
                pltpu.make_async_copy(wout_hbm.at[layer, cols, :], wout_s.at[cols, :], sem.at[1, c]))

    def body(first_step):
        h = h_ref[...]
        hg = (h * g_ref[...]).astype(BF16)
        r = _inv_rms(h)
        acts = []
        for c in range(n_chunks):
            cols = slice(c * MLP_FF_CHUNK, (c + 1) * MLP_FF_CHUNK)
            if first_step:
                for copy in weight_copies(c):
                    copy.wait()
            a = jnp.dot(hg, win_s[:, cols].astype(BF16), preferred_element_type=F32)
            acts.append(jnp.square(jnp.maximum(a, 0.0)).astype(BF16))
        acc = jnp.dot(jnp.concatenate(acts, axis=1), wout_s[...].astype(BF16),
                      preferred_element_type=F32)
        out = h + (r * r) * acc
        if with_kv:
            o_ref[...] = out
            kv = jnp.dot((out * gkv_ref[...]).astype(BF16), wkv_ref[...].astype(BF16),
                         preferred_element_type=F32)
            kv_ref[...] = (_inv_rms(out) * kv + bkv_ref[...]).astype(BF16)
        else:
            o_ref[...] = out * _inv_rms(out) * gfin_ref[...]

    @pl.when(pl.program_id(0) == 0)
    def _():
        for matrix in range(2):
            for c in range(n_chunks):
                weight_copies(c)[matrix].start()
        body(True)

    @pl.when(pl.program_id(0) > 0)
    def _():
        body(False)


def _mlp(h, g, w_in, w_out, layer, *, g_kv=None, w_kv=None, b_kv=None, g_fin=None):
    t, d = h.shape
    with_kv = w_kv is not None
    row = lambda n: pl.BlockSpec((MLP_TM, n), lambda i: (i, 0))
    hbm = pl.BlockSpec(memory_space=pl.ANY)
    in_specs = [row(d), _const_spec((1, d)), hbm, hbm]
    operands = [h, g.reshape(1, d), w_in, w_out]
    if with_kv:
        nkv = w_kv.shape[1]
        in_specs += [_const_spec((1, d)), _const_spec(w_kv.shape), _const_spec((1, nkv))]
        operands += [g_kv.reshape(1, d), w_kv, b_kv.reshape(1, nkv)]
        out_shape = (jax.ShapeDtypeStruct((t, d), F32), jax.ShapeDtypeStruct((t, nkv), BF16))
        out_specs = (row(d), row(nkv))
    else:
        in_specs += [_const_spec((1, d))]
        operands += [g_fin.reshape(1, d)]
        out_shape = jax.ShapeDtypeStruct((t, d), F32)
        out_specs = row(d)
    return pl.pallas_call(
        functools.partial(_mlp_kernel, layer=layer, with_kv=with_kv),
        out_shape=out_shape,
        grid=(t // MLP_TM,),
        in_specs=in_specs,
        out_specs=out_specs,
        scratch_shapes=[pltpu.VMEM(w_in.shape[1:], F32),
                        pltpu.VMEM(w_out.shape[1:], F32),
                        pltpu.SemaphoreType.DMA((2, D_FF // MLP_FF_CHUNK))],
        compiler_params=pltpu.CompilerParams(
            dimension_semantics=("arbitrary",), vmem_limit_bytes=MLP_VMEM_LIMIT_BYTES),
        name="mlp_kv" if with_kv else "mlp_final",
    )(*operands)


def _pair_tile_sources(a, g):
    head_lo = (2 * a) * Q_PER_KV + g
    head_hi = (2 * a + 1) * Q_PER_KV + g
    return (head_lo // 2, head_lo % 2), (head_hi // 2, head_hi % 2)


def _attn_kernel(sink_ref, h_ref, kvp_ref, kvc_ref, g_ref, wq_ref, bq_ref, wo_ref, bo_ref,
                 o_ref, wq_s, bq_s, wo_s, q_s, oh_s, *, tq):
    nsb = tq // WINDOW
    n = pl.program_id(1)
    n_kv_tiles = N_KV_HEADS // 2
    lo_row = lax.broadcasted_iota(jnp.int32, (1, LANES), 1) < HEAD_DIM

    @pl.when((pl.program_id(0) == 0) & (n == 0))
    def _():
        for a in range(n_kv_tiles):
            for g in range(Q_PER_KV):
                (t_lo, h_lo), (t_hi, h_hi) = _pair_tile_sources(a, g)
                dst = slice((a * Q_PER_KV + g) * LANES, (a * Q_PER_KV + g + 1) * LANES)

                def pair(ref):
                    src_lo = ref[:, t_lo * LANES:(t_lo + 1) * LANES]
                    src_hi = ref[:, t_hi * LANES:(t_hi + 1) * LANES]
                    if h_lo == 1:
                        src_lo = pltpu.roll(src_lo, HEAD_DIM, axis=1)
                    if h_hi == 0:
                        src_hi = pltpu.roll(src_hi, HEAD_DIM, axis=1)
                    return jnp.where(lo_row, src_lo, src_hi)

                wq_s[:, dst] = pair(wq_ref).astype(BF16)
                bq_s[:, dst] = pair(bq_ref)
                for half in range(2):
                    head = (2 * a + half) * Q_PER_KV + g
                    r0 = (a * Q_PER_KV + g) * LANES + half * HEAD_DIM
                    wo_s[r0:r0 + HEAD_DIM, :] = (
                        wo_ref[head * HEAD_DIM:(head + 1) * HEAD_DIM, :].astype(BF16))

    h = h_ref[0]
    hn = _rmsnorm(h, g_ref[...]).astype(BF16)
    q = jnp.dot(hn, wq_s[...], preferred_element_type=F32) + bq_s[...]
    q_s[...] = (q * (LOG2_E / math.sqrt(HEAD_DIM))).astype(BF16)

    lo = lax.broadcasted_iota(jnp.int32, (WINDOW, LANES), 1) < HEAD_DIM
    own = (lax.broadcasted_iota(jnp.int32, (WINDOW, WINDOW), 1)
           <= lax.broadcasted_iota(jnp.int32, (WINDOW, WINDOW), 0))
    zero = jnp.zeros((WINDOW, LANES), BF16)
    fzero = jnp.zeros((WINDOW, WINDOW), F32)

    for sb in range(nsb):
        rows = slice(sb * WINDOW, (sb + 1) * WINDOW)
        prev_bias = jnp.where(n == 0, -jnp.inf, 0.0).astype(F32) if sb == 0 else None
        for a in range(n_kv_tiles):
            kl = slice(a * LANES, (a + 1) * LANES)
            vl = slice((n_kv_tiles + a) * LANES, (n_kv_tiles + a + 1) * LANES)
            if sb == 0:
                kprev, vprev = kvp_ref[0, :, kl], kvp_ref[0, :, vl]
            else:
                prow = slice((sb - 1) * WINDOW, sb * WINDOW)
                kprev, vprev = kvc_ref[0, prow, kl], kvc_ref[0, prow, vl]
            kd = jnp.concatenate([kprev, kvc_ref[0, rows, kl]], axis=0)
            vd = jnp.concatenate([vprev, kvc_ref[0, rows, vl]], axis=0)
            vd1 = jnp.concatenate([vd, jnp.ones((2 * WINDOW, LANES), BF16)], axis=1)
            qt = [q_s[rows, (a * Q_PER_KV + g) * LANES:(a * Q_PER_KV + g + 1) * LANES]
                  for g in range(Q_PER_KV)]
            lhs = jnp.concatenate([jnp.where(lo, t, zero) for t in qt]
                                  + [jnp.where(lo, zero, t) for t in qt], axis=0)
            s = lax.dot_general(lhs, kd, (((1,), (1,)), ((), ())),
                                preferred_element_type=F32)
            ps, sink_terms = [], []
            for half in range(2):
                for g in range(Q_PER_KV):
                    blk = half * Q_PER_KV + g
                    s_prev = s[blk * WINDOW:(blk + 1) * WINDOW, :WINDOW]
                    s_own = s[blk * WINDOW:(blk + 1) * WINDOW, WINDOW:]
                    if prev_bias is not None:
                        s_prev = s_prev + prev_bias
                    sg = jnp.where(own, s_own, s_prev)
                    sink = sink_ref[(2 * a + half) * Q_PER_KV + g] * LOG2_E
                    m = jnp.maximum(jnp.max(sg, axis=-1, keepdims=True), sink)
                    p = jnp.exp2(sg - m)
                    ps.append(jnp.concatenate([jnp.where(own, fzero, p).astype(BF16),
                                               jnp.where(own, p, fzero).astype(BF16)], axis=1))
                    sink_terms.append(jnp.exp2(sink - m))
            od = jnp.dot(jnp.concatenate(ps, axis=0), vd1, preferred_element_type=F32)
            for g in range(Q_PER_KV):
                r_lo = slice(g * WINDOW, (g + 1) * WINDOW)
                r_hi = slice((Q_PER_KV + g) * WINDOW, (Q_PER_KV + g + 1) * WINDOW)
                num = jnp.where(lo, od[r_lo, :LANES], od[r_hi, :LANES])
                den = (jnp.where(lo, od[r_lo, LANES:], od[r_hi, LANES:])
                       + jnp.where(lo, sink_terms[g], sink_terms[Q_PER_KV + g]))
                oh_s[rows, (a * Q_PER_KV + g) * LANES:(a * Q_PER_KV + g + 1) * LANES] = (
                    (num * (1.0 / den)).astype(BF16))

    o_ref[0] = h + jnp.dot(oh_s[...], wo_s[...], preferred_element_type=F32) + bo_ref[...]


def _attn_layer(h, kv, sinks, g, w_q, b_q, w_o, b_o):
    bsz, seq, d = h.shape
    nkv = kv.shape[-1]
    tq = ATTN_TQ
    nsb = tq // WINDOW
    kernel = functools.partial(_attn_kernel, tq=tq)
    return pl.pallas_call(
        kernel,
        out_shape=jax.ShapeDtypeStruct((bsz, seq, d), F32),
        grid=(bsz, seq // tq),
        in_specs=[
            pl.BlockSpec(memory_space=pltpu.SMEM),
            pl.BlockSpec((1, tq, d), lambda b, n: (b, n, 0)),
            pl.BlockSpec((1, WINDOW, nkv), lambda b, n: (b, jnp.maximum(n * nsb - 1, 0), 0)),
            pl.BlockSpec((1, tq, nkv), lambda b, n: (b, n, 0)),
            _const_spec((1, d)),
            _const_spec(w_q.shape),
            _const_spec((1, d)),
            _const_spec(w_o.shape),
            _const_spec((1, d)),
        ],
        out_specs=pl.BlockSpec((1, tq, d), lambda b, n: (b, n, 0)),
        scratch_shapes=[pltpu.VMEM((d, d), BF16),
                        pltpu.VMEM((1, d), F32),
                        pltpu.VMEM((d, d), BF16),
                        pltpu.VMEM((tq, d), BF16),
                        pltpu.VMEM((tq, d), BF16)],
        compiler_params=pltpu.CompilerParams(
            dimension_semantics=("arbitrary", "arbitrary"), vmem_limit_bytes=ATTN_VMEM_LIMIT_BYTES),
        name="attn",
    )(sinks, h, kv, kv, g.reshape(1, d), w_q, b_q.reshape(1, d), w_o, b_o.reshape(1, d))


def kernel(x, norm_mix, norm_mlp, norm_kv, norm_final, s5_a_re, s5_a_im, s5_log_dt, s5_b_re, s5_b_im, s5_c_re, s5_c_im, s5_d, s5_w_glu, s5_b_glu, w_kv, b_kv, w_q, b_q, sinks, w_o, b_o, w_mlp_in, w_mlp_out):
    bsz, seq, d = x.shape

    w1, cx, lam2_re_t, lam2_im_t = _s5_params(s5_a_re[0], s5_a_im[0], s5_log_dt[0], s5_b_re[0],
                                              s5_b_im[0], s5_c_re[0], s5_c_im[0])
    h = _s5_layer(x, norm_mix[0], w1, lam2_re_t, lam2_im_t, cx, s5_d[0], s5_w_glu[0], s5_b_glu[0])

    h, kv = _mlp(h.reshape(bsz * seq, d), norm_mlp[0], w_mlp_in, w_mlp_out, 0,
                 g_kv=norm_kv, w_kv=w_kv, b_kv=b_kv)

    h = _attn_layer(h.reshape(bsz, seq, d), kv.reshape(bsz, seq, -1), sinks[0], norm_mix[1],
                    w_q[0], b_q[0], w_o[0], b_o[0])

    out = _mlp(h.reshape(bsz * seq, d), norm_mlp[1], w_mlp_in, w_mlp_out, 1, g_fin=norm_final)
    return out.reshape(bsz, seq, d)
```

```python
import functools
import math

import jax
import jax.numpy as jnp
from jax import lax
from jax.experimental import pallas as pl
from jax.experimental.pallas import tpu as pltpu

F32 = jnp.float32
BF16 = jnp.bfloat16

D_MODEL = 1024
S5_GROUP = 16
S5_STATE = 64
LAMBDA_RE_MAX = -1e-4
HEAD_DIM = 64
N_Q_HEADS = D_MODEL // HEAD_DIM
N_KV_HEADS = 4
Q_PER_KV = N_Q_HEADS // N_KV_HEADS
WINDOW = 128
D_FF = 4 * D_MODEL
NORM_EPS = 1e-5
LOG2_E = math.log2(math.e)

LANES = 128
SUBLANES = 8
N_LANE_TILES = D_MODEL // LANES
GROUPS_PER_TILE = LANES // S5_GROUP
STATES_PER_TILE = GROUPS_PER_TILE * S5_STATE
MIB = 1024 * 1024
S5_VMEM_LIMIT_BYTES = 56 * MIB
MLP_VMEM_LIMIT_BYTES = 52 * MIB
ATTN_VMEM_LIMIT_BYTES = 40 * MIB

S5_TL = 64
S5_PARAM_TILES_PER_STEP = 4
S5_TILES_IN_FLIGHT = 2
MLP_TM = 512
MLP_FF_CHUNK = 1024
ATTN_TQ = 1024


def _rmsnorm(x, g):
    return x * lax.rsqrt(jnp.mean(x * x, axis=-1, keepdims=True) + NORM_EPS) * g


def _gelu_tanh(x):
    c = math.sqrt(2.0 / math.pi)
    return 0.5 * x * (1.0 + jnp.tanh(c * (x + 0.044715 * (x * x * x))))


def _const_spec(shape):
    nd = len(shape)
    return pl.BlockSpec(shape, lambda *_: (0,) * nd, pipeline_mode=pl.Buffered(1))


def _s5_params_kernel(abl_ref, bt_ref, cre_ref, cim_ref, w1_ref, cx_ref, l2r_ref, l2i_ref):
    for tile in range(S5_PARAM_TILES_PER_STEP):
        _s5_params_tile(abl_ref.at[:, tile], bt_ref.at[:, tile], cre_ref.at[tile], cim_ref.at[tile],
                        w1_ref.at[tile], cx_ref.at[tile], l2r_ref.at[tile], l2i_ref.at[tile])


def _s5_params_tile(abl_ref, bt_ref, cre_ref, cim_ref, w1_ref, cx_ref, l2r_ref, l2i_ref):
    gpt, grp, nst = GROUPS_PER_TILE, S5_GROUP, S5_STATE
    ar = jnp.minimum(abl_ref[0], LAMBDA_RE_MAX)
    ai = abl_ref[1]
    dt = jnp.exp(abl_ref[2])
    mag = jnp.exp(ar * dt)
    ang = ai * dt
    lr = mag * jnp.cos(ang)
    li = mag * jnp.sin(ang)
    den = ar * ar + ai * ai
    cr = ((lr - 1.0) * ar + li * ai) / den
    ci = (li * ar - (lr - 1.0) * ai) / den
    l2r = lr * lr - li * li
    l2i = 2.0 * (lr * li)

    def per_row(v):
        return jnp.concatenate([jnp.broadcast_to(v[g:g + 1], (grp, nst)) for g in range(gpt)], axis=0)

    row_g = lax.broadcasted_iota(jnp.int32, (LANES, STATES_PER_TILE), 0) // grp
    col_g = lax.broadcasted_iota(jnp.int32, (LANES, STATES_PER_TILE), 1) // nst
    own = row_g == col_g

    def blockdiag(v):
        return jnp.where(own, jnp.concatenate([v] * gpt, axis=1), 0.0)

    lr_c, li_c, cr_c, ci_c = per_row(lr), per_row(li), per_row(cr), per_row(ci)
    l2r_c, l2i_c = per_row(l2r), per_row(l2i)
    btr, bti = bt_ref[0], bt_ref[1]
    bbr = cr_c * btr - ci_c * bti
    bbi = cr_c * bti + ci_c * btr
    lbr = lr_c * bbr - li_c * bbi
    lbi = lr_c * bbi + li_c * bbr
    wbr, wbi = blockdiag(bbr), blockdiag(bbi)
    w1r, w1i = blockdiag(lbr), blockdiag(lbi)
    ccr, cci = cre_ref[...], cim_ref[...]

    lhs = jnp.concatenate([jnp.concatenate([bbr, -bbi], axis=1),
                           jnp.concatenate([lbr, -lbi], axis=1)], axis=0)
    kk = lax.dot_general(lhs, jnp.concatenate([ccr, cci], axis=1), (((1,), (1,)), ((), ())),
                         precision=lax.Precision.HIGHEST, preferred_element_type=F32)
    same_group = (lax.broadcasted_iota(jnp.int32, (LANES, LANES), 0) // grp
                  == lax.broadcasted_iota(jnp.int32, (LANES, LANES), 1) // grp)
    k0 = jnp.where(same_group, kk[:LANES], 0.0)
    k1 = jnp.where(same_group, kk[LANES:], 0.0)
    top =jnp.concatenate([k0, k1, w1r, w1i], axis=1)
    bot = jnp.concatenate([jnp.zeros_like(k0), k0, wbr, wbi], axis=1)
    w1_ref[...] = jnp.concatenate([top, bot], axis=0).astype(BF16)

    def readout_t(pr, pi):
        return jnp.concatenate([blockdiag(pr * ccr - pi * cci),
                                -blockdiag(pi * ccr + pr * cci)], axis=1)

    cx_t = jnp.concatenate([readout_t(lr_c, li_c), readout_t(l2r_c, l2i_c)], axis=0)
    cx_ref[...] = cx_t.T.astype(BF16)

    def flat_row(v):
        row = jnp.concatenate([v[g:g + 1] for g in range(gpt)], axis=1)
        return jnp.broadcast_to(row, (SUBLANES, STATES_PER_TILE))

    l2r_ref[...] = flat_row(l2r)
    l2i_ref[...] = flat_row(l2i)


def _s5_params(a_re, a_im, log_dt, b_re, b_im, c_re, c_im):
    nt, gpt, sp = N_LANE_TILES, GROUPS_PER_TILE, STATES_PER_TILE
    tps = S5_PARAM_TILES_PER_STEP
    abl = jnp.stack([a_re, a_im, jnp.broadcast_to(log_dt[:, None], a_re.shape)])
    abl = abl.reshape(3, nt, gpt, S5_STATE)
    bt = jnp.swapaxes(jnp.stack([b_re, b_im]), -1, -2).reshape(2, nt, LANES, S5_STATE)
    c_tile = lambda c: c.reshape(nt, LANES, S5_STATE)
    return pl.pallas_call(
        _s5_params_kernel,
        out_shape=(jax.ShapeDtypeStruct((nt, 2 * LANES, 2 * LANES + 2 * sp), BF16),
                   jax.ShapeDtypeStruct((nt, 2 * sp, 2 * LANES), BF16),
                   jax.ShapeDtypeStruct((nt, SUBLANES, sp), F32),
                   jax.ShapeDtypeStruct((nt, SUBLANES, sp), F32)),
        grid=(nt // tps,),
        in_specs=[pl.BlockSpec((3, tps, gpt, S5_STATE), lambda j: (0, j, 0, 0)),
                  pl.BlockSpec((2, tps, LANES, S5_STATE), lambda j: (0, j, 0, 0)),
                  pl.BlockSpec((tps, LANES, S5_STATE), lambda j: (j, 0, 0)),
                  pl.BlockSpec((tps, LANES, S5_STATE), lambda j: (j, 0, 0))],
        out_specs=(pl.BlockSpec((tps, 2 * LANES, 2 * LANES + 2 * sp), lambda j: (j, 0, 0)),
                   pl.BlockSpec((tps, 2 * sp, 2 * LANES), lambda j: (j, 0, 0)),
                   pl.BlockSpec((tps, SUBLANES, sp), lambda j: (j, 0, 0)),
                   pl.BlockSpec((tps, SUBLANES, sp), lambda j: (j, 0, 0))),
        compiler_params=pltpu.CompilerParams(dimension_semantics=("parallel",)),
        name="s5_params",
    )(abl, bt, c_tile(c_re), c_tile(c_im))


def _s5_kernel(x_hbm, g_ref, w1_ref, l2r_ref, l2i_ref, cx_ref, d_ref, wglu_ref, bglu_ref,
               o_hbm, xin_s, hn_s, hnu_s, zy_s, bu_s, xs_s, y_s, st_s, res_s, sem_in, sem_out,
               *, tl, n_steps):
    half = tl // 2
    prow = half * SUBLANES
    sp = STATES_PER_TILE
    i = pl.program_id(0)
    slot = i % 2

    def in_copies(step, sl):
        return [pltpu.make_async_copy(x_hbm.at[b, pl.ds(step * tl, tl), :],
                                      xin_s.at[sl, :, b, :], sem_in.at[sl, b])
                for b in range(SUBLANES)]

    def out_copies(step, sl):
        return [pltpu.make_async_copy(res_s.at[sl, :, b, :],
                                      o_hbm.at[b, pl.ds(step * tl, tl), :], sem_out.at[sl, b])
                for b in range(SUBLANES)]

    in_slot = i % 3

    @pl.when(i == 0)
    def _():
        st_s[...] = jnp.zeros_like(st_s)
        for copy in in_copies(0, 0) + in_copies(1, 1):
            copy.start()

    @pl.when(i + 2 < n_steps)
    def _():
        for copy in in_copies(i + 2, (i + 2) % 3):
            copy.start()

    for copy in in_copies(i, in_slot):
        copy.wait()

    @pl.when(i >= 2)
    def _():
        for copy in out_copies(i - 2, slot):
            copy.wait()

    def row_group(v, t):
        return v[t * SUBLANES:(t + 1) * SUBLANES]

    hn = _rmsnorm(xin_s[in_slot].reshape(tl * SUBLANES, D_MODEL), g_ref[...])
    for t in range(tl):
        g = (t % 2) * half + t // 2
        hn_s[g * SUBLANES:(g + 1) * SUBLANES, :] = row_group(hn, t)
    for m in range(tl // 4):
        r16 = slice(2 * m * SUBLANES, (2 * m + 2) * SUBLANES)
        even = jnp.concatenate([row_group(hn, 4 * m), row_group(hn, 4 * m + 2)], axis=0).astype(BF16)
        odd = jnp.concatenate([row_group(hn, 4 * m + 1), row_group(hn, 4 * m + 3)], axis=0).astype(BF16)
        for j in range(N_LANE_TILES):
            lanes = slice(j * LANES, (j + 1) * LANES)
            hnu_s[r16, 2 * j * LANES:(2 * j + 1) * LANES] = even[:, lanes]
            hnu_s[r16, (2 * j + 1) * LANES:(2 * j + 2) * LANES] = odd[:, lanes]

    def project_in(j):
        z = jnp.dot(hnu_s[:, 2 * j * LANES:(2 * j + 2) * LANES], w1_ref[j],
                    preferred_element_type=F32)
        zy_s[j % S5_TILES_IN_FLIGHT] = z[:, :2 * LANES]
        bu_s[j % S5_TILES_IN_FLIGHT] = z[:, 2 * LANES:]

    ahead = S5_TILES_IN_FLIGHT - 1
    for j in range(ahead):
        project_in(j)
    for j in range(N_LANE_TILES):
        if j + ahead < N_LANE_TILES:
            project_in(j + ahead)
        p = j % S5_TILES_IN_FLIGHT
        lanes = slice(j * LANES, (j + 1) * LANES)
        ar = l2r_ref[j]
        ai = l2i_ref[j]
        xr = st_s[j, :, 0:sp]
        xi = st_s[j, :, sp:2 * sp]
        for k in range(half):
            r = slice(k * SUBLANES, (k + 1) * SUBLANES)
            xs_s[p, r, 0:sp] = xr
            xs_s[p, r, sp:2 * sp] = xi
            nxr = ar * xr - ai * xi + bu_s[p, r, 0:sp]
            nxi = ar * xi + ai * xr + bu_s[p, r, sp:2 * sp]
            xr, xi = nxr, nxi
        st_s[j, :, 0:sp] = xr
        st_s[j, :, sp:2 * sp] = xi
        y = zy_s[p] + jnp.dot(xs_s[p].astype(BF16), cx_ref[j], preferred_element_type=F32)
        y_s[0:prow, lanes] = y[:, :LANES]
        y_s[prow:2 * prow, lanes] = y[:, LANES:]

    for par in range(2):
        rs = slice(par * prow, (par + 1) * prow)
        y = y_s[rs, :] + d_ref[...] * hn_s[rs, :]
        z = jnp.dot(_gelu_tanh(y).astype(BF16), wglu_ref[...].astype(BF16),
                    preferred_element_type=F32) + bglu_ref[...]
        mix = z[:, :D_MODEL] * (1.0 / (1.0 + jnp.exp(-z[:, D_MODEL:])))
        for k in range(half):
            t = 2 * k + par
            res_s[slot, t] = xin_s[in_slot, t] + row_group(mix, k)

    for copy in out_copies(i, slot):
        copy.start()

    @pl.when(i == n_steps - 1)
    def _():
        for copy in out_copies(i - 1, 1 - slot) + out_copies(i, slot):
            copy.wait()


def _s5_layer(x, g_mix, w1, lam2_re_t, lam2_im_t, cx, d_skip, w_glu, b_glu):
    bsz, seq, d = x.shape
    tl = S5_TL
    n_steps = seq // tl
    assert bsz == SUBLANES and n_steps >= 3
    rows = tl * bsz
    prow = rows // 2
    kernel = functools.partial(_s5_kernel, tl=tl, n_steps=n_steps)
    hbm = pl.BlockSpec(memory_space=pl.ANY)
    return pl.pallas_call(
        kernel,
        out_shape=jax.ShapeDtypeStruct((bsz, seq, d), F32),
        grid=(n_steps,),
        in_specs=[
            hbm,
            _const_spec((1, d)),
            _const_spec(w1.shape),
            _const_spec(lam2_re_t.shape),
            _const_spec(lam2_im_t.shape),
            _const_spec(cx.shape),
            _const_spec((1, d)),
            _const_spec(w_glu.shape),
            _const_spec((1, 2 * d)),
        ],
        out_specs=hbm,
        scratch_shapes=[
            pltpu.VMEM((3, tl, bsz, d), F32),
            pltpu.VMEM((rows, d), F32),
            pltpu.VMEM((prow, 2 * d), BF16),
            pltpu.VMEM((S5_TILES_IN_FLIGHT, prow, 2 * LANES), F32),
            pltpu.VMEM((S5_TILES_IN_FLIGHT, prow, 2 * STATES_PER_TILE), F32),
            pltpu.VMEM((S5_TILES_IN_FLIGHT, prow, 2 * STATES_PER_TILE), F32),
            pltpu.VMEM((rows, d), F32),
            pltpu.VMEM((N_LANE_TILES, SUBLANES, 2 * STATES_PER_TILE), F32),
            pltpu.VMEM((2, tl, bsz, d), F32),
            pltpu.SemaphoreType.DMA((3, SUBLANES)),
            pltpu.SemaphoreType.DMA((2, SUBLANES)),
        ],
        compiler_params=pltpu.CompilerParams(
            dimension_semantics=("arbitrary",), vmem_limit_bytes=S5_VMEM_LIMIT_BYTES),
        name="s5_layer",
    )(x, g_mix.reshape(1, d), w1, lam2_re_t, lam2_im_t, cx, d_skip.reshape(1, d), w_glu,
      b_glu.reshape(1, 2 * d))


def _inv_rms(x):
    return lax.rsqrt(jnp.mean(x * x, axis=-1, keepdims=True) + NORM_EPS)


def _mlp_kernel(h_ref, g_ref, win_hbm, wout_hbm, *rest, layer, with_kv):
    if with_kv:
        gkv_ref, wkv_ref, bkv_ref, o_ref, kv_ref, win_s, wout_s, sem = rest
    else:
        gfin_ref, o_ref, win_s, wout_s, sem = rest
    n_chunks = D_FF // MLP_FF_CHUNK

    def weight_copies(c):
        cols = pl.ds(c * MLP_FF_CHUNK, MLP_FF_CHUNK)
        return (pltpu.make_async_copy(win_hbm.at[layer, :, cols], win_s.at[:, cols], sem.at[0, c]),
                pltpu.make_async_copy(wout_hbm.at[layer, cols, :], wout_s.at[cols, :], sem.at[1, c]))

    def body(first_step):
        h = h_ref[...]
        hg = (h * g_ref[...]).astype(BF16)
        r = _inv_rms(h)
        acts = []
        for c in range(n_chunks):
            cols = slice(c * MLP_FF_CHUNK, (c + 1) * MLP_FF_CHUNK)
            if first_step:
                for copy in weight_copies(c):
                    copy.wait()
            a = jnp.dot(hg, win_s[:, cols].astype(BF16), preferred_element_type=F32)
            acts.append(jnp.square(jnp.maximum(a, 0.0)).astype(BF16))
        acc = jnp.dot(jnp.concatenate(acts, axis=1), wout_s[...].astype(BF16),
                      preferred_element_type=F32)
        out = h + (r * r) * acc
        if with_kv:
            o_ref[...] = out
            kv = jnp.dot((out * gkv_ref[...]).astype(BF16), wkv_ref[...].astype(BF16),
                         preferred_element_type=F32)
            kv_ref[...] = (_inv_rms(out) * kv + bkv_ref[...]).astype(BF16)
        else:
            o_ref[...] = out * _inv_rms(out) * gfin_ref[...]

    @pl.when(pl.program_id(0) == 0)
    def _():
        for c in range(n_chunks):
            for copy in weight_copies(c):
                copy.start()
        body(True)

    @pl.when(pl.program_id(0) > 0)
    def _():
        body(False)


def _mlp(h, g, w_in, w_out, layer, *, g_kv=None, w_kv=None, b_kv=None, g_fin=None):
    t, d = h.shape
    with_kv = w_kv is not None
    row = lambda n: pl.BlockSpec((MLP_TM, n), lambda i: (i, 0))
    hbm = pl.BlockSpec(memory_space=pl.ANY)
    in_specs = [row(d), _const_spec((1, d)), hbm, hbm]
    operands = [h, g.reshape(1, d), w_in, w_out]
    if with_kv:
        nkv = w_kv.shape[1]
        in_specs += [_const_spec((1, d)), _const_spec(w_kv.shape), _const_spec((1, nkv))]
        operands += [g_kv.reshape(1, d), w_kv, b_kv.reshape(1, nkv)]
        out_shape = (jax.ShapeDtypeStruct((t, d), F32), jax.ShapeDtypeStruct((t, nkv), BF16))
        out_specs = (row(d), row(nkv))
    else:
        in_specs += [_const_spec((1, d))]
        operands += [g_fin.reshape(1, d)]
        out_shape = jax.ShapeDtypeStruct((t, d), F32)
        out_specs = row(d)
    return pl.pallas_call(
        functools.partial(_mlp_kernel, layer=layer, with_kv=with_kv),
        out_shape=out_shape,
        grid=(t // MLP_TM,),
        in_specs=in_specs,
        out_specs=out_specs,
        scratch_shapes=[pltpu.VMEM(w_in.shape[1:], F32),
                        pltpu.VMEM(w_out.shape[1:], F32),
                        pltpu.SemaphoreType.DMA((2, D_FF // MLP_FF_CHUNK))],
        compiler_params=pltpu.CompilerParams(
            dimension_semantics=("arbitrary",), vmem_limit_bytes=MLP_VMEM_LIMIT_BYTES),
        name="mlp_kv" if with_kv else "mlp_final",
    )(*operands)


def _pair_tile_sources(a, g):
    head_lo = (2 * a) * Q_PER_KV + g
    head_hi = (2 * a + 1) * Q_PER_KV + g
    return (head_lo // 2, head_lo % 2), (head_hi // 2, head_hi % 2)


def _attn_kernel(sink_ref, h_ref, kvp_ref, kvc_ref, g_ref, wq_ref, bq_ref, wo_ref, bo_ref,
                 o_ref, wq_s, bq_s, wo_s, q_s, oh_s, *, tq):
    nsb = tq // WINDOW
    n = pl.program_id(1)
    n_kv_tiles = N_KV_HEADS // 2
    lo_row = lax.broadcasted_iota(jnp.int32, (1, LANES), 1) < HEAD_DIM

    @pl.when((pl.program_id(0) == 0) & (n == 0))
    def _():
        for a in range(n_kv_tiles):
            for g in range(Q_PER_KV):
                (t_lo, h_lo), (t_hi, h_hi) = _pair_tile_sources(a, g)
                dst = slice((a * Q_PER_KV + g) * LANES, (a * Q_PER_KV + g + 1) * LANES)

                def pair(ref):
                    src_lo = ref[:, t_lo * LANES:(t_lo + 1) * LANES]
                    src_hi = ref[:, t_hi * LANES:(t_hi + 1) * LANES]
                    if h_lo == 1:
                        src_lo = pltpu.roll(src_lo, HEAD_DIM, axis=1)
                    if h_hi == 0:
                        src_hi = pltpu.roll(src_hi, HEAD_DIM, axis=1)
                    return jnp.where(lo_row, src_lo, src_hi)

                wq_s[:, dst] = pair(wq_ref).astype(BF16)
                bq_s[:, dst] = pair(bq_ref)
                for half in range(2):
                    head = (2 * a + half) * Q_PER_KV + g
                    r0 = (a * Q_PER_KV + g) * LANES + half * HEAD_DIM
                    wo_s[r0:r0 + HEAD_DIM, :] = (
                        wo_ref[head * HEAD_DIM:(head + 1) * HEAD_DIM, :].astype(BF16))

    h = h_ref[0]
    hn = _rmsnorm(h, g_ref[...]).astype(BF16)
    q = jnp.dot(hn, wq_s[...], preferred_element_type=F32) + bq_s[...]
    q_s[...] = (q * (LOG2_E / math.sqrt(HEAD_DIM))).astype(BF16)

    lo = lax.broadcasted_iota(jnp.int32, (WINDOW, LANES), 1) < HEAD_DIM
    own = (lax.broadcasted_iota(jnp.int32, (WINDOW, WINDOW), 1)
           <= lax.broadcasted_iota(jnp.int32, (WINDOW, WINDOW), 0))
    zero = jnp.zeros((WINDOW, LANES), BF16)
    fzero = jnp.zeros((WINDOW, WINDOW), F32)

    for sb in range(nsb):
        rows = slice(sb * WINDOW, (sb + 1) * WINDOW)
        prev_bias = jnp.where(n == 0, -jnp.inf, 0.0).astype(F32) if sb == 0 else None
        for a in range(n_kv_tiles):
            kl = slice(a * LANES, (a + 1) * LANES)
            vl = slice((n_kv_tiles + a) * LANES, (n_kv_tiles + a + 1) * LANES)
            if sb == 0:
                kprev, vprev = kvp_ref[0, :, kl], kvp_ref[0, :, vl]
            else:
                prow = slice((sb - 1) * WINDOW, sb * WINDOW)
                kprev, vprev = kvc_ref[0, prow, kl], kvc_ref[0, prow, vl]
            kd = jnp.concatenate([kprev, kvc_ref[0, rows, kl]], axis=0)
            vd = jnp.concatenate([vprev, kvc_ref[0, rows, vl]], axis=0)
            vd1 = jnp.concatenate([vd, jnp.ones((2 * WINDOW, LANES), BF16)], axis=1)
            qt = [q_s[rows, (a * Q_PER_KV + g) * LANES:(a * Q_PER_KV + g + 1) * LANES]
                  for g in range(Q_PER_KV)]
            lhs = jnp.concatenate([jnp.where(lo, t, zero) for t in qt]
                                  + [jnp.where(lo, zero, t) for t in qt], axis=0)
            s = lax.dot_general(lhs, kd, (((1,), (1,)), ((), ())),
                                preferred_element_type=F32)
            ps, sink_terms = [], []
            for half in range(2):
                for g in range(Q_PER_KV):
                    blk = half * Q_PER_KV + g
                    s_prev = s[blk * WINDOW:(blk + 1) * WINDOW, :WINDOW]
                    s_own = s[blk * WINDOW:(blk + 1) * WINDOW, WINDOW:]
                    if prev_bias is not None:
                        s_prev = s_prev + prev_bias
                    sg = jnp.where(own, s_own, s_prev)
                    sink = sink_ref[(2 * a + half) * Q_PER_KV + g] * LOG2_E
                    m = jnp.maximum(jnp.max(sg, axis=-1, keepdims=True), sink)
                    p = jnp.exp2(sg - m)
                    ps.append(jnp.concatenate([jnp.where(own, fzero, p).astype(BF16),
                                               jnp.where(own, p, fzero).astype(BF16)], axis=1))
                    sink_terms.append(jnp.exp2(sink - m))
            od = jnp.dot(jnp.concatenate(ps, axis=0), vd1, preferred_element_type=F32)
            for g in range(Q_PER_KV):
                r_lo = slice(g * WINDOW, (g + 1) * WINDOW)
                r_hi = slice((Q_PER_KV + g) * WINDOW, (Q_PER_KV + g + 1) * WINDOW)
                num = jnp.where(lo, od[r_lo, :LANES], od[r_hi, :LANES])
                den = (jnp.where(lo, od[r_lo, LANES:], od[r_hi, LANES:])
                       + jnp.where(lo, sink_terms[g], sink_terms[Q_PER_KV + g]))
                oh_s[rows, (a * Q_PER_KV + g) * LANES:(a * Q_PER_KV + g + 1) * LANES] = (
                    (num * (1.0 / den)).astype(BF16))

    o_ref[0] = h + jnp.dot(oh_s[...], wo_s[...], preferred_element_type=F32) + bo_ref[...]


def _attn_layer(h, kv, sinks, g, w_q, b_q, w_o, b_o):
    bsz, seq, d = h.shape
    nkv = kv.shape[-1]
    tq = ATTN_TQ
    nsb = tq // WINDOW
    kernel = functools.partial(_attn_kernel, tq=tq)
    return pl.pallas_call(
        kernel,
        out_shape=jax.ShapeDtypeStruct((bsz, seq, d), F32),
        grid=(bsz, seq // tq),
        in_specs=[
            pl.BlockSpec(memory_space=pltpu.SMEM),
            pl.BlockSpec((1, tq, d), lambda b, n: (b, n, 0)),
            pl.BlockSpec((1, WINDOW, nkv), lambda b, n: (b, jnp.maximum(n * nsb - 1, 0), 0)),
            pl.BlockSpec((1, tq, nkv), lambda b, n: (b, n, 0)),
            _const_spec((1, d)),
            _const_spec(w_q.shape),
            _const_spec((1, d)),
            _const_spec(w_o.shape),
            _const_spec((1, d)),
        ],
        out_specs=pl.BlockSpec((1, tq, d), lambda b, n: (b, n, 0)),
        scratch_shapes=[pltpu.VMEM((d, d), BF16),
                        pltpu.VMEM((1, d), F32),
                        pltpu.VMEM((d, d), BF16),
                        pltpu.VMEM((tq, d), BF16),
                        pltpu.VMEM((tq, d), BF16)],
        compiler_params=pltpu.CompilerParams(
            dimension_semantics=("arbitrary", "arbitrary"), vmem_limit_bytes=ATTN_VMEM_LIMIT_BYTES),
        name="attn",
    )(sinks, h, kv, kv, g.reshape(1, d), w_q, b_q.reshape(1, d), w_o, b_o.reshape(1, d))


def kernel(x, norm_mix, norm_mlp, norm_kv, norm_final, s5_a_re, s5_a_im, s5_log_dt, s5_b_re, s5_b_im, s5_c_re, s5_c_im, s5_d, s5_w_glu, s5_b_glu, w_kv, b_kv, w_q, b_q, sinks, w_o, b_o, w_mlp_in, w_mlp_out):
    bsz, seq, d = x.shape

    w1, cx, lam2_re_t, lam2_im_t = _s5_params(s5_a_re[0], s5_a_im[0], s5_log_dt[0], s5_b_re[0],
                                              s5_b_im[0], s5_c_re[0], s5_c_im[0])
    h = _s5_layer(x, norm_mix[0], w1, lam2_re_t, lam2_im_t, cx, s5_d[0], s5_w_glu[0], s5_b_glu[0])

    h, kv = _mlp(h.reshape(bsz * seq, d), norm_mlp[0], w_mlp_in, w_mlp_out, 0,
                 g_kv=norm_kv, w_kv=w_kv, b_kv=b_kv)

    h = _attn_layer(h.reshape(bsz, seq, d), kv.reshape(bsz, seq, -1), sinks[0], norm_mix[1],
                    w_q[0], b_q[0], w_o[0], b_o[0])

    out = _mlp(h.reshape(bsz * seq, d), norm_mlp[1], w_mlp_in, w_mlp_out, 1, g_fin=norm_final)
    return out.reshape(bsz, seq, d)
```

```python
import functools
import math

import jax
import jax.numpy as jnp
from jax import lax
from jax.experimental import pallas as pl
from jax.experimental.pallas import tpu as pltpu

F32 = jnp.float32
BF16 = jnp.bfloat16

D_MODEL = 1024
S5_GROUP = 16
S5_STATE = 64
LAMBDA_RE_MAX = -1e-4
HEAD_DIM = 64
N_Q_HEADS = D_MODEL // HEAD_DIM
N_KV_HEADS = 4
Q_PER_KV = N_Q_HEADS // N_KV_HEADS
WINDOW = 128
D_FF = 4 * D_MODEL
NORM_EPS = 1e-5
LOG2_E = math.log2(math.e)

LANES = 128
SUBLANES = 8
N_LANE_TILES = D_MODEL // LANES
GROUPS_PER_TILE = LANES // S5_GROUP
STATES_PER_TILE = GROUPS_PER_TILE * S5_STATE
MIB = 1024 * 1024
S5_VMEM_LIMIT_BYTES = 56 * MIB
MLP_VMEM_LIMIT_BYTES = 52 * MIB
ATTN_VMEM_LIMIT_BYTES = 40 * MIB

S5_TL = 64
S5_PARAM_TILES_PER_STEP = 4
S5_TILES_IN_FLIGHT = 2
MLP_TM = 512
MLP_FF_CHUNK = 1024
ATTN_TQ = 1024


def _rmsnorm(x, g):
    return x * lax.rsqrt(jnp.mean(x * x, axis=-1, keepdims=True) + NORM_EPS) * g


def _gelu_tanh(x):
    c = math.sqrt(2.0 / math.pi)
    return 0.5 * x * (1.0 + jnp.tanh(c * (x + 0.044715 * (x * x * x))))


def _const_spec(shape):
    nd = len(shape)
    return pl.BlockSpec(shape, lambda *_: (0,) * nd, pipeline_mode=pl.Buffered(1))


def _s5_params_kernel(ldt_ref, are_ref, aim_ref, bt_ref, cre_ref, cim_ref,
                      w1_ref, cx_ref, l2r_ref, l2i_ref):
    first_group = pl.program_id(0) * (S5_PARAM_TILES_PER_STEP * GROUPS_PER_TILE)
    for tile in range(S5_PARAM_TILES_PER_STEP):
        log_dt = [ldt_ref[first_group + tile * GROUPS_PER_TILE + g] for g in range(GROUPS_PER_TILE)]
        _s5_params_tile(are_ref.at[tile], aim_ref.at[tile], log_dt, bt_ref.at[:, tile],
                        cre_ref.at[tile], cim_ref.at[tile],
                        w1_ref.at[tile], cx_ref.at[tile], l2r_ref.at[tile], l2i_ref.at[tile])


def _s5_params_tile(are_ref, aim_ref, log_dt, bt_ref, cre_ref, cim_ref,
                    w1_ref, cx_ref, l2r_ref, l2i_ref):
    gpt, grp, nst = GROUPS_PER_TILE, S5_GROUP, S5_STATE
    ar = jnp.minimum(are_ref[...], LAMBDA_RE_MAX)
    ai = aim_ref[...]
    dt = jnp.exp(jnp.concatenate([jnp.full((1, nst), v, F32) for v in log_dt], axis=0))
    mag = jnp.exp(ar * dt)
    ang = ai * dt
    lr = mag * jnp.cos(ang)
    li = mag * jnp.sin(ang)
    den = ar * ar + ai * ai
    cr = ((lr - 1.0) * ar + li * ai) / den
    ci = (li * ar - (lr - 1.0) * ai) / den
    l2r = lr * lr - li * li
    l2i = 2.0 * (lr * li)

    def per_row(v):
        return jnp.concatenate([jnp.broadcast_to(v[g:g + 1], (grp, nst)) for g in range(gpt)], axis=0)

    row_g = lax.broadcasted_iota(jnp.int32, (LANES, STATES_PER_TILE), 0) // grp
    col_g = lax.broadcasted_iota(jnp.int32, (LANES, STATES_PER_TILE), 1) // nst
    own = row_g == col_g

    def blockdiag(v):
        return jnp.where(own, jnp.concatenate([v] * gpt, axis=1), 0.0)

    lr_c, li_c, cr_c, ci_c = per_row(lr), per_row(li), per_row(cr), per_row(ci)
    l2r_c, l2i_c = per_row(l2r), per_row(l2i)
    btr, bti = bt_ref[0], bt_ref[1]
    bbr = cr_c * btr - ci_c * bti
    bbi = cr_c * bti + ci_c * btr
    lbr = lr_c * bbr - li_c * bbi
    lbi = lr_c * bbi + li_c * bbr
    wbr, wbi = blockdiag(bbr), blockdiag(bbi)
    w1r, w1i = blockdiag(lbr), blockdiag(lbi)
    ccr, cci = cre_ref[...], cim_ref[...]

    lhs = jnp.concatenate([jnp.concatenate([bbr, -bbi], axis=1),
                           jnp.concatenate([lbr, -lbi], axis=1)], axis=0)
    kk = lax.dot_general(lhs, jnp.concatenate([ccr, cci], axis=1), (((1,), (1,)), ((), ())),
                         precision=lax.Precision.HIGHEST, preferred_element_type=F32)
    same_group = (lax.broadcasted_iota(jnp.int32, (LANES, LANES), 0) // grp
                  == lax.broadcasted_iota(jnp.int32, (LANES, LANES), 1) // grp)
    k0 = jnp.where(same_group, kk[:LANES], 0.0)
    k1 = jnp.where(same_group, kk[LANES:], 0.0)
    top =jnp.concatenate([k0, k1, w1r, w1i], axis=1)
    bot = jnp.concatenate([jnp.zeros_like(k0), k0, wbr, wbi], axis=1)
    w1_ref[...] = jnp.concatenate([top, bot], axis=0).astype(BF16)

    def readout_t(pr, pi):
        return jnp.concatenate([blockdiag(pr * ccr - pi * cci),
                                -blockdiag(pi * ccr + pr * cci)], axis=1)

    cx_t = jnp.concatenate([readout_t(lr_c, li_c), readout_t(l2r_c, l2i_c)], axis=0)
    cx_ref[...] = cx_t.T.astype(BF16)

    def flat_row(v):
        row = jnp.concatenate([v[g:g + 1] for g in range(gpt)], axis=1)
        return jnp.broadcast_to(row, (SUBLANES, STATES_PER_TILE))

    l2r_ref[...] = flat_row(l2r)
    l2i_ref[...] = flat_row(l2i)


def _s5_params(a_re, a_im, log_dt, b_re, b_im, c_re, c_im):
    nt, gpt, sp = N_LANE_TILES, GROUPS_PER_TILE, STATES_PER_TILE
    tps = S5_PARAM_TILES_PER_STEP
    a_tile = lambda a: a.reshape(nt, gpt, S5_STATE)
    bt =jnp.swapaxes(jnp.stack([b_re, b_im]), -1, -2).reshape(2, nt, LANES, S5_STATE)
    c_tile = lambda c: c.reshape(nt, LANES, S5_STATE)
    return pl.pallas_call(
        _s5_params_kernel,
        out_shape=(jax.ShapeDtypeStruct((nt, 2 * LANES, 2 * LANES + 2 * sp), BF16),
                   jax.ShapeDtypeStruct((nt, 2 * sp, 2 * LANES), BF16),
                   jax.ShapeDtypeStruct((nt, SUBLANES, sp), F32),
                   jax.ShapeDtypeStruct((nt, SUBLANES, sp), F32)),
        grid=(nt // tps,),
        in_specs=[pl.BlockSpec(memory_space=pltpu.SMEM),
                  pl.BlockSpec((tps, gpt, S5_STATE), lambda j: (j, 0, 0)),
                  pl.BlockSpec((tps, gpt, S5_STATE), lambda j: (j, 0, 0)),
                  pl.BlockSpec((2, tps, LANES, S5_STATE), lambda j: (0, j, 0, 0)),
                  pl.BlockSpec((tps, LANES, S5_STATE), lambda j: (j, 0, 0)),
                  pl.BlockSpec((tps, LANES, S5_STATE), lambda j: (j, 0, 0))],
        out_specs=(pl.BlockSpec((tps, 2 * LANES, 2 * LANES + 2 * sp), lambda j: (j, 0, 0)),
                   pl.BlockSpec((tps, 2 * sp, 2 * LANES), lambda j: (j, 0, 0)),
                   pl.BlockSpec((tps, SUBLANES, sp), lambda j: (j, 0, 0)),
                   pl.BlockSpec((tps, SUBLANES, sp), lambda j: (j, 0, 0))),
        compiler_params=pltpu.CompilerParams(dimension_semantics=("parallel",)),
        name="s5_params",
    )(log_dt, a_tile(a_re), a_tile(a_im), bt, c_tile(c_re), c_tile(c_im))


def _s5_kernel(x_hbm, g_ref, w1_ref, l2r_ref, l2i_ref, cx_ref, d_ref, wglu_ref, bglu_ref,
               o_hbm, xin_s, hn_s, hnu_s, zy_s, bu_s, xs_s, y_s, st_s, res_s, sem_in, sem_out,
               *, tl, n_steps):
    half = tl // 2
    prow = half * SUBLANES
    sp = STATES_PER_TILE
    i = pl.program_id(0)
    slot = i % 2

    def in_copies(step, sl):
        return [pltpu.make_async_copy(x_hbm.at[b, pl.ds(step * tl, tl), :],
                                      xin_s.at[sl, :, b, :], sem_in.at[sl, b])
                for b in range(SUBLANES)]

    def out_copies(step, sl):
        return [pltpu.make_async_copy(res_s.at[sl, :, b, :],
                                      o_hbm.at[b, pl.ds(step * tl, tl), :], sem_out.at[sl, b])
                for b in range(SUBLANES)]

    in_slot = i % 3

    @pl.when(i == 0)
    def _():
        st_s[...] = jnp.zeros_like(st_s)
        for copy in in_copies(0, 0) + in_copies(1, 1):
            copy.start()

    @pl.when(i + 2 < n_steps)
    def _():
        for copy in in_copies(i + 2, (i + 2) % 3):
            copy.start()

    for copy in in_copies(i, in_slot):
        copy.wait()

    @pl.when(i >= 2)
    def _():
        for copy in out_copies(i - 2, slot):
            copy.wait()

    def row_group(v, t):
        return v[t * SUBLANES:(t + 1) * SUBLANES]

    hn = _rmsnorm(xin_s[in_slot].reshape(tl * SUBLANES, D_MODEL), g_ref[...])
    for t in range(tl):
        g = (t % 2) * half + t // 2
        hn_s[g * SUBLANES:(g + 1) * SUBLANES, :] = row_group(hn, t)
    for m in range(tl // 4):
        r16 = slice(2 * m * SUBLANES, (2 * m + 2) * SUBLANES)
        even = jnp.concatenate([row_group(hn, 4 * m), row_group(hn, 4 * m + 2)], axis=0).astype(BF16)
        odd = jnp.concatenate([row_group(hn, 4 * m + 1), row_group(hn, 4 * m + 3)], axis=0).astype(BF16)
        for j in range(N_LANE_TILES):
            lanes = slice(j * LANES, (j + 1) * LANES)
            hnu_s[r16, 2 * j * LANES:(2 * j + 1) * LANES] = even[:, lanes]
            hnu_s[r16, (2 * j + 1) * LANES:(2 * j + 2) * LANES] = odd[:, lanes]

    def project_in(j):
        z = jnp.dot(hnu_s[:, 2 * j * LANES:(2 * j + 2) * LANES], w1_ref[j],
                    preferred_element_type=F32)
        zy_s[j % S5_TILES_IN_FLIGHT] = z[:, :2 * LANES]
        bu_s[j % S5_TILES_IN_FLIGHT] = z[:, 2 * LANES:]

    ahead = S5_TILES_IN_FLIGHT - 1
    for j in range(ahead):
        project_in(j)
    for j in range(N_LANE_TILES):
        if j + ahead < N_LANE_TILES:
            project_in(j + ahead)
        p = j % S5_TILES_IN_FLIGHT
        lanes = slice(j * LANES, (j + 1) * LANES)
        ar = l2r_ref[j]
        ai = l2i_ref[j]
        xr = st_s[j, :, 0:sp]
        xi = st_s[j, :, sp:2 * sp]
        for k in range(half):
            r = slice(k * SUBLANES, (k + 1) * SUBLANES)
            xs_s[p, r, 0:sp] = xr
            xs_s[p, r, sp:2 * sp] = xi
            nxr = ar * xr - ai * xi + bu_s[p, r, 0:sp]
            nxi = ar * xi + ai * xr + bu_s[p, r, sp:2 * sp]
            xr, xi = nxr, nxi
        st_s[j, :, 0:sp] = xr
        st_s[j, :, sp:2 * sp] = xi
        y = zy_s[p] + jnp.dot(xs_s[p].astype(BF16), cx_ref[j], preferred_element_type=F32)
        y_s[0:prow, lanes] = y[:, :LANES]
        y_s[prow:2 * prow, lanes] = y[:, LANES:]

    for par in range(2):
        rs = slice(par * prow, (par + 1) * prow)
        y = y_s[rs, :] + d_ref[...] * hn_s[rs, :]
        z = jnp.dot(_gelu_tanh(y).astype(BF16), wglu_ref[...].astype(BF16),
                    preferred_element_type=F32) + bglu_ref[...]
        mix = z[:, :D_MODEL] * (1.0 / (1.0 + jnp.exp(-z[:, D_MODEL:])))
        for k in range(half):
            t = 2 * k + par
            res_s[slot, t] = xin_s[in_slot, t] + row_group(mix, k)

    for copy in out_copies(i, slot):
        copy.start()

    @pl.when(i == n_steps - 1)
    def _():
        for copy in out_copies(i - 1, 1 - slot) + out_copies(i, slot):
            copy.wait()


def _s5_layer(x, g_mix, w1, lam2_re_t, lam2_im_t, cx, d_skip, w_glu, b_glu):
    bsz, seq, d = x.shape
    tl = S5_TL
    n_steps = seq // tl
    assert bsz == SUBLANES and n_steps >= 3
    rows = tl * bsz
    prow = rows // 2
    kernel = functools.partial(_s5_kernel, tl=tl, n_steps=n_steps)
    hbm = pl.BlockSpec(memory_space=pl.ANY)
    return pl.pallas_call(
        kernel,
        out_shape=jax.ShapeDtypeStruct((bsz, seq, d), F32),
        grid=(n_steps,),
        in_specs=[
            hbm,
            _const_spec((1, d)),
            _const_spec(w1.shape),
            _const_spec(lam2_re_t.shape),
            _const_spec(lam2_im_t.shape),
            _const_spec(cx.shape),
            _const_spec((1, d)),
            _const_spec(w_glu.shape),
            _const_spec((1, 2 * d)),
        ],
        out_specs=hbm,
        scratch_shapes=[
            pltpu.VMEM((3, tl, bsz, d), F32),
            pltpu.VMEM((rows, d), F32),
            pltpu.VMEM((prow, 2 * d), BF16),
            pltpu.VMEM((S5_TILES_IN_FLIGHT, prow, 2 * LANES), F32),
            pltpu.VMEM((S5_TILES_IN_FLIGHT, prow, 2 * STATES_PER_TILE), F32),
            pltpu.VMEM((S5_TILES_IN_FLIGHT, prow, 2 * STATES_PER_TILE), F32),
            pltpu.VMEM((rows, d), F32),
            pltpu.VMEM((N_LANE_TILES, SUBLANES, 2 * STATES_PER_TILE), F32),
            pltpu.VMEM((2, tl, bsz, d), F32),
            pltpu.SemaphoreType.DMA((3, SUBLANES)),
            pltpu.SemaphoreType.DMA((2, SUBLANES)),
        ],
        compiler_params=pltpu.CompilerParams(
            dimension_semantics=("arbitrary",), vmem_limit_bytes=S5_VMEM_LIMIT_BYTES),
        name="s5_layer",
    )(x, g_mix.reshape(1, d), w1, lam2_re_t, lam2_im_t, cx, d_skip.reshape(1, d), w_glu,
      b_glu.reshape(1, 2 * d))


def _inv_rms(x):
    return lax.rsqrt(jnp.mean(x * x, axis=-1, keepdims=True) + NORM_EPS)


def _mlp_kernel(h_ref, g_ref, win_hbm, wout_hbm, *rest, layer, with_kv):
    if with_kv:
        gkv_ref, wkv_ref, bkv_ref, o_ref, kv_ref, win_s, wout_s, sem = rest
    else:
        gfin_ref, o_ref, win_s, wout_s, sem = rest
    n_chunks = D_FF // MLP_FF_CHUNK

    def weight_copies(c):
        cols = pl.ds(c * MLP_FF_CHUNK, MLP_FF_CHUNK)
        return (pltpu.make_async_copy(win_hbm.at[layer, :, cols], win_s.at[:, cols], sem.at[0, c]),
                pltpu.make_async_copy(wout_hbm.at[layer, cols, :], wout_s.at[cols, :], sem.at[1, c]))

    def body(first_step):
        h = h_ref[...]
        hg = (h * g_ref[...]).astype(BF16)
        r = _inv_rms(h)
        acts = []
        for c in range(n_chunks):
            cols = slice(c * MLP_FF_CHUNK, (c + 1) * MLP_FF_CHUNK)
            if first_step:
                for copy in weight_copies(c):
                    copy.wait()
            a = jnp.dot(hg, win_s[:, cols].astype(BF16), preferred_element_type=F32)
            acts.append(jnp.square(jnp.maximum(a, 0.0)).astype(BF16))
        acc = jnp.dot(jnp.concatenate(acts, axis=1), wout_s[...].astype(BF16),
                      preferred_element_type=F32)
        out = h + (r * r) * acc
        if with_kv:
            o_ref[...] = out
            kv = jnp.dot((out * gkv_ref[...]).astype(BF16), wkv_ref[...].astype(BF16),
                         preferred_element_type=F32)
            kv_ref[...] = (_inv_rms(out) * kv + bkv_ref[...]).astype(BF16)
        else:
            o_ref[...] = out * _inv_rms(out) * gfin_ref[...]

    @pl.when(pl.program_id(0) == 0)
    def _():
        for c in range(n_chunks):
            for copy in weight_copies(c):
                copy.start()
        body(True)

    @pl.when(pl.program_id(0) > 0)
    def _():
        body(False)


def _mlp(h, g, w_in, w_out, layer, *, g_kv=None, w_kv=None, b_kv=None, g_fin=None):
    t, d = h.shape
    with_kv = w_kv is not None
    row = lambda n: pl.BlockSpec((MLP_TM, n), lambda i: (i, 0))
    hbm = pl.BlockSpec(memory_space=pl.ANY)
    in_specs = [row(d), _const_spec((1, d)), hbm, hbm]
    operands = [h, g.reshape(1, d), w_in, w_out]
    if with_kv:
        nkv = w_kv.shape[1]
        in_specs += [_const_spec((1, d)), _const_spec(w_kv.shape), _const_spec((1, nkv))]
        operands += [g_kv.reshape(1, d), w_kv, b_kv.reshape(1, nkv)]
        out_shape = (jax.ShapeDtypeStruct((t, d), F32), jax.ShapeDtypeStruct((t, nkv), BF16))
        out_specs = (row(d), row(nkv))
    else:
        in_specs += [_const_spec((1, d))]
        operands += [g_fin.reshape(1, d)]
        out_shape = jax.ShapeDtypeStruct((t, d), F32)
        out_specs = row(d)
    return pl.pallas_call(
        functools.partial(_mlp_kernel, layer=layer, with_kv=with_kv),
        out_shape=out_shape,
        grid=(t // MLP_TM,),
        in_specs=in_specs,
        out_specs=out_specs,
        scratch_shapes=[pltpu.VMEM(w_in.shape[1:], F32),
                        pltpu.VMEM(w_out.shape[1:], F32),
                        pltpu.SemaphoreType.DMA((2, D_FF // MLP_FF_CHUNK))],
        compiler_params=pltpu.CompilerParams(
            dimension_semantics=("arbitrary",), vmem_limit_bytes=MLP_VMEM_LIMIT_BYTES),
        name="mlp_kv" if with_kv else "mlp_final",
    )(*operands)


def _pair_tile_sources(a, g):
    head_lo = (2 * a) * Q_PER_KV + g
    head_hi = (2 * a + 1) * Q_PER_KV + g
    return (head_lo // 2, head_lo % 2), (head_hi // 2, head_hi % 2)


def _attn_kernel(sink_ref, h_ref, kvp_ref, kvc_ref, g_ref, wq_ref, bq_ref, wo_ref, bo_ref,
                 o_ref, wq_s, bq_s, wo_s, q_s, oh_s, *, tq):
    nsb = tq // WINDOW
    n = pl.program_id(1)
    n_kv_tiles = N_KV_HEADS // 2
    lo_row = lax.broadcasted_iota(jnp.int32, (1, LANES), 1) < HEAD_DIM

    @pl.when((pl.program_id(0) == 0) & (n == 0))
    def _():
        for a in range(n_kv_tiles):
            for g in range(Q_PER_KV):
                (t_lo, h_lo), (t_hi, h_hi) = _pair_tile_sources(a, g)
                dst = slice((a * Q_PER_KV + g) * LANES, (a * Q_PER_KV + g + 1) * LANES)

                def pair(ref):
                    src_lo = ref[:, t_lo * LANES:(t_lo + 1) * LANES]
                    src_hi = ref[:, t_hi * LANES:(t_hi + 1) * LANES]
                    if h_lo == 1:
                        src_lo = pltpu.roll(src_lo, HEAD_DIM, axis=1)
                    if h_hi == 0:
                        src_hi = pltpu.roll(src_hi, HEAD_DIM, axis=1)
                    return jnp.where(lo_row, src_lo, src_hi)

                wq_s[:, dst] = pair(wq_ref).astype(BF16)
                bq_s[:, dst] = pair(bq_ref)
                for half in range(2):
                    head = (2 * a + half) * Q_PER_KV + g
                    r0 = (a * Q_PER_KV + g) * LANES + half * HEAD_DIM
                    wo_s[r0:r0 + HEAD_DIM, :] = (
                        wo_ref[head * HEAD_DIM:(head + 1) * HEAD_DIM, :].astype(BF16))

    h = h_ref[0]
    hn = _rmsnorm(h, g_ref[...]).astype(BF16)
    q = jnp.dot(hn, wq_s[...], preferred_element_type=F32) + bq_s[...]
    q_s[...] = (q * (LOG2_E / math.sqrt(HEAD_DIM))).astype(BF16)

    lo = lax.broadcasted_iota(jnp.int32, (WINDOW, LANES), 1) < HEAD_DIM
    own = (lax.broadcasted_iota(jnp.int32, (WINDOW, WINDOW), 1)
           <= lax.broadcasted_iota(jnp.int32, (WINDOW, WINDOW), 0))
    zero = jnp.zeros((WINDOW, LANES), BF16)
    fzero = jnp.zeros((WINDOW, WINDOW), F32)

    for sb in range(nsb):
        rows = slice(sb * WINDOW, (sb + 1) * WINDOW)
        prev_bias = jnp.where(n == 0, -jnp.inf, 0.0).astype(F32) if sb == 0 else None
        for a in range(n_kv_tiles):
            kl = slice(a * LANES, (a + 1) * LANES)
            vl = slice((n_kv_tiles + a) * LANES, (n_kv_tiles + a + 1) * LANES)
            if sb == 0:
                kprev, vprev = kvp_ref[0, :, kl], kvp_ref[0, :, vl]
            else:
                prow = slice((sb - 1) * WINDOW, sb * WINDOW)
                kprev, vprev = kvc_ref[0, prow, kl], kvc_ref[0, prow, vl]
            kd = jnp.concatenate([kprev, kvc_ref[0, rows, kl]], axis=0)
            vd = jnp.concatenate([vprev, kvc_ref[0, rows, vl]], axis=0)
            vd1 = jnp.concatenate([vd, jnp.ones((2 * WINDOW, LANES), BF16)], axis=1)
            qt = [q_s[rows, (a * Q_PER_KV + g) * LANES:(a * Q_PER_KV + g + 1) * LANES]
                  for g in range(Q_PER_KV)]
            lhs = jnp.concatenate([jnp.where(lo, t, zero) for t in qt]
                                  + [jnp.where(lo, zero, t) for t in qt], axis=0)
            s = lax.dot_general(lhs, kd, (((1,), (1,)), ((), ())),
                                preferred_element_type=F32)
            ps, sink_terms = [], []
            for half in range(2):
                for g in range(Q_PER_KV):
                    blk = half * Q_PER_KV + g
                    s_prev = s[blk * WINDOW:(blk + 1) * WINDOW, :WINDOW]
                    s_own = s[blk * WINDOW:(blk + 1) * WINDOW, WINDOW:]
                    if prev_bias is not None:
                        s_prev = s_prev + prev_bias
                    sg = jnp.where(own, s_own, s_prev)
                    sink = sink_ref[(2 * a + half) * Q_PER_KV + g] * LOG2_E
                    m = jnp.maximum(jnp.max(sg, axis=-1, keepdims=True), sink)
                    p = jnp.exp2(sg - m)
                    ps.append(jnp.concatenate([jnp.where(own, fzero, p).astype(BF16),
                                               jnp.where(own, p, fzero).astype(BF16)], axis=1))
                    sink_terms.append(jnp.exp2(sink - m))
            od = jnp.dot(jnp.concatenate(ps, axis=0), vd1, preferred_element_type=F32)
            for g in range(Q_PER_KV):
                r_lo = slice(g * WINDOW, (g + 1) * WINDOW)
                r_hi = slice((Q_PER_KV + g) * WINDOW, (Q_PER_KV + g + 1) * WINDOW)
                num = jnp.where(lo, od[r_lo, :LANES], od[r_hi, :LANES])
                den = (jnp.where(lo, od[r_lo, LANES:], od[r_hi, LANES:])
                       + jnp.where(lo, sink_terms[g], sink_terms[Q_PER_KV + g]))
                oh_s[rows, (a * Q_PER_KV + g) * LANES:(a * Q_PER_KV + g + 1) * LANES] = (
                    (num * (1.0 / den)).astype(BF16))

    o_ref[0] = h + jnp.dot(oh_s[...], wo_s[...], preferred_element_type=F32) + bo_ref[...]


def _attn_layer(h, kv, sinks, g, w_q, b_q, w_o, b_o):
    bsz, seq, d = h.shape
    nkv = kv.shape[-1]
    tq = ATTN_TQ
    nsb = tq // WINDOW
    kernel = functools.partial(_attn_kernel, tq=tq)
    return pl.pallas_call(
        kernel,
        out_shape=jax.ShapeDtypeStruct((bsz, seq, d), F32),
        grid=(bsz, seq // tq),
        in_specs=[
            pl.BlockSpec(memory_space=pltpu.SMEM),
            pl.BlockSpec((1, tq, d), lambda b, n: (b, n, 0)),
            pl.BlockSpec((1, WINDOW, nkv), lambda b, n: (b, jnp.maximum(n * nsb - 1, 0), 0)),
            pl.BlockSpec((1, tq, nkv), lambda b, n: (b, n, 0)),
            _const_spec((1, d)),
            _const_spec(w_q.shape),
            _const_spec((1, d)),
            _const_spec(w_o.shape),
            _const_spec((1, d)),
        ],
        out_specs=pl.BlockSpec((1, tq, d), lambda b, n: (b, n, 0)),
        scratch_shapes=[pltpu.VMEM((d, d), BF16),
                        pltpu.VMEM((1, d), F32),
                        pltpu.VMEM((d, d), BF16),
                        pltpu.VMEM((tq, d), BF16),
                        pltpu.VMEM((tq, d), BF16)],
        compiler_params=pltpu.CompilerParams(
            dimension_semantics=("arbitrary", "arbitrary"), vmem_limit_bytes=ATTN_VMEM_LIMIT_BYTES),
        name="attn",
    )(sinks, h, kv, kv, g.reshape(1, d), w_q, b_q.reshape(1, d), w_o, b_o.reshape(1, d))


def kernel(x, norm_mix, norm_mlp, norm_kv, norm_final, s5_a_re, s5_a_im, s5_log_dt, s5_b_re, s5_b_im, s5_c_re, s5_c_im, s5_d, s5_w_glu, s5_b_glu, w_kv, b_kv, w_q, b_q, sinks, w_o, b_o, w_mlp_in, w_mlp_out):
    bsz, seq, d = x.shape

    w1, cx, lam2_re_t, lam2_im_t = _s5_params(s5_a_re[0], s5_a_im[0], s5_log_dt[0], s5_b_re[0],
                                              s5_b_im[0], s5_c_re[0], s5_c_im[0])
    h = _s5_layer(x, norm_mix[0], w1, lam2_re_t, lam2_im_t, cx, s5_d[0], s5_w_glu[0], s5_b_glu[0])

    h, kv = _mlp(h.reshape(bsz * seq, d), norm_mlp[0], w_mlp_in, w_mlp_out, 0,
                 g_kv=norm_kv, w_kv=w_kv, b_kv=b_kv)

    h = _attn_layer(h.reshape(bsz, seq, d), kv.reshape(bsz, seq, -1), sinks[0], norm_mix[1],
                    w_q[0], b_q[0], w_o[0], b_o[0])

    out = _mlp(h.reshape(bsz * seq, d), norm_mlp[1], w_mlp_in, w_mlp_out, 1, g_fin=norm_final)
    return out.reshape(bsz, seq, d)
```

```python
import functools
import math

import jax
import jax.numpy as jnp
from jax import lax
from jax.experimental import pallas as pl
from jax.experimental.pallas import tpu as pltpu

F32 = jnp.float32
BF16 = jnp.bfloat16

D_MODEL = 1024
S5_GROUP = 16
S5_STATE = 64
LAMBDA_RE_MAX = -1e-4
HEAD_DIM = 64
N_Q_HEADS = D_MODEL // HEAD_DIM
N_KV_HEADS = 4
Q_PER_KV = N_Q_HEADS // N_KV_HEADS
WINDOW = 128
D_FF = 4 * D_MODEL
NORM_EPS = 1e-5
LOG2_E = math.log2(math.e)

LANES = 128
SUBLANES = 8
N_LANE_TILES = D_MODEL // LANES
GROUPS_PER_TILE = LANES // S5_GROUP
STATES_PER_TILE = GROUPS_PER_TILE * S5_STATE
MIB = 1024 * 1024
S5_VMEM_LIMIT_BYTES = 56 * MIB
MLP_VMEM_LIMIT_BYTES = 52 * MIB
ATTN_VMEM_LIMIT_BYTES = 40 * MIB

S5_TL = 64
S5_PARAM_TILES_PER_STEP = 4
S5_TILES_IN_FLIGHT = 2
MLP_TM = 512
MLP_FF_CHUNK = 1024
ATTN_TQ = 1024


def _rmsnorm(x, g):
    return x * lax.rsqrt(jnp.mean(x * x, axis=-1, keepdims=True) + NORM_EPS) * g


def _gelu_tanh(x):
    c = math.sqrt(2.0 / math.pi)
    return 0.5 * x * (1.0 + jnp.tanh(c * (x + 0.044715 * (x * x * x))))


def _const_spec(shape):
    nd = len(shape)
    return pl.BlockSpec(shape, lambda *_: (0,) * nd, pipeline_mode=pl.Buffered(1))


def _s5_params_kernel(ldt_ref, are_ref, aim_ref, btr_ref, bti_ref, cre_ref, cim_ref,
                      w1_ref, cx_ref, l2r_ref, l2i_ref):
    first_group = pl.program_id(0) * (S5_PARAM_TILES_PER_STEP * GROUPS_PER_TILE)
    for tile in range(S5_PARAM_TILES_PER_STEP):
        log_dt = [ldt_ref[first_group + tile * GROUPS_PER_TILE + g] for g in range(GROUPS_PER_TILE)]
        _s5_params_tile(are_ref.at[tile], aim_ref.at[tile], log_dt, btr_ref.at[tile],
                        bti_ref.at[tile], cre_ref.at[tile], cim_ref.at[tile],
                        w1_ref.at[tile], cx_ref.at[tile], l2r_ref.at[tile], l2i_ref.at[tile])


def _s5_params_tile(are_ref, aim_ref, log_dt, btr_ref, bti_ref, cre_ref, cim_ref,
                    w1_ref, cx_ref, l2r_ref, l2i_ref):
    gpt, grp, nst = GROUPS_PER_TILE, S5_GROUP, S5_STATE
    ar = jnp.minimum(are_ref[...], LAMBDA_RE_MAX)
    ai = aim_ref[...]
    dt = jnp.exp(jnp.concatenate([jnp.full((1, nst), v, F32) for v in log_dt], axis=0))
    mag = jnp.exp(ar * dt)
    ang = ai * dt
    lr = mag * jnp.cos(ang)
    li = mag * jnp.sin(ang)
    den = ar * ar + ai * ai
    cr = ((lr - 1.0) * ar + li * ai) / den
    ci = (li * ar - (lr - 1.0) * ai) / den
    l2r = lr * lr - li * li
    l2i = 2.0 * (lr * li)

    def per_row(v):
        return jnp.concatenate([jnp.broadcast_to(v[g:g + 1], (grp, nst)) for g in range(gpt)], axis=0)

    row_g = lax.broadcasted_iota(jnp.int32, (LANES, STATES_PER_TILE), 0) // grp
    col_g = lax.broadcasted_iota(jnp.int32, (LANES, STATES_PER_TILE), 1) // nst
    own = row_g == col_g

    def blockdiag(v):
        return jnp.where(own, jnp.concatenate([v] * gpt, axis=1), 0.0)

    lr_c, li_c, cr_c, ci_c = per_row(lr), per_row(li), per_row(cr), per_row(ci)
    l2r_c, l2i_c = per_row(l2r), per_row(l2i)
    btr, bti = btr_ref[...], bti_ref[...]
    bbr = cr_c * btr - ci_c * bti
    bbi = cr_c * bti + ci_c * btr
    lbr = lr_c * bbr - li_c * bbi
    lbi = lr_c * bbi + li_c * bbr
    wbr, wbi = blockdiag(bbr), blockdiag(bbi)
    w1r, w1i = blockdiag(lbr), blockdiag(lbi)
    ccr, cci = cre_ref[...], cim_ref[...]

    lhs = jnp.concatenate([jnp.concatenate([bbr, -bbi], axis=1),
                           jnp.concatenate([lbr, -lbi], axis=1)], axis=0)
    kk = lax.dot_general(lhs, jnp.concatenate([ccr, cci], axis=1), (((1,), (1,)), ((), ())),
                         precision=lax.Precision.HIGHEST, preferred_element_type=F32)
    same_group = (lax.broadcasted_iota(jnp.int32, (LANES, LANES), 0) // grp
                  == lax.broadcasted_iota(jnp.int32, (LANES, LANES), 1) // grp)
    k0 = jnp.where(same_group, kk[:LANES], 0.0)
    k1 = jnp.where(same_group, kk[LANES:], 0.0)
    top =jnp.concatenate([k0, k1, w1r, w1i], axis=1)
    bot = jnp.concatenate([jnp.zeros_like(k0), k0, wbr, wbi], axis=1)
    w1_ref[...] = jnp.concatenate([top, bot], axis=0).astype(BF16)

    def readout_t(pr, pi):
        return jnp.concatenate([blockdiag(pr * ccr - pi * cci),
                                -blockdiag(pi * ccr + pr * cci)], axis=1)

    cx_t = jnp.concatenate([readout_t(lr_c, li_c), readout_t(l2r_c, l2i_c)], axis=0)
    cx_ref[...] = cx_t.T.astype(BF16)

    def flat_row(v):
        row = jnp.concatenate([v[g:g + 1] for g in range(gpt)], axis=1)
        return jnp.broadcast_to(row, (SUBLANES, STATES_PER_TILE))

    l2r_ref[...] = flat_row(l2r)
    l2i_ref[...] = flat_row(l2i)


def _s5_params(a_re, a_im, log_dt, b_re, b_im, c_re, c_im):
    nt, gpt, sp = N_LANE_TILES, GROUPS_PER_TILE, STATES_PER_TILE
    tps = S5_PARAM_TILES_PER_STEP
    a_tile = lambda a: a.reshape(nt, gpt, S5_STATE)
    b_tile = lambda b: jnp.swapaxes(b, -1, -2).reshape(nt, LANES, S5_STATE)
    c_tile = lambda c: c.reshape(nt, LANES, S5_STATE)
    return pl.pallas_call(
        _s5_params_kernel,
        out_shape=(jax.ShapeDtypeStruct((nt, 2 * LANES, 2 * LANES + 2 * sp), BF16),
                   jax.ShapeDtypeStruct((nt, 2 * sp, 2 * LANES), BF16),
                   jax.ShapeDtypeStruct((nt, SUBLANES, sp), F32),
                   jax.ShapeDtypeStruct((nt, SUBLANES, sp), F32)),
        grid=(nt // tps,),
        in_specs=[pl.BlockSpec(memory_space=pltpu.SMEM),
                  pl.BlockSpec((tps, gpt, S5_STATE), lambda j: (j, 0, 0)),
                  pl.BlockSpec((tps, gpt, S5_STATE), lambda j: (j, 0, 0)),
                  pl.BlockSpec((tps, LANES, S5_STATE), lambda j: (j, 0, 0)),
                  pl.BlockSpec((tps, LANES, S5_STATE), lambda j: (j, 0, 0)),
                  pl.BlockSpec((tps, LANES, S5_STATE), lambda j: (j, 0, 0)),
                  pl.BlockSpec((tps, LANES, S5_STATE), lambda j: (j, 0, 0))],
        out_specs=(pl.BlockSpec((tps, 2 * LANES, 2 * LANES + 2 * sp), lambda j: (j, 0, 0)),
                   pl.BlockSpec((tps, 2 * sp, 2 * LANES), lambda j: (j, 0, 0)),
                   pl.BlockSpec((tps, SUBLANES, sp), lambda j: (j, 0, 0)),
                   pl.BlockSpec((tps, SUBLANES, sp), lambda j: (j, 0, 0))),
        compiler_params=pltpu.CompilerParams(dimension_semantics=("parallel",)),
        name="s5_params",
    )(log_dt, a_tile(a_re), a_tile(a_im), b_tile(b_re), b_tile(b_im), c_tile(c_re), c_tile(c_im))


def _s5_kernel(x_hbm, g_ref, w1_ref, l2r_ref, l2i_ref, cx_ref, d_ref, wglu_ref, bglu_ref,
               o_hbm, xin_s, hn_s, hnu_s, zy_s, bu_s, xs_s, y_s, st_s, res_s, sem_in, sem_out,
               *, tl, n_steps):
    half = tl // 2
    prow = half * SUBLANES
    sp = STATES_PER_TILE
    i = pl.program_id(0)
    slot = i % 2

    def in_copies(step, sl):
        return [pltpu.make_async_copy(x_hbm.at[b, pl.ds(step * tl, tl), :],
                                      xin_s.at[sl, :, b, :], sem_in.at[sl, b])
                for b in range(SUBLANES)]

    def out_copies(step, sl):
        return [pltpu.make_async_copy(res_s.at[sl, :, b, :],
                                      o_hbm.at[b, pl.ds(step * tl, tl), :], sem_out.at[sl, b])
                for b in range(SUBLANES)]

    in_slot = i % 3

    @pl.when(i == 0)
    def _():
        st_s[...] = jnp.zeros_like(st_s)
        for copy in in_copies(0, 0) + in_copies(1, 1):
            copy.start()

    @pl.when(i + 2 < n_steps)
    def _():
        for copy in in_copies(i + 2, (i + 2) % 3):
            copy.start()

    for copy in in_copies(i, in_slot):
        copy.wait()

    @pl.when(i >= 2)
    def _():
        for copy in out_copies(i - 2, slot):
            copy.wait()

    def row_group(v, t):
        return v[t * SUBLANES:(t + 1) * SUBLANES]

    hn = _rmsnorm(xin_s[in_slot].reshape(tl * SUBLANES, D_MODEL), g_ref[...])
    for t in range(tl):
        g = (t % 2) * half + t // 2
        hn_s[g * SUBLANES:(g + 1) * SUBLANES, :] = row_group(hn, t)
    for m in range(tl // 4):
        r16 = slice(2 * m * SUBLANES, (2 * m + 2) * SUBLANES)
        even = jnp.concatenate([row_group(hn, 4 * m), row_group(hn, 4 * m + 2)], axis=0).astype(BF16)
        odd = jnp.concatenate([row_group(hn, 4 * m + 1), row_group(hn, 4 * m + 3)], axis=0).astype(BF16)
        for j in range(N_LANE_TILES):
            lanes = slice(j * LANES, (j + 1) * LANES)
            hnu_s[r16, 2 * j * LANES:(2 * j + 1) * LANES] = even[:, lanes]
            hnu_s[r16, (2 * j + 1) * LANES:(2 * j + 2) * LANES] = odd[:, lanes]

    def project_in(j):
        z = jnp.dot(hnu_s[:, 2 * j * LANES:(2 * j + 2) * LANES], w1_ref[j],
                    preferred_element_type=F32)
        zy_s[j % S5_TILES_IN_FLIGHT] = z[:, :2 * LANES]
        bu_s[j % S5_TILES_IN_FLIGHT] = z[:, 2 * LANES:]

    ahead = S5_TILES_IN_FLIGHT - 1
    for j in range(ahead):
        project_in(j)
    for j in range(N_LANE_TILES):
        if j + ahead < N_LANE_TILES:
            project_in(j + ahead)
        p = j % S5_TILES_IN_FLIGHT
        lanes = slice(j * LANES, (j + 1) * LANES)
        ar = l2r_ref[j]
        ai = l2i_ref[j]
        xr = st_s[j, :, 0:sp]
        xi = st_s[j, :, sp:2 * sp]
        for k in range(half):
            r = slice(k * SUBLANES, (k + 1) * SUBLANES)
            xs_s[p, r, 0:sp] = xr
            xs_s[p, r, sp:2 * sp] = xi
            nxr = ar * xr - ai * xi + bu_s[p, r, 0:sp]
            nxi = ar * xi + ai * xr + bu_s[p, r, sp:2 * sp]
            xr, xi = nxr, nxi
        st_s[j, :, 0:sp] = xr
        st_s[j, :, sp:2 * sp] = xi
        y = zy_s[p] + jnp.dot(xs_s[p].astype(BF16), cx_ref[j], preferred_element_type=F32)
        y_s[0:prow, lanes] = y[:, :LANES]
        y_s[prow:2 * prow, lanes] = y[:, LANES:]

    for par in range(2):
        rs = slice(par * prow, (par + 1) * prow)
        y = y_s[rs, :] + d_ref[...] * hn_s[rs, :]
        z = jnp.dot(_gelu_tanh(y).astype(BF16), wglu_ref[...].astype(BF16),
                    preferred_element_type=F32) + bglu_ref[...]
        mix = z[:, :D_MODEL] * (1.0 / (1.0 + jnp.exp(-z[:, D_MODEL:])))
        for k in range(half):
            t = 2 * k + par
            res_s[slot, t] = xin_s[in_slot, t] + row_group(mix, k)

    for copy in out_copies(i, slot):
        copy.start()

    @pl.when(i == n_steps - 1)
    def _():
        for copy in out_copies(i - 1, 1 - slot) + out_copies(i, slot):
            copy.wait()


def _s5_layer(x, g_mix, w1, lam2_re_t, lam2_im_t, cx, d_skip, w_glu, b_glu):
    bsz, seq, d = x.shape
    tl = S5_TL
    n_steps = seq // tl
    assert bsz == SUBLANES and n_steps >= 3
    rows = tl * bsz
    prow = rows // 2
    kernel = functools.partial(_s5_kernel, tl=tl, n_steps=n_steps)
    hbm = pl.BlockSpec(memory_space=pl.ANY)
    return pl.pallas_call(
        kernel,
        out_shape=jax.ShapeDtypeStruct((bsz, seq, d), F32),
        grid=(n_steps,),
        in_specs=[
            hbm,
            _const_spec((1, d)),
            _const_spec(w1.shape),
            _const_spec(lam2_re_t.shape),
            _const_spec(lam2_im_t.shape),
            _const_spec(cx.shape),
            _const_spec((1, d)),
            _const_spec(w_glu.shape),
            _const_spec((1, 2 * d)),
        ],
        out_specs=hbm,
        scratch_shapes=[
            pltpu.VMEM((3, tl, bsz, d), F32),
            pltpu.VMEM((rows, d), F32),
            pltpu.VMEM((prow, 2 * d), BF16),
            pltpu.VMEM((S5_TILES_IN_FLIGHT, prow, 2 * LANES), F32),
            pltpu.VMEM((S5_TILES_IN_FLIGHT, prow, 2 * STATES_PER_TILE), F32),
            pltpu.VMEM((S5_TILES_IN_FLIGHT, prow, 2 * STATES_PER_TILE), F32),
            pltpu.VMEM((rows, d), F32),
            pltpu.VMEM((N_LANE_TILES, SUBLANES, 2 * STATES_PER_TILE), F32),
            pltpu.VMEM((2, tl, bsz, d), F32),
            pltpu.SemaphoreType.DMA((3, SUBLANES)),
            pltpu.SemaphoreType.DMA((2, SUBLANES)),
        ],
        compiler_params=pltpu.CompilerParams(
            dimension_semantics=("arbitrary",), vmem_limit_bytes=S5_VMEM_LIMIT_BYTES),
        name="s5_layer",
    )(x, g_mix.reshape(1, d), w1, lam2_re_t, lam2_im_t, cx, d_skip.reshape(1, d), w_glu,
      b_glu.reshape(1, 2 * d))


def _inv_rms(x):
    return lax.rsqrt(jnp.mean(x * x, axis=-1, keepdims=True) + NORM_EPS)


def _mlp_kernel(h_ref, g_ref, win_hbm, wout_hbm, *rest, layer, with_kv):
    if with_kv:
        gkv_ref, wkv_ref, bkv_ref, o_ref, kv_ref, win_s, wout_s, sem = rest
    else:
        gfin_ref, o_ref, win_s, wout_s, sem = rest
    n_chunks = D_FF // MLP_FF_CHUNK

    def weight_copies(c):
        cols = pl.ds(c * MLP_FF_CHUNK, MLP_FF_CHUNK)
        return (pltpu.make_async_copy(win_hbm.at[layer, :, cols], win_s.at[:, cols], sem.at[0, c]),
                pltpu.make_async_copy(wout_hbm.at[layer, cols, :], wout_s.at[cols, :], sem.at[1, c]))

    def body(first_step):
        h = h_ref[...]
        hg = (h * g_ref[...]).astype(BF16)
        r = _inv_rms(h)
        acts = []
        for c in range(n_chunks):
            cols = slice(c * MLP_FF_CHUNK, (c + 1) * MLP_FF_CHUNK)
            if first_step:
                for copy in weight_copies(c):
                    copy.wait()
            a = jnp.dot(hg, win_s[:, cols].astype(BF16), preferred_element_type=F32)
            acts.append(jnp.square(jnp.maximum(a, 0.0)).astype(BF16))
        acc = jnp.dot(jnp.concatenate(acts, axis=1), wout_s[...].astype(BF16),
                      preferred_element_type=F32)
        out = h + (r * r) * acc
        if with_kv:
            o_ref[...] = out
            kv = jnp.dot((out * gkv_ref[...]).astype(BF16), wkv_ref[...].astype(BF16),
                         preferred_element_type=F32)
            kv_ref[...] = (_inv_rms(out) * kv + bkv_ref[...]).astype(BF16)
        else:
            o_ref[...] = out * _inv_rms(out) * gfin_ref[...]

    @pl.when(pl.program_id(0) == 0)
    def _():
        for c in range(n_chunks):
            for copy in weight_copies(c):
                copy.start()
        body(True)

    @pl.when(pl.program_id(0) > 0)
    def _():
        body(False)


def _mlp(h, g, w_in, w_out, layer, *, g_kv=None, w_kv=None, b_kv=None, g_fin=None):
    t, d = h.shape
    with_kv = w_kv is not None
    row = lambda n: pl.BlockSpec((MLP_TM, n), lambda i: (i, 0))
    hbm = pl.BlockSpec(memory_space=pl.ANY)
    in_specs = [row(d), _const_spec((1, d)), hbm, hbm]
    operands = [h, g.reshape(1, d), w_in, w_out]
    if with_kv:
        nkv = w_kv.shape[1]
        in_specs += [_const_spec((1, d)), _const_spec(w_kv.shape), _const_spec((1, nkv))]
        operands += [g_kv.reshape(1, d), w_kv, b_kv.reshape(1, nkv)]
        out_shape = (jax.ShapeDtypeStruct((t, d), F32), jax.ShapeDtypeStruct((t, nkv), BF16))
        out_specs = (row(d), row(nkv))
    else:
        in_specs += [_const_spec((1, d))]
        operands += [g_fin.reshape(1, d)]
        out_shape = jax.ShapeDtypeStruct((t, d), F32)
        out_specs = row(d)
    return pl.pallas_call(
        functools.partial(_mlp_kernel, layer=layer, with_kv=with_kv),
        out_shape=out_shape,
        grid=(t // MLP_TM,),
        in_specs=in_specs,
        out_specs=out_specs,
        scratch_shapes=[pltpu.VMEM(w_in.shape[1:], F32),
                        pltpu.VMEM(w_out.shape[1:], F32),
                        pltpu.SemaphoreType.DMA((2, D_FF // MLP_FF_CHUNK))],
        compiler_params=pltpu.CompilerParams(
            dimension_semantics=("arbitrary",), vmem_limit_bytes=MLP_VMEM_LIMIT_BYTES),
        name="mlp_kv" if with_kv else "mlp_final",
    )(*operands)


def _pair_tile_sources(a, g):
    head_lo = (2 * a) * Q_PER_KV + g
    head_hi = (2 * a + 1) * Q_PER_KV + g
    return (head_lo // 2, head_lo % 2), (head_hi // 2, head_hi % 2)


def _attn_kernel(sink_ref, h_ref, kvp_ref, kvc_ref, g_ref, wq_ref, bq_ref, wo_ref, bo_ref,
                 o_ref, wq_s, bq_s, wo_s, q_s, oh_s, *, tq):
    nsb = tq // WINDOW
    n = pl.program_id(1)
    n_kv_tiles = N_KV_HEADS // 2
    lo_row = lax.broadcasted_iota(jnp.int32, (1, LANES), 1) < HEAD_DIM

    @pl.when((pl.program_id(0) == 0) & (n == 0))
    def _():
        for a in range(n_kv_tiles):
            for g in range(Q_PER_KV):
                (t_lo, h_lo), (t_hi, h_hi) = _pair_tile_sources(a, g)
                dst = slice((a * Q_PER_KV + g) * LANES, (a * Q_PER_KV + g + 1) * LANES)

                def pair(ref):
                    src_lo = ref[:, t_lo * LANES:(t_lo + 1) * LANES]
                    src_hi = ref[:, t_hi * LANES:(t_hi + 1) * LANES]
                    if h_lo == 1:
                        src_lo = pltpu.roll(src_lo, HEAD_DIM, axis=1)
                    if h_hi == 0:
                        src_hi = pltpu.roll(src_hi, HEAD_DIM, axis=1)
                    return jnp.where(lo_row, src_lo, src_hi)

                wq_s[:, dst] = pair(wq_ref).astype(BF16)
                bq_s[:, dst] = pair(bq_ref)
                for half in range(2):
                    head = (2 * a + half) * Q_PER_KV + g
                    r0 = (a * Q_PER_KV + g) * LANES + half * HEAD_DIM
                    wo_s[r0:r0 + HEAD_DIM, :] = (
                        wo_ref[head * HEAD_DIM:(head + 1) * HEAD_DIM, :].astype(BF16))

    h = h_ref[0]
    hn = _rmsnorm(h, g_ref[...]).astype(BF16)
    q = jnp.dot(hn, wq_s[...], preferred_element_type=F32) + bq_s[...]
    q_s[...] = (q * (LOG2_E / math.sqrt(HEAD_DIM))).astype(BF16)

    lo = lax.broadcasted_iota(jnp.int32, (WINDOW, LANES), 1) < HEAD_DIM
    own = (lax.broadcasted_iota(jnp.int32, (WINDOW, WINDOW), 1)
           <= lax.broadcasted_iota(jnp.int32, (WINDOW, WINDOW), 0))
    zero = jnp.zeros((WINDOW, LANES), BF16)
    fzero = jnp.zeros((WINDOW, WINDOW), F32)

    for sb in range(nsb):
        rows = slice(sb * WINDOW, (sb + 1) * WINDOW)
        prev_bias = jnp.where(n == 0, -jnp.inf, 0.0).astype(F32) if sb == 0 else None
        for a in range(n_kv_tiles):
            kl = slice(a * LANES, (a + 1) * LANES)
            vl = slice((n_kv_tiles + a) * LANES, (n_kv_tiles + a + 1) * LANES)
            if sb == 0:
                kprev, vprev = kvp_ref[0, :, kl], kvp_ref[0, :, vl]
            else:
                prow = slice((sb - 1) * WINDOW, sb * WINDOW)
                kprev, vprev = kvc_ref[0, prow, kl], kvc_ref[0, prow, vl]
            kd = jnp.concatenate([kprev, kvc_ref[0, rows, kl]], axis=0)
            vd = jnp.concatenate([vprev, kvc_ref[0, rows, vl]], axis=0)
            vd1 = jnp.concatenate([vd, jnp.ones((2 * WINDOW, LANES), BF16)], axis=1)
            qt = [q_s[rows, (a * Q_PER_KV + g) * LANES:(a * Q_PER_KV + g + 1) * LANES]
                  for g in range(Q_PER_KV)]
            lhs = jnp.concatenate([jnp.where(lo, t, zero) for t in qt]
                                  + [jnp.where(lo, zero, t) for t in qt], axis=0)
            s = lax.dot_general(lhs, kd, (((1,), (1,)), ((), ())),
                                preferred_element_type=F32)
            ps, sink_terms = [], []
            for half in range(2):
                for g in range(Q_PER_KV):
                    blk = half * Q_PER_KV + g
                    s_prev = s[blk * WINDOW:(blk + 1) * WINDOW, :WINDOW]
                    s_own = s[blk * WINDOW:(blk + 1) * WINDOW, WINDOW:]
                    if prev_bias is not None:
                        s_prev = s_prev + prev_bias
                    sg = jnp.where(own, s_own, s_prev)
                    sink = sink_ref[(2 * a + half) * Q_PER_KV + g] * LOG2_E
                    m = jnp.maximum(jnp.max(sg, axis=-1, keepdims=True), sink)
                    p = jnp.exp2(sg - m)
                    ps.append(jnp.concatenate([jnp.where(own, fzero, p).astype(BF16),
                                               jnp.where(own, p, fzero).astype(BF16)], axis=1))
                    sink_terms.append(jnp.exp2(sink - m))
            od = jnp.dot(jnp.concatenate(ps, axis=0), vd1, preferred_element_type=F32)
            for g in range(Q_PER_KV):
                r_lo = slice(g * WINDOW, (g + 1) * WINDOW)
                r_hi = slice((Q_PER_KV + g) * WINDOW, (Q_PER_KV + g + 1) * WINDOW)
                num = jnp.where(lo, od[r_lo, :LANES], od[r_hi, :LANES])
                den = (jnp.where(lo, od[r_lo, LANES:], od[r_hi, LANES:])
                       + jnp.where(lo, sink_terms[g], sink_terms[Q_PER_KV + g]))
                oh_s[rows, (a * Q_PER_KV + g) * LANES:(a * Q_PER_KV + g + 1) * LANES] = (
                    (num * (1.0 / den)).astype(BF16))

    o_ref[0] = h + jnp.dot(oh_s[...], wo_s[...], preferred_element_type=F32) + bo_ref[...]


def _attn_layer(h, kv, sinks, g, w_q, b_q, w_o, b_o):
    bsz, seq, d = h.shape
    nkv = kv.shape[-1]
    tq = ATTN_TQ
    nsb = tq // WINDOW
    kernel = functools.partial(_attn_kernel, tq=tq)
    return pl.pallas_call(
        kernel,
        out_shape=jax.ShapeDtypeStruct((bsz, seq, d), F32),
        grid=(bsz, seq // tq),
        in_specs=[
            pl.BlockSpec(memory_space=pltpu.SMEM),
            pl.BlockSpec((1, tq, d), lambda b, n: (b, n, 0)),
            pl.BlockSpec((1, WINDOW, nkv), lambda b, n: (b, jnp.maximum(n * nsb - 1, 0), 0)),
            pl.BlockSpec((1, tq, nkv), lambda b, n: (b, n, 0)),
            _const_spec((1, d)),
            _const_spec(w_q.shape),
            _const_spec((1, d)),
            _const_spec(w_o.shape),
            _const_spec((1, d)),
        ],
        out_specs=pl.BlockSpec((1, tq, d), lambda b, n: (b, n, 0)),
        scratch_shapes=[pltpu.VMEM((d, d), BF16),
                        pltpu.VMEM((1, d), F32),
                        pltpu.VMEM((d, d), BF16),
                        pltpu.VMEM((tq, d), BF16),
                        pltpu.VMEM((tq, d), BF16)],
        compiler_params=pltpu.CompilerParams(
            dimension_semantics=("arbitrary", "arbitrary"), vmem_limit_bytes=ATTN_VMEM_LIMIT_BYTES),
        name="attn",
    )(sinks, h, kv, kv, g.reshape(1, d), w_q, b_q.reshape(1, d), w_o, b_o.reshape(1, d))


def kernel(x, norm_mix, norm_mlp, norm_kv, norm_final, s5_a_re, s5_a_im, s5_log_dt, s5_b_re, s5_b_im, s5_c_re, s5_c_im, s5_d, s5_w_glu, s5_b_glu, w_kv, b_kv, w_q, b_q, sinks, w_o, b_o, w_mlp_in, w_mlp_out):
    bsz, seq, d = x.shape

    w1, cx, lam2_re_t, lam2_im_t = _s5_params(s5_a_re[0], s5_a_im[0], s5_log_dt[0], s5_b_re[0],
                                              s5_b_im[0], s5_c_re[0], s5_c_im[0])
    h = _s5_layer(x, norm_mix[0], w1, lam2_re_t, lam2_im_t, cx, s5_d[0], s5_w_glu[0], s5_b_glu[0])

    h, kv = _mlp(h.reshape(bsz * seq, d), norm_mlp[0], w_mlp_in, w_mlp_out, 0,
                 g_kv=norm_kv, w_kv=w_kv, b_kv=b_kv)

    h = _attn_layer(h.reshape(bsz, seq, d), kv.reshape(bsz, seq, -1), sinks[0], norm_mix[1],
                    w_q[0], b_q[0], w_o[0], b_o[0])

    out = _mlp(h.reshape(bsz * seq, d), norm_mlp[1], w_mlp_in, w_mlp_out, 1, g_fin=norm_final)
    return out.reshape(bsz, seq, d)
```

```python
import functools
import math

import jax
import jax.numpy as jnp
from jax import lax
from jax.experimental import pallas as pl
from jax.experimental.pallas import tpu as pltpu

F32 = jnp.float32
BF16 = jnp.bfloat16

D_MODEL = 1024
S5_GROUP = 16
S5_STATE = 64
LAMBDA_RE_MAX = -1e-4
HEAD_DIM = 64
N_Q_HEADS = D_MODEL // HEAD_DIM
N_KV_HEADS = 4
Q_PER_KV = N_Q_HEADS // N_KV_HEADS
WINDOW = 128
D_FF = 4 * D_MODEL
NORM_EPS = 1e-5
LOG2_E = math.log2(math.e)

LANES = 128
SUBLANES = 8
N_LANE_TILES = D_MODEL // LANES
GROUPS_PER_TILE = LANES // S5_GROUP
STATES_PER_TILE = GROUPS_PER_TILE * S5_STATE
MIB = 1024 * 1024
S5_VMEM_LIMIT_BYTES = 56 * MIB
MLP_VMEM_LIMIT_BYTES = 52 * MIB
ATTN_VMEM_LIMIT_BYTES = 40 * MIB

S5_TL = 64
S5_PARAM_TILES_PER_STEP = 4
S5_TILES_IN_FLIGHT = 2
MLP_TM = 512
MLP_FF_CHUNK = 1024
ATTN_TQ = 1024


def _rmsnorm(x, g):
    return x * lax.rsqrt(jnp.mean(x * x, axis=-1, keepdims=True) + NORM_EPS) * g


def _gelu_tanh(x):
    c = math.sqrt(2.0 / math.pi)
    return 0.5 * x * (1.0 + jnp.tanh(c * (x + 0.044715 * (x * x * x))))


def _const_spec(shape):
    nd = len(shape)
    return pl.BlockSpec(shape, lambda *_: (0,) * nd, pipeline_mode=pl.Buffered(1))


def _s5_params_kernel(ldt_ref, are_ref, aim_ref, btr_ref, bti_ref, cre_ref, cim_ref,
                      w1_ref, cx_ref, l2r_ref, l2i_ref):
    first_group = pl.program_id(0) * (S5_PARAM_TILES_PER_STEP * GROUPS_PER_TILE)
    for tile in range(S5_PARAM_TILES_PER_STEP):
        log_dt = [ldt_ref[first_group + tile * GROUPS_PER_TILE + g] for g in range(GROUPS_PER_TILE)]
        _s5_params_tile(are_ref.at[tile], aim_ref.at[tile], log_dt, btr_ref.at[tile],
                        bti_ref.at[tile], cre_ref.at[tile], cim_ref.at[tile],
                        w1_ref.at[tile], cx_ref.at[tile], l2r_ref.at[tile], l2i_ref.at[tile])


def _s5_params_tile(are_ref, aim_ref, log_dt, btr_ref, bti_ref, cre_ref, cim_ref,
                    w1_ref, cx_ref, l2r_ref, l2i_ref):
    gpt, grp, nst = GROUPS_PER_TILE, S5_GROUP, S5_STATE
    ar = jnp.minimum(are_ref[...], LAMBDA_RE_MAX)
    ai = aim_ref[...]
    dt = jnp.exp(jnp.concatenate([jnp.full((1, nst), v, F32) for v in log_dt], axis=0))
    mag = jnp.exp(ar * dt)
    ang = ai * dt
    lr = mag * jnp.cos(ang)
    li = mag * jnp.sin(ang)
    den = ar * ar + ai * ai
    cr = ((lr - 1.0) * ar + li * ai) / den
    ci = (li * ar - (lr - 1.0) * ai) / den
    l2r = lr * lr - li * li
    l2i = 2.0 * (lr * li)

    def per_row(v):
        return jnp.concatenate([jnp.broadcast_to(v[g:g + 1], (grp, nst)) for g in range(gpt)], axis=0)

    row_g = lax.broadcasted_iota(jnp.int32, (LANES, STATES_PER_TILE), 0) // grp
    col_g = lax.broadcasted_iota(jnp.int32, (LANES, STATES_PER_TILE), 1) // nst
    own = row_g == col_g

    def blockdiag(v):
        return jnp.where(own, jnp.concatenate([v] * gpt, axis=1), 0.0)

    lr_c, li_c, cr_c, ci_c = per_row(lr), per_row(li), per_row(cr), per_row(ci)
    l2r_c, l2i_c = per_row(l2r), per_row(l2i)
    btr, bti = btr_ref[...], bti_ref[...]
    bbr = cr_c * btr - ci_c * bti
    bbi = cr_c * bti + ci_c * btr
    lbr = lr_c * bbr - li_c * bbi
    lbi = lr_c * bbi + li_c * bbr
    wbr, wbi = blockdiag(bbr), blockdiag(bbi)
    w1r, w1i = blockdiag(lbr), blockdiag(lbi)
    ccr, cci = cre_ref[...], cim_ref[...]

    lhs = jnp.concatenate([jnp.concatenate([bbr, -bbi], axis=1),
                           jnp.concatenate([lbr, -lbi], axis=1)], axis=0)
    kk = lax.dot_general(lhs, jnp.concatenate([ccr, cci], axis=1), (((1,), (1,)), ((), ())),
                         precision=lax.Precision.HIGHEST, preferred_element_type=F32)
    same_group = (lax.broadcasted_iota(jnp.int32, (LANES, LANES), 0) // grp
                  == lax.broadcasted_iota(jnp.int32, (LANES, LANES), 1) // grp)
    k0 = jnp.where(same_group, kk[:LANES], 0.0)
    k1 = jnp.where(same_group, kk[LANES:], 0.0)
    top =jnp.concatenate([k0, k1, w1r, w1i], axis=1)
    bot = jnp.concatenate([jnp.zeros_like(k0), k0, wbr, wbi], axis=1)
    w1_ref[...] = jnp.concatenate([top, bot], axis=0).astype(BF16)

    def readout_t(pr, pi):
        return jnp.concatenate([blockdiag(pr * ccr - pi * cci),
                                -blockdiag(pi * ccr + pr * cci)], axis=1)

    cx_t = jnp.concatenate([readout_t(lr_c, li_c), readout_t(l2r_c, l2i_c)], axis=0)
    cx_ref[...] = cx_t.T.astype(BF16)

    def flat_row(v):
        row = jnp.concatenate([v[g:g + 1] for g in range(gpt)], axis=1)
        return jnp.broadcast_to(row, (SUBLANES, STATES_PER_TILE))

    l2r_ref[...] = flat_row(l2r)
    l2i_ref[...] = flat_row(l2i)


def _s5_params(a_re, a_im, log_dt, b_re, b_im, c_re, c_im):
    nt, gpt, sp = N_LANE_TILES, GROUPS_PER_TILE, STATES_PER_TILE
    tps = S5_PARAM_TILES_PER_STEP
    a_tile = lambda a: a.reshape(nt, gpt, S5_STATE)
    b_tile = lambda b: jnp.swapaxes(b, -1, -2).reshape(nt, LANES, S5_STATE)
    c_tile = lambda c: c.reshape(nt, LANES, S5_STATE)
    return pl.pallas_call(
        _s5_params_kernel,
        out_shape=(jax.ShapeDtypeStruct((nt, 2 * LANES, 2 * LANES + 2 * sp), BF16),
                   jax.ShapeDtypeStruct((nt, 2 * sp, 2 * LANES), BF16),
                   jax.ShapeDtypeStruct((nt, SUBLANES, sp), F32),
                   jax.ShapeDtypeStruct((nt, SUBLANES, sp), F32)),
        grid=(nt // tps,),
        in_specs=[pl.BlockSpec(memory_space=pltpu.SMEM),
                  pl.BlockSpec((tps, gpt, S5_STATE), lambda j: (j, 0, 0)),
                  pl.BlockSpec((tps, gpt, S5_STATE), lambda j: (j, 0, 0)),
                  pl.BlockSpec((tps, LANES, S5_STATE), lambda j: (j, 0, 0)),
                  pl.BlockSpec((tps, LANES, S5_STATE), lambda j: (j, 0, 0)),
                  pl.BlockSpec((tps, LANES, S5_STATE), lambda j: (j, 0, 0)),
                  pl.BlockSpec((tps, LANES, S5_STATE), lambda j: (j, 0, 0))],
        out_specs=(pl.BlockSpec((tps, 2 * LANES, 2 * LANES + 2 * sp), lambda j: (j, 0, 0)),
                   pl.BlockSpec((tps, 2 * sp, 2 * LANES), lambda j: (j, 0, 0)),
                   pl.BlockSpec((tps, SUBLANES, sp), lambda j: (j, 0, 0)),
                   pl.BlockSpec((tps, SUBLANES, sp), lambda j: (j, 0, 0))),
        compiler_params=pltpu.CompilerParams(dimension_semantics=("parallel",)),
        name="s5_params",
    )(log_dt, a_tile(a_re), a_tile(a_im), b_tile(b_re), b_tile(b_im), c_tile(c_re), c_tile(c_im))


def _s5_kernel(x_hbm, g_ref, w1_ref, l2r_ref, l2i_ref, cx_ref, d_ref, wglu_ref, bglu_ref,
               o_hbm, xin_s, hn_s, hnu_s, zy_s, bu_s, xs_s, y_s, st_s, res_s, sem_in, sem_out,
               *, tl, n_steps):
    half = tl // 2
    prow = half * SUBLANES
    sp = STATES_PER_TILE
    i = pl.program_id(0)
    slot = i % 2

    def in_copies(step, sl):
        return [pltpu.make_async_copy(x_hbm.at[b, pl.ds(step * tl, tl), :],
                                      xin_s.at[sl, :, b, :], sem_in.at[sl, b])
                for b in range(SUBLANES)]

    def out_copies(step, sl):
        return [pltpu.make_async_copy(res_s.at[sl, :, b, :],
                                      o_hbm.at[b, pl.ds(step * tl, tl), :], sem_out.at[sl, b])
                for b in range(SUBLANES)]

    in_slot = i % 3

    @pl.when(i == 0)
    def _():
        st_s[...] = jnp.zeros_like(st_s)
        for copy in in_copies(0, 0) + in_copies(1, 1):
            copy.start()

    @pl.when(i + 2 < n_steps)
    def _():
        for copy in in_copies(i + 2, (i + 2) % 3):
            copy.start()

    for copy in in_copies(i, in_slot):
        copy.wait()

    @pl.when(i >= 2)
    def _():
        for copy in out_copies(i - 2, slot):
            copy.wait()

    def row_group(v, t):
        return v[t * SUBLANES:(t + 1) * SUBLANES]

    hn = _rmsnorm(xin_s[in_slot].reshape(tl * SUBLANES, D_MODEL), g_ref[...])
    for t in range(tl):
        g = (t % 2) * half + t // 2
        hn_s[g * SUBLANES:(g + 1) * SUBLANES, :] = row_group(hn, t)
    for m in range(tl // 4):
        r16 = slice(2 * m * SUBLANES, (2 * m + 2) * SUBLANES)
        even = jnp.concatenate([row_group(hn, 4 * m), row_group(hn, 4 * m + 2)], axis=0).astype(BF16)
        odd = jnp.concatenate([row_group(hn, 4 * m + 1), row_group(hn, 4 * m + 3)], axis=0).astype(BF16)
        for j in range(N_LANE_TILES):
            lanes = slice(j * LANES, (j + 1) * LANES)
            hnu_s[r16, 2 * j * LANES:(2 * j + 1) * LANES] = even[:, lanes]
            hnu_s[r16, (2 * j + 1) * LANES:(2 * j + 2) * LANES] = odd[:, lanes]

    def project_in(j):
        z = jnp.dot(hnu_s[:, 2 * j * LANES:(2 * j + 2) * LANES], w1_ref[j],
                    preferred_element_type=F32)
        zy_s[j % S5_TILES_IN_FLIGHT] = z[:, :2 * LANES]
        bu_s[j % S5_TILES_IN_FLIGHT] = z[:, 2 * LANES:]

    ahead = S5_TILES_IN_FLIGHT - 1
    for j in range(ahead):
        project_in(j)
    for j in range(N_LANE_TILES):
        if j + ahead < N_LANE_TILES:
            project_in(j + ahead)
        p = j % S5_TILES_IN_FLIGHT
        lanes = slice(j * LANES, (j + 1) * LANES)
        ar = l2r_ref[j]
        ai = l2i_ref[j]
        xr = st_s[j, :, 0:sp]
        xi = st_s[j, :, sp:2 * sp]
        for k in range(half):
            r = slice(k * SUBLANES, (k + 1) * SUBLANES)
            xs_s[p, r, 0:sp] = xr
            xs_s[p, r, sp:2 * sp] = xi
            nxr = ar * xr - ai * xi + bu_s[p, r, 0:sp]
            nxi = ar * xi + ai * xr + bu_s[p, r, sp:2 * sp]
            xr, xi = nxr, nxi
        st_s[j, :, 0:sp] = xr
        st_s[j, :, sp:2 * sp] = xi
        y = zy_s[p] + jnp.dot(xs_s[p].astype(BF16), cx_ref[j], preferred_element_type=F32)
        y_s[0:prow, lanes] = y[:, :LANES]
        y_s[prow:2 * prow, lanes] = y[:, LANES:]

    for par in range(2):
        rs = slice(par * prow, (par + 1) * prow)
        y = y_s[rs, :] + d_ref[...] * hn_s[rs, :]
        z = jnp.dot(_gelu_tanh(y).astype(BF16), wglu_ref[...].astype(BF16),
                    preferred_element_type=F32) + bglu_ref[...]
        mix = z[:, :D_MODEL] * (1.0 / (1.0 + jnp.exp(-z[:, D_MODEL:])))
        for k in range(half):
            t = 2 * k + par
            res_s[slot, t] = xin_s[in_slot, t] + row_group(mix, k)

    for copy in out_copies(i, slot):
        copy.start()

    @pl.when(i == n_steps - 1)
    def _():
        for copy in out_copies(i - 1, 1 - slot) + out_copies(i, slot):
            copy.wait()


def _s5_layer(x, g_mix, w1, lam2_re_t, lam2_im_t, cx, d_skip, w_glu, b_glu):
    bsz, seq, d = x.shape
    tl = S5_TL
    n_steps = seq // tl
    assert bsz == SUBLANES and n_steps >= 3
    rows = tl * bsz
    prow = rows // 2
    kernel = functools.partial(_s5_kernel, tl=tl, n_steps=n_steps)
    hbm = pl.BlockSpec(memory_space=pl.ANY)
    return pl.pallas_call(
        kernel,
        out_shape=jax.ShapeDtypeStruct((bsz, seq, d), F32),
        grid=(n_steps,),
        in_specs=[
            hbm,
            _const_spec((1, d)),
            _const_spec(w1.shape),
            _const_spec(lam2_re_t.shape),
            _const_spec(lam2_im_t.shape),
            _const_spec(cx.shape),
            _const_spec((1, d)),
            _const_spec(w_glu.shape),
            _const_spec((1, 2 * d)),
        ],
        out_specs=hbm,
        scratch_shapes=[
            pltpu.VMEM((3, tl, bsz, d), F32),
            pltpu.VMEM((rows, d), F32),
            pltpu.VMEM((prow, 2 * d), BF16),
            pltpu.VMEM((S5_TILES_IN_FLIGHT, prow, 2 * LANES), F32),
            pltpu.VMEM((S5_TILES_IN_FLIGHT, prow, 2 * STATES_PER_TILE), F32),
            pltpu.VMEM((S5_TILES_IN_FLIGHT, prow, 2 * STATES_PER_TILE), F32),
            pltpu.VMEM((rows, d), F32),
            pltpu.VMEM((N_LANE_TILES, SUBLANES, 2 * STATES_PER_TILE), F32),
            pltpu.VMEM((2, tl, bsz, d), F32),
            pltpu.SemaphoreType.DMA((3, SUBLANES)),
            pltpu.SemaphoreType.DMA((2, SUBLANES)),
        ],
        compiler_params=pltpu.CompilerParams(
            dimension_semantics=("arbitrary",), vmem_limit_bytes=S5_VMEM_LIMIT_BYTES),
        name="s5_layer",
    )(x, g_mix.reshape(1, d), w1, lam2_re_t, lam2_im_t, cx, d_skip.reshape(1, d), w_glu,
      b_glu.reshape(1, 2 * d))


def _inv_rms(x):
    return lax.rsqrt(jnp.mean(x * x, axis=-1, keepdims=True) + NORM_EPS)


def _mlp_kernel(h_ref, g_ref, win_hbm, wout_hbm, *rest, layer, with_kv):
    if with_kv:
        gkv_ref, wkv_ref, bkv_ref, o_ref, kv_ref, win_s, wout_s, sem = rest
    else:
        gfin_ref, o_ref, win_s, wout_s, sem = rest
    n_chunks = D_FF // MLP_FF_CHUNK

    def weight_copies(c):
        cols = pl.ds(c * MLP_FF_CHUNK, MLP_FF_CHUNK)
        return (pltpu.make_async_copy(win_hbm.at[layer, :, cols], win_s.at[:, cols], sem.at[0, c]),
                pltpu.make_async_copy(wout_hbm.at[layer, cols, :], wout_s.at[cols, :], sem.at[1, c]))

    def body(first_step):
        h = h_ref[...]
        hg = (h * g_ref[...]).astype(BF16)
        r = _inv_rms(h)
        acts = []
        for c in range(n_chunks):
            cols = slice(c * MLP_FF_CHUNK, (c + 1) * MLP_FF_CHUNK)
            if first_step:
                weight_copies(c)[0].wait()
            a = jnp.dot(hg, win_s[:, cols].astype(BF16), preferred_element_type=F32)
            acts.append(jnp.square(jnp.maximum(a, 0.0)).astype(BF16))
            if first_step:
                weight_copies(c)[1].wait()
                part = jnp.dot(acts[c], wout_s[cols, :].astype(BF16), preferred_element_type=F32)
                o_ref[...] = part if c == 0 else o_ref[...] + part
        if first_step:
            acc = o_ref[...]
        else:
            acc = jnp.dot(jnp.concatenate(acts, axis=1), wout_s[...].astype(BF16),
                          preferred_element_type=F32)
        out = h + (r * r) * acc
        if with_kv:
            o_ref[...] = out
            kv = jnp.dot((out * gkv_ref[...]).astype(BF16), wkv_ref[...].astype(BF16),
                         preferred_element_type=F32)
            kv_ref[...] = (_inv_rms(out) * kv + bkv_ref[...]).astype(BF16)
        else:
            o_ref[...] = out * _inv_rms(out) * gfin_ref[...]

    @pl.when(pl.program_id(0) == 0)
    def _():
        for c in range(n_chunks):
            for copy in weight_copies(c):
                copy.start()
        body(True)

    @pl.when(pl.program_id(0) > 0)
    def _():
        body(False)


def _mlp(h, g, w_in, w_out, layer, *, g_kv=None, w_kv=None, b_kv=None, g_fin=None):
    t, d = h.shape
    with_kv = w_kv is not None
    row = lambda n: pl.BlockSpec((MLP_TM, n), lambda i: (i, 0))
    hbm = pl.BlockSpec(memory_space=pl.ANY)
    in_specs = [row(d), _const_spec((1, d)), hbm, hbm]
    operands = [h, g.reshape(1, d), w_in, w_out]
    if with_kv:
        nkv = w_kv.shape[1]
        in_specs += [_const_spec((1, d)), _const_spec(w_kv.shape), _const_spec((1, nkv))]
        operands += [g_kv.reshape(1, d), w_kv, b_kv.reshape(1, nkv)]
        out_shape = (jax.ShapeDtypeStruct((t, d), F32), jax.ShapeDtypeStruct((t, nkv), BF16))
        out_specs = (row(d), row(nkv))
    else:
        in_specs += [_const_spec((1, d))]
        operands += [g_fin.reshape(1, d)]
        out_shape = jax.ShapeDtypeStruct((t, d), F32)
        out_specs = row(d)
    return pl.pallas_call(
        functools.partial(_mlp_kernel, layer=layer, with_kv=with_kv),
        out_shape=out_shape,
        grid=(t // MLP_TM,),
        in_specs=in_specs,
        out_specs=out_specs,
        scratch_shapes=[pltpu.VMEM(w_in.shape[1:], F32),
                        pltpu.VMEM(w_out.shape[1:], F32),
                        pltpu.SemaphoreType.DMA((2, D_FF // MLP_FF_CHUNK))],
        compiler_params=pltpu.CompilerParams(
            dimension_semantics=("arbitrary",), vmem_limit_bytes=MLP_VMEM_LIMIT_BYTES),
        name="mlp_kv" if with_kv else "mlp_final",
    )(*operands)


def _pair_tile_sources(a, g):
    head_lo = (2 * a) * Q_PER_KV + g
    head_hi = (2 * a + 1) * Q_PER_KV + g
    return (head_lo // 2, head_lo % 2), (head_hi // 2, head_hi % 2)


def _attn_kernel(sink_ref, h_ref, kvp_ref, kvc_ref, g_ref, wq_ref, bq_ref, wo_ref, bo_ref,
                 o_ref, wq_s, bq_s, wo_s, q_s, oh_s, *, tq):
    nsb = tq // WINDOW
    n = pl.program_id(1)
    n_kv_tiles = N_KV_HEADS // 2
    lo_row = lax.broadcasted_iota(jnp.int32, (1, LANES), 1) < HEAD_DIM

    @pl.when((pl.program_id(0) == 0) & (n == 0))
    def _():
        for a in range(n_kv_tiles):
            for g in range(Q_PER_KV):
                (t_lo, h_lo), (t_hi, h_hi) = _pair_tile_sources(a, g)
                dst = slice((a * Q_PER_KV + g) * LANES, (a * Q_PER_KV + g + 1) * LANES)

                def pair(ref):
                    src_lo = ref[:, t_lo * LANES:(t_lo + 1) * LANES]
                    src_hi = ref[:, t_hi * LANES:(t_hi + 1) * LANES]
                    if h_lo == 1:
                        src_lo = pltpu.roll(src_lo, HEAD_DIM, axis=1)
                    if h_hi == 0:
                        src_hi = pltpu.roll(src_hi, HEAD_DIM, axis=1)
                    return jnp.where(lo_row, src_lo, src_hi)

                wq_s[:, dst] = pair(wq_ref).astype(BF16)
                bq_s[:, dst] = pair(bq_ref)
                for half in range(2):
                    head = (2 * a + half) * Q_PER_KV + g
                    r0 = (a * Q_PER_KV + g) * LANES + half * HEAD_DIM
                    wo_s[r0:r0 + HEAD_DIM, :] = (
                        wo_ref[head * HEAD_DIM:(head + 1) * HEAD_DIM, :].astype(BF16))

    h = h_ref[0]
    hn = _rmsnorm(h, g_ref[...]).astype(BF16)
    q = jnp.dot(hn, wq_s[...], preferred_element_type=F32) + bq_s[...]
    q_s[...] = (q * (LOG2_E / math.sqrt(HEAD_DIM))).astype(BF16)

    lo = lax.broadcasted_iota(jnp.int32, (WINDOW, LANES), 1) < HEAD_DIM
    own = (lax.broadcasted_iota(jnp.int32, (WINDOW, WINDOW), 1)
           <= lax.broadcasted_iota(jnp.int32, (WINDOW, WINDOW), 0))
    zero = jnp.zeros((WINDOW, LANES), BF16)
    fzero = jnp.zeros((WINDOW, WINDOW), F32)

    for sb in range(nsb):
        rows = slice(sb * WINDOW, (sb + 1) * WINDOW)
        prev_bias = jnp.where(n == 0, -jnp.inf, 0.0).astype(F32) if sb == 0 else None
        for a in range(n_kv_tiles):
            kl = slice(a * LANES, (a + 1) * LANES)
            vl = slice((n_kv_tiles + a) * LANES, (n_kv_tiles + a + 1) * LANES)
            if sb == 0:
                kprev, vprev = kvp_ref[0, :, kl], kvp_ref[0, :, vl]
            else:
                prow = slice((sb - 1) * WINDOW, sb * WINDOW)
                kprev, vprev = kvc_ref[0, prow, kl], kvc_ref[0, prow, vl]
            kd = jnp.concatenate([kprev, kvc_ref[0, rows, kl]], axis=0)
            vd = jnp.concatenate([vprev, kvc_ref[0, rows, vl]], axis=0)
            vd1 = jnp.concatenate([vd, jnp.ones((2 * WINDOW, LANES), BF16)], axis=1)
            qt = [q_s[rows, (a * Q_PER_KV + g) * LANES:(a * Q_PER_KV + g + 1) * LANES]
                  for g in range(Q_PER_KV)]
            lhs = jnp.concatenate([jnp.where(lo, t, zero) for t in qt]
                                  + [jnp.where(lo, zero, t) for t in qt], axis=0)
            s = lax.dot_general(lhs, kd, (((1,), (1,)), ((), ())),
                                preferred_element_type=F32)
            ps, sink_terms = [], []
            for half in range(2):
                for g in range(Q_PER_KV):
                    blk = half * Q_PER_KV + g
                    s_prev = s[blk * WINDOW:(blk + 1) * WINDOW, :WINDOW]
                    s_own = s[blk * WINDOW:(blk + 1) * WINDOW, WINDOW:]
                    if prev_bias is not None:
                        s_prev = s_prev + prev_bias
                    sg = jnp.where(own, s_own, s_prev)
                    sink = sink_ref[(2 * a + half) * Q_PER_KV + g] * LOG2_E
                    m = jnp.maximum(jnp.max(sg, axis=-1, keepdims=True), sink)
                    p = jnp.exp2(sg - m)
                    ps.append(jnp.concatenate([jnp.where(own, fzero, p).astype(BF16),
                                               jnp.where(own, p, fzero).astype(BF16)], axis=1))
                    sink_terms.append(jnp.exp2(sink - m))
            od = jnp.dot(jnp.concatenate(ps, axis=0), vd1, preferred_element_type=F32)
            for g in range(Q_PER_KV):
                r_lo = slice(g * WINDOW, (g + 1) * WINDOW)
                r_hi = slice((Q_PER_KV + g) * WINDOW, (Q_PER_KV + g + 1) * WINDOW)
                num = jnp.where(lo, od[r_lo, :LANES], od[r_hi, :LANES])
                den = (jnp.where(lo, od[r_lo, LANES:], od[r_hi, LANES:])
                       + jnp.where(lo, sink_terms[g], sink_terms[Q_PER_KV + g]))
                oh_s[rows, (a * Q_PER_KV + g) * LANES:(a * Q_PER_KV + g + 1) * LANES] = (
                    (num * (1.0 / den)).astype(BF16))

    o_ref[0] = h + jnp.dot(oh_s[...], wo_s[...], preferred_element_type=F32) + bo_ref[...]


def _attn_layer(h, kv, sinks, g, w_q, b_q, w_o, b_o):
    bsz, seq, d = h.shape
    nkv = kv.shape[-1]
    tq = ATTN_TQ
    nsb = tq // WINDOW
    kernel = functools.partial(_attn_kernel, tq=tq)
    return pl.pallas_call(
        kernel,
        out_shape=jax.ShapeDtypeStruct((bsz, seq, d), F32),
        grid=(bsz, seq // tq),
        in_specs=[
            pl.BlockSpec(memory_space=pltpu.SMEM),
            pl.BlockSpec((1, tq, d), lambda b, n: (b, n, 0)),
            pl.BlockSpec((1, WINDOW, nkv), lambda b, n: (b, jnp.maximum(n * nsb - 1, 0), 0)),
            pl.BlockSpec((1, tq, nkv), lambda b, n: (b, n, 0)),
            _const_spec((1, d)),
            _const_spec(w_q.shape),
            _const_spec((1, d)),
            _const_spec(w_o.shape),
            _const_spec((1, d)),
        ],
        out_specs=pl.BlockSpec((1, tq, d), lambda b, n: (b, n, 0)),
        scratch_shapes=[pltpu.VMEM((d, d), BF16),
                        pltpu.VMEM((1, d), F32),
                        pltpu.VMEM((d, d), BF16),
                        pltpu.VMEM((tq, d), BF16),
                        pltpu.VMEM((tq, d), BF16)],
        compiler_params=pltpu.CompilerParams(
            dimension_semantics=("arbitrary", "arbitrary"), vmem_limit_bytes=ATTN_VMEM_LIMIT_BYTES),
        name="attn",
    )(sinks, h, kv, kv, g.reshape(1, d), w_q, b_q.reshape(1, d), w_o, b_o.reshape(1, d))


def kernel(x, norm_mix, norm_mlp, norm_kv, norm_final, s5_a_re, s5_a_im, s5_log_dt, s5_b_re, s5_b_im, s5_c_re, s5_c_im, s5_d, s5_w_glu, s5_b_glu, w_kv, b_kv, w_q, b_q, sinks, w_o, b_o, w_mlp_in, w_mlp_out):
    bsz, seq, d = x.shape

    w1, cx, lam2_re_t, lam2_im_t = _s5_params(s5_a_re[0], s5_a_im[0], s5_log_dt[0], s5_b_re[0],
                                              s5_b_im[0], s5_c_re[0], s5_c_im[0])
    h = _s5_layer(x, norm_mix[0], w1, lam2_re_t, lam2_im_t, cx, s5_d[0], s5_w_glu[0], s5_b_glu[0])

    h, kv = _mlp(h.reshape(bsz * seq, d), norm_mlp[0], w_mlp_in, w_mlp_out, 0,
                 g_kv=norm_kv, w_kv=w_kv, b_kv=b_kv)

    h = _attn_layer(h.reshape(bsz, seq, d), kv.reshape(bsz, seq, -1), sinks[0], norm_mix[1],
                    w_q[0], b_q[0], w_o[0], b_o[0])

    out = _mlp(h.reshape(bsz * seq, d), norm_mlp[1], w_mlp_in, w_mlp_out, 1, g_fin=norm_final)
    return out.reshape(bsz, seq, d)
```

```python
import functools
import math

import jax
import jax.numpy as jnp
from jax import lax
from jax.experimental import pallas as pl
from jax.experimental.pallas import tpu as pltpu

F32 = jnp.float32
BF16 = jnp.bfloat16

D_MODEL = 1024
S5_GROUP = 16
S5_STATE = 64
LAMBDA_RE_MAX = -1e-4
HEAD_DIM = 64
N_Q_HEADS = D_MODEL // HEAD_DIM
N_KV_HEADS = 4
Q_PER_KV = N_Q_HEADS // N_KV_HEADS
WINDOW = 128
D_FF = 4 * D_MODEL
NORM_EPS = 1e-5
LOG2_E = math.log2(math.e)

LANES = 128
SUBLANES = 8
N_LANE_TILES = D_MODEL // LANES
GROUPS_PER_TILE = LANES // S5_GROUP
STATES_PER_TILE = GROUPS_PER_TILE * S5_STATE
MIB = 1024 * 1024
S5_VMEM_LIMIT_BYTES = 56 * MIB
MLP_VMEM_LIMIT_BYTES = 52 * MIB
ATTN_VMEM_LIMIT_BYTES = 40 * MIB

S5_TL = 64
S5_TILES_IN_FLIGHT = 2
MLP_TM = 512
MLP_FF_CHUNK = 1024
ATTN_TQ = 1024


def _rmsnorm(x, g):
    return x * lax.rsqrt(jnp.mean(x * x, axis=-1, keepdims=True) + NORM_EPS) * g


def _gelu_tanh(x):
    c = math.sqrt(2.0 / math.pi)
    return 0.5 * x * (1.0 + jnp.tanh(c * (x + 0.044715 * (x * x * x))))


def _const_spec(shape):
    nd = len(shape)
    return pl.BlockSpec(shape, lambda *_: (0,) * nd, pipeline_mode=pl.Buffered(1))


def _s5_fold_params(ldt_ref, are_ref, aim_ref, btr_ref, bti_ref, cre_ref, cim_ref,
                    w1_ref, cx_ref, l2r_ref, l2i_ref):
    for tile in range(N_LANE_TILES):
        log_dt = [ldt_ref[tile * GROUPS_PER_TILE + g] for g in range(GROUPS_PER_TILE)]
        _s5_params_tile(are_ref.at[tile], aim_ref.at[tile], log_dt, btr_ref.at[tile],
                        bti_ref.at[tile], cre_ref.at[tile], cim_ref.at[tile],
                        w1_ref.at[tile], cx_ref.at[tile], l2r_ref.at[tile], l2i_ref.at[tile])


def _s5_params_tile(are_ref, aim_ref, log_dt, btr_ref, bti_ref, cre_ref, cim_ref,
                    w1_ref, cx_ref, l2r_ref, l2i_ref):
    gpt, grp, nst = GROUPS_PER_TILE, S5_GROUP, S5_STATE
    ar = jnp.minimum(are_ref[...], LAMBDA_RE_MAX)
    ai = aim_ref[...]
    dt = jnp.exp(jnp.concatenate([jnp.full((1, nst), v, F32) for v in log_dt], axis=0))
    mag = jnp.exp(ar * dt)
    ang = ai * dt
    lr = mag * jnp.cos(ang)
    li = mag * jnp.sin(ang)
    den = ar * ar + ai * ai
    cr = ((lr - 1.0) * ar + li * ai) / den
    ci = (li * ar - (lr - 1.0) * ai) / den
    l2r = lr * lr - li * li
    l2i = 2.0 * (lr * li)

    def per_row(v):
        return jnp.concatenate([jnp.broadcast_to(v[g:g + 1], (grp, nst)) for g in range(gpt)], axis=0)

    row_g = lax.broadcasted_iota(jnp.int32, (LANES, STATES_PER_TILE), 0) // grp
    col_g = lax.broadcasted_iota(jnp.int32, (LANES, STATES_PER_TILE), 1) // nst
    own = row_g == col_g

    def blockdiag(v):
        return jnp.where(own, jnp.concatenate([v] * gpt, axis=1), 0.0)

    lr_c, li_c, cr_c, ci_c = per_row(lr), per_row(li), per_row(cr), per_row(ci)
    l2r_c, l2i_c = per_row(l2r), per_row(l2i)
    btr, bti = btr_ref[...], bti_ref[...]
    bbr = cr_c * btr - ci_c * bti
    bbi = cr_c * bti + ci_c * btr
    lbr = lr_c * bbr - li_c * bbi
    lbi = lr_c * bbi + li_c * bbr
    wbr, wbi = blockdiag(bbr), blockdiag(bbi)
    w1r, w1i = blockdiag(lbr), blockdiag(lbi)
    ccr, cci = cre_ref[...], cim_ref[...]

    lhs = jnp.concatenate([jnp.concatenate([bbr, -bbi], axis=1),
                           jnp.concatenate([lbr, -lbi], axis=1)], axis=0)
    kk = lax.dot_general(lhs, jnp.concatenate([ccr, cci], axis=1), (((1,), (1,)), ((), ())),
                         precision=lax.Precision.HIGHEST, preferred_element_type=F32)
    same_group = (lax.broadcasted_iota(jnp.int32, (LANES, LANES), 0) // grp
                  == lax.broadcasted_iota(jnp.int32, (LANES, LANES), 1) // grp)
    k0 = jnp.where(same_group, kk[:LANES], 0.0)
    k1 = jnp.where(same_group, kk[LANES:], 0.0)
    top =jnp.concatenate([k0, k1, w1r, w1i], axis=1)
    bot = jnp.concatenate([jnp.zeros_like(k0), k0, wbr, wbi], axis=1)
    w1_ref[...] = jnp.concatenate([top, bot], axis=0).astype(BF16)

    def readout_t(pr, pi):
        return jnp.concatenate([blockdiag(pr * ccr - pi * cci),
                                -blockdiag(pi * ccr + pr * cci)], axis=1)

    cx_t = jnp.concatenate([readout_t(lr_c, li_c), readout_t(l2r_c, l2i_c)], axis=0)
    cx_ref[...] = cx_t.T.astype(BF16)

    def flat_row(v):
        row = jnp.concatenate([v[g:g + 1] for g in range(gpt)], axis=1)
        return jnp.broadcast_to(row, (SUBLANES, STATES_PER_TILE))

    l2r_ref[...] = flat_row(l2r)
    l2i_ref[...] = flat_row(l2i)


def _s5_kernel(x_hbm, g_ref, ldt_ref, are_ref, aim_ref, btr_ref, bti_ref, cre_ref, cim_ref,
               d_ref, wglu_hbm, bglu_ref,
               o_hbm, xin_s, hn_s, hnu_s, zy_s, bu_s, xs_s, y_s, st_s, res_s,
               w1_s, cx_s, l2r_s, l2i_s, wglu_s, sem_in, sem_out, sem_w, *, tl, n_steps):
    half = tl // 2
    prow = half * SUBLANES
    sp = STATES_PER_TILE
    i = pl.program_id(0)
    slot = i % 2

    def in_copies(step, sl):
        return [pltpu.make_async_copy(x_hbm.at[b, pl.ds(step * tl, tl), :],
                                      xin_s.at[sl, :, b, :], sem_in.at[sl, b])
                for b in range(SUBLANES)]

    def out_copies(step, sl):
        return [pltpu.make_async_copy(res_s.at[sl, :, b, :],
                                      o_hbm.at[b, pl.ds(step * tl, tl), :], sem_out.at[sl, b])
                for b in range(SUBLANES)]

    in_slot = i % 3

    @pl.when(i == 0)
    def _():
        st_s[...] = jnp.zeros_like(st_s)
        wglu_copy = pltpu.make_async_copy(wglu_hbm, wglu_s, sem_w)
        for copy in in_copies(0, 0) + [wglu_copy] + in_copies(1, 1):
            copy.start()
        _s5_fold_params(ldt_ref, are_ref, aim_ref, btr_ref, bti_ref, cre_ref, cim_ref,
                        w1_s, cx_s, l2r_s, l2i_s)
        wglu_copy.wait()

    @pl.when(i + 2 < n_steps)
    def _():
        for copy in in_copies(i + 2, (i + 2) % 3):
            copy.start()

    for copy in in_copies(i, in_slot):
        copy.wait()

    @pl.when(i >= 2)
    def _():
        for copy in out_copies(i - 2, slot):
            copy.wait()

    def row_group(v, t):
        return v[t * SUBLANES:(t + 1) * SUBLANES]

    hn = _rmsnorm(xin_s[in_slot].reshape(tl * SUBLANES, D_MODEL), g_ref[...])
    for t in range(tl):
        g = (t % 2) * half + t // 2
        hn_s[g * SUBLANES:(g + 1) * SUBLANES, :] = row_group(hn, t)
    for m in range(tl // 4):
        r16 = slice(2 * m * SUBLANES, (2 * m + 2) * SUBLANES)
        even = jnp.concatenate([row_group(hn, 4 * m), row_group(hn, 4 * m + 2)], axis=0).astype(BF16)
        odd = jnp.concatenate([row_group(hn, 4 * m + 1), row_group(hn, 4 * m + 3)], axis=0).astype(BF16)
        for j in range(N_LANE_TILES):
            lanes = slice(j * LANES, (j + 1) * LANES)
            hnu_s[r16, 2 * j * LANES:(2 * j + 1) * LANES] = even[:, lanes]
            hnu_s[r16, (2 * j + 1) * LANES:(2 * j + 2) * LANES] = odd[:, lanes]

    def project_in(j):
        z = jnp.dot(hnu_s[:, 2 * j * LANES:(2 * j + 2) * LANES], w1_s[j],
                    preferred_element_type=F32)
        zy_s[j % S5_TILES_IN_FLIGHT] = z[:, :2 * LANES]
        bu_s[j % S5_TILES_IN_FLIGHT] = z[:, 2 * LANES:]

    ahead = S5_TILES_IN_FLIGHT - 1
    for j in range(ahead):
        project_in(j)
    for j in range(N_LANE_TILES):
        if j + ahead < N_LANE_TILES:
            project_in(j + ahead)
        p = j % S5_TILES_IN_FLIGHT
        lanes = slice(j * LANES, (j + 1) * LANES)
        ar = l2r_s[j]
        ai = l2i_s[j]
        xr = st_s[j, :, 0:sp]
        xi = st_s[j, :, sp:2 * sp]
        for k in range(half):
            r = slice(k * SUBLANES, (k + 1) * SUBLANES)
            xs_s[p, r, 0:sp] = xr
            xs_s[p, r, sp:2 * sp] = xi
            nxr = ar * xr - ai * xi + bu_s[p, r, 0:sp]
            nxi = ar * xi + ai * xr + bu_s[p, r, sp:2 * sp]
            xr, xi = nxr, nxi
        st_s[j, :, 0:sp] = xr
        st_s[j, :, sp:2 * sp] = xi
        y = zy_s[p] + jnp.dot(xs_s[p].astype(BF16), cx_s[j], preferred_element_type=F32)
        y_s[0:prow, lanes] = y[:, :LANES]
        y_s[prow:2 * prow, lanes] = y[:, LANES:]

    for par in range(2):
        rs = slice(par * prow, (par + 1) * prow)
        y = y_s[rs, :] + d_ref[...] * hn_s[rs, :]
        z = jnp.dot(_gelu_tanh(y).astype(BF16), wglu_s[...].astype(BF16),
                    preferred_element_type=F32) + bglu_ref[...]
        mix = z[:, :D_MODEL] * (1.0 / (1.0 + jnp.exp(-z[:, D_MODEL:])))
        for k in range(half):
            t = 2 * k + par
            res_s[slot, t] = xin_s[in_slot, t] + row_group(mix, k)

    for copy in out_copies(i, slot):
        copy.start()

    @pl.when(i == n_steps - 1)
    def _():
        for copy in out_copies(i - 1, 1 - slot) + out_copies(i, slot):
            copy.wait()


def _s5_layer(x, g_mix, a_re, a_im, log_dt, b_re, b_im, c_re, c_im, d_skip, w_glu, b_glu):
    bsz, seq, d = x.shape
    nt, gpt, sp = N_LANE_TILES, GROUPS_PER_TILE, STATES_PER_TILE
    a_tile = lambda a: a.reshape(nt, gpt, S5_STATE)
    b_tile = lambda b: jnp.swapaxes(b, -1, -2).reshape(nt, LANES, S5_STATE)
    c_tile = lambda c: c.reshape(nt, LANES, S5_STATE)
    tl = S5_TL
    n_steps = seq // tl
    assert bsz == SUBLANES and n_steps >= 3
    rows = tl * bsz
    prow = rows // 2
    kernel = functools.partial(_s5_kernel, tl=tl, n_steps=n_steps)
    hbm = pl.BlockSpec(memory_space=pl.ANY)
    return pl.pallas_call(
        kernel,
        out_shape=jax.ShapeDtypeStruct((bsz, seq, d), F32),
        grid=(n_steps,),
        in_specs=[
            hbm,
            _const_spec((1, d)),
            pl.BlockSpec(memory_space=pltpu.SMEM),
            _const_spec((nt, gpt, S5_STATE)),
            _const_spec((nt, gpt, S5_STATE)),
            _const_spec((nt, LANES, S5_STATE)),
            _const_spec((nt, LANES, S5_STATE)),
            _const_spec((nt, LANES, S5_STATE)),
            _const_spec((nt, LANES, S5_STATE)),
            _const_spec((1, d)),
            hbm,
            _const_spec((1, 2 * d)),
        ],
        out_specs=hbm,
        scratch_shapes=[
            pltpu.VMEM((3, tl, bsz, d), F32),
            pltpu.VMEM((rows, d), F32),
            pltpu.VMEM((prow, 2 * d), BF16),
            pltpu.VMEM((S5_TILES_IN_FLIGHT, prow, 2 * LANES), F32),
            pltpu.VMEM((S5_TILES_IN_FLIGHT, prow, 2 * STATES_PER_TILE), F32),
            pltpu.VMEM((S5_TILES_IN_FLIGHT, prow, 2 * STATES_PER_TILE), F32),
            pltpu.VMEM((rows, d), F32),
            pltpu.VMEM((N_LANE_TILES, SUBLANES, 2 * STATES_PER_TILE), F32),
            pltpu.VMEM((2, tl, bsz, d), F32),
            pltpu.VMEM((nt, 2 * LANES, 2 * LANES + 2 * sp), BF16),
            pltpu.VMEM((nt, 2 * sp, 2 * LANES), BF16),
            pltpu.VMEM((nt, SUBLANES, sp), F32),
            pltpu.VMEM((nt, SUBLANES, sp), F32),
            pltpu.VMEM(w_glu.shape, w_glu.dtype),
            pltpu.SemaphoreType.DMA((3, SUBLANES)),
            pltpu.SemaphoreType.DMA((2, SUBLANES)),
            pltpu.SemaphoreType.DMA(()),
        ],
        compiler_params=pltpu.CompilerParams(
            dimension_semantics=("arbitrary",), vmem_limit_bytes=S5_VMEM_LIMIT_BYTES),
        name="s5_layer",
    )(x, g_mix.reshape(1, d), log_dt, a_tile(a_re), a_tile(a_im), b_tile(b_re), b_tile(b_im),
      c_tile(c_re), c_tile(c_im), d_skip.reshape(1, d), w_glu, b_glu.reshape(1, 2 * d))


def _inv_rms(x):
    return lax.rsqrt(jnp.mean(x * x, axis=-1, keepdims=True) + NORM_EPS)


def _mlp_kernel(h_ref, g_ref, win_hbm, wout_hbm, *rest, layer, with_kv):
    if with_kv:
        gkv_ref, wkv_ref, bkv_ref, o_ref, kv_ref, win_s, wout_s, sem = rest
    else:
        gfin_ref, o_ref, win_s, wout_s, sem = rest
    n_chunks = D_FF // MLP_FF_CHUNK

    def weight_copies(c):
        cols = pl.ds(c * MLP_FF_CHUNK, MLP_FF_CHUNK)
        return (pltpu.make_async_copy(win_hbm.at[layer, :, cols], win_s.at[:, cols], sem.at[0, c]),
                pltpu.make_async_copy(wout_hbm.at[layer, cols, :], wout_s.at[cols, :], sem.at[1, c]))

    def body(first_step):
        h = h_ref[...]
        hg = (h * g_ref[...]).astype(BF16)
        r = _inv_rms(h)
        acts = []
        for c in range(n_chunks):
            cols = slice(c * MLP_FF_CHUNK, (c + 1) * MLP_FF_CHUNK)
            if first_step:
                weight_copies(c)[0].wait()
            a = jnp.dot(hg, win_s[:, cols].astype(BF16), preferred_element_type=F32)
            acts.append(jnp.square(jnp.maximum(a, 0.0)).astype(BF16))
            if first_step:
                weight_copies(c)[1].wait()
                part = jnp.dot(acts[c], wout_s[cols, :].astype(BF16), preferred_element_type=F32)
                o_ref[...] = part if c == 0 else o_ref[...] + part
        if first_step:
            acc = o_ref[...]
        else:
            acc = jnp.dot(jnp.concatenate(acts, axis=1), wout_s[...].astype(BF16),
                          preferred_element_type=F32)
        out = h + (r * r) * acc
        if with_kv:
            o_ref[...] = out
            kv = jnp.dot((out * gkv_ref[...]).astype(BF16), wkv_ref[...].astype(BF16),
                         preferred_element_type=F32)
            kv_ref[...] = (_inv_rms(out) * kv + bkv_ref[...]).astype(BF16)
        else:
            o_ref[...] = out * _inv_rms(out) * gfin_ref[...]

    @pl.when(pl.program_id(0) == 0)
    def _():
        for c in range(n_chunks):
            for copy in weight_copies(c):
                copy.start()
        body(True)

    @pl.when(pl.program_id(0) > 0)
    def _():
        body(False)


def _mlp(h, g, w_in, w_out, layer, *, g_kv=None, w_kv=None, b_kv=None, g_fin=None):
    t, d = h.shape
    with_kv = w_kv is not None
    row = lambda n: pl.BlockSpec((MLP_TM, n), lambda i: (i, 0))
    hbm = pl.BlockSpec(memory_space=pl.ANY)
    in_specs = [row(d), _const_spec((1, d)), hbm, hbm]
    operands = [h, g.reshape(1, d), w_in, w_out]
    if with_kv:
        nkv = w_kv.shape[1]
        in_specs += [_const_spec((1, d)), _const_spec(w_kv.shape), _const_spec((1, nkv))]
        operands += [g_kv.reshape(1, d), w_kv, b_kv.reshape(1, nkv)]
        out_shape = (jax.ShapeDtypeStruct((t, d), F32), jax.ShapeDtypeStruct((t, nkv), BF16))
        out_specs = (row(d), row(nkv))
    else:
        in_specs += [_const_spec((1, d))]
        operands += [g_fin.reshape(1, d)]
        out_shape = jax.ShapeDtypeStruct((t, d), F32)
        out_specs = row(d)
    return pl.pallas_call(
        functools.partial(_mlp_kernel, layer=layer, with_kv=with_kv),
        out_shape=out_shape,
        grid=(t // MLP_TM,),
        in_specs=in_specs,
        out_specs=out_specs,
        scratch_shapes=[pltpu.VMEM(w_in.shape[1:], F32),
                        pltpu.VMEM(w_out.shape[1:], F32),
                        pltpu.SemaphoreType.DMA((2, D_FF // MLP_FF_CHUNK))],
        compiler_params=pltpu.CompilerParams(
            dimension_semantics=("arbitrary",), vmem_limit_bytes=MLP_VMEM_LIMIT_BYTES),
        name="mlp_kv" if with_kv else "mlp_final",
    )(*operands)


def _pair_tile_sources(a, g):
    head_lo = (2 * a) * Q_PER_KV + g
    head_hi = (2 * a + 1) * Q_PER_KV + g
    return (head_lo // 2, head_lo % 2), (head_hi // 2, head_hi % 2)


def _attn_kernel(sink_ref, h_ref, kvp_ref, kvc_ref, g_ref, wq_ref, bq_ref, wo_ref, bo_ref,
                 o_ref, wq_s, bq_s, wo_s, q_s, oh_s, *, tq):
    nsb = tq // WINDOW
    n = pl.program_id(1)
    n_kv_tiles = N_KV_HEADS // 2
    lo_row = lax.broadcasted_iota(jnp.int32, (1, LANES), 1) < HEAD_DIM

    @pl.when((pl.program_id(0) == 0) & (n == 0))
    def _():
        for a in range(n_kv_tiles):
            for g in range(Q_PER_KV):
                (t_lo, h_lo), (t_hi, h_hi) = _pair_tile_sources(a, g)
                dst = slice((a * Q_PER_KV + g) * LANES, (a * Q_PER_KV + g + 1) * LANES)

                def pair(ref):
                    src_lo = ref[:, t_lo * LANES:(t_lo + 1) * LANES]
                    src_hi = ref[:, t_hi * LANES:(t_hi + 1) * LANES]
                    if h_lo == 1:
                        src_lo = pltpu.roll(src_lo, HEAD_DIM, axis=1)
                    if h_hi == 0:
                        src_hi = pltpu.roll(src_hi, HEAD_DIM, axis=1)
                    return jnp.where(lo_row, src_lo, src_hi)

                wq_s[:, dst] = pair(wq_ref).astype(BF16)
                bq_s[:, dst] = pair(bq_ref)
                for half in range(2):
                    head = (2 * a + half) * Q_PER_KV + g
                    r0 = (a * Q_PER_KV + g) * LANES + half * HEAD_DIM
                    wo_s[r0:r0 + HEAD_DIM, :] = (
                        wo_ref[head * HEAD_DIM:(head + 1) * HEAD_DIM, :].astype(BF16))

    h = h_ref[0]
    hn = _rmsnorm(h, g_ref[...]).astype(BF16)
    q = jnp.dot(hn, wq_s[...], preferred_element_type=F32) + bq_s[...]
    q_s[...] = (q * (LOG2_E / math.sqrt(HEAD_DIM))).astype(BF16)

    lo = lax.broadcasted_iota(jnp.int32, (WINDOW, LANES), 1) < HEAD_DIM
    own = (lax.broadcasted_iota(jnp.int32, (WINDOW, WINDOW), 1)
           <= lax.broadcasted_iota(jnp.int32, (WINDOW, WINDOW), 0))
    zero = jnp.zeros((WINDOW, LANES), BF16)
    fzero = jnp.zeros((WINDOW, WINDOW), F32)

    for sb in range(nsb):
        rows = slice(sb * WINDOW, (sb + 1) * WINDOW)
        prev_bias = jnp.where(n == 0, -jnp.inf, 0.0).astype(F32) if sb == 0 else None
        for a in range(n_kv_tiles):
            kl = slice(a * LANES, (a + 1) * LANES)
            vl = slice((n_kv_tiles + a) * LANES, (n_kv_tiles + a + 1) * LANES)
            if sb == 0:
                kprev, vprev = kvp_ref[0, :, kl], kvp_ref[0, :, vl]
            else:
                prow = slice((sb - 1) * WINDOW, sb * WINDOW)
                kprev, vprev = kvc_ref[0, prow, kl], kvc_ref[0, prow, vl]
            kd = jnp.concatenate([kprev, kvc_ref[0, rows, kl]], axis=0)
            vd = jnp.concatenate([vprev, kvc_ref[0, rows, vl]], axis=0)
            vd1 = jnp.concatenate([vd, jnp.ones((2 * WINDOW, LANES), BF16)], axis=1)
            qt = [q_s[rows, (a * Q_PER_KV + g) * LANES:(a * Q_PER_KV + g + 1) * LANES]
                  for g in range(Q_PER_KV)]
            lhs = jnp.concatenate([jnp.where(lo, t, zero) for t in qt]
                                  + [jnp.where(lo, zero, t) for t in qt], axis=0)
            s = lax.dot_general(lhs, kd, (((1,), (1,)), ((), ())),
                                preferred_element_type=F32)
            ps, sink_terms = [], []
            for half in range(2):
                for g in range(Q_PER_KV):
                    blk = half * Q_PER_KV + g
                    s_prev = s[blk * WINDOW:(blk + 1) * WINDOW, :WINDOW]
                    s_own = s[blk * WINDOW:(blk + 1) * WINDOW, WINDOW:]
                    if prev_bias is not None:
                        s_prev = s_prev + prev_bias
                    sg = jnp.where(own, s_own, s_prev)
                    sink = sink_ref[(2 * a + half) * Q_PER_KV + g] * LOG2_E
                    m = jnp.maximum(jnp.max(sg, axis=-1, keepdims=True), sink)
                    p = jnp.exp2(sg - m)
                    ps.append(jnp.concatenate([jnp.where(own, fzero, p).astype(BF16),
                                               jnp.where(own, p, fzero).astype(BF16)], axis=1))
                    sink_terms.append(jnp.exp2(sink - m))
            od = jnp.dot(jnp.concatenate(ps, axis=0), vd1, preferred_element_type=F32)
            for g in range(Q_PER_KV):
                r_lo = slice(g * WINDOW, (g + 1) * WINDOW)
                r_hi = slice((Q_PER_KV + g) * WINDOW, (Q_PER_KV + g + 1) * WINDOW)
                num = jnp.where(lo, od[r_lo, :LANES], od[r_hi, :LANES])
                den = (jnp.where(lo, od[r_lo, LANES:], od[r_hi, LANES:])
                       + jnp.where(lo, sink_terms[g], sink_terms[Q_PER_KV + g]))
                oh_s[rows, (a * Q_PER_KV + g) * LANES:(a * Q_PER_KV + g + 1) * LANES] = (
                    (num * (1.0 / den)).astype(BF16))

    o_ref[0] = h + jnp.dot(oh_s[...], wo_s[...], preferred_element_type=F32) + bo_ref[...]


def _attn_layer(h, kv, sinks, g, w_q, b_q, w_o, b_o):
    bsz, seq, d = h.shape
    nkv = kv.shape[-1]
    tq = ATTN_TQ
    nsb = tq // WINDOW
    kernel = functools.partial(_attn_kernel, tq=tq)
    return pl.pallas_call(
        kernel,
        out_shape=jax.ShapeDtypeStruct((bsz, seq, d), F32),
        grid=(bsz, seq // tq),
        in_specs=[
            pl.BlockSpec(memory_space=pltpu.SMEM),
            pl.BlockSpec((1, tq, d), lambda b, n: (b, n, 0)),
            pl.BlockSpec((1, WINDOW, nkv), lambda b, n: (b, jnp.maximum(n * nsb - 1, 0), 0)),
            pl.BlockSpec((1, tq, nkv), lambda b, n: (b, n, 0)),
            _const_spec((1, d)),
            _const_spec(w_q.shape),
            _const_spec((1, d)),
            _const_spec(w_o.shape),
            _const_spec((1, d)),
        ],
        out_specs=pl.BlockSpec((1, tq, d), lambda b, n: (b, n, 0)),
        scratch_shapes=[pltpu.VMEM((d, d), BF16),
                        pltpu.VMEM((1, d), F32),
                        pltpu.VMEM((d, d), BF16),
                        pltpu.VMEM((tq, d), BF16),
                        pltpu.VMEM((tq, d), BF16)],
        compiler_params=pltpu.CompilerParams(
            dimension_semantics=("arbitrary", "arbitrary"), vmem_limit_bytes=ATTN_VMEM_LIMIT_BYTES),
        name="attn",
    )(sinks, h, kv, kv, g.reshape(1, d), w_q, b_q.reshape(1, d), w_o, b_o.reshape(1, d))


def kernel(x, norm_mix, norm_mlp, norm_kv, norm_final, s5_a_re, s5_a_im, s5_log_dt, s5_b_re, s5_b_im, s5_c_re, s5_c_im, s5_d, s5_w_glu, s5_b_glu, w_kv, b_kv, w_q, b_q, sinks, w_o, b_o, w_mlp_in, w_mlp_out):
    bsz, seq, d = x.shape

    h = _s5_layer(x, norm_mix[0], s5_a_re[0], s5_a_im[0], s5_log_dt[0], s5_b_re[0], s5_b_im[0],
                  s5_c_re[0], s5_c_im[0], s5_d[0], s5_w_glu[0], s5_b_glu[0])

    h, kv = _mlp(h.reshape(bsz * seq, d), norm_mlp[0], w_mlp_in, w_mlp_out, 0,
                 g_kv=norm_kv, w_kv=w_kv, b_kv=b_kv)

    h = _attn_layer(h.reshape(bsz, seq, d), kv.reshape(bsz, seq, -1), sinks[0], norm_mix[1],
                    w_q[0], b_q[0], w_o[0], b_o[0])

    out = _mlp(h.reshape(bsz * seq, d), norm_mlp[1], w_mlp_in, w_mlp_out, 1, g_fin=norm_final)
    return out.reshape(bsz, seq, d)
```

```python
import functools
import math

import jax
import jax.numpy as jnp
from jax import lax
from jax.experimental import pallas as pl
from jax.experimental.pallas import tpu as pltpu

F32 = jnp.float32
BF16 = jnp.bfloat16

D_MODEL = 1024
S5_GROUP = 16
S5_STATE = 64
LAMBDA_RE_MAX = -1e-4
HEAD_DIM = 64
N_Q_HEADS = D_MODEL // HEAD_DIM
N_KV_HEADS = 4
Q_PER_KV = N_Q_HEADS // N_KV_HEADS
WINDOW = 128
D_FF = 4 * D_MODEL
NORM_EPS = 1e-5
LOG2_E = math.log2(math.e)

LANES = 128
SUBLANES = 8
N_LANE_TILES = D_MODEL // LANES
GROUPS_PER_TILE = LANES // S5_GROUP
STATES_PER_TILE = GROUPS_PER_TILE * S5_STATE
MIB = 1024 * 1024
S5_VMEM_LIMIT_BYTES = 56 * MIB
MLP_VMEM_LIMIT_BYTES = 52 * MIB
ATTN_VMEM_LIMIT_BYTES = 40 * MIB

S5_TL = 64
S5_TILES_IN_FLIGHT = 2
MLP_TM = 512
MLP_FF_CHUNK = 1024
ATTN_TQ = 1024


def _rmsnorm(x, g):
    return x * lax.rsqrt(jnp.mean(x * x, axis=-1, keepdims=True) + NORM_EPS) * g


def _gelu_tanh(x):
    c = math.sqrt(2.0 / math.pi)
    return 0.5 * x * (1.0 + jnp.tanh(c * (x + 0.044715 * (x * x * x))))


def _const_spec(shape):
    nd = len(shape)
    return pl.BlockSpec(shape, lambda *_: (0,) * nd, pipeline_mode=pl.Buffered(1))


def _s5_fold_params(ldt_ref, are_ref, aim_ref, btr_ref, bti_ref, cre_ref, cim_ref,
                    w1_ref, cx_ref, l2r_ref, l2i_ref):
    for tile in range(N_LANE_TILES):
        log_dt = [ldt_ref[tile * GROUPS_PER_TILE + g] for g in range(GROUPS_PER_TILE)]
        _s5_params_tile(are_ref.at[tile], aim_ref.at[tile], log_dt, btr_ref.at[tile],
                        bti_ref.at[tile], cre_ref.at[tile], cim_ref.at[tile],
                        w1_ref.at[tile], cx_ref.at[tile], l2r_ref.at[tile], l2i_ref.at[tile])


def _s5_params_tile(are_ref, aim_ref, log_dt, btr_ref, bti_ref, cre_ref, cim_ref,
                    w1_ref, cx_ref, l2r_ref, l2i_ref):
    gpt, grp, nst = GROUPS_PER_TILE, S5_GROUP, S5_STATE
    ar = jnp.minimum(are_ref[...], LAMBDA_RE_MAX)
    ai = aim_ref[...]
    dt = jnp.exp(jnp.concatenate([jnp.full((1, nst), v, F32) for v in log_dt], axis=0))
    mag = jnp.exp(ar * dt)
    ang = ai * dt
    lr = mag * jnp.cos(ang)
    li = mag * jnp.sin(ang)
    den = ar * ar + ai * ai
    cr = ((lr - 1.0) * ar + li * ai) / den
    ci = (li * ar - (lr - 1.0) * ai) / den
    l2r = lr * lr - li * li
    l2i = 2.0 * (lr * li)

    def per_row(v):
        return jnp.concatenate([jnp.broadcast_to(v[g:g + 1], (grp, nst)) for g in range(gpt)], axis=0)

    row_g = lax.broadcasted_iota(jnp.int32, (LANES, STATES_PER_TILE), 0) // grp
    col_g = lax.broadcasted_iota(jnp.int32, (LANES, STATES_PER_TILE), 1) // nst
    own = row_g == col_g

    def blockdiag(v):
        return jnp.where(own, jnp.concatenate([v] * gpt, axis=1), 0.0)

    lr_c, li_c, cr_c, ci_c = per_row(lr), per_row(li), per_row(cr), per_row(ci)
    l2r_c, l2i_c = per_row(l2r), per_row(l2i)
    btr, bti = btr_ref[...], bti_ref[...]
    bbr = cr_c * btr - ci_c * bti
    bbi = cr_c * bti + ci_c * btr
    lbr = lr_c * bbr - li_c * bbi
    lbi = lr_c * bbi + li_c * bbr
    wbr, wbi = blockdiag(bbr), blockdiag(bbi)
    w1r, w1i = blockdiag(lbr), blockdiag(lbi)
    ccr, cci = cre_ref[...], cim_ref[...]

    lhs = jnp.concatenate([jnp.concatenate([bbr, -bbi], axis=1),
                           jnp.concatenate([lbr, -lbi], axis=1)], axis=0)
    kk = lax.dot_general(lhs, jnp.concatenate([ccr, cci], axis=1), (((1,), (1,)), ((), ())),
                         precision=lax.Precision.HIGHEST, preferred_element_type=F32)
    same_group = (lax.broadcasted_iota(jnp.int32, (LANES, LANES), 0) // grp
                  == lax.broadcasted_iota(jnp.int32, (LANES, LANES), 1) // grp)
    k0 = jnp.where(same_group, kk[:LANES], 0.0)
    k1 = jnp.where(same_group, kk[LANES:], 0.0)
    top =jnp.concatenate([k0, k1, w1r, w1i], axis=1)
    bot = jnp.concatenate([jnp.zeros_like(k0), k0, wbr, wbi], axis=1)
    w1_ref[...] = jnp.concatenate([top, bot], axis=0).astype(BF16)

    def readout_t(pr, pi):
        return jnp.concatenate([blockdiag(pr * ccr - pi * cci),
                                -blockdiag(pi * ccr + pr * cci)], axis=1)

    cx_t = jnp.concatenate([readout_t(lr_c, li_c), readout_t(l2r_c, l2i_c)], axis=0)
    cx_ref[...] = cx_t.T.astype(BF16)

    def flat_row(v):
        row = jnp.concatenate([v[g:g + 1] for g in range(gpt)], axis=1)
        return jnp.broadcast_to(row, (SUBLANES, STATES_PER_TILE))

    l2r_ref[...] = flat_row(l2r)
    l2i_ref[...] = flat_row(l2i)


def _s5_kernel(x_hbm, g_ref, ldt_ref, are_ref, aim_ref, btr_ref, bti_ref, cre_ref, cim_ref,
               d_ref, wglu_hbm, bglu_ref,
               o_hbm, xin_s, hn_s, hnu_s, zy_s, bu_s, xs_s, y_s, st_s, res_s,
               w1_s, cx_s, l2r_s, l2i_s, wglu_s, sem_in, sem_out, sem_w, *, tl, n_steps):
    half = tl // 2
    prow = half * SUBLANES
    sp = STATES_PER_TILE
    i = pl.program_id(0)
    slot = i % 2

    def in_copies(step, sl):
        return [pltpu.make_async_copy(x_hbm.at[b, pl.ds(step * tl, tl), :],
                                      xin_s.at[sl, :, b, :], sem_in.at[sl, b])
                for b in range(SUBLANES)]

    def out_copies(step, sl):
        return [pltpu.make_async_copy(res_s.at[sl, :, b, :],
                                      o_hbm.at[b, pl.ds(step * tl, tl), :], sem_out.at[sl, b])
                for b in range(SUBLANES)]

    in_slot = i % 3

    @pl.when(i == 0)
    def _():
        st_s[...] = jnp.zeros_like(st_s)
        wglu_copy = pltpu.make_async_copy(wglu_hbm, wglu_s, sem_w)
        for copy in in_copies(0, 0) + [wglu_copy]:
            copy.start()
        _s5_fold_params(ldt_ref, are_ref, aim_ref, btr_ref, bti_ref, cre_ref, cim_ref,
                        w1_s, cx_s, l2r_s, l2i_s)
        wglu_copy.wait()

    for copy in in_copies(i, in_slot):
        copy.wait()

    @pl.when(i == 0)
    def _():
        for copy in in_copies(1, 1):
            copy.start()

    @pl.when(i + 2 < n_steps)
    def _():
        for copy in in_copies(i + 2, (i + 2) % 3):
            copy.start()

    @pl.when(i >= 2)
    def _():
        for copy in out_copies(i - 2, slot):
            copy.wait()

    def row_group(v, t):
        return v[t * SUBLANES:(t + 1) * SUBLANES]

    hn = _rmsnorm(xin_s[in_slot].reshape(tl * SUBLANES, D_MODEL), g_ref[...])
    for t in range(tl):
        g = (t % 2) * half + t // 2
        hn_s[g * SUBLANES:(g + 1) * SUBLANES, :] = row_group(hn, t)
    for m in range(tl // 4):
        r16 = slice(2 * m * SUBLANES, (2 * m + 2) * SUBLANES)
        even = jnp.concatenate([row_group(hn, 4 * m), row_group(hn, 4 * m + 2)], axis=0).astype(BF16)
        odd = jnp.concatenate([row_group(hn, 4 * m + 1), row_group(hn, 4 * m + 3)], axis=0).astype(BF16)
        for j in range(N_LANE_TILES):
            lanes = slice(j * LANES, (j + 1) * LANES)
            hnu_s[r16, 2 * j * LANES:(2 * j + 1) * LANES] = even[:, lanes]
            hnu_s[r16, (2 * j + 1) * LANES:(2 * j + 2) * LANES] = odd[:, lanes]

    def project_in(j):
        z = jnp.dot(hnu_s[:, 2 * j * LANES:(2 * j + 2) * LANES], w1_s[j],
                    preferred_element_type=F32)
        zy_s[j % S5_TILES_IN_FLIGHT] = z[:, :2 * LANES]
        bu_s[j % S5_TILES_IN_FLIGHT] = z[:, 2 * LANES:]

    ahead = S5_TILES_IN_FLIGHT - 1
    for j in range(ahead):
        project_in(j)
    for j in range(N_LANE_TILES):
        if j + ahead < N_LANE_TILES:
            project_in(j + ahead)
        p = j % S5_TILES_IN_FLIGHT
        lanes = slice(j * LANES, (j + 1) * LANES)
        ar = l2r_s[j]
        ai = l2i_s[j]
        xr = st_s[j, :, 0:sp]
        xi = st_s[j, :, sp:2 * sp]
        for k in range(half):
            r = slice(k * SUBLANES, (k + 1) * SUBLANES)
            xs_s[p, r, 0:sp] = xr
            xs_s[p, r, sp:2 * sp] = xi
            nxr = ar * xr - ai * xi + bu_s[p, r, 0:sp]
            nxi = ar * xi + ai * xr + bu_s[p, r, sp:2 * sp]
            xr, xi = nxr, nxi
        st_s[j, :, 0:sp] = xr
        st_s[j, :, sp:2 * sp] = xi
        y = zy_s[p] + jnp.dot(xs_s[p].astype(BF16), cx_s[j], preferred_element_type=F32)
        y_s[0:prow, lanes] = y[:, :LANES]
        y_s[prow:2 * prow, lanes] = y[:, LANES:]

    for par in range(2):
        rs = slice(par * prow, (par + 1) * prow)
        y = y_s[rs, :] + d_ref[...] * hn_s[rs, :]
        z = jnp.dot(_gelu_tanh(y).astype(BF16), wglu_s[...].astype(BF16),
                    preferred_element_type=F32) + bglu_ref[...]
        mix = z[:, :D_MODEL] * (1.0 / (1.0 + jnp.exp(-z[:, D_MODEL:])))
        for k in range(half):
            t = 2 * k + par
            res_s[slot, t] = xin_s[in_slot, t] + row_group(mix, k)

    for copy in out_copies(i, slot):
        copy.start()

    @pl.when(i == n_steps - 1)
    def _():
        for copy in out_copies(i - 1, 1 - slot) + out_copies(i, slot):
            copy.wait()


def _s5_layer(x, g_mix, a_re, a_im, log_dt, b_re, b_im, c_re, c_im, d_skip, w_glu, b_glu):
    bsz, seq, d = x.shape
    nt, gpt, sp = N_LANE_TILES, GROUPS_PER_TILE, STATES_PER_TILE
    a_tile = lambda a: a.reshape(nt, gpt, S5_STATE)
    b_tile = lambda b: jnp.swapaxes(b, -1, -2).reshape(nt, LANES, S5_STATE)
    c_tile = lambda c: c.reshape(nt, LANES, S5_STATE)
    tl = S5_TL
    n_steps = seq // tl
    assert bsz == SUBLANES and n_steps >= 3
    rows = tl * bsz
    prow = rows // 2
    kernel = functools.partial(_s5_kernel, tl=tl, n_steps=n_steps)
    hbm = pl.BlockSpec(memory_space=pl.ANY)
    return pl.pallas_call(
        kernel,
        out_shape=jax.ShapeDtypeStruct((bsz, seq, d), F32),
        grid=(n_steps,),
        in_specs=[
            hbm,
            _const_spec((1, d)),
            pl.BlockSpec(memory_space=pltpu.SMEM),
            _const_spec((nt, gpt, S5_STATE)),
            _const_spec((nt, gpt, S5_STATE)),
            _const_spec((nt, LANES, S5_STATE)),
            _const_spec((nt, LANES, S5_STATE)),
            _const_spec((nt, LANES, S5_STATE)),
            _const_spec((nt, LANES, S5_STATE)),
            _const_spec((1, d)),
            hbm,
            _const_spec((1, 2 * d)),
        ],
        out_specs=hbm,
        scratch_shapes=[
            pltpu.VMEM((3, tl, bsz, d), F32),
            pltpu.VMEM((rows, d), F32),
            pltpu.VMEM((prow, 2 * d), BF16),
            pltpu.VMEM((S5_TILES_IN_FLIGHT, prow, 2 * LANES), F32),
            pltpu.VMEM((S5_TILES_IN_FLIGHT, prow, 2 * STATES_PER_TILE), F32),
            pltpu.VMEM((S5_TILES_IN_FLIGHT, prow, 2 * STATES_PER_TILE), F32),
            pltpu.VMEM((rows, d), F32),
            pltpu.VMEM((N_LANE_TILES, SUBLANES, 2 * STATES_PER_TILE), F32),
            pltpu.VMEM((2, tl, bsz, d), F32),
            pltpu.VMEM((nt, 2 * LANES, 2 * LANES + 2 * sp), BF16),
            pltpu.VMEM((nt, 2 * sp, 2 * LANES), BF16),
            pltpu.VMEM((nt, SUBLANES, sp), F32),
            pltpu.VMEM((nt, SUBLANES, sp), F32),
            pltpu.VMEM(w_glu.shape, w_glu.dtype),
            pltpu.SemaphoreType.DMA((3, SUBLANES)),
            pltpu.SemaphoreType.DMA((2, SUBLANES)),
            pltpu.SemaphoreType.DMA(()),
        ],
        compiler_params=pltpu.CompilerParams(
            dimension_semantics=("arbitrary",), vmem_limit_bytes=S5_VMEM_LIMIT_BYTES),
        name="s5_layer",
    )(x, g_mix.reshape(1, d), log_dt, a_tile(a_re), a_tile(a_im), b_tile(b_re), b_tile(b_im),
      c_tile(c_re), c_tile(c_im), d_skip.reshape(1, d), w_glu, b_glu.reshape(1, 2 * d))


def _inv_rms(x):
    return lax.rsqrt(jnp.mean(x * x, axis=-1, keepdims=True) + NORM_EPS)


def _mlp_kernel(h_ref, g_ref, win_hbm, wout_hbm, *rest, layer, with_kv):
    if with_kv:
        gkv_ref, wkv_ref, bkv_ref, o_ref, kv_ref, win_s, wout_s, sem = rest
    else:
        gfin_ref, o_ref, win_s, wout_s, sem = rest
    n_chunks = D_FF // MLP_FF_CHUNK

    def weight_copies(c):
        cols = pl.ds(c * MLP_FF_CHUNK, MLP_FF_CHUNK)
        return (pltpu.make_async_copy(win_hbm.at[layer, :, cols], win_s.at[:, cols], sem.at[0, c]),
                pltpu.make_async_copy(wout_hbm.at[layer, cols, :], wout_s.at[cols, :], sem.at[1, c]))

    def body(first_step):
        h = h_ref[...]
        hg = (h * g_ref[...]).astype(BF16)
        r = _inv_rms(h)
        acts = []
        for c in range(n_chunks):
            cols = slice(c * MLP_FF_CHUNK, (c + 1) * MLP_FF_CHUNK)
            if first_step:
                weight_copies(c)[0].wait()
            a = jnp.dot(hg, win_s[:, cols].astype(BF16), preferred_element_type=F32)
            acts.append(jnp.square(jnp.maximum(a, 0.0)).astype(BF16))
            if first_step:
                weight_copies(c)[1].wait()
                part = jnp.dot(acts[c], wout_s[cols, :].astype(BF16), preferred_element_type=F32)
                o_ref[...] = part if c == 0 else o_ref[...] + part
        if first_step:
            acc = o_ref[...]
        else:
            acc = jnp.dot(jnp.concatenate(acts, axis=1), wout_s[...].astype(BF16),
                          preferred_element_type=F32)
        out = h + (r * r) * acc
        if with_kv:
            o_ref[...] = out
            kv = jnp.dot((out * gkv_ref[...]).astype(BF16), wkv_ref[...].astype(BF16),
                         preferred_element_type=F32)
            kv_ref[...] = (_inv_rms(out) * kv + bkv_ref[...]).astype(BF16)
        else:
            o_ref[...] = out * _inv_rms(out) * gfin_ref[...]

    @pl.when(pl.program_id(0) == 0)
    def _():
        for c in range(n_chunks):
            for copy in weight_copies(c):
                copy.start()
        body(True)

    @pl.when(pl.program_id(0) > 0)
    def _():
        body(False)


def _mlp(h, g, w_in, w_out, layer, *, g_kv=None, w_kv=None, b_kv=None, g_fin=None):
    t, d = h.shape
    with_kv = w_kv is not None
    row = lambda n: pl.BlockSpec((MLP_TM, n), lambda i: (i, 0))
    hbm = pl.BlockSpec(memory_space=pl.ANY)
    in_specs = [row(d), _const_spec((1, d)), hbm, hbm]
    operands = [h, g.reshape(1, d), w_in, w_out]
    if with_kv:
        nkv = w_kv.shape[1]
        in_specs += [_const_spec((1, d)), _const_spec(w_kv.shape), _const_spec((1, nkv))]
        operands += [g_kv.reshape(1, d), w_kv, b_kv.reshape(1, nkv)]
        out_shape = (jax.ShapeDtypeStruct((t, d), F32), jax.ShapeDtypeStruct((t, nkv), BF16))
        out_specs = (row(d), row(nkv))
    else:
        in_specs += [_const_spec((1, d))]
        operands += [g_fin.reshape(1, d)]
        out_shape = jax.ShapeDtypeStruct((t, d), F32)
        out_specs = row(d)
    return pl.pallas_call(
        functools.partial(_mlp_kernel, layer=layer, with_kv=with_kv),
        out_shape=out_shape,
        grid=(t // MLP_TM,),
        in_specs=in_specs,
        out_specs=out_specs,
        scratch_shapes=[pltpu.VMEM(w_in.shape[1:], F32),
                        pltpu.VMEM(w_out.shape[1:], F32),
                        pltpu.SemaphoreType.DMA((2, D_FF // MLP_FF_CHUNK))],
        compiler_params=pltpu.CompilerParams(
            dimension_semantics=("arbitrary",), vmem_limit_bytes=MLP_VMEM_LIMIT_BYTES),
        name="mlp_kv" if with_kv else "mlp_final",
    )(*operands)


def _pair_tile_sources(a, g):
    head_lo = (2 * a) * Q_PER_KV + g
    head_hi = (2 * a + 1) * Q_PER_KV + g
    return (head_lo // 2, head_lo % 2), (head_hi // 2, head_hi % 2)


def _attn_kernel(sink_ref, h_ref, kvp_ref, kvc_ref, g_ref, wq_ref, bq_ref, wo_ref, bo_ref,
                 o_ref, wq_s, bq_s, wo_s, q_s, oh_s, *, tq):
    nsb = tq // WINDOW
    n = pl.program_id(1)
    n_kv_tiles = N_KV_HEADS // 2
    lo_row = lax.broadcasted_iota(jnp.int32, (1, LANES), 1) < HEAD_DIM

    @pl.when((pl.program_id(0) == 0) & (n == 0))
    def _():
        for a in range(n_kv_tiles):
            for g in range(Q_PER_KV):
                (t_lo, h_lo), (t_hi, h_hi) = _pair_tile_sources(a, g)
                dst = slice((a * Q_PER_KV + g) * LANES, (a * Q_PER_KV + g + 1) * LANES)

                def pair(ref):
                    src_lo = ref[:, t_lo * LANES:(t_lo + 1) * LANES]
                    src_hi = ref[:, t_hi * LANES:(t_hi + 1) * LANES]
                    if h_lo == 1:
                        src_lo = pltpu.roll(src_lo, HEAD_DIM, axis=1)
                    if h_hi == 0:
                        src_hi = pltpu.roll(src_hi, HEAD_DIM, axis=1)
                    return jnp.where(lo_row, src_lo, src_hi)

                wq_s[:, dst] = pair(wq_ref).astype(BF16)
                bq_s[:, dst] = pair(bq_ref)
                for half in range(2):
                    head = (2 * a + half) * Q_PER_KV + g
                    r0 = (a * Q_PER_KV + g) * LANES + half * HEAD_DIM
                    wo_s[r0:r0 + HEAD_DIM, :] = (
                        wo_ref[head * HEAD_DIM:(head + 1) * HEAD_DIM, :].astype(BF16))

    h = h_ref[0]
    hn = _rmsnorm(h, g_ref[...]).astype(BF16)
    q = jnp.dot(hn, wq_s[...], preferred_element_type=F32) + bq_s[...]
    q_s[...] = (q * (LOG2_E / math.sqrt(HEAD_DIM))).astype(BF16)

    lo = lax.broadcasted_iota(jnp.int32, (WINDOW, LANES), 1) < HEAD_DIM
    own = (lax.broadcasted_iota(jnp.int32, (WINDOW, WINDOW), 1)
           <= lax.broadcasted_iota(jnp.int32, (WINDOW, WINDOW), 0))
    zero = jnp.zeros((WINDOW, LANES), BF16)
    fzero = jnp.zeros((WINDOW, WINDOW), F32)

    for sb in range(nsb):
        rows = slice(sb * WINDOW, (sb + 1) * WINDOW)
        prev_bias = jnp.where(n == 0, -jnp.inf, 0.0).astype(F32) if sb == 0 else None
        for a in range(n_kv_tiles):
            kl = slice(a * LANES, (a + 1) * LANES)
            vl = slice((n_kv_tiles + a) * LANES, (n_kv_tiles + a + 1) * LANES)
            if sb == 0:
                kprev, vprev = kvp_ref[0, :, kl], kvp_ref[0, :, vl]
            else:
                prow = slice((sb - 1) * WINDOW, sb * WINDOW)
                kprev, vprev = kvc_ref[0, prow, kl], kvc_ref[0, prow, vl]
            kd = jnp.concatenate([kprev, kvc_ref[0, rows, kl]], axis=0)
            vd = jnp.concatenate([vprev, kvc_ref[0, rows, vl]], axis=0)
            vd1 = jnp.concatenate([vd, jnp.ones((2 * WINDOW, LANES), BF16)], axis=1)
            qt = [q_s[rows, (a * Q_PER_KV + g) * LANES:(a * Q_PER_KV + g + 1) * LANES]
                  for g in range(Q_PER_KV)]
            lhs = jnp.concatenate([jnp.where(lo, t, zero) for t in qt]
                                  + [jnp.where(lo, zero, t) for t in qt], axis=0)
            s = lax.dot_general(lhs, kd, (((1,), (1,)), ((), ())),
                                preferred_element_type=F32)
            ps, sink_terms = [], []
            for half in range(2):
                for g in range(Q_PER_KV):
                    blk = half * Q_PER_KV + g
                    s_prev = s[blk * WINDOW:(blk + 1) * WINDOW, :WINDOW]
                    s_own = s[blk * WINDOW:(blk + 1) * WINDOW, WINDOW:]
                    if prev_bias is not None:
                        s_prev = s_prev + prev_bias
                    sg = jnp.where(own, s_own, s_prev)
                    sink = sink_ref[(2 * a + half) * Q_PER_KV + g] * LOG2_E
                    m = jnp.maximum(jnp.max(sg, axis=-1, keepdims=True), sink)
                    p = jnp.exp2(sg - m)
                    ps.append(jnp.concatenate([jnp.where(own, fzero, p).astype(BF16),
                                               jnp.where(own, p, fzero).astype(BF16)], axis=1))
                    sink_terms.append(jnp.exp2(sink - m))
            od = jnp.dot(jnp.concatenate(ps, axis=0), vd1, preferred_element_type=F32)
            for g in range(Q_PER_KV):
                r_lo = slice(g * WINDOW, (g + 1) * WINDOW)
                r_hi = slice((Q_PER_KV + g) * WINDOW, (Q_PER_KV + g + 1) * WINDOW)
                num = jnp.where(lo, od[r_lo, :LANES], od[r_hi, :LANES])
                den = (jnp.where(lo, od[r_lo, LANES:], od[r_hi, LANES:])
                       + jnp.where(lo, sink_terms[g], sink_terms[Q_PER_KV + g]))
                oh_s[rows, (a * Q_PER_KV + g) * LANES:(a * Q_PER_KV + g + 1) * LANES] = (
                    (num * (1.0 / den)).astype(BF16))

    o_ref[0] = h + jnp.dot(oh_s[...], wo_s[...], preferred_element_type=F32) + bo_ref[...]


def _attn_layer(h, kv, sinks, g, w_q, b_q, w_o, b_o):
    bsz, seq, d = h.shape
    nkv = kv.shape[-1]
    tq = ATTN_TQ
    nsb = tq // WINDOW
    kernel = functools.partial(_attn_kernel, tq=tq)
    return pl.pallas_call(
        kernel,
        out_shape=jax.ShapeDtypeStruct((bsz, seq, d), F32),
        grid=(bsz, seq // tq),
        in_specs=[
            pl.BlockSpec(memory_space=pltpu.SMEM),
            pl.BlockSpec((1, tq, d), lambda b, n: (b, n, 0)),
            pl.BlockSpec((1, WINDOW, nkv), lambda b, n: (b, jnp.maximum(n * nsb - 1, 0), 0)),
            pl.BlockSpec((1, tq, nkv), lambda b, n: (b, n, 0)),
            _const_spec((1, d)),
            _const_spec(w_q.shape),
            _const_spec((1, d)),
            _const_spec(w_o.shape),
            _const_spec((1, d)),
        ],
        out_specs=pl.BlockSpec((1, tq, d), lambda b, n: (b, n, 0)),
        scratch_shapes=[pltpu.VMEM((d, d), BF16),
                        pltpu.VMEM((1, d), F32),
                        pltpu.VMEM((d, d), BF16),
                        pltpu.VMEM((tq, d), BF16),
                        pltpu.VMEM((tq, d), BF16)],
        compiler_params=pltpu.CompilerParams(
            dimension_semantics=("arbitrary", "arbitrary"), vmem_limit_bytes=ATTN_VMEM_LIMIT_BYTES),
        name="attn",
    )(sinks, h, kv, kv, g.reshape(1, d), w_q, b_q.reshape(1, d), w_o, b_o.reshape(1, d))


def kernel(x, norm_mix, norm_mlp, norm_kv, norm_final, s5_a_re, s5_a_im, s5_log_dt, s5_b_re, s5_b_im, s5_c_re, s5_c_im, s5_d, s5_w_glu, s5_b_glu, w_kv, b_kv, w_q, b_q, sinks, w_o, b_o, w_mlp_in, w_mlp_out):
    bsz, seq, d = x.shape

    h = _s5_layer(x, norm_mix[0], s5_a_re[0], s5_a_im[0], s5_log_dt[0], s5_b_re[0], s5_b_im[0],
                  s5_c_re[0], s5_c_im[0], s5_d[0], s5_w_glu[0], s5_b_glu[0])

    h, kv = _mlp(h.reshape(bsz * seq, d), norm_mlp[0], w_mlp_in, w_mlp_out, 0,
                 g_kv=norm_kv, w_kv=w_kv, b_kv=b_kv)

    h = _attn_layer(h.reshape(bsz, seq, d), kv.reshape(bsz, seq, -1), sinks[0], norm_mix[1],
                    w_q[0], b_q[0], w_o[0], b_o[0])

    out = _mlp(h.reshape(bsz * seq, d), norm_mlp[1], w_mlp_in, w_mlp_out, 1, g_fin=norm_final)
    return out.reshape(bsz, seq, d)
```

```python
import functools
import math

import jax
import jax.numpy as jnp
from jax import lax
from jax.experimental import pallas as pl
from jax.experimental.pallas import tpu as pltpu

F32 = jnp.float32
BF16 = jnp.bfloat16

D_MODEL = 1024
S5_GROUP = 16
S5_STATE = 64
LAMBDA_RE_MAX = -1e-4
HEAD_DIM = 64
N_Q_HEADS = D_MODEL // HEAD_DIM
N_KV_HEADS = 4
Q_PER_KV = N_Q_HEADS // N_KV_HEADS
WINDOW = 128
D_FF = 4 * D_MODEL
NORM_EPS = 1e-5
LOG2_E = math.log2(math.e)

LANES = 128
SUBLANES = 8
N_LANE_TILES = D_MODEL // LANES
GROUPS_PER_TILE = LANES // S5_GROUP
STATES_PER_TILE = GROUPS_PER_TILE * S5_STATE
MIB = 1024 * 1024
S5_VMEM_LIMIT_BYTES = 56 * MIB
MLP_VMEM_LIMIT_BYTES = 52 * MIB
ATTN_VMEM_LIMIT_BYTES = 40 * MIB

S5_TL = 64
S5_TILES_IN_FLIGHT = 2
MLP_TM = 512
MLP_FF_CHUNK = 1024
ATTN_TQ = 1024


def _rmsnorm(x, g):
    return x * lax.rsqrt(jnp.mean(x * x, axis=-1, keepdims=True) + NORM_EPS) * g


def _gelu_tanh(x):
    c = math.sqrt(2.0 / math.pi)
    return 0.5 * x * (1.0 + jnp.tanh(c * (x + 0.044715 * (x * x * x))))


def _const_spec(shape):
    nd = len(shape)
    return pl.BlockSpec(shape, lambda *_: (0,) * nd, pipeline_mode=pl.Buffered(1))


def _s5_fold_params(ldt_ref, are_ref, aim_ref, btr_ref, bti_ref, cre_ref, cim_ref,
                    w1_ref, cx_ref, l2r_ref, l2i_ref):
    for tile in range(N_LANE_TILES):
        log_dt = [ldt_ref[tile * GROUPS_PER_TILE + g] for g in range(GROUPS_PER_TILE)]
        _s5_params_tile(are_ref.at[tile], aim_ref.at[tile], log_dt, btr_ref.at[tile],
                        bti_ref.at[tile], cre_ref.at[tile], cim_ref.at[tile],
                        w1_ref.at[tile], cx_ref.at[tile], l2r_ref.at[tile], l2i_ref.at[tile])


def _s5_params_tile(are_ref, aim_ref, log_dt, btr_ref, bti_ref, cre_ref, cim_ref,
                    w1_ref, cx_ref, l2r_ref, l2i_ref):
    gpt, grp, nst = GROUPS_PER_TILE, S5_GROUP, S5_STATE
    ar = jnp.minimum(are_ref[...], LAMBDA_RE_MAX)
    ai = aim_ref[...]
    dt = jnp.exp(jnp.concatenate([jnp.full((1, nst), v, F32) for v in log_dt], axis=0))
    mag = jnp.exp(ar * dt)
    ang = ai * dt
    lr = mag * jnp.cos(ang)
    li = mag * jnp.sin(ang)
    den = ar * ar + ai * ai
    cr = ((lr - 1.0) * ar + li * ai) / den
    ci = (li * ar - (lr - 1.0) * ai) / den
    l2r = lr * lr - li * li
    l2i = 2.0 * (lr * li)

    def per_row(v):
        return jnp.concatenate([jnp.broadcast_to(v[g:g + 1], (grp, nst)) for g in range(gpt)], axis=0)

    row_g = lax.broadcasted_iota(jnp.int32, (LANES, STATES_PER_TILE), 0) // grp
    col_g = lax.broadcasted_iota(jnp.int32, (LANES, STATES_PER_TILE), 1) // nst
    own = row_g == col_g

    def blockdiag(v):
        return jnp.where(own, jnp.concatenate([v] * gpt, axis=1), 0.0)

    lr_c, li_c, cr_c, ci_c = per_row(lr), per_row(li), per_row(cr), per_row(ci)
    l2r_c, l2i_c = per_row(l2r), per_row(l2i)
    btr, bti = btr_ref[...], bti_ref[...]
    bbr = cr_c * btr - ci_c * bti
    bbi = cr_c * bti + ci_c * btr
    lbr = lr_c * bbr - li_c * bbi
    lbi = lr_c * bbi + li_c * bbr
    wbr, wbi = blockdiag(bbr), blockdiag(bbi)
    w1r, w1i = blockdiag(lbr), blockdiag(lbi)
    ccr, cci = cre_ref[...], cim_ref[...]

    lhs = jnp.concatenate([jnp.concatenate([bbr, -bbi], axis=1),
                           jnp.concatenate([lbr, -lbi], axis=1)], axis=0)
    kk = lax.dot_general(lhs, jnp.concatenate([ccr, cci], axis=1), (((1,), (1,)), ((), ())),
                         precision=lax.Precision.HIGHEST, preferred_element_type=F32)
    same_group = (lax.broadcasted_iota(jnp.int32, (LANES, LANES), 0) // grp
                  == lax.broadcasted_iota(jnp.int32, (LANES, LANES), 1) // grp)
    k0 = jnp.where(same_group, kk[:LANES], 0.0)
    k1 = jnp.where(same_group, kk[LANES:], 0.0)
    top =jnp.concatenate([k0, k1, w1r, w1i], axis=1)
    bot = jnp.concatenate([jnp.zeros_like(k0), k0, wbr, wbi], axis=1)
    w1_ref[...] = jnp.concatenate([top, bot], axis=0).astype(BF16)

    def readout_t(pr, pi):
        return jnp.concatenate([blockdiag(pr * ccr - pi * cci),
                                -blockdiag(pi * ccr + pr * cci)], axis=1)

    cx_t = jnp.concatenate([readout_t(lr_c, li_c), readout_t(l2r_c, l2i_c)], axis=0)
    cx_ref[...] = cx_t.T.astype(BF16)

    def flat_row(v):
        row = jnp.concatenate([v[g:g + 1] for g in range(gpt)], axis=1)
        return jnp.broadcast_to(row, (SUBLANES, STATES_PER_TILE))

    l2r_ref[...] = flat_row(l2r)
    l2i_ref[...] = flat_row(l2i)


def _s5_kernel(x_hbm, g_ref, ldt_ref, are_ref, aim_ref, btr_ref, bti_ref, cre_ref, cim_ref,
               d_ref, wglu_hbm, bglu_ref,
               o_hbm, xin_s, hn_s, hnu_s, zy_s, bu_s, xs_s, y_s, st_s, res_s,
               w1_s, cx_s, l2r_s, l2i_s, wglu_s, wglu_b, sem_in, sem_out, sem_w, *, tl, n_steps):
    half = tl // 2
    prow = half * SUBLANES
    sp = STATES_PER_TILE
    i = pl.program_id(0)
    slot = i % 2

    def in_copies(step, sl):
        return [pltpu.make_async_copy(x_hbm.at[b, pl.ds(step * tl, tl), :],
                                      xin_s.at[sl, :, b, :], sem_in.at[sl, b])
                for b in range(SUBLANES)]

    def out_copies(step, sl):
        return [pltpu.make_async_copy(res_s.at[sl, :, b, :],
                                      o_hbm.at[b, pl.ds(step * tl, tl), :], sem_out.at[sl, b])
                for b in range(SUBLANES)]

    in_slot = i % 3

    @pl.when(i == 0)
    def _():
        st_s[...] = jnp.zeros_like(st_s)
        wglu_copy = pltpu.make_async_copy(wglu_hbm, wglu_s, sem_w)
        for copy in in_copies(0, 0) + [wglu_copy] + in_copies(1, 1):
            copy.start()
        _s5_fold_params(ldt_ref, are_ref, aim_ref, btr_ref, bti_ref, cre_ref, cim_ref,
                        w1_s, cx_s, l2r_s, l2i_s)
        wglu_copy.wait()
        wglu_b[...] = wglu_s[...].astype(BF16)

    @pl.when(i + 2 < n_steps)
    def _():
        for copy in in_copies(i + 2, (i + 2) % 3):
            copy.start()

    for copy in in_copies(i, in_slot):
        copy.wait()

    @pl.when(i >= 2)
    def _():
        for copy in out_copies(i - 2, slot):
            copy.wait()

    def row_group(v, t):
        return v[t * SUBLANES:(t + 1) * SUBLANES]

    hn = _rmsnorm(xin_s[in_slot].reshape(tl * SUBLANES, D_MODEL), g_ref[...])
    for t in range(tl):
        g = (t % 2) * half + t // 2
        hn_s[g * SUBLANES:(g + 1) * SUBLANES, :] = row_group(hn, t)
    for m in range(tl // 4):
        r16 = slice(2 * m * SUBLANES, (2 * m + 2) * SUBLANES)
        even = jnp.concatenate([row_group(hn, 4 * m), row_group(hn, 4 * m + 2)], axis=0).astype(BF16)
        odd = jnp.concatenate([row_group(hn, 4 * m + 1), row_group(hn, 4 * m + 3)], axis=0).astype(BF16)
        for j in range(N_LANE_TILES):
            lanes = slice(j * LANES, (j + 1) * LANES)
            hnu_s[r16, 2 * j * LANES:(2 * j + 1) * LANES] = even[:, lanes]
            hnu_s[r16, (2 * j + 1) * LANES:(2 * j + 2) * LANES] = odd[:, lanes]

    def project_in(j):
        z = jnp.dot(hnu_s[:, 2 * j * LANES:(2 * j + 2) * LANES], w1_s[j],
                    preferred_element_type=F32)
        zy_s[j % S5_TILES_IN_FLIGHT] = z[:, :2 * LANES]
        bu_s[j % S5_TILES_IN_FLIGHT] = z[:, 2 * LANES:]

    ahead = S5_TILES_IN_FLIGHT - 1
    for j in range(ahead):
        project_in(j)
    for j in range(N_LANE_TILES):
        if j + ahead < N_LANE_TILES:
            project_in(j + ahead)
        p = j % S5_TILES_IN_FLIGHT
        lanes = slice(j * LANES, (j + 1) * LANES)
        ar = l2r_s[j]
        ai = l2i_s[j]
        xr = st_s[j, :, 0:sp]
        xi = st_s[j, :, sp:2 * sp]
        for k in range(half):
            r = slice(k * SUBLANES, (k + 1) * SUBLANES)
            xs_s[p, r, 0:sp] = xr
            xs_s[p, r, sp:2 * sp] = xi
            nxr = ar * xr - ai * xi + bu_s[p, r, 0:sp]
            nxi = ar * xi + ai * xr + bu_s[p, r, sp:2 * sp]
            xr, xi = nxr, nxi
        st_s[j, :, 0:sp] = xr
        st_s[j, :, sp:2 * sp] = xi
        y = zy_s[p] + jnp.dot(xs_s[p].astype(BF16), cx_s[j], preferred_element_type=F32)
        y_s[0:prow, lanes] = y[:, :LANES]
        y_s[prow:2 * prow, lanes] = y[:, LANES:]

    for par in range(2):
        rs = slice(par * prow, (par + 1) * prow)
        y = y_s[rs, :] + d_ref[...] * hn_s[rs, :]
        z = jnp.dot(_gelu_tanh(y).astype(BF16), wglu_b[...],
                    preferred_element_type=F32) + bglu_ref[...]
        mix = z[:, :D_MODEL] * (1.0 / (1.0 + jnp.exp(-z[:, D_MODEL:])))
        for k in range(half):
            t = 2 * k + par
            res_s[slot, t] = xin_s[in_slot, t] + row_group(mix, k)

    for copy in out_copies(i, slot):
        copy.start()

    @pl.when(i == n_steps - 1)
    def _():
        for copy in out_copies(i - 1, 1 - slot) + out_copies(i, slot):
            copy.wait()


def _s5_layer(x, g_mix, a_re, a_im, log_dt, b_re, b_im, c_re, c_im, d_skip, w_glu, b_glu):
    bsz, seq, d = x.shape
    nt, gpt, sp = N_LANE_TILES, GROUPS_PER_TILE, STATES_PER_TILE
    a_tile = lambda a: a.reshape(nt, gpt, S5_STATE)
    b_tile = lambda b: jnp.swapaxes(b, -1, -2).reshape(nt, LANES, S5_STATE)
    c_tile = lambda c: c.reshape(nt, LANES, S5_STATE)
    tl = S5_TL
    n_steps = seq // tl
    assert bsz == SUBLANES and n_steps >= 3
    rows = tl * bsz
    prow = rows // 2
    kernel = functools.partial(_s5_kernel, tl=tl, n_steps=n_steps)
    hbm = pl.BlockSpec(memory_space=pl.ANY)
    return pl.pallas_call(
        kernel,
        out_shape=jax.ShapeDtypeStruct((bsz, seq, d), F32),
        grid=(n_steps,),
        in_specs=[
            hbm,
            _const_spec((1, d)),
            pl.BlockSpec(memory_space=pltpu.SMEM),
            _const_spec((nt, gpt, S5_STATE)),
            _const_spec((nt, gpt, S5_STATE)),
            _const_spec((nt, LANES, S5_STATE)),
            _const_spec((nt, LANES, S5_STATE)),
            _const_spec((nt, LANES, S5_STATE)),
            _const_spec((nt, LANES, S5_STATE)),
            _const_spec((1, d)),
            hbm,
            _const_spec((1, 2 * d)),
        ],
        out_specs=hbm,
        scratch_shapes=[
            pltpu.VMEM((3, tl, bsz, d), F32),
            pltpu.VMEM((rows, d), F32),
            pltpu.VMEM((prow, 2 * d), BF16),
            pltpu.VMEM((S5_TILES_IN_FLIGHT, prow, 2 * LANES), F32),
            pltpu.VMEM((S5_TILES_IN_FLIGHT, prow, 2 * STATES_PER_TILE), F32),
            pltpu.VMEM((S5_TILES_IN_FLIGHT, prow, 2 * STATES_PER_TILE), F32),
            pltpu.VMEM((rows, d), F32),
            pltpu.VMEM((N_LANE_TILES, SUBLANES, 2 * STATES_PER_TILE), F32),
            pltpu.VMEM((2, tl, bsz, d), F32),
            pltpu.VMEM((nt, 2 * LANES, 2 * LANES + 2 * sp), BF16),
            pltpu.VMEM((nt, 2 * sp, 2 * LANES), BF16),
            pltpu.VMEM((nt, SUBLANES, sp), F32),
            pltpu.VMEM((nt, SUBLANES, sp), F32),
            pltpu.VMEM(w_glu.shape, w_glu.dtype),
            pltpu.VMEM(w_glu.shape, BF16),
            pltpu.SemaphoreType.DMA((3, SUBLANES)),
            pltpu.SemaphoreType.DMA((2, SUBLANES)),
            pltpu.SemaphoreType.DMA(()),
        ],
        compiler_params=pltpu.CompilerParams(
            dimension_semantics=("arbitrary",), vmem_limit_bytes=S5_VMEM_LIMIT_BYTES),
        name="s5_layer",
    )(x, g_mix.reshape(1, d), log_dt, a_tile(a_re), a_tile(a_im), b_tile(b_re), b_tile(b_im),
      c_tile(c_re), c_tile(c_im), d_skip.reshape(1, d), w_glu, b_glu.reshape(1, 2 * d))


def _inv_rms(x):
    return lax.rsqrt(jnp.mean(x * x, axis=-1, keepdims=True) + NORM_EPS)


def _mlp_kernel(h_ref, g_ref, win_hbm, wout_hbm, *rest, layer, with_kv):
    if with_kv:
        gkv_ref, wkv_ref, bkv_ref, o_ref, kv_ref, win_s, wout_s, sem = rest
    else:
        gfin_ref, o_ref, win_s, wout_s, sem = rest
    n_chunks = D_FF // MLP_FF_CHUNK

    def weight_copies(c):
        cols = pl.ds(c * MLP_FF_CHUNK, MLP_FF_CHUNK)
        return (pltpu.make_async_copy(win_hbm.at[layer, :, cols], win_s.at[:, cols], sem.at[0, c]),
                pltpu.make_async_copy(wout_hbm.at[layer, cols, :], wout_s.at[cols, :], sem.at[1, c]))

    def body(first_step):
        h = h_ref[...]
        hg = (h * g_ref[...]).astype(BF16)
        r = _inv_rms(h)
        acts = []
        for c in range(n_chunks):
            cols = slice(c * MLP_FF_CHUNK, (c + 1) * MLP_FF_CHUNK)
            if first_step:
                weight_copies(c)[0].wait()
            a = jnp.dot(hg, win_s[:, cols].astype(BF16), preferred_element_type=F32)
            acts.append(jnp.square(jnp.maximum(a, 0.0)).astype(BF16))
            if first_step:
                weight_copies(c)[1].wait()
                part = jnp.dot(acts[c], wout_s[cols, :].astype(BF16), preferred_element_type=F32)
                o_ref[...] = part if c == 0 else o_ref[...] + part
        if first_step:
            acc = o_ref[...]
        else:
            acc = jnp.dot(jnp.concatenate(acts, axis=1), wout_s[...].astype(BF16),
                          preferred_element_type=F32)
        out = h + (r * r) * acc
        if with_kv:
            o_ref[...] = out
            kv = jnp.dot((out * gkv_ref[...]).astype(BF16), wkv_ref[...].astype(BF16),
                         preferred_element_type=F32)
            kv_ref[...] = (_inv_rms(out) * kv + bkv_ref[...]).astype(BF16)
        else:
            o_ref[...] = out * _inv_rms(out) * gfin_ref[...]

    @pl.when(pl.program_id(0) == 0)
    def _():
        for c in range(n_chunks):
            for copy in weight_copies(c):
                copy.start()
        body(True)

    @pl.when(pl.program_id(0) > 0)
    def _():
        body(False)


def _mlp(h, g, w_in, w_out, layer, *, g_kv=None, w_kv=None, b_kv=None, g_fin=None):
    t, d = h.shape
    with_kv = w_kv is not None
    row = lambda n: pl.BlockSpec((MLP_TM, n), lambda i: (i, 0))
    hbm = pl.BlockSpec(memory_space=pl.ANY)
    in_specs = [row(d), _const_spec((1, d)), hbm, hbm]
    operands = [h, g.reshape(1, d), w_in, w_out]
    if with_kv:
        nkv = w_kv.shape[1]
        in_specs += [_const_spec((1, d)), _const_spec(w_kv.shape), _const_spec((1, nkv))]
        operands += [g_kv.reshape(1, d), w_kv, b_kv.reshape(1, nkv)]
        out_shape = (jax.ShapeDtypeStruct((t, d), F32), jax.ShapeDtypeStruct((t, nkv), BF16))
        out_specs = (row(d), row(nkv))
    else:
        in_specs += [_const_spec((1, d))]
        operands += [g_fin.reshape(1, d)]
        out_shape = jax.ShapeDtypeStruct((t, d), F32)
        out_specs = row(d)
    return pl.pallas_call(
        functools.partial(_mlp_kernel, layer=layer, with_kv=with_kv),
        out_shape=out_shape,
        grid=(t // MLP_TM,),
        in_specs=in_specs,
        out_specs=out_specs,
        scratch_shapes=[pltpu.VMEM(w_in.shape[1:], F32),
                        pltpu.VMEM(w_out.shape[1:], F32),
                        pltpu.SemaphoreType.DMA((2, D_FF // MLP_FF_CHUNK))],
        compiler_params=pltpu.CompilerParams(
            dimension_semantics=("arbitrary",), vmem_limit_bytes=MLP_VMEM_LIMIT_BYTES),
        name="mlp_kv" if with_kv else "mlp_final",
    )(*operands)


def _pair_tile_sources(a, g):
    head_lo = (2 * a) * Q_PER_KV + g
    head_hi = (2 * a + 1) * Q_PER_KV + g
    return (head_lo // 2, head_lo % 2), (head_hi // 2, head_hi % 2)


def _attn_kernel(sink_ref, h_ref, kvp_ref, kvc_ref, g_ref, wq_ref, bq_ref, wo_ref, bo_ref,
                 o_ref, wq_s, bq_s, wo_s, q_s, oh_s, *, tq):
    nsb = tq // WINDOW
    n = pl.program_id(1)
    n_kv_tiles = N_KV_HEADS // 2
    lo_row = lax.broadcasted_iota(jnp.int32, (1, LANES), 1) < HEAD_DIM

    @pl.when((pl.program_id(0) == 0) & (n == 0))
    def _():
        for a in range(n_kv_tiles):
            for g in range(Q_PER_KV):
                (t_lo, h_lo), (t_hi, h_hi) = _pair_tile_sources(a, g)
                dst = slice((a * Q_PER_KV + g) * LANES, (a * Q_PER_KV + g + 1) * LANES)

                def pair(ref):
                    src_lo = ref[:, t_lo * LANES:(t_lo + 1) * LANES]
                    src_hi = ref[:, t_hi * LANES:(t_hi + 1) * LANES]
                    if h_lo == 1:
                        src_lo = pltpu.roll(src_lo, HEAD_DIM, axis=1)
                    if h_hi == 0:
                        src_hi = pltpu.roll(src_hi, HEAD_DIM, axis=1)
                    return jnp.where(lo_row, src_lo, src_hi)

                wq_s[:, dst] = pair(wq_ref).astype(BF16)
                bq_s[:, dst] = pair(bq_ref)
                for half in range(2):
                    head = (2 * a + half) * Q_PER_KV + g
                    r0 = (a * Q_PER_KV + g) * LANES + half * HEAD_DIM
                    wo_s[r0:r0 + HEAD_DIM, :] = (
                        wo_ref[head * HEAD_DIM:(head + 1) * HEAD_DIM, :].astype(BF16))

    h = h_ref[0]
    hn = _rmsnorm(h, g_ref[...]).astype(BF16)
    q = jnp.dot(hn, wq_s[...], preferred_element_type=F32) + bq_s[...]
    q_s[...] = (q * (LOG2_E / math.sqrt(HEAD_DIM))).astype(BF16)

    lo = lax.broadcasted_iota(jnp.int32, (WINDOW, LANES), 1) < HEAD_DIM
    own = (lax.broadcasted_iota(jnp.int32, (WINDOW, WINDOW), 1)
           <= lax.broadcasted_iota(jnp.int32, (WINDOW, WINDOW), 0))
    zero = jnp.zeros((WINDOW, LANES), BF16)
    fzero = jnp.zeros((WINDOW, WINDOW), F32)

    for sb in range(nsb):
        rows = slice(sb * WINDOW, (sb + 1) * WINDOW)
        prev_bias = jnp.where(n == 0, -jnp.inf, 0.0).astype(F32) if sb == 0 else None
        for a in range(n_kv_tiles):
            kl = slice(a * LANES, (a + 1) * LANES)
            vl = slice((n_kv_tiles + a) * LANES, (n_kv_tiles + a + 1) * LANES)
            if sb == 0:
                kprev, vprev = kvp_ref[0, :, kl], kvp_ref[0, :, vl]
            else:
                prow = slice((sb - 1) * WINDOW, sb * WINDOW)
                kprev, vprev = kvc_ref[0, prow, kl], kvc_ref[0, prow, vl]
            kd = jnp.concatenate([kprev, kvc_ref[0, rows, kl]], axis=0)
            vd = jnp.concatenate([vprev, kvc_ref[0, rows, vl]], axis=0)
            vd1 = jnp.concatenate([vd, jnp.ones((2 * WINDOW, LANES), BF16)], axis=1)
            qt = [q_s[rows, (a * Q_PER_KV + g) * LANES:(a * Q_PER_KV + g + 1) * LANES]
                  for g in range(Q_PER_KV)]
            lhs = jnp.concatenate([jnp.where(lo, t, zero) for t in qt]
                                  + [jnp.where(lo, zero, t) for t in qt], axis=0)
            s = lax.dot_general(lhs, kd, (((1,), (1,)), ((), ())),
                                preferred_element_type=F32)
            ps, sink_terms = [], []
            for half in range(2):
                for g in range(Q_PER_KV):
                    blk = half * Q_PER_KV + g
                    s_prev = s[blk * WINDOW:(blk + 1) * WINDOW, :WINDOW]
                    s_own = s[blk * WINDOW:(blk + 1) * WINDOW, WINDOW:]
                    if prev_bias is not None:
                        s_prev = s_prev + prev_bias
                    sg = jnp.where(own, s_own, s_prev)
                    sink = sink_ref[(2 * a + half) * Q_PER_KV + g] * LOG2_E
                    m = jnp.maximum(jnp.max(sg, axis=-1, keepdims=True), sink)
                    p = jnp.exp2(sg - m)
                    ps.append(jnp.concatenate([jnp.where(own, fzero, p).astype(BF16),
                                               jnp.where(own, p, fzero).astype(BF16)], axis=1))
                    sink_terms.append(jnp.exp2(sink - m))
            od = jnp.dot(jnp.concatenate(ps, axis=0), vd1, preferred_element_type=F32)
            for g in range(Q_PER_KV):
                r_lo = slice(g * WINDOW, (g + 1) * WINDOW)
                r_hi = slice((Q_PER_KV + g) * WINDOW, (Q_PER_KV + g + 1) * WINDOW)
                num = jnp.where(lo, od[r_lo, :LANES], od[r_hi, :LANES])
                den = (jnp.where(lo, od[r_lo, LANES:], od[r_hi, LANES:])
                       + jnp.where(lo, sink_terms[g], sink_terms[Q_PER_KV + g]))
                oh_s[rows, (a * Q_PER_KV + g) * LANES:(a * Q_PER_KV + g + 1) * LANES] = (
                    (num * (1.0 / den)).astype(BF16))

    o_ref[0] = h + jnp.dot(oh_s[...], wo_s[...], preferred_element_type=F32) + bo_ref[...]


def _attn_layer(h, kv, sinks, g, w_q, b_q, w_o, b_o):
    bsz, seq, d = h.shape
    nkv = kv.shape[-1]
    tq = ATTN_TQ
    nsb = tq // WINDOW
    kernel = functools.partial(_attn_kernel, tq=tq)
    return pl.pallas_call(
        kernel,
        out_shape=jax.ShapeDtypeStruct((bsz, seq, d), F32),
        grid=(bsz, seq // tq),
        in_specs=[
            pl.BlockSpec(memory_space=pltpu.SMEM),
            pl.BlockSpec((1, tq, d), lambda b, n: (b, n, 0)),
            pl.BlockSpec((1, WINDOW, nkv), lambda b, n: (b, jnp.maximum(n * nsb - 1, 0), 0)),
            pl.BlockSpec((1, tq, nkv), lambda b, n: (b, n, 0)),
            _const_spec((1, d)),
            _const_spec(w_q.shape),
            _const_spec((1, d)),
            _const_spec(w_o.shape),
            _const_spec((1, d)),
        ],
        out_specs=pl.BlockSpec((1, tq, d), lambda b, n: (b, n, 0)),
        scratch_shapes=[pltpu.VMEM((d, d), BF16),
                        pltpu.VMEM((1, d), F32),
                        pltpu.VMEM((d, d), BF16),
                        pltpu.VMEM((tq, d), BF16),
                        pltpu.VMEM((tq, d), BF16)],
        compiler_params=pltpu.CompilerParams(
            dimension_semantics=("arbitrary", "arbitrary"), vmem_limit_bytes=ATTN_VMEM_LIMIT_BYTES),
        name="attn",
    )(sinks, h, kv, kv, g.reshape(1, d), w_q, b_q.reshape(1, d), w_o, b_o.reshape(1, d))


def kernel(x, norm_mix, norm_mlp, norm_kv, norm_final, s5_a_re, s5_a_im, s5_log_dt, s5_b_re, s5_b_im, s5_c_re, s5_c_im, s5_d, s5_w_glu, s5_b_glu, w_kv, b_kv, w_q, b_q, sinks, w_o, b_o, w_mlp_in, w_mlp_out):
    bsz, seq, d = x.shape

    h = _s5_layer(x, norm_mix[0], s5_a_re[0], s5_a_im[0], s5_log_dt[0], s5_b_re[0], s5_b_im[0],
                  s5_c_re[0], s5_c_im[0], s5_d[0], s5_w_glu[0], s5_b_glu[0])

    h, kv = _mlp(h.reshape(bsz * seq, d), norm_mlp[0], w_mlp_in, w_mlp_out, 0,
                 g_kv=norm_kv, w_kv=w_kv, b_kv=b_kv)

    h = _attn_layer(h.reshape(bsz, seq, d), kv.reshape(bsz, seq, -1), sinks[0], norm_mix[1],
                    w_q[0], b_q[0], w_o[0], b_o[0])

    out = _mlp(h.reshape(bsz * seq, d), norm_mlp[1], w_mlp_in, w_mlp_out, 1, g_fin=norm_final)
    return out.reshape(bsz, seq, d)
```

```python
import functools
import math

import jax
import jax.numpy as jnp
from jax import lax
from jax.experimental import pallas as pl
from jax.experimental.pallas import tpu as pltpu

F32 = jnp.float32
BF16 = jnp.bfloat16

D_MODEL = 1024
S5_GROUP = 16
S5_STATE = 64
LAMBDA_RE_MAX = -1e-4
HEAD_DIM = 64
N_Q_HEADS = D_MODEL // HEAD_DIM
N_KV_HEADS = 4
Q_PER_KV = N_Q_HEADS // N_KV_HEADS
WINDOW = 128
D_FF = 4 * D_MODEL
NORM_EPS = 1e-5
LOG2_E = math.log2(math.e)

LANES = 128
SUBLANES = 8
N_LANE_TILES = D_MODEL // LANES
GROUPS_PER_TILE = LANES // S5_GROUP
STATES_PER_TILE = GROUPS_PER_TILE * S5_STATE
MIB = 1024 * 1024
S5_VMEM_LIMIT_BYTES = 56 * MIB
MLP_VMEM_LIMIT_BYTES = 52 * MIB
ATTN_VMEM_LIMIT_BYTES = 40 * MIB

S5_TL = 64
S5_TILES_IN_FLIGHT = 2
MLP_TM = 512
MLP_FF_CHUNK = 1024
ATTN_TQ = 1024


def _rmsnorm(x, g):
    return x * lax.rsqrt(jnp.mean(x * x, axis=-1, keepdims=True) + NORM_EPS) * g


def _gelu_tanh(x):
    c = math.sqrt(2.0 / math.pi)
    return 0.5 * x * (1.0 + jnp.tanh(c * (x + 0.044715 * (x * x * x))))


def _const_spec(shape):
    nd = len(shape)
    return pl.BlockSpec(shape, lambda *_: (0,) * nd, pipeline_mode=pl.Buffered(1))


def _s5_fold_params(ldt_ref, are_ref, aim_ref, btr_ref, bti_ref, cre_ref, cim_ref,
                    w1_ref, cx_ref, l2r_ref, l2i_ref):
    for tile in range(N_LANE_TILES):
        log_dt = [ldt_ref[tile * GROUPS_PER_TILE + g] for g in range(GROUPS_PER_TILE)]
        _s5_params_tile(are_ref.at[tile], aim_ref.at[tile], log_dt, btr_ref.at[tile],
                        bti_ref.at[tile], cre_ref.at[tile], cim_ref.at[tile],
                        w1_ref.at[tile], cx_ref.at[tile], l2r_ref.at[tile], l2i_ref.at[tile])


def _s5_params_tile(are_ref, aim_ref, log_dt, btr_ref, bti_ref, cre_ref, cim_ref,
                    w1_ref, cx_ref, l2r_ref, l2i_ref):
    gpt, grp, nst = GROUPS_PER_TILE, S5_GROUP, S5_STATE
    ar = jnp.minimum(are_ref[...], LAMBDA_RE_MAX)
    ai = aim_ref[...]
    dt = jnp.exp(jnp.concatenate([jnp.full((1, nst), v, F32) for v in log_dt], axis=0))
    mag = jnp.exp(ar * dt)
    ang = ai * dt
    lr = mag * jnp.cos(ang)
    li = mag * jnp.sin(ang)
    den = ar * ar + ai * ai
    cr = ((lr - 1.0) * ar + li * ai) / den
    ci = (li * ar - (lr - 1.0) * ai) / den
    l2r = lr * lr - li * li
    l2i = 2.0 * (lr * li)

    def per_row(v):
        return jnp.concatenate([jnp.broadcast_to(v[g:g + 1], (grp, nst)) for g in range(gpt)], axis=0)

    row_g = lax.broadcasted_iota(jnp.int32, (LANES, STATES_PER_TILE), 0) // grp
    col_g = lax.broadcasted_iota(jnp.int32, (LANES, STATES_PER_TILE), 1) // nst
    own = row_g == col_g

    def blockdiag(v):
        return jnp.where(own, jnp.concatenate([v] * gpt, axis=1), 0.0)

    lr_c, li_c, cr_c, ci_c = per_row(lr), per_row(li), per_row(cr), per_row(ci)
    l2r_c, l2i_c = per_row(l2r), per_row(l2i)
    btr, bti = btr_ref[...], bti_ref[...]
    bbr = cr_c * btr - ci_c * bti
    bbi = cr_c * bti + ci_c * btr
    lbr = lr_c * bbr - li_c * bbi
    lbi = lr_c * bbi + li_c * bbr
    wbr, wbi = blockdiag(bbr), blockdiag(bbi)
    w1r, w1i = blockdiag(lbr), blockdiag(lbi)
    ccr, cci = cre_ref[...], cim_ref[...]

    lhs = jnp.concatenate([jnp.concatenate([bbr, -bbi], axis=1),
                           jnp.concatenate([lbr, -lbi], axis=1)], axis=0)
    kk = lax.dot_general(lhs, jnp.concatenate([ccr, cci], axis=1), (((1,), (1,)), ((), ())),
                         precision=lax.Precision.HIGHEST, preferred_element_type=F32)
    same_group = (lax.broadcasted_iota(jnp.int32, (LANES, LANES), 0) // grp
                  == lax.broadcasted_iota(jnp.int32, (LANES, LANES), 1) // grp)
    k0 = jnp.where(same_group, kk[:LANES], 0.0)
    k1 = jnp.where(same_group, kk[LANES:], 0.0)
    top =jnp.concatenate([k0, k1, w1r, w1i], axis=1)
    bot = jnp.concatenate([jnp.zeros_like(k0), k0, wbr, wbi], axis=1)
    w1_ref[...] = jnp.concatenate([top, bot], axis=0).astype(BF16)

    def readout_t(pr, pi):
        return jnp.concatenate([blockdiag(pr * ccr - pi * cci),
                                -blockdiag(pi * ccr + pr * cci)], axis=1)

    cx_t = jnp.concatenate([readout_t(lr_c, li_c), readout_t(l2r_c, l2i_c)], axis=0)
    cx_ref[...] = cx_t.T.astype(BF16)

    def flat_row(v):
        row = jnp.concatenate([v[g:g + 1] for g in range(gpt)], axis=1)
        return jnp.broadcast_to(row, (SUBLANES, STATES_PER_TILE))

    l2r_ref[...] = flat_row(l2r)
    l2i_ref[...] = flat_row(l2i)


def _s5_kernel(x_hbm, g_ref, ldt_ref, are_ref, aim_ref, btr_ref, bti_ref, cre_ref, cim_ref,
               d_ref, wglu_hbm, bglu_ref,
               o_hbm, xin_s, hn_s, hnu_s, zy_s, bu_s, xs_s, y_s, st_s, res_s,
               w1_s, cx_s, l2r_s, l2i_s, wglu_s, wglu_b, sem_in, sem_out, sem_w, *, tl, n_steps):
    half = tl // 2
    prow = half * SUBLANES
    sp = STATES_PER_TILE
    i = pl.program_id(0)
    slot = i % 2

    def in_copies(step, sl):
        return [pltpu.make_async_copy(x_hbm.at[b, pl.ds(step * tl, tl), :],
                                      xin_s.at[sl, :, b, :], sem_in.at[sl, b])
                for b in range(SUBLANES)]

    def out_copies(step, sl):
        return [pltpu.make_async_copy(res_s.at[sl, :, b, :],
                                      o_hbm.at[b, pl.ds(step * tl, tl), :], sem_out.at[sl, b])
                for b in range(SUBLANES)]

    in_slot = i % 3

    @pl.when(i == 0)
    def _():
        st_s[...] = jnp.zeros_like(st_s)
        wglu_copy = pltpu.make_async_copy(wglu_hbm, wglu_s, sem_w)
        for copy in in_copies(0, 0) + [wglu_copy] + in_copies(1, 1):
            copy.start()
        _s5_fold_params(ldt_ref, are_ref, aim_ref, btr_ref, bti_ref, cre_ref, cim_ref,
                        w1_s, cx_s, l2r_s, l2i_s)
        wglu_copy.wait()
        wglu_b[...] = wglu_s[...].astype(BF16)

    @pl.when(i + 2 < n_steps)
    def _():
        for copy in in_copies(i + 2, (i + 2) % 3):
            copy.start()

    for copy in in_copies(i, in_slot):
        copy.wait()

    @pl.when(i >= 2)
    def _():
        for copy in out_copies(i - 2, slot):
            copy.wait()

    def row_group(v, t):
        return v[t * SUBLANES:(t + 1) * SUBLANES]

    hn = _rmsnorm(xin_s[in_slot].reshape(tl * SUBLANES, D_MODEL), g_ref[...])
    for t in range(tl):
        g = (t % 2) * half + t // 2
        hn_s[g * SUBLANES:(g + 1) * SUBLANES, :] = row_group(hn, t)
    for m in range(tl // 4):
        r16 = slice(2 * m * SUBLANES, (2 * m + 2) * SUBLANES)
        even = jnp.concatenate([row_group(hn, 4 * m), row_group(hn, 4 * m + 2)], axis=0).astype(BF16)
        odd = jnp.concatenate([row_group(hn, 4 * m + 1), row_group(hn, 4 * m + 3)], axis=0).astype(BF16)
        for j in range(N_LANE_TILES):
            lanes = slice(j * LANES, (j + 1) * LANES)
            hnu_s[r16, 2 * j * LANES:(2 * j + 1) * LANES] = even[:, lanes]
            hnu_s[r16, (2 * j + 1) * LANES:(2 * j + 2) * LANES] = odd[:, lanes]

    def project_in(j):
        z = jnp.dot(hnu_s[:, 2 * j * LANES:(2 * j + 2) * LANES], w1_s[j],
                    preferred_element_type=F32)
        zy_s[j % S5_TILES_IN_FLIGHT] = z[:, :2 * LANES]
        bu_s[j % S5_TILES_IN_FLIGHT] = z[:, 2 * LANES:]

    ahead = S5_TILES_IN_FLIGHT - 1
    for j in range(ahead):
        project_in(j)
    for j in range(N_LANE_TILES):
        if j + ahead < N_LANE_TILES:
            project_in(j + ahead)
        p = j % S5_TILES_IN_FLIGHT
        lanes = slice(j * LANES, (j + 1) * LANES)
        ar = l2r_s[j]
        ai = l2i_s[j]
        xr = st_s[j, :, 0:sp]
        xi = st_s[j, :, sp:2 * sp]
        for k in range(half):
            r = slice(k * SUBLANES, (k + 1) * SUBLANES)
            xs_s[p, r, 0:sp] = xr
            xs_s[p, r, sp:2 * sp] = xi
            nxr = ar * xr - ai * xi + bu_s[p, r, 0:sp]
            nxi = ar * xi + ai * xr + bu_s[p, r, sp:2 * sp]
            xr, xi = nxr, nxi
        st_s[j, :, 0:sp] = xr
        st_s[j, :, sp:2 * sp] = xi
        y = zy_s[p] + jnp.dot(xs_s[p].astype(BF16), cx_s[j], preferred_element_type=F32)
        y_s[0:prow, lanes] = y[:, :LANES]
        y_s[prow:2 * prow, lanes] = y[:, LANES:]

    for par in range(2):
        rs = slice(par * prow, (par + 1) * prow)
        y = y_s[rs, :] + d_ref[...] * hn_s[rs, :]
        z = jnp.dot(_gelu_tanh(y).astype(BF16), wglu_b[...],
                    preferred_element_type=F32) + bglu_ref[...]
        mix = z[:, :D_MODEL] * (1.0 / (1.0 + jnp.exp(-z[:, D_MODEL:])))
        for k in range(half):
            t = 2 * k + par
            res_s[slot, t] = xin_s[in_slot, t] + row_group(mix, k)

    for copy in out_copies(i, slot):
        copy.start(priority=1)

    @pl.when(i == n_steps - 1)
    def _():
        for copy in out_copies(i - 1, 1 - slot) + out_copies(i, slot):
            copy.wait()


def _s5_layer(x, g_mix, a_re, a_im, log_dt, b_re, b_im, c_re, c_im, d_skip, w_glu, b_glu):
    bsz, seq, d = x.shape
    nt, gpt, sp = N_LANE_TILES, GROUPS_PER_TILE, STATES_PER_TILE
    a_tile = lambda a: a.reshape(nt, gpt, S5_STATE)
    b_tile = lambda b: jnp.swapaxes(b, -1, -2).reshape(nt, LANES, S5_STATE)
    c_tile = lambda c: c.reshape(nt, LANES, S5_STATE)
    tl = S5_TL
    n_steps = seq // tl
    assert bsz == SUBLANES and n_steps >= 3
    rows = tl * bsz
    prow = rows // 2
    kernel = functools.partial(_s5_kernel, tl=tl, n_steps=n_steps)
    hbm = pl.BlockSpec(memory_space=pl.ANY)
    return pl.pallas_call(
        kernel,
        out_shape=jax.ShapeDtypeStruct((bsz, seq, d), F32),
        grid=(n_steps,),
        in_specs=[
            hbm,
            _const_spec((1, d)),
            pl.BlockSpec(memory_space=pltpu.SMEM),
            _const_spec((nt, gpt, S5_STATE)),
            _const_spec((nt, gpt, S5_STATE)),
            _const_spec((nt, LANES, S5_STATE)),
            _const_spec((nt, LANES, S5_STATE)),
            _const_spec((nt, LANES, S5_STATE)),
            _const_spec((nt, LANES, S5_STATE)),
            _const_spec((1, d)),
            hbm,
            _const_spec((1, 2 * d)),
        ],
        out_specs=hbm,
        scratch_shapes=[
            pltpu.VMEM((3, tl, bsz, d), F32),
            pltpu.VMEM((rows, d), F32),
            pltpu.VMEM((prow, 2 * d), BF16),
            pltpu.VMEM((S5_TILES_IN_FLIGHT, prow, 2 * LANES), F32),
            pltpu.VMEM((S5_TILES_IN_FLIGHT, prow, 2 * STATES_PER_TILE), F32),
            pltpu.VMEM((S5_TILES_IN_FLIGHT, prow, 2 * STATES_PER_TILE), F32),
            pltpu.VMEM((rows, d), F32),
            pltpu.VMEM((N_LANE_TILES, SUBLANES, 2 * STATES_PER_TILE), F32),
            pltpu.VMEM((2, tl, bsz, d), F32),
            pltpu.VMEM((nt, 2 * LANES, 2 * LANES + 2 * sp), BF16),
            pltpu.VMEM((nt, 2 * sp, 2 * LANES), BF16),
            pltpu.VMEM((nt, SUBLANES, sp), F32),
            pltpu.VMEM((nt, SUBLANES, sp), F32),
            pltpu.VMEM(w_glu.shape, w_glu.dtype),
            pltpu.VMEM(w_glu.shape, BF16),
            pltpu.SemaphoreType.DMA((3, SUBLANES)),
            pltpu.SemaphoreType.DMA((2, SUBLANES)),
            pltpu.SemaphoreType.DMA(()),
        ],
        compiler_params=pltpu.CompilerParams(
            dimension_semantics=("arbitrary",), vmem_limit_bytes=S5_VMEM_LIMIT_BYTES),
        name="s5_layer",
    )(x, g_mix.reshape(1, d), log_dt, a_tile(a_re), a_tile(a_im), b_tile(b_re), b_tile(b_im),
      c_tile(c_re), c_tile(c_im), d_skip.reshape(1, d), w_glu, b_glu.reshape(1, 2 * d))


def _inv_rms(x):
    return lax.rsqrt(jnp.mean(x * x, axis=-1, keepdims=True) + NORM_EPS)


def _mlp_kernel(h_ref, g_ref, win_hbm, wout_hbm, *rest, layer, with_kv):
    if with_kv:
        gkv_ref, wkv_ref, bkv_ref, o_ref, kv_ref, win_s, wout_s, sem = rest
    else:
        gfin_ref, o_ref, win_s, wout_s, sem = rest
    n_chunks = D_FF // MLP_FF_CHUNK

    def weight_copies(c):
        cols = pl.ds(c * MLP_FF_CHUNK, MLP_FF_CHUNK)
        return (pltpu.make_async_copy(win_hbm.at[layer, :, cols], win_s.at[:, cols], sem.at[0, c]),
                pltpu.make_async_copy(wout_hbm.at[layer, cols, :], wout_s.at[cols, :], sem.at[1, c]))

    def body(first_step):
        h = h_ref[...]
        hg = (h * g_ref[...]).astype(BF16)
        r = _inv_rms(h)
        acts = []
        for c in range(n_chunks):
            cols = slice(c * MLP_FF_CHUNK, (c + 1) * MLP_FF_CHUNK)
            if first_step:
                weight_copies(c)[0].wait()
            a = jnp.dot(hg, win_s[:, cols].astype(BF16), preferred_element_type=F32)
            acts.append(jnp.square(jnp.maximum(a, 0.0)).astype(BF16))
            if first_step:
                weight_copies(c)[1].wait()
                part = jnp.dot(acts[c], wout_s[cols, :].astype(BF16), preferred_element_type=F32)
                o_ref[...] = part if c == 0 else o_ref[...] + part
        if first_step:
            acc = o_ref[...]
        else:
            acc = jnp.dot(jnp.concatenate(acts, axis=1), wout_s[...].astype(BF16),
                          preferred_element_type=F32)
        out = h + (r * r) * acc
        if with_kv:
            o_ref[...] = out
            kv = jnp.dot((out * gkv_ref[...]).astype(BF16), wkv_ref[...].astype(BF16),
                         preferred_element_type=F32)
            kv_ref[...] = (_inv_rms(out) * kv + bkv_ref[...]).astype(BF16)
        else:
            o_ref[...] = out * _inv_rms(out) * gfin_ref[...]

    @pl.when(pl.program_id(0) == 0)
    def _():
        for c in range(n_chunks):
            for copy in weight_copies(c):
                copy.start()
        body(True)

    @pl.when(pl.program_id(0) > 0)
    def _():
        body(False)


def _mlp(h, g, w_in, w_out, layer, *, g_kv=None, w_kv=None, b_kv=None, g_fin=None):
    t, d = h.shape
    with_kv = w_kv is not None
    row = lambda n: pl.BlockSpec((MLP_TM, n), lambda i: (i, 0))
    hbm = pl.BlockSpec(memory_space=pl.ANY)
    in_specs = [row(d), _const_spec((1, d)), hbm, hbm]
    operands = [h, g.reshape(1, d), w_in, w_out]
    if with_kv:
        nkv = w_kv.shape[1]
        in_specs += [_const_spec((1, d)), _const_spec(w_kv.shape), _const_spec((1, nkv))]
        operands += [g_kv.reshape(1, d), w_kv, b_kv.reshape(1, nkv)]
        out_shape = (jax.ShapeDtypeStruct((t, d), F32), jax.ShapeDtypeStruct((t, nkv), BF16))
        out_specs = (row(d), row(nkv))
    else:
        in_specs += [_const_spec((1, d))]
        operands += [g_fin.reshape(1, d)]
        out_shape = jax.ShapeDtypeStruct((t, d), F32)
        out_specs = row(d)
    return pl.pallas_call(
        functools.partial(_mlp_kernel, layer=layer, with_kv=with_kv),
        out_shape=out_shape,
        grid=(t // MLP_TM,),
        in_specs=in_specs,
        out_specs=out_specs,
        scratch_shapes=[pltpu.VMEM(w_in.shape[1:], F32),
                        pltpu.VMEM(w_out.shape[1:], F32),
                        pltpu.SemaphoreType.DMA((2, D_FF // MLP_FF_CHUNK))],
        compiler_params=pltpu.CompilerParams(
            dimension_semantics=("arbitrary",), vmem_limit_bytes=MLP_VMEM_LIMIT_BYTES),
        name="mlp_kv" if with_kv else "mlp_final",
    )(*operands)


def _pair_tile_sources(a, g):
    head_lo = (2 * a) * Q_PER_KV + g
    head_hi = (2 * a + 1) * Q_PER_KV + g
    return (head_lo // 2, head_lo % 2), (head_hi // 2, head_hi % 2)


def _attn_kernel(sink_ref, h_ref, kvp_ref, kvc_ref, g_ref, wq_ref, bq_ref, wo_ref, bo_ref,
                 o_ref, wq_s, bq_s, wo_s, q_s, oh_s, *, tq):
    nsb = tq // WINDOW
    n = pl.program_id(1)
    n_kv_tiles = N_KV_HEADS // 2
    lo_row = lax.broadcasted_iota(jnp.int32, (1, LANES), 1) < HEAD_DIM

    @pl.when((pl.program_id(0) == 0) & (n == 0))
    def _():
        for a in range(n_kv_tiles):
            for g in range(Q_PER_KV):
                (t_lo, h_lo), (t_hi, h_hi) = _pair_tile_sources(a, g)
                dst = slice((a * Q_PER_KV + g) * LANES, (a * Q_PER_KV + g + 1) * LANES)

                def pair(ref):
                    src_lo = ref[:, t_lo * LANES:(t_lo + 1) * LANES]
                    src_hi = ref[:, t_hi * LANES:(t_hi + 1) * LANES]
                    if h_lo == 1:
                        src_lo = pltpu.roll(src_lo, HEAD_DIM, axis=1)
                    if h_hi == 0:
                        src_hi = pltpu.roll(src_hi, HEAD_DIM, axis=1)
                    return jnp.where(lo_row, src_lo, src_hi)

                wq_s[:, dst] = pair(wq_ref).astype(BF16)
                bq_s[:, dst] = pair(bq_ref)
                for half in range(2):
                    head = (2 * a + half) * Q_PER_KV + g
                    r0 = (a * Q_PER_KV + g) * LANES + half * HEAD_DIM
                    wo_s[r0:r0 + HEAD_DIM, :] = (
                        wo_ref[head * HEAD_DIM:(head + 1) * HEAD_DIM, :].astype(BF16))

    h = h_ref[0]
    hn = _rmsnorm(h, g_ref[...]).astype(BF16)
    q = jnp.dot(hn, wq_s[...], preferred_element_type=F32) + bq_s[...]
    q_s[...] = (q * (LOG2_E / math.sqrt(HEAD_DIM))).astype(BF16)

    lo = lax.broadcasted_iota(jnp.int32, (WINDOW, LANES), 1) < HEAD_DIM
    own = (lax.broadcasted_iota(jnp.int32, (WINDOW, WINDOW), 1)
           <= lax.broadcasted_iota(jnp.int32, (WINDOW, WINDOW), 0))
    zero = jnp.zeros((WINDOW, LANES), BF16)
    fzero = jnp.zeros((WINDOW, WINDOW), F32)

    for sb in range(nsb):
        rows = slice(sb * WINDOW, (sb + 1) * WINDOW)
        prev_bias = jnp.where(n == 0, -jnp.inf, 0.0).astype(F32) if sb == 0 else None
        for a in range(n_kv_tiles):
            kl = slice(a * LANES, (a + 1) * LANES)
            vl = slice((n_kv_tiles + a) * LANES, (n_kv_tiles + a + 1) * LANES)
            if sb == 0:
                kprev, vprev = kvp_ref[0, :, kl], kvp_ref[0, :, vl]
            else:
                prow = slice((sb - 1) * WINDOW, sb * WINDOW)
                kprev, vprev = kvc_ref[0, prow, kl], kvc_ref[0, prow, vl]
            kd = jnp.concatenate([kprev, kvc_ref[0, rows, kl]], axis=0)
            vd = jnp.concatenate([vprev, kvc_ref[0, rows, vl]], axis=0)
            vd1 = jnp.concatenate([vd, jnp.ones((2 * WINDOW, LANES), BF16)], axis=1)
            qt = [q_s[rows, (a * Q_PER_KV + g) * LANES:(a * Q_PER_KV + g + 1) * LANES]
                  for g in range(Q_PER_KV)]
            lhs = jnp.concatenate([jnp.where(lo, t, zero) for t in qt]
                                  + [jnp.where(lo, zero, t) for t in qt], axis=0)
            s = lax.dot_general(lhs, kd, (((1,), (1,)), ((), ())),
                                preferred_element_type=F32)
            ps, sink_terms = [], []
            for half in range(2):
                for g in range(Q_PER_KV):
                    blk = half * Q_PER_KV + g
                    s_prev = s[blk * WINDOW:(blk + 1) * WINDOW, :WINDOW]
                    s_own = s[blk * WINDOW:(blk + 1) * WINDOW, WINDOW:]
                    if prev_bias is not None:
                        s_prev = s_prev + prev_bias
                    sg = jnp.where(own, s_own, s_prev)
                    sink = sink_ref[(2 * a + half) * Q_PER_KV + g] * LOG2_E
                    m = jnp.maximum(jnp.max(sg, axis=-1, keepdims=True), sink)
                    p = jnp.exp2(sg - m)
                    ps.append(jnp.concatenate([jnp.where(own, fzero, p).astype(BF16),
                                               jnp.where(own, p, fzero).astype(BF16)], axis=1))
                    sink_terms.append(jnp.exp2(sink - m))
            od = jnp.dot(jnp.concatenate(ps, axis=0), vd1, preferred_element_type=F32)
            for g in range(Q_PER_KV):
                r_lo = slice(g * WINDOW, (g + 1) * WINDOW)
                r_hi = slice((Q_PER_KV + g) * WINDOW, (Q_PER_KV + g + 1) * WINDOW)
                num = jnp.where(lo, od[r_lo, :LANES], od[r_hi, :LANES])
                den = (jnp.where(lo, od[r_lo, LANES:], od[r_hi, LANES:])
                       + jnp.where(lo, sink_terms[g], sink_terms[Q_PER_KV + g]))
                oh_s[rows, (a * Q_PER_KV + g) * LANES:(a * Q_PER_KV + g + 1) * LANES] = (
                    (num * (1.0 / den)).astype(BF16))

    o_ref[0] = h + jnp.dot(oh_s[...], wo_s[...], preferred_element_type=F32) + bo_ref[...]


def _attn_layer(h, kv, sinks, g, w_q, b_q, w_o, b_o):
    bsz, seq, d = h.shape
    nkv = kv.shape[-1]
    tq = ATTN_TQ
    nsb = tq // WINDOW
    kernel = functools.partial(_attn_kernel, tq=tq)
    return pl.pallas_call(
        kernel,
        out_shape=jax.ShapeDtypeStruct((bsz, seq, d), F32),
        grid=(bsz, seq // tq),
        in_specs=[
            pl.BlockSpec(memory_space=pltpu.SMEM),
            pl.BlockSpec((1, tq, d), lambda b, n: (b, n, 0)),
            pl.BlockSpec((1, WINDOW, nkv), lambda b, n: (b, jnp.maximum(n * nsb - 1, 0), 0)),
            pl.BlockSpec((1, tq, nkv), lambda b, n: (b, n, 0)),
            _const_spec((1, d)),
            _const_spec(w_q.shape),
            _const_spec((1, d)),
            _const_spec(w_o.shape),
            _const_spec((1, d)),
        ],
        out_specs=pl.BlockSpec((1, tq, d), lambda b, n: (b, n, 0)),
        scratch_shapes=[pltpu.VMEM((d, d), BF16),
                        pltpu.VMEM((1, d), F32),
                        pltpu.VMEM((d, d), BF16),
                        pltpu.VMEM((tq, d), BF16),
                        pltpu.VMEM((tq, d), BF16)],
        compiler_params=pltpu.CompilerParams(
            dimension_semantics=("arbitrary", "arbitrary"), vmem_limit_bytes=ATTN_VMEM_LIMIT_BYTES),
        name="attn",
    )(sinks, h, kv, kv, g.reshape(1, d), w_q, b_q.reshape(1, d), w_o, b_o.reshape(1, d))


def kernel(x, norm_mix, norm_mlp, norm_kv, norm_final, s5_a_re, s5_a_im, s5_log_dt, s5_b_re, s5_b_im, s5_c_re, s5_c_im, s5_d, s5_w_glu, s5_b_glu, w_kv, b_kv, w_q, b_q, sinks, w_o, b_o, w_mlp_in, w_mlp_out):
    bsz, seq, d = x.shape

    h = _s5_layer(x, norm_mix[0], s5_a_re[0], s5_a_im[0], s5_log_dt[0], s5_b_re[0], s5_b_im[0],
                  s5_c_re[0], s5_c_im[0], s5_d[0], s5_w_glu[0], s5_b_glu[0])

    h, kv = _mlp(h.reshape(bsz * seq, d), norm_mlp[0], w_mlp_in, w_mlp_out, 0,
                 g_kv=norm_kv, w_kv=w_kv, b_kv=b_kv)

    h = _attn_layer(h.reshape(bsz, seq, d), kv.reshape(bsz, seq, -1), sinks[0], norm_mix[1],
                    w_q[0], b_q[0], w_o[0], b_o[0])

    out = _mlp(h.reshape(bsz * seq, d), norm_mlp[1], w_mlp_in, w_mlp_out, 1, g_fin=norm_final)
    return out.reshape(bsz, seq, d)
```
